```python
import jax, jax.numpy as jnp
from jax import lax
import numpy as np

D_MODEL = 4096
BATCH = 2
SEQ = 8192
DEPTH = 1

V_HEAD_DIM = 128
QK_NOPE_DIM = 128
QK_ROPE_DIM = 64
MLA_WIDTH = D_MODEL // 2
MLA_HEADS = MLA_WIDTH // V_HEAD_DIM
Q_LORA_RANK = D_MODEL // 4
KV_LORA_RANK = D_MODEL // 8
ROPE_THETA = 10000.0
Q_BLOCK = 128
FOURIER_WIDTH = D_MODEL - MLA_WIDTH
FOURIER_GROUP_DIM = 128
FOURIER_GROUPS = FOURIER_WIDTH // FOURIER_GROUP_DIM
IN_Q_END = Q_LORA_RANK
IN_KV_END = IN_Q_END + KV_LORA_RANK
IN_ROPE_END = IN_KV_END + QK_ROPE_DIM
IN_PROJ_WIDTH = IN_ROPE_END + FOURIER_WIDTH
N_EXPERTS = 32
TOP_K = 4
D_FF_EXPERT = D_MODEL // 2
SWIGLU_LIMIT = 7.0
SWIGLU_ALPHA = 1.702
MOE_BLOCK = 128
DEEPNORM_ALPHA = (2.0 * DEPTH) ** 0.25
DEEPNORM_BETA = (8.0 * DEPTH) ** -0.25
LN_EPS = 1e-5
RMS_EPS = 1e-6

kernel_name = "hybrid_mla_fnet_moe_deepnorm_encoder"


def layer_norm(x, g, b):
    xf = x.astype(jnp.float32)
    mu = jnp.mean(xf, axis=-1, keepdims=True)
    xc = xf - mu
    var = jnp.mean(xc * xc, axis=-1, keepdims=True)
    y = xc * lax.rsqrt(var + LN_EPS) * g.astype(jnp.float32) + b.astype(jnp.float32)
    return y.astype(x.dtype)


def rms_norm(x, g):
    xf = x.astype(jnp.float32)
    y = xf * lax.rsqrt(jnp.mean(xf * xf, axis=-1, keepdims=True) + RMS_EPS) * g.astype(jnp.float32)
    return y.astype(x.dtype)


def rope_cos_sin(positions):
    inv_freq = ROPE_THETA ** (-jnp.arange(0, QK_ROPE_DIM, 2, dtype=jnp.float32) / QK_ROPE_DIM)
    ang = positions.astype(jnp.float32)[..., None] * inv_freq
    return jnp.cos(ang), jnp.sin(ang)


def apply_rope(t, cos, sin):
    half = QK_ROPE_DIM // 2
    tf = t.astype(jnp.float32)
    t1, t2 = tf[..., :half], tf[..., half:]
    return jnp.concatenate([t1 * cos - t2 * sin, t2 * cos + t1 * sin], axis=-1).astype(t.dtype)


def mla_attention(q_nope, q_pe, k_nope, k_pe, v):
    B, S, H, _ = q_nope.shape
    n_blk = S // Q_BLOCK
    scale = (QK_NOPE_DIM + QK_ROPE_DIM) ** -0.5

    def to_blocks(t):
        return jnp.moveaxis(t.reshape(B, n_blk, Q_BLOCK, *t.shape[2:]), 1, 0)

    def block(args):
        qn, qr = args
        s = (jnp.einsum('bqhd,bkhd->bhqk', qn, k_nope)
             + jnp.einsum('bqhr,bkr->bhqk', qr, k_pe))
        p = jax.nn.softmax(s.astype(jnp.float32) * scale, axis=-1)
        return jnp.einsum('bhqk,bkhd->bqhd', p.astype(v.dtype), v)

    o = lax.map(block, (to_blocks(q_nope), to_blocks(q_pe)))
    return jnp.moveaxis(o, 0, 1).reshape(B, S, H * V_HEAD_DIM)


def fourier_mix(u_f):
    B, S, _ = u_f.shape
    g = u_f.reshape(B, S, FOURIER_GROUPS, FOURIER_GROUP_DIM).transpose(0, 2, 1, 3).astype(jnp.float32)
    y = jnp.fft.fft2(g, norm='ortho').real
    return y.transpose(0, 2, 1, 3).reshape(B, S, FOURIER_WIDTH).astype(u_f.dtype)


def moe_ffn(x2d, router_w, router_b, w_gate, b_gate, w_up, b_up, w_down, b_down):
    T, D = x2d.shape
    M = T * TOP_K
    logits = jnp.dot(x2d, router_w).astype(jnp.float32) + router_b.astype(jnp.float32)
    top_vals, top_idx = lax.top_k(logits, TOP_K)
    gates = jax.nn.softmax(top_vals, axis=-1)
    flat_e = top_idx.reshape(M).astype(jnp.int32)
    flat_g = gates.reshape(M)
    flat_tok = jnp.arange(M, dtype=jnp.int32) // TOP_K
    order = jnp.argsort(flat_e)
    sorted_e = flat_e[order]
    counts = jnp.bincount(flat_e, length=N_EXPERTS).astype(jnp.int32)
    starts = jnp.cumsum(counts) - counts
    padded = (counts + MOE_BLOCK - 1) // MOE_BLOCK * MOE_BLOCK
    padded_end = jnp.cumsum(padded)
    padded_start = padded_end - padded
    dest = padded_start[sorted_e] + jnp.arange(M, dtype=jnp.int32) - starts[sorted_e]
    P = M + N_EXPERTS * MOE_BLOCK
    n_blocks = P // MOE_BLOCK
    slot_tok = jnp.zeros((P,), jnp.int32).at[dest].set(flat_tok[order])
    slot_gate = jnp.zeros((P,), jnp.float32).at[dest].set(flat_g[order])
    block_start = jnp.arange(n_blocks, dtype=jnp.int32) * MOE_BLOCK
    block_e = jnp.minimum(jnp.searchsorted(padded_end, block_start, side='right'),
                          N_EXPERTS - 1).astype(jnp.int32)

    def expert_block(args):
        tok, g, e = args
        xb = x2d[tok]
        hg = jnp.minimum(jnp.dot(xb, w_gate[e]) + b_gate[e], SWIGLU_LIMIT)
        hu = jnp.clip(jnp.dot(xb, w_up[e]) + b_up[e], -SWIGLU_LIMIT, SWIGLU_LIMIT)
        h = (hu + 1.0) * (hg * jax.nn.sigmoid(SWIGLU_ALPHA * hg))
        out = jnp.dot(h, w_down[e]) + b_down[e]
        return out * g[:, None].astype(out.dtype)

    y_slots = lax.map(expert_block, (slot_tok.reshape(n_blocks, MOE_BLOCK),
                                     slot_gate.reshape(n_blocks, MOE_BLOCK), block_e))
    return jnp.zeros_like(x2d).at[slot_tok].add(y_slots.reshape(P, D))


def hybrid_layer(x, cos, sin, w_in, q_a_norm_g, w_q_b, kv_a_norm_g, w_kv_b,
                 mla_out_norm_g, fourier_out_norm_g, w_o, ln1_g, ln1_b,
                 router_w, router_b, w_gate, b_gate, w_up, b_up, w_down, b_down, ln2_g, ln2_b):
    B, S, D = x.shape
    u = jnp.dot(x, w_in)
    c_q = u[..., :IN_Q_END]
    c_kv = u[..., IN_Q_END:IN_KV_END]
    k_pe = u[..., IN_KV_END:IN_ROPE_END]
    u_f = u[..., IN_ROPE_END:]

    q = jnp.dot(rms_norm(c_q, q_a_norm_g), w_q_b).reshape(B, S, MLA_HEADS, QK_NOPE_DIM + QK_ROPE_DIM)
    q_nope = q[..., :QK_NOPE_DIM]
    q_pe = apply_rope(q[..., QK_NOPE_DIM:], cos[:, :, None, :], sin[:, :, None, :])
    kv = jnp.dot(rms_norm(c_kv, kv_a_norm_g), w_kv_b).reshape(B, S, MLA_HEADS, QK_NOPE_DIM + V_HEAD_DIM)
    k_nope = kv[..., :QK_NOPE_DIM]
    v = kv[..., QK_NOPE_DIM:]
    k_pe = apply_rope(k_pe, cos, sin)
    y_mla = mla_attention(q_nope, q_pe, k_nope, k_pe, v)

    y_f = fourier_mix(u_f)

    mix = jnp.concatenate([rms_norm(y_mla, mla_out_norm_g), rms_norm(y_f, fourier_out_norm_g)], axis=-1)
    x = layer_norm(DEEPNORM_ALPHA * x + jnp.dot(mix, w_o), ln1_g, ln1_b)

    y = moe_ffn(x.reshape(B * S, D), router_w, router_b, w_gate, b_gate, w_up, b_up, w_down, b_down)
    return layer_norm(DEEPNORM_ALPHA * x + y.reshape(B, S, D), ln2_g, ln2_b)


def setup_inputs(seed: int = 0) -> dict:
    key = jax.random.key(seed)
    ks = jax.random.split(key, 26)
    L = DEPTH
    f32 = jnp.float32

    def dense(k, shape, fan_in, gain=1.0):
        return jax.random.normal(k, shape, f32) * (gain * fan_in ** -0.5)

    def gain_vec(k, shape):
        return 1.0 + 0.02 * jax.random.normal(k, shape, f32)

    def small(k, shape, s=0.02):
        return s * jax.random.normal(k, shape, f32)

    x = jax.random.normal(ks[0], (BATCH, SEQ, D_MODEL), f32)
    positions = (jnp.arange(SEQ, dtype=jnp.int32)[None, :]
                 + jax.random.randint(ks[1], (BATCH, 1), 0, 4096, dtype=jnp.int32))
    return {
        "x": x,
        "positions": positions,
        "ln_in_g": gain_vec(ks[2], (D_MODEL,)),
        "ln_in_b": small(ks[3], (D_MODEL,)),
        "w_in": dense(ks[4], (L, D_MODEL, IN_PROJ_WIDTH), D_MODEL),
        "q_a_norm_g": gain_vec(ks[5], (L, Q_LORA_RANK)),
        "w_q_b": dense(ks[6], (L, Q_LORA_RANK, MLA_HEADS * (QK_NOPE_DIM + QK_ROPE_DIM)), Q_LORA_RANK),
        "kv_a_norm_g": gain_vec(ks[7], (L, KV_LORA_RANK)),
        "w_kv_b": dense(ks[8], (L, KV_LORA_RANK, MLA_HEADS * (QK_NOPE_DIM + V_HEAD_DIM)), KV_LORA_RANK),
        "mla_out_norm_g": gain_vec(ks[9], (L, MLA_WIDTH)),
        "fourier_out_norm_g": gain_vec(ks[10], (L, FOURIER_WIDTH)),
        "w_o": dense(ks[11], (L, MLA_WIDTH + FOURIER_WIDTH, D_MODEL), MLA_WIDTH + FOURIER_WIDTH, DEEPNORM_BETA),
        "ln1_g": gain_vec(ks[12], (L, D_MODEL)),
        "ln1_b": small(ks[13], (L, D_MODEL)),
        "router_w": dense(ks[14], (L, D_MODEL, N_EXPERTS), D_MODEL),
        "router_b": small(ks[15], (L, N_EXPERTS), 0.01),
        "w_gate": dense(ks[16], (L, N_EXPERTS, D_MODEL, D_FF_EXPERT), D_MODEL),
        "b_gate": small(ks[17], (L, N_EXPERTS, D_FF_EXPERT)),
        "w_up": dense(ks[18], (L, N_EXPERTS, D_MODEL, D_FF_EXPERT), D_MODEL),
        "b_up": small(ks[19], (L, N_EXPERTS, D_FF_EXPERT)),
        "w_down": dense(ks[20], (L, N_EXPERTS, D_FF_EXPERT, D_MODEL), D_FF_EXPERT, DEEPNORM_BETA),
        "b_down": small(ks[21], (L, N_EXPERTS, D_MODEL)),
        "ln2_g": gain_vec(ks[22], (L, D_MODEL)),
        "ln2_b": small(ks[23], (L, D_MODEL)),
    }


def reference(x, positions, ln_in_g, ln_in_b, w_in, q_a_norm_g, w_q_b, kv_a_norm_g, w_kv_b,
              mla_out_norm_g, fourier_out_norm_g, w_o, ln1_g, ln1_b, router_w, router_b,
              w_gate, b_gate, w_up, b_up, w_down, b_down, ln2_g, ln2_b):
    cos, sin = rope_cos_sin(positions)
    h = layer_norm(x, ln_in_g, ln_in_b)
    for l in range(DEPTH):
        h = hybrid_layer(h, cos, sin, w_in[l], q_a_norm_g[l], w_q_b[l], kv_a_norm_g[l], w_kv_b[l],
                         mla_out_norm_g[l], fourier_out_norm_g[l], w_o[l], ln1_g[l], ln1_b[l],
                         router_w[l], router_b[l], w_gate[l], b_gate[l], w_up[l], b_up[l],
                         w_down[l], b_down[l], ln2_g[l], ln2_b[l])
    return h
```

```python
import functools
import math

import jax
import jax.numpy as jnp
from jax import lax
from jax.experimental import pallas as pl
from jax.experimental.pallas import tpu as pltpu

F32 = jnp.float32
BF16 = jnp.bfloat16

V_HEAD_DIM = 128
QK_NOPE_DIM = 128
QK_ROPE_DIM = 64
QK_PAD_DIM = 256
ROPE_THETA = 10000.0
FOURIER_GROUP_DIM = 128
TOP_K = 4
SWIGLU_LIMIT = 7.0
SWIGLU_ALPHA = 1.702
LN_EPS = 1e-5
RMS_EPS = 1e-6
LOG2E = 1.4426950408889634

LANES = 128
V7X_VMEM_BYTES = 64 * 1024 * 1024
HEADS_PER_TILE = 4
DFT_ROW_SPLIT = 64

MOE_CHUNK = 1280
MOE_SUB = 256


def _cparams(semantics, vmem_mb):
    return pltpu.CompilerParams(dimension_semantics=semantics,
                                vmem_limit_bytes=min(vmem_mb * 1024 * 1024, V7X_VMEM_BYTES - (4 << 20)))


def _tile(dim, pref):
    t = min(dim, pref)
    while dim % t:
        t //= 2
    return t


def _ln_in_kernel(x_ref, g_ref, b_ref, hb_ref, mu_ref, rs_ref):
    x = x_ref[...]
    mu = jnp.mean(x, axis=-1, keepdims=True)
    xc = x - mu
    var = jnp.mean(xc * xc, axis=-1, keepdims=True)
    rs = lax.rsqrt(var + LN_EPS)
    hb_ref[...] = (xc * rs * g_ref[...] + b_ref[...]).astype(BF16)
    mu_ref[...] = mu
    rs_ref[...] = rs


def _ln_in(x2, g, b):
    T, D = x2.shape
    tm = _tile(T, 256)
    return pl.pallas_call(
        _ln_in_kernel,
        grid=(T // tm,),
        in_specs=[pl.BlockSpec((tm, D), lambda i: (i, 0)),
                  pl.BlockSpec((1, D), lambda i: (0, 0)),
                  pl.BlockSpec((1, D), lambda i: (0, 0))],
        out_specs=[pl.BlockSpec((tm, D), lambda i: (i, 0)),
                   pl.BlockSpec((tm, 1), lambda i: (i, 0)),
                   pl.BlockSpec((tm, 1), lambda i: (i, 0))],
        out_shape=[jax.ShapeDtypeStruct((T, D), BF16),
                   jax.ShapeDtypeStruct((T, 1), F32),
                   jax.ShapeDtypeStruct((T, 1), F32)],
        compiler_params=_cparams(("parallel",), 40),
        name="ln_in",
    )(x2, g.reshape(1, D), b.reshape(1, D))


def _rope128(p, cos4, sin4):
    lane = lax.broadcasted_iota(jnp.int32, p.shape, 1)
    first_half = (lane % QK_ROPE_DIM) < (QK_ROPE_DIM // 2)
    rot = jnp.where(first_half, -pltpu.roll(p, LANES - QK_ROPE_DIM // 2, 1),
                    pltpu.roll(p, QK_ROPE_DIM // 2, 1))
    return p * cos4 + rot * sin4


def _inproj_a_kernel(h_ref, w_ref, gq_ref, gkv_ref, cos_ref, sin_ref,
                     cq_ref, ckv_ref, kpe_ref, *, qr, kvr):
    acc = jnp.dot(h_ref[...], w_ref[...], preferred_element_type=F32)
    cq = acc[:, :qr]
    cq_ref[...] = (cq * lax.rsqrt(jnp.mean(cq * cq, axis=-1, keepdims=True) + RMS_EPS)
                   * gq_ref[...]).astype(BF16)
    ckv = acc[:, qr:qr + kvr]
    ckv_ref[...] = (ckv * lax.rsqrt(jnp.mean(ckv * ckv, axis=-1, keepdims=True) + RMS_EPS)
                    * gkv_ref[...]).astype(BF16)
    roped = _rope128(acc[:, qr + kvr:qr + kvr + LANES], cos_ref[...], sin_ref[...])
    lane = lax.broadcasted_iota(jnp.int32, roped.shape, 1)
    even = jnp.where(lane < QK_ROPE_DIM, roped, 0.0)
    kpe_ref[:, :LANES] = even.astype(BF16)
    kpe_ref[:, LANES:] = pltpu.roll(even, QK_ROPE_DIM, 1).astype(BF16)


def _inproj_a(hb, w_a, gq, gkv, cos4, sin4):
    T, D = hb.shape
    qr, kvr = gq.shape[0], gkv.shape[0]
    wa = w_a.shape[1]
    tm = _tile(T, 512)
    return pl.pallas_call(
        functools.partial(_inproj_a_kernel, qr=qr, kvr=kvr),
        grid=(T // tm,),
        in_specs=[pl.BlockSpec((tm, D), lambda i: (i, 0)),
                  pl.BlockSpec((D, wa), lambda i: (0, 0)),
                  pl.BlockSpec((1, qr), lambda i: (0, 0)),
                  pl.BlockSpec((1, kvr), lambda i: (0, 0)),
                  pl.BlockSpec((tm, LANES), lambda i: (i, 0)),
                  pl.BlockSpec((tm, LANES), lambda i: (i, 0))],
        out_specs=[pl.BlockSpec((tm, qr), lambda i: (i, 0)),
                   pl.BlockSpec((tm, kvr), lambda i: (i, 0)),
                   pl.BlockSpec((tm, 2 * LANES), lambda i: (i, 0))],
        out_shape=[jax.ShapeDtypeStruct((T, qr), BF16),
                   jax.ShapeDtypeStruct((T, kvr), BF16),
                   jax.ShapeDtypeStruct((T, 2 * LANES), BF16)],
        compiler_params=_cparams(("parallel",), 56),
        name="inproj_mla",
    )(hb, w_a, gq.reshape(1, qr), gkv.reshape(1, kvr), cos4, sin4)


def _q_up_kernel(c_ref, w_ref, cos_ref, sin_ref, q_ref, *, qscale):
    acc = jnp.dot(c_ref[...], w_ref[...], preferred_element_type=F32)
    nope_w = HEADS_PER_TILE * QK_NOPE_DIM
    lane = lax.broadcasted_iota(jnp.int32, (acc.shape[0], LANES), 1)
    for pair in range(HEADS_PER_TILE // 2):
        roped = _rope128(acc[:, nope_w + pair * LANES:nope_w + (pair + 1) * LANES],
                         cos_ref[...], sin_ref[...]) * qscale
        for par in range(2):
            j = 2 * pair + par
            keep = (lane < QK_ROPE_DIM) if par == 0 else (lane >= QK_ROPE_DIM)
            base = j * QK_PAD_DIM
            q_ref[:, base:base + QK_NOPE_DIM] = (
                acc[:, j * QK_NOPE_DIM:(j + 1) * QK_NOPE_DIM] * qscale).astype(BF16)
            q_ref[:, base + QK_NOPE_DIM:base + QK_PAD_DIM] = jnp.where(keep, roped, 0.0).astype(BF16)


def _q_up(cq, wq_perm, cos4, sin4, qscale):
    T, qr = cq.shape
    n_tiles = wq_perm.shape[1] // (HEADS_PER_TILE * (QK_NOPE_DIM + QK_ROPE_DIM))
    tw = HEADS_PER_TILE * (QK_NOPE_DIM + QK_ROPE_DIM)
    to = HEADS_PER_TILE * QK_PAD_DIM
    tm = _tile(T, 1024)
    return pl.pallas_call(
        functools.partial(_q_up_kernel, qscale=qscale),
        grid=(T // tm, n_tiles),
        in_specs=[pl.BlockSpec((tm, qr), lambda i, j: (i, 0)),
                  pl.BlockSpec((qr, tw), lambda i, j: (0, j)),
                  pl.BlockSpec((tm, LANES), lambda i, j: (i, 0)),
                  pl.BlockSpec((tm, LANES), lambda i, j: (i, 0))],
        out_specs=pl.BlockSpec((tm, to), lambda i, j: (i, j)),
        out_shape=jax.ShapeDtypeStruct((T, n_tiles * to), BF16),
        compiler_params=_cparams(("parallel", "arbitrary"), 40),
        name="q_up",
    )(cq, wq_perm, cos4, sin4)


def _kv_up_kernel(c_ref, w_ref, kpe_ref, k_ref, v_ref):
    acc = jnp.dot(c_ref[...], w_ref[...], preferred_element_type=F32)
    for j in range(HEADS_PER_TILE):
        src = j * (QK_NOPE_DIM + V_HEAD_DIM)
        k_ref[:, j * QK_PAD_DIM:j * QK_PAD_DIM + QK_NOPE_DIM] = acc[:, src:src + QK_NOPE_DIM].astype(BF16)
        par = j % 2
        k_ref[:, j * QK_PAD_DIM + QK_NOPE_DIM:(j + 1) * QK_PAD_DIM] = kpe_ref[:, par * LANES:(par + 1) * LANES]
        v_ref[:, j * V_HEAD_DIM:(j + 1) * V_HEAD_DIM] = (
            acc[:, src + QK_NOPE_DIM:src + QK_NOPE_DIM + V_HEAD_DIM].astype(BF16))


def _kv_up(ckv, w_kv, kpe2):
    T, kvr = ckv.shape
    tw = HEADS_PER_TILE * (QK_NOPE_DIM + V_HEAD_DIM)
    n_tiles = w_kv.shape[1] // tw
    tm = _tile(T, 1024)
    return pl.pallas_call(
        _kv_up_kernel,
        grid=(T // tm, n_tiles),
        in_specs=[pl.BlockSpec((tm, kvr), lambda i, j: (i, 0)),
                  pl.BlockSpec((kvr, tw), lambda i, j: (0, j)),
                  pl.BlockSpec((tm, 2 * LANES), lambda i, j: (i, 0))],
        out_specs=[pl.BlockSpec((tm, HEADS_PER_TILE * QK_PAD_DIM), lambda i, j: (i, j)),
                   pl.BlockSpec((tm, HEADS_PER_TILE * V_HEAD_DIM), lambda i, j: (i, j))],
        out_shape=[jax.ShapeDtypeStruct((T, n_tiles * HEADS_PER_TILE * QK_PAD_DIM), BF16),
                   jax.ShapeDtypeStruct((T, n_tiles * HEADS_PER_TILE * V_HEAD_DIM), BF16)],
        compiler_params=_cparams(("parallel", "arbitrary"), 40),
        name="kv_up",
    )(ckv, w_kv, kpe2)


def _attn_kernel(q_ref, k_ref, v_ref, o_ref, m_s, l_s, acc_s, *, tkv):
    n_kv = k_ref.shape[0] // tkv
    m_s[...] = jnp.full(m_s.shape, -jnp.inf, F32)
    l_s[...] = jnp.zeros(l_s.shape, F32)
    acc_s[...] = jnp.zeros(acc_s.shape, F32)
    q = q_ref[...]

    def body(i, carry):
        off = pl.multiple_of(i * tkv, tkv)
        k = k_ref[pl.ds(off, tkv), :]
        v = v_ref[pl.ds(off, tkv), :]
        s = lax.dot_general(q, k, (((1,), (1,)), ((), ())), preferred_element_type=F32)
        m_prev = m_s[...]
        m_new = jnp.maximum(m_prev, jnp.max(s, axis=1, keepdims=True))
        a = jnp.exp2(m_prev - m_new)
        p = jnp.exp2(s - m_new)
        l_s[...] = a * l_s[...] + jnp.sum(p, axis=1, keepdims=True)
        acc_s[...] = a * acc_s[...] + jnp.dot(p.astype(BF16), v, preferred_element_type=F32)
        m_s[...] = m_new
        return carry

    lax.fori_loop(0, n_kv, body, 0)
    o_ref[...] = (acc_s[...] / l_s[...]).astype(o_ref.dtype)


def _attention(q, k, v, B, S, H):
    T = B * S
    tq = _tile(S, 512)
    tkv = _tile(S, 512)
    nq = S // tq
    return pl.pallas_call(
        functools.partial(_attn_kernel, tkv=tkv),
        grid=(B, H, nq),
        in_specs=[pl.BlockSpec((tq, QK_PAD_DIM), lambda b, h, i: (b * nq + i, h)),
                  pl.BlockSpec((S, QK_PAD_DIM), lambda b, h, i: (b, h)),
                  pl.BlockSpec((S, V_HEAD_DIM), lambda b, h, i: (b, h))],
        out_specs=pl.BlockSpec((tq, V_HEAD_DIM), lambda b, h, i: (b * nq + i, h)),
        out_shape=jax.ShapeDtypeStruct((T, H * V_HEAD_DIM), BF16),
        scratch_shapes=[pltpu.VMEM((tq, 1), F32), pltpu.VMEM((tq, 1), F32),
                        pltpu.VMEM((tq, V_HEAD_DIM), F32)],
        compiler_params=_cparams(("parallel", "parallel", "arbitrary"), 40),
        name="mla_attention",
    )(q, k, v)


def _inproj_f_kernel(h_ref, w_ref, cs_ref, a_ref, b_ref, *, groups):
    acc = jnp.dot(h_ref[...], w_ref[...], preferred_element_type=F32)
    C = FOURIER_GROUP_DIM
    for g in range(groups):
        ab = jnp.dot(acc[:, g * C:(g + 1) * C].astype(BF16), cs_ref[...], preferred_element_type=F32)
        a_ref[:, g * C:(g + 1) * C] = ab[:, :C].astype(BF16)
        b_ref[:, g * C:(g + 1) * C] = ab[:, C:].astype(BF16)


def _inproj_f(hb, w_f, cs_tab):
    T, D = hb.shape
    fw = w_f.shape[1]
    tm = _tile(T, 1024)
    tn = _tile(fw, 512)
    return pl.pallas_call(
        functools.partial(_inproj_f_kernel, groups=tn // FOURIER_GROUP_DIM),
        grid=(T // tm, fw // tn),
        in_specs=[pl.BlockSpec((tm, D), lambda i, j: (i, 0)),
                  pl.BlockSpec((D, tn), lambda i, j: (0, j)),
                  pl.BlockSpec((FOURIER_GROUP_DIM, 2 * FOURIER_GROUP_DIM), lambda i, j: (0, 0))],
        out_specs=[pl.BlockSpec((tm, tn), lambda i, j: (i, j)),
                   pl.BlockSpec((tm, tn), lambda i, j: (i, j))],
        out_shape=[jax.ShapeDtypeStruct((T, fw), BF16), jax.ShapeDtypeStruct((T, fw), BF16)],
        compiler_params=_cparams(("parallel", "arbitrary"), 48),
        name="inproj_fourier",
    )(hb, w_f, cs_tab)


def _dft_gen_kernel(tac_ref, tas_ref, tbc_ref, tbs_ref, cs_ref, sn_ref):
    tbc = tbc_ref[...]
    tbs = tbs_ref[...]
    for aa in range(tac_ref.shape[0]):
        ca = tac_ref[aa:aa + 1, :]
        sa = tas_ref[aa:aa + 1, :]
        rows = slice(aa * DFT_ROW_SPLIT, (aa + 1) * DFT_ROW_SPLIT)
        cs_ref[rows, :] = (ca * tbc - sa * tbs).astype(BF16)
        sn_ref[rows, :] = (-(sa * tbc + ca * tbs)).astype(BF16)


def _dft_matrices(S, n_chan):
    na = S // DFT_ROW_SPLIT
    scale = 1.0 / math.sqrt(S * n_chan)
    col = jnp.arange(S, dtype=jnp.int32)[None, :]
    ang_a = (2.0 * math.pi / na) * ((jnp.arange(na, dtype=jnp.int32)[:, None] * col) % na).astype(F32)
    ang_b = (2.0 * math.pi / S) * ((jnp.arange(DFT_ROW_SPLIT, dtype=jnp.int32)[:, None] * col) % S).astype(F32)
    tac, tas = jnp.cos(ang_a), jnp.sin(ang_a)
    tbc, tbs = scale * jnp.cos(ang_b), scale * jnp.sin(ang_b)
    ta = 8
    tc = _tile(S, 2048)
    return pl.pallas_call(
        _dft_gen_kernel,
        grid=(na // ta, S // tc),
        in_specs=[pl.BlockSpec((ta, tc), lambda i, j: (i, j)),
                  pl.BlockSpec((ta, tc), lambda i, j: (i, j)),
                  pl.BlockSpec((DFT_ROW_SPLIT, tc), lambda i, j: (0, j)),
                  pl.BlockSpec((DFT_ROW_SPLIT, tc), lambda i, j: (0, j))],
        out_specs=[pl.BlockSpec((ta * DFT_ROW_SPLIT, tc), lambda i, j: (i, j)),
                   pl.BlockSpec((ta * DFT_ROW_SPLIT, tc), lambda i, j: (i, j))],
        out_shape=[jax.ShapeDtypeStruct((S, S), BF16), jax.ShapeDtypeStruct((S, S), BF16)],
        compiler_params=_cparams(("parallel", "parallel"), 40),
        name="dft_matrices",
    )(tac, tas, tbc, tbs)


def _seq_dft_kernel(cs_ref, sn_ref, a_ref, b_ref, y_ref, acc_s):
    kk = pl.program_id(3)

    @pl.when(kk == 0)
    def _():
        acc_s[...] = jnp.zeros(acc_s.shape, F32)

    acc_s[...] += (jnp.dot(cs_ref[...], a_ref[...], preferred_element_type=F32)
                   + jnp.dot(sn_ref[...], b_ref[...], preferred_element_type=F32))

    @pl.when(kk == pl.num_programs(3) - 1)
    def _():
        y_ref[...] = acc_s[...].astype(y_ref.dtype)


def _seq_dft(cs, sn, a, b, B, S):
    fw = a.shape[1]
    tm = _tile(S, 1024)
    tn = _tile(fw, 1024)
    tk = _tile(S, 1024)
    nm, nk = S // tm, S // tk
    return pl.pallas_call(
        _seq_dft_kernel,
        grid=(B, nm, fw // tn, nk),
        in_specs=[pl.BlockSpec((tm, tk), lambda bb, i, j, k: (i, k)),
                  pl.BlockSpec((tm, tk), lambda bb, i, j, k: (i, k)),
                  pl.BlockSpec((tk, tn), lambda bb, i, j, k: (bb * nk + k, j)),
                  pl.BlockSpec((tk, tn), lambda bb, i, j, k: (bb * nk + k, j))],
        out_specs=pl.BlockSpec((tm, tn), lambda bb, i, j, k: (bb * nm + i, j)),
        out_shape=jax.ShapeDtypeStruct((B * S, fw), BF16),
        scratch_shapes=[pltpu.VMEM((tm, tn), F32)],
        compiler_params=_cparams(("parallel", "parallel", "parallel", "arbitrary"), 48),
        name="seq_dft",
    )(cs, sn, a, b)


def _wo_kernel(ym_ref, yf_ref, gm_ref, gf_ref, w_ref, x_ref, mu_ref, rs_ref, lg_ref, lb_ref,
               pre_ref, mix_s, *, alpha, mw):
    @pl.when(pl.program_id(1) == 0)
    def _():
        ym = ym_ref[...].astype(F32)
        mix_s[:, :mw] = (ym * lax.rsqrt(jnp.mean(ym * ym, axis=-1, keepdims=True) + RMS_EPS)
                         * gm_ref[...]).astype(BF16)
        yf = yf_ref[...].astype(F32)
        mix_s[:, mw:] = (yf * lax.rsqrt(jnp.mean(yf * yf, axis=-1, keepdims=True) + RMS_EPS)
                         * gf_ref[...]).astype(BF16)

    h = (x_ref[...] - mu_ref[...]) * rs_ref[...] * lg_ref[...] + lb_ref[...]
    pre_ref[...] = alpha * h + jnp.dot(mix_s[...], w_ref[...], preferred_element_type=F32)


def _wo(ym, yf, gm, gf, w_o, x2, mu, rs, ln_g, ln_b, alpha):
    T, mw = ym.shape
    fw = yf.shape[1]
    D = w_o.shape[1]
    tm = _tile(T, 1024)
    tn = _tile(D, 512)
    return pl.pallas_call(
        functools.partial(_wo_kernel, alpha=alpha, mw=mw),
        grid=(T // tm, D // tn),
        in_specs=[pl.BlockSpec((tm, mw), lambda i, j: (i, 0)),
                  pl.BlockSpec((tm, fw), lambda i, j: (i, 0)),
                  pl.BlockSpec((1, mw), lambda i, j: (0, 0)),
                  pl.BlockSpec((1, fw), lambda i, j: (0, 0)),
                  pl.BlockSpec((mw + fw, tn), lambda i, j: (0, j)),
                  pl.BlockSpec((tm, tn), lambda i, j: (i, j)),
                  pl.BlockSpec((tm, 1), lambda i, j: (i, 0)),
                  pl.BlockSpec((tm, 1), lambda i, j: (i, 0)),
                  pl.BlockSpec((1, tn), lambda i, j: (0, j)),
                  pl.BlockSpec((1, tn), lambda i, j: (0, j))],
        out_specs=pl.BlockSpec((tm, tn), lambda i, j: (i, j)),
        out_shape=jax.ShapeDtypeStruct((T, D), F32),
        scratch_shapes=[pltpu.VMEM((tm, mw + fw), BF16)],
        compiler_params=_cparams(("parallel", "arbitrary"), 56),
        name="w_o_residual",
    )(ym, yf, gm.reshape(1, mw), gf.reshape(1, fw), w_o, x2, mu, rs,
      ln_g.reshape(1, D), ln_b.reshape(1, D))


def _ln1_router_kernel(pre_ref, g_ref, b_ref, rw_ref, rb_ref, x1_ref, idx_ref, gate_ref):
    x = pre_ref[...]
    mu = jnp.mean(x, axis=-1, keepdims=True)
    xc = x - mu
    var = jnp.mean(xc * xc, axis=-1, keepdims=True)
    x1 = xc * lax.rsqrt(var + LN_EPS) * g_ref[...] + b_ref[...]
    x1_ref[...] = x1
    logits = lax.dot_general(rw_ref[...], x1, (((1,), (1,)), ((), ())),
                             precision=lax.Precision.HIGHEST, preferred_element_type=F32) + rb_ref[...]
    n_e = logits.shape[0]
    eidx = lax.broadcasted_iota(jnp.int32, logits.shape, 0)
    vals, idxs = [], []
    for _ in range(TOP_K):
        m = jnp.max(logits, axis=0, keepdims=True)
        sel = jnp.min(jnp.where(logits == m, eidx, n_e), axis=0, keepdims=True)
        logits = jnp.where(eidx == sel, -jnp.inf, logits)
        vals.append(m)
        idxs.append(sel)
    exps = [jnp.exp(v - vals[0]) for v in vals]
    denom = exps[0] + exps[1] + exps[2] + exps[3]
    for kk in range(TOP_K):
        idx_ref[kk:kk + 1, :] = idxs[kk]
        gate_ref[kk:kk + 1, :] = exps[kk] / denom


def _ln1_router(pre, g, b, router_w, router_b):
    T, D = pre.shape
    E = router_w.shape[1]
    tm = _tile(T, 256)
    return pl.pallas_call(
        _ln1_router_kernel,
        grid=(T // tm,),
        in_specs=[pl.BlockSpec((tm, D), lambda i: (i, 0)),
                  pl.BlockSpec((1, D), lambda i: (0, 0)),
                  pl.BlockSpec((1, D), lambda i: (0, 0)),
                  pl.BlockSpec((E, D), lambda i: (0, 0)),
                  pl.BlockSpec((E, 1), lambda i: (0, 0))],
        out_specs=[pl.BlockSpec((tm, D), lambda i: (i, 0)),
                   pl.BlockSpec((TOP_K, tm), lambda i: (0, i)),
                   pl.BlockSpec((TOP_K, tm), lambda i: (0, i))],
        out_shape=[jax.ShapeDtypeStruct((T, D), F32),
                   jax.ShapeDtypeStruct((TOP_K, T), jnp.int32),
                   jax.ShapeDtypeStruct((TOP_K, T), F32)],
        compiler_params=_cparams(("parallel",), 48),
        name="ln1_router",
    )(pre, g.reshape(1, D), b.reshape(1, D), router_w.T, router_b.reshape(E, 1))


def _row_copy(src_hbm, dst_ref, src_row, dst_row, sem):
    return pltpu.make_async_copy(src_hbm.at[pl.ds(src_row, 1)], dst_ref.at[pl.ds(dst_row, 1)], sem)


def _dispatch_kernel(tok_ref, nv_ref, x_hbm, o_ref, buf, sem, *, n_sub):
    c = pl.program_id(0)
    s = pl.program_id(1)
    rows = o_ref.shape[0]
    base = (c * n_sub + s) * rows
    active = s * rows < nv_ref[c]

    @pl.when(active)
    def _():
        def issue(r, carry):
            _row_copy(x_hbm, buf, tok_ref[base + r], r, sem).start()
            return carry
        lax.fori_loop(0, rows, issue, 0)
        pltpu.make_async_copy(x_hbm.at[pl.ds(0, rows)], buf, sem).wait()
        o_ref[...] = buf[...].astype(o_ref.dtype)

    @pl.when(jnp.logical_not(active))
    def _():
        o_ref[...] = jnp.zeros(o_ref.shape, o_ref.dtype)


def _dispatch(slot_tok, chunk_nv, x1, n_chunks):
    T, D = x1.shape
    n_sub = MOE_CHUNK // MOE_SUB
    grid_spec = pltpu.PrefetchScalarGridSpec(
        num_scalar_prefetch=2,
        grid=(n_chunks, n_sub),
        in_specs=[pl.BlockSpec(memory_space=pl.ANY)],
        out_specs=pl.BlockSpec((MOE_SUB, D), lambda c, s, tok, nv: (c * n_sub + s, 0)),
        scratch_shapes=[pltpu.VMEM((MOE_SUB, D), F32), pltpu.SemaphoreType.DMA(())],
    )
    return pl.pallas_call(
        functools.partial(_dispatch_kernel, n_sub=n_sub),
        grid_spec=grid_spec,
        out_shape=jax.ShapeDtypeStruct((n_chunks * MOE_CHUNK, D), BF16),
        compiler_params=_cparams(("arbitrary", "arbitrary"), 32),
        name="moe_dispatch",
    )(slot_tok, chunk_nv, x1)


def _expert_up_kernel(ce_ref, nv_ref, nu_ref, x_ref, wg_ref, wu_ref, bg_ref, bu_ref, h_ref,
                      wgb_s, wub_s):
    c = pl.program_id(0)
    nv = nv_ref[c]

    @pl.when(nv > 0)
    def _():
        wgb_s[...] = wg_ref[...].astype(BF16)
        wub_s[...] = wu_ref[...].astype(BF16)

    for s in range(MOE_CHUNK // MOE_SUB):
        rows = slice(s * MOE_SUB, (s + 1) * MOE_SUB)
        active = s * MOE_SUB < nv

        @pl.when(active)
        def _():
            xb = x_ref[rows, :]
            hg = jnp.minimum(jnp.dot(xb, wgb_s[...], preferred_element_type=F32) + bg_ref[...], SWIGLU_LIMIT)
            hu = jnp.clip(jnp.dot(xb, wub_s[...], preferred_element_type=F32) + bu_ref[...],
                          -SWIGLU_LIMIT, SWIGLU_LIMIT)
            act = (hu + 1.0) * (hg * jax.nn.sigmoid(SWIGLU_ALPHA * hg))
            h_ref[rows, :] = act.astype(h_ref.dtype)

        @pl.when(jnp.logical_not(active))
        def _():
            h_ref[rows, :] = jnp.zeros((MOE_SUB, h_ref.shape[1]), h_ref.dtype)


def _expert_up(chunk_e, chunk_nv, n_used, xs, w_gate, w_up, b_gate, b_up, n_chunks):
    E, D, F = w_gate.shape
    tf = _tile(F, 256)
    nf = F // tf

    def used(c, nu):
        return jnp.minimum(c, nu[0] - 1)

    def jeff(c, j, nu):
        return jnp.where(c < nu[0], j, nf - 1)

    grid_spec = pltpu.PrefetchScalarGridSpec(
        num_scalar_prefetch=3,
        grid=(n_chunks, nf),
        in_specs=[pl.BlockSpec((MOE_CHUNK, D), lambda c, j, ce, nv, nu: (used(c, nu), 0)),
                  pl.BlockSpec((None, D, tf), lambda c, j, ce, nv, nu: (ce[c], 0, jeff(c, j, nu))),
                  pl.BlockSpec((None, D, tf), lambda c, j, ce, nv, nu: (ce[c], 0, jeff(c, j, nu))),
                  pl.BlockSpec((None, 1, tf), lambda c, j, ce, nv, nu: (ce[c], 0, jeff(c, j, nu))),
                  pl.BlockSpec((None, 1, tf), lambda c, j, ce, nv, nu: (ce[c], 0, jeff(c, j, nu)))],
        out_specs=pl.BlockSpec((MOE_CHUNK, tf), lambda c, j, ce, nv, nu: (c, j)),
        scratch_shapes=[pltpu.VMEM((D, tf), BF16),
                        pltpu.VMEM((D, tf), BF16)],
    )
    return pl.pallas_call(
        _expert_up_kernel,
        grid_spec=grid_spec,
        out_shape=jax.ShapeDtypeStruct((n_chunks * MOE_CHUNK, F), BF16),
        compiler_params=_cparams(("arbitrary", "arbitrary"), 60),
        name="expert_gate_up",
    )(chunk_e, chunk_nv, n_used, xs, w_gate, w_up, b_gate.reshape(E, 1, F), b_up.reshape(E, 1, F))


def _expert_down_kernel(ce_ref, nv_ref, nu_ref, h_ref, wd_ref, bd_ref, gate_ref, y_ref, wdb_s):
    c = pl.program_id(0)
    nv = nv_ref[c]

    @pl.when(nv > 0)
    def _():
        wdb_s[...] = wd_ref[...].astype(BF16)

    for s in range(MOE_CHUNK // MOE_SUB):
        rows = slice(s * MOE_SUB, (s + 1) * MOE_SUB)
        active = s * MOE_SUB < nv

        @pl.when(active)
        def _():
            out = jnp.dot(h_ref[rows, :], wdb_s[...], preferred_element_type=F32) + bd_ref[...]
            y_ref[rows, :] = out * gate_ref[rows, :]

        @pl.when(jnp.logical_not(active))
        def _():
            y_ref[rows, :] = jnp.zeros((MOE_SUB, y_ref.shape[1]), y_ref.dtype)


def _expert_down(chunk_e, chunk_nv, n_used, hmid, w_down, b_down, slot_gate, n_chunks):
    E, F, D = w_down.shape
    tn = _tile(D, 512)
    nn = D // tn

    def used(c, nu):
        return jnp.minimum(c, nu[0] - 1)

    def jeff(c, j, nu):
        return jnp.where(c < nu[0], j, nn - 1)

    grid_spec = pltpu.PrefetchScalarGridSpec(
        num_scalar_prefetch=3,
        grid=(n_chunks, nn),
        in_specs=[pl.BlockSpec((MOE_CHUNK, F), lambda c, j, ce, nv, nu: (used(c, nu), 0)),
                  pl.BlockSpec((None, F, tn), lambda c, j, ce, nv, nu: (ce[c], 0, jeff(c, j, nu))),
                  pl.BlockSpec((None, 1, tn), lambda c, j, ce, nv, nu: (ce[c], 0, jeff(c, j, nu))),
                  pl.BlockSpec((MOE_CHUNK, 1), lambda c, j, ce, nv, nu: (used(c, nu), 0))],
        out_specs=pl.BlockSpec((MOE_CHUNK, tn), lambda c, j, ce, nv, nu: (c, j)),
        scratch_shapes=[pltpu.VMEM((F, tn), BF16)],
    )
    return pl.pallas_call(
        _expert_down_kernel,
        grid_spec=grid_spec,
        out_shape=jax.ShapeDtypeStruct((n_chunks * MOE_CHUNK, D), F32),
        compiler_params=_cparams(("arbitrary", "arbitrary"), 56),
        name="expert_down",
    )(chunk_e, chunk_nv, n_used, hmid, w_down, b_down.reshape(E, 1, D), slot_gate)


def _combine_kernel(dest_ref, y_hbm, x1_ref, g_ref, b_ref, o_ref, buf, sem, *, alpha):
    i = pl.program_id(0)
    tt = o_ref.shape[0]

    def issue(r, carry):
        for kk in range(TOP_K):
            _row_copy(y_hbm, buf.at[kk], dest_ref[(i * tt + r) * TOP_K + kk], r, sem).start()
        return carry
    lax.fori_loop(0, tt, issue, 0)
    for kk in range(TOP_K):
        pltpu.make_async_copy(y_hbm.at[pl.ds(0, tt)], buf.at[kk], sem).wait()
    y = buf[0] + buf[1] + buf[2] + buf[3]
    z = alpha * x1_ref[...] + y
    mu = jnp.mean(z, axis=-1, keepdims=True)
    zc = z - mu
    var = jnp.mean(zc * zc, axis=-1, keepdims=True)
    o_ref[...] = zc * lax.rsqrt(var + LN_EPS) * g_ref[...] + b_ref[...]


def _combine(dest, yslots, x1, g, b, alpha):
    T, D = x1.shape
    tt = _tile(T, 64)
    grid_spec = pltpu.PrefetchScalarGridSpec(
        num_scalar_prefetch=1,
        grid=(T // tt,),
        in_specs=[pl.BlockSpec(memory_space=pl.ANY),
                  pl.BlockSpec((tt, D), lambda i, d: (i, 0)),
                  pl.BlockSpec((1, D), lambda i, d: (0, 0)),
                  pl.BlockSpec((1, D), lambda i, d: (0, 0))],
        out_specs=pl.BlockSpec((tt, D), lambda i, d: (i, 0)),
        scratch_shapes=[pltpu.VMEM((TOP_K, tt, D), F32), pltpu.SemaphoreType.DMA(())],
    )
    return pl.pallas_call(
        functools.partial(_combine_kernel, alpha=alpha),
        grid_spec=grid_spec,
        out_shape=jax.ShapeDtypeStruct((T, D), F32),
        compiler_params=_cparams(("arbitrary",), 32),
        name="moe_combine_ln2",
    )(dest, yslots, x1, g.reshape(1, D), b.reshape(1, D))


def _routing_tables(top_idx, gates, n_experts, n_chunks):
    T = top_idx.shape[1]
    M = T * TOP_K
    flat_e = top_idx.T.reshape(M)
    flat_g = gates.T.reshape(M)
    onehot = (flat_e[:, None] == jnp.arange(n_experts, dtype=jnp.int32)[None, :]).astype(jnp.int32)
    csum = jnp.cumsum(onehot, axis=0)
    rank = jnp.sum(csum * onehot, axis=1) - 1
    counts = csum[-1]
    chunks_e = (counts + MOE_CHUNK - 1) // MOE_CHUNK
    chunk_end = jnp.cumsum(chunks_e)
    chunk_start = chunk_end - chunks_e
    n_used = chunk_end[-1]
    dest = chunk_start[flat_e] * MOE_CHUNK + rank
    P = n_chunks * MOE_CHUNK
    flat_tok = jnp.arange(M, dtype=jnp.int32) // TOP_K
    slot_tok = jnp.zeros((P,), jnp.int32).at[dest].set(flat_tok)
    slot_gate = jnp.zeros((P,), F32).at[dest].set(flat_g)
    cid = jnp.arange(n_chunks, dtype=jnp.int32)
    chunk_e = jnp.minimum(jnp.searchsorted(chunk_end, cid, side='right'), n_experts - 1).astype(jnp.int32)
    last_e = chunk_e[jnp.maximum(n_used - 1, 0)]
    chunk_e = jnp.where(cid < n_used, chunk_e, last_e)
    chunk_nv = jnp.where(cid < n_used,
                         jnp.clip(counts[chunk_e] - (cid - chunk_start[chunk_e]) * MOE_CHUNK, 0, MOE_CHUNK),
                         0).astype(jnp.int32)
    return (slot_tok, slot_gate.reshape(P, 1), dest.astype(jnp.int32), chunk_e, chunk_nv,
            n_used.astype(jnp.int32).reshape(1))


def kernel(x, positions, ln_in_g, ln_in_b, w_in, q_a_norm_g, w_q_b, kv_a_norm_g, w_kv_b, mla_out_norm_g, fourier_out_norm_g, w_o, ln1_g, ln1_b, router_w, router_b, w_gate, b_gate, w_up, b_up, w_down, b_down, ln2_g, ln2_b):
    B, S, D = x.shape
    T = B * S
    depth = w_in.shape[0]
    assert depth == 1, "single-layer trunk only"
    qr = q_a_norm_g.shape[1]
    kvr = kv_a_norm_g.shape[1]
    H = w_q_b.shape[2] // (QK_NOPE_DIM + QK_ROPE_DIM)
    fw = fourier_out_norm_g.shape[1]
    E = router_w.shape[2]
    assert H % HEADS_PER_TILE == 0 and (qr + kvr) % LANES == 0 and S % (8 * DFT_ROW_SPLIT) == 0
    alpha = (2.0 * depth) ** 0.25

    inv_freq = ROPE_THETA ** (-jnp.arange(0, QK_ROPE_DIM, 2, dtype=F32) / QK_ROPE_DIM)
    ang = positions.astype(F32)[..., None] * inv_freq
    cos4 = jnp.tile(jnp.cos(ang), (1, 1, 2 * LANES // QK_ROPE_DIM)).reshape(T, LANES)
    sin4 = jnp.tile(jnp.sin(ang), (1, 1, 2 * LANES // QK_ROPE_DIM)).reshape(T, LANES)

    rope_end = qr + kvr + QK_ROPE_DIM
    w_a = w_in[0, :, :qr + kvr + LANES].astype(BF16)
    w_f = w_in[0, :, rope_end:].astype(BF16)
    wq = w_q_b[0].reshape(qr, H // HEADS_PER_TILE, HEADS_PER_TILE, QK_NOPE_DIM + QK_ROPE_DIM)
    wq_perm = jnp.concatenate(
        [wq[..., :QK_NOPE_DIM].reshape(qr, H // HEADS_PER_TILE, HEADS_PER_TILE * QK_NOPE_DIM),
         wq[..., QK_NOPE_DIM:].reshape(qr, H // HEADS_PER_TILE, HEADS_PER_TILE * QK_ROPE_DIM)],
        axis=-1).reshape(qr, H * (QK_NOPE_DIM + QK_ROPE_DIM)).astype(BF16)
    w_kv = w_kv_b[0].astype(BF16)
    w_o_b = w_o[0].astype(BF16)
    ch = jnp.arange(FOURIER_GROUP_DIM, dtype=jnp.int32)
    ang_c = (2.0 * math.pi / FOURIER_GROUP_DIM) * ((ch[:, None] * ch[None, :]) % FOURIER_GROUP_DIM).astype(F32)
    cs_tab = jnp.concatenate([jnp.cos(ang_c), jnp.sin(ang_c)], axis=1).astype(BF16)

    x2 = x.reshape(T, D)
    hb, mu, rs = _ln_in(x2, ln_in_g, ln_in_b)

    cq, ckv, kpe2 = _inproj_a(hb, w_a, q_a_norm_g[0], kv_a_norm_g[0], cos4, sin4)
    qscale = (QK_NOPE_DIM + QK_ROPE_DIM) ** -0.5 * LOG2E
    q = _q_up(cq, wq_perm, cos4, sin4, qscale)
    k, v = _kv_up(ckv, w_kv, kpe2)
    y_mla = _attention(q, k, v, B, S, H)

    fa, fb = _inproj_f(hb, w_f, cs_tab)
    cs_mat, sn_mat = _dft_matrices(S, FOURIER_GROUP_DIM)
    y_f = _seq_dft(cs_mat, sn_mat, fa, fb, B, S)

    pre = _wo(y_mla, y_f, mla_out_norm_g[0], fourier_out_norm_g[0], w_o_b, x2, mu, rs,
              ln_in_g, ln_in_b, alpha)
    x1, top_idx, gates = _ln1_router(pre, ln1_g[0], ln1_b[0], router_w[0], router_b[0])

    n_chunks = -(-T * TOP_K // MOE_CHUNK) + E
    slot_tok, slot_gate, dest, chunk_e, chunk_nv, n_used = _routing_tables(top_idx, gates, E, n_chunks)
    xs = _dispatch(slot_tok, chunk_nv, x1, n_chunks)
    hmid = _expert_up(chunk_e, chunk_nv, n_used, xs, w_gate[0], w_up[0], b_gate[0], b_up[0], n_chunks)
    yslots = _expert_down(chunk_e, chunk_nv, n_used, hmid, w_down[0], b_down[0], slot_gate, n_chunks)
    out = _combine(dest, yslots, x1, ln2_g[0], ln2_b[0], alpha)
    return out.reshape(B, S, D)
```

```python
import functools
import math

import jax
import jax.numpy as jnp
from jax import lax
from jax.experimental import pallas as pl
from jax.experimental.pallas import tpu as pltpu

F32 = jnp.float32
BF16 = jnp.bfloat16

V_HEAD_DIM = 128
QK_NOPE_DIM = 128
QK_ROPE_DIM = 64
QK_PAD_DIM = 256
V_PAD_DIM = 256
ROPE_THETA = 10000.0
FOURIER_GROUP_DIM = 128
TOP_K = 4
SWIGLU_LIMIT = 7.0
SWIGLU_ALPHA = 1.702
LN_EPS = 1e-5
RMS_EPS = 1e-6
LOG2E = 1.4426950408889634

LANES = 128
V7X_VMEM_BYTES = 64 * 1024 * 1024
HEADS_PER_TILE = 4
DFT_ROW_SPLIT = 64
ATTN_TQ = 1024
ATTN_TKV = 1024

MOE_CHUNK = 1280
MOE_SUB = 256
DISPATCH_BATCH = 256
COMBINE_TOKENS = 64


def _cparams(semantics, vmem_mb):
    return pltpu.CompilerParams(dimension_semantics=semantics,
                                vmem_limit_bytes=min(vmem_mb * 1024 * 1024, V7X_VMEM_BYTES - (4 << 20)))


def _tile(dim, pref):
    t = min(dim, pref)
    while dim % t:
        t //= 2
    return t


def _ln_in_kernel(x_ref, g_ref, b_ref, hb_ref, mu_ref, rs_ref):
    x = x_ref[...]
    mu = jnp.mean(x, axis=-1, keepdims=True)
    xc = x - mu
    var = jnp.mean(xc * xc, axis=-1, keepdims=True)
    rs = lax.rsqrt(var + LN_EPS)
    hb_ref[...] = (xc * rs * g_ref[...] + b_ref[...]).astype(BF16)
    mu_ref[...] = mu
    rs_ref[...] = rs


def _ln_in(x2, g, b):
    T, D = x2.shape
    tm = _tile(T, 256)
    return pl.pallas_call(
        _ln_in_kernel,
        grid=(T // tm,),
        in_specs=[pl.BlockSpec((tm, D), lambda i: (i, 0)),
                  pl.BlockSpec((1, D), lambda i: (0, 0)),
                  pl.BlockSpec((1, D), lambda i: (0, 0))],
        out_specs=[pl.BlockSpec((tm, D), lambda i: (i, 0)),
                   pl.BlockSpec((tm, 1), lambda i: (i, 0)),
                   pl.BlockSpec((tm, 1), lambda i: (i, 0))],
        out_shape=[jax.ShapeDtypeStruct((T, D), BF16),
                   jax.ShapeDtypeStruct((T, 1), F32),
                   jax.ShapeDtypeStruct((T, 1), F32)],
        compiler_params=_cparams(("parallel",), 40),
        name="ln_in",
    )(x2, g.reshape(1, D), b.reshape(1, D))


def _rope128(p, cos4, sin4):
    lane = lax.broadcasted_iota(jnp.int32, p.shape, 1)
    first_half = (lane % QK_ROPE_DIM) < (QK_ROPE_DIM // 2)
    rot = jnp.where(first_half, -pltpu.roll(p, LANES - QK_ROPE_DIM // 2, 1),
                    pltpu.roll(p, QK_ROPE_DIM // 2, 1))
    return p * cos4 + rot * sin4


def _inproj_a_kernel(h_ref, w_ref, gq_ref, gkv_ref, cos_ref, sin_ref,
                     cq_ref, ckv_ref, kpe_ref, *, qr, kvr):
    acc = jnp.dot(h_ref[...], w_ref[...], preferred_element_type=F32)
    cq = acc[:, :qr]
    cq_ref[...] = (cq * lax.rsqrt(jnp.mean(cq * cq, axis=-1, keepdims=True) + RMS_EPS)
                   * gq_ref[...]).astype(BF16)
    ckv = acc[:, qr:qr + kvr]
    ckv_ref[...] = (ckv * lax.rsqrt(jnp.mean(ckv * ckv, axis=-1, keepdims=True) + RMS_EPS)
                    * gkv_ref[...]).astype(BF16)
    roped = _rope128(acc[:, qr + kvr:qr + kvr + LANES], cos_ref[...], sin_ref[...])
    lane = lax.broadcasted_iota(jnp.int32, roped.shape, 1)
    even = jnp.where(lane < QK_ROPE_DIM, roped, 0.0)
    kpe_ref[:, :LANES] = even.astype(BF16)
    kpe_ref[:, LANES:] = pltpu.roll(even, QK_ROPE_DIM, 1).astype(BF16)


def _inproj_a(hb, w_a, gq, gkv, cos4, sin4):
    T, D = hb.shape
    qr, kvr = gq.shape[0], gkv.shape[0]
    wa = w_a.shape[1]
    tm = _tile(T, 512)
    return pl.pallas_call(
        functools.partial(_inproj_a_kernel, qr=qr, kvr=kvr),
        grid=(T // tm,),
        in_specs=[pl.BlockSpec((tm, D), lambda i: (i, 0)),
                  pl.BlockSpec((D, wa), lambda i: (0, 0)),
                  pl.BlockSpec((1, qr), lambda i: (0, 0)),
                  pl.BlockSpec((1, kvr), lambda i: (0, 0)),
                  pl.BlockSpec((tm, LANES), lambda i: (i, 0)),
                  pl.BlockSpec((tm, LANES), lambda i: (i, 0))],
        out_specs=[pl.BlockSpec((tm, qr), lambda i: (i, 0)),
                   pl.BlockSpec((tm, kvr), lambda i: (i, 0)),
                   pl.BlockSpec((tm, 2 * LANES), lambda i: (i, 0))],
        out_shape=[jax.ShapeDtypeStruct((T, qr), BF16),
                   jax.ShapeDtypeStruct((T, kvr), BF16),
                   jax.ShapeDtypeStruct((T, 2 * LANES), BF16)],
        compiler_params=_cparams(("parallel",), 56),
        name="inproj_mla",
    )(hb, w_a, gq.reshape(1, qr), gkv.reshape(1, kvr), cos4, sin4)


def _q_up_kernel(c_ref, w_ref, cos_ref, sin_ref, q_ref, *, qscale):
    acc = jnp.dot(c_ref[...], w_ref[...], preferred_element_type=F32)
    nope_w = HEADS_PER_TILE * QK_NOPE_DIM
    lane = lax.broadcasted_iota(jnp.int32, (acc.shape[0], LANES), 1)
    for pair in range(HEADS_PER_TILE // 2):
        roped = _rope128(acc[:, nope_w + pair * LANES:nope_w + (pair + 1) * LANES],
                         cos_ref[...], sin_ref[...]) * qscale
        for par in range(2):
            j = 2 * pair + par
            keep = (lane < QK_ROPE_DIM) if par == 0 else (lane >= QK_ROPE_DIM)
            base = j * QK_PAD_DIM
            q_ref[:, base:base + QK_NOPE_DIM] = (
                acc[:, j * QK_NOPE_DIM:(j + 1) * QK_NOPE_DIM] * qscale).astype(BF16)
            q_ref[:, base + QK_NOPE_DIM:base + QK_PAD_DIM] = jnp.where(keep, roped, 0.0).astype(BF16)


def _q_up(cq, wq_perm, cos4, sin4, qscale):
    T, qr = cq.shape
    n_tiles = wq_perm.shape[1] // (HEADS_PER_TILE * (QK_NOPE_DIM + QK_ROPE_DIM))
    tw = HEADS_PER_TILE * (QK_NOPE_DIM + QK_ROPE_DIM)
    to = HEADS_PER_TILE * QK_PAD_DIM
    tm = _tile(T, 1024)
    return pl.pallas_call(
        functools.partial(_q_up_kernel, qscale=qscale),
        grid=(T // tm, n_tiles),
        in_specs=[pl.BlockSpec((tm, qr), lambda i, j: (i, 0)),
                  pl.BlockSpec((qr, tw), lambda i, j: (0, j)),
                  pl.BlockSpec((tm, LANES), lambda i, j: (i, 0)),
                  pl.BlockSpec((tm, LANES), lambda i, j: (i, 0))],
        out_specs=pl.BlockSpec((tm, to), lambda i, j: (i, j)),
        out_shape=jax.ShapeDtypeStruct((T, n_tiles * to), BF16),
        compiler_params=_cparams(("parallel", "arbitrary"), 40),
        name="q_up",
    )(cq, wq_perm, cos4, sin4)


def _kv_up_kernel(c_ref, w_ref, kpe_ref, k_ref, v_ref):
    acc = jnp.dot(c_ref[...], w_ref[...], preferred_element_type=F32)
    for j in range(HEADS_PER_TILE):
        src = j * (QK_NOPE_DIM + V_HEAD_DIM)
        k_ref[:, j * QK_PAD_DIM:j * QK_PAD_DIM + QK_NOPE_DIM] = acc[:, src:src + QK_NOPE_DIM].astype(BF16)
        par = j % 2
        k_ref[:, j * QK_PAD_DIM + QK_NOPE_DIM:(j + 1) * QK_PAD_DIM] = kpe_ref[:, par * LANES:(par + 1) * LANES]
        v_ref[:, j * V_PAD_DIM:j * V_PAD_DIM + V_HEAD_DIM] = (
            acc[:, src + QK_NOPE_DIM:src + QK_NOPE_DIM + V_HEAD_DIM].astype(BF16))
        v_ref[:, j * V_PAD_DIM + V_HEAD_DIM:(j + 1) * V_PAD_DIM] = jnp.ones(
            (acc.shape[0], V_PAD_DIM - V_HEAD_DIM), BF16)


def _kv_up(ckv, w_kv, kpe2):
    T, kvr = ckv.shape
    tw = HEADS_PER_TILE * (QK_NOPE_DIM + V_HEAD_DIM)
    n_tiles = w_kv.shape[1] // tw
    tm = _tile(T, 1024)
    return pl.pallas_call(
        _kv_up_kernel,
        grid=(T // tm, n_tiles),
        in_specs=[pl.BlockSpec((tm, kvr), lambda i, j: (i, 0)),
                  pl.BlockSpec((kvr, tw), lambda i, j: (0, j)),
                  pl.BlockSpec((tm, 2 * LANES), lambda i, j: (i, 0))],
        out_specs=[pl.BlockSpec((tm, HEADS_PER_TILE * QK_PAD_DIM), lambda i, j: (i, j)),
                   pl.BlockSpec((tm, HEADS_PER_TILE * V_PAD_DIM), lambda i, j: (i, j))],
        out_shape=[jax.ShapeDtypeStruct((T, n_tiles * HEADS_PER_TILE * QK_PAD_DIM), BF16),
                   jax.ShapeDtypeStruct((T, n_tiles * HEADS_PER_TILE * V_PAD_DIM), BF16)],
        compiler_params=_cparams(("parallel", "arbitrary"), 40),
        name="kv_up",
    )(ckv, w_kv, kpe2)


def _lane_tile(x, reps):
    return jnp.concatenate([x] * reps, axis=1)


def _attn_kernel(q_ref, k_ref, v_ref, o_ref, m_s, acc_s, s0_s, s1_s, x0_s, x1_s, p0_s, p1_s, a0_s, a1_s,
                 *, tkv):
    n_kv = k_ref.shape[0] // tkv
    s_buf, x_buf, p_buf, a_buf = (s0_s, s1_s), (x0_s, x1_s), (p0_s, p1_s), (a0_s, a1_s)
    m_s[...] = jnp.full(m_s.shape, -jnp.inf, F32)
    acc_s[...] = jnp.zeros(acc_s.shape, F32)

    def scores(i, slot):
        off = pl.multiple_of(i * tkv, tkv)
        s = lax.dot_general(q_ref[...], k_ref[pl.ds(off, tkv), :], (((1,), (1,)), ((), ())),
                            preferred_element_type=F32)
        s_buf[slot][...] = s
        x_buf[slot][...] = jnp.broadcast_to(jnp.max(s, axis=1, keepdims=True), x_buf[slot].shape)

    def probs(slot):
        m_prev = m_s[...]
        m_new = jnp.maximum(m_prev, x_buf[slot][...])
        m_s[...] = m_new
        a_buf[slot][...] = jnp.exp2(m_prev - m_new)
        p_buf[slot][...] = jnp.exp2(s_buf[slot][...] - _lane_tile(m_new, tkv // LANES)).astype(BF16)

    def values(i, slot):
        off = pl.multiple_of(i * tkv, tkv)
        acc_s[...] = (_lane_tile(a_buf[slot][...], V_PAD_DIM // LANES) * acc_s[...]
                      + jnp.dot(p_buf[slot][...], v_ref[pl.ds(off, tkv), :], preferred_element_type=F32))

    def trip(i, slot):
        scores(i + 1, 1 - slot)
        values(i - 1, 1 - slot)
        probs(slot)

    def pair(t, carry):
        trip(2 * t + 1, 1)
        trip(2 * t + 2, 0)
        return carry

    scores(0, 0)
    scores(1, 1)
    probs(0)
    lax.fori_loop(0, (n_kv - 2) // 2, pair, 0)
    values(n_kv - 2, 0)
    probs(1)
    values(n_kv - 1, 1)
    acc = acc_s[...]
    o_ref[...] = (acc[:, :V_HEAD_DIM] / acc[:, V_HEAD_DIM:]).astype(o_ref.dtype)


def _attention(q, k, v, B, S, H):
    T = B * S
    tq = _tile(S, ATTN_TQ)
    tkv = _tile(S, min(ATTN_TKV, S // 2))
    assert (S // tkv) % 2 == 0, "the key-chunk pipeline is unrolled in pairs"
    nq = S // tq
    return pl.pallas_call(
        functools.partial(_attn_kernel, tkv=tkv),
        grid=(B, H, nq),
        in_specs=[pl.BlockSpec((tq, QK_PAD_DIM), lambda b, h, i: (b * nq + i, h)),
                  pl.BlockSpec((S, QK_PAD_DIM), lambda b, h, i: (b, h)),
                  pl.BlockSpec((S, V_PAD_DIM), lambda b, h, i: (b, h))],
        out_specs=pl.BlockSpec((tq, V_HEAD_DIM), lambda b, h, i: (b * nq + i, h)),
        out_shape=jax.ShapeDtypeStruct((T, H * V_HEAD_DIM), BF16),
        scratch_shapes=[pltpu.VMEM((tq, LANES), F32), pltpu.VMEM((tq, V_PAD_DIM), F32),
                        pltpu.VMEM((tq, tkv), F32), pltpu.VMEM((tq, tkv), F32),
                        pltpu.VMEM((tq, LANES), F32), pltpu.VMEM((tq, LANES), F32),
                        pltpu.VMEM((tq, tkv), BF16), pltpu.VMEM((tq, tkv), BF16),
                        pltpu.VMEM((tq, LANES), F32), pltpu.VMEM((tq, LANES), F32)],
        compiler_params=_cparams(("parallel", "parallel", "arbitrary"), 40),
        name="mla_attention",
    )(q, k, v)


def _inproj_f_kernel(h_ref, w_ref, cs_ref, a_ref, b_ref, *, groups):
    acc = jnp.dot(h_ref[...], w_ref[...], preferred_element_type=F32)
    C = FOURIER_GROUP_DIM
    for g in range(groups):
        ab = jnp.dot(acc[:, g * C:(g + 1) * C].astype(BF16), cs_ref[...], preferred_element_type=F32)
        a_ref[:, g * C:(g + 1) * C] = ab[:, :C].astype(BF16)
        b_ref[:, g * C:(g + 1) * C] = ab[:, C:].astype(BF16)


def _inproj_f(hb, w_f, cs_tab):
    T, D = hb.shape
    fw = w_f.shape[1]
    tm = _tile(T, 1024)
    tn = _tile(fw, 512)
    return pl.pallas_call(
        functools.partial(_inproj_f_kernel, groups=tn // FOURIER_GROUP_DIM),
        grid=(T // tm, fw // tn),
        in_specs=[pl.BlockSpec((tm, D), lambda i, j: (i, 0)),
                  pl.BlockSpec((D, tn), lambda i, j: (0, j)),
                  pl.BlockSpec((FOURIER_GROUP_DIM, 2 * FOURIER_GROUP_DIM), lambda i, j: (0, 0))],
        out_specs=[pl.BlockSpec((tm, tn), lambda i, j: (i, j)),
                   pl.BlockSpec((tm, tn), lambda i, j: (i, j))],
        out_shape=[jax.ShapeDtypeStruct((T, fw), BF16), jax.ShapeDtypeStruct((T, fw), BF16)],
        compiler_params=_cparams(("parallel", "arbitrary"), 48),
        name="inproj_fourier",
    )(hb, w_f, cs_tab)


def _dft_gen_kernel(tac_ref, tas_ref, tbc_ref, tbs_ref, cs_ref, sn_ref):
    tbc = tbc_ref[...]
    tbs = tbs_ref[...]
    for aa in range(tac_ref.shape[0]):
        ca = tac_ref[aa:aa + 1, :]
        sa = tas_ref[aa:aa + 1, :]
        rows = slice(aa * DFT_ROW_SPLIT, (aa + 1) * DFT_ROW_SPLIT)
        cs_ref[rows, :] = (ca * tbc - sa * tbs).astype(BF16)
        sn_ref[rows, :] = (-(sa * tbc + ca * tbs)).astype(BF16)


def _dft_matrices(S, n_chan):
    na = S // DFT_ROW_SPLIT
    scale = 1.0 / math.sqrt(S * n_chan)
    col = jnp.arange(S, dtype=jnp.int32)[None, :]
    ang_a = (2.0 * math.pi / na) * ((jnp.arange(na, dtype=jnp.int32)[:, None] * col) % na).astype(F32)
    ang_b = (2.0 * math.pi / S) * ((jnp.arange(DFT_ROW_SPLIT, dtype=jnp.int32)[:, None] * col) % S).astype(F32)
    tac, tas = jnp.cos(ang_a), jnp.sin(ang_a)
    tbc, tbs = scale * jnp.cos(ang_b), scale * jnp.sin(ang_b)
    ta = 8
    tc = _tile(S, 2048)
    return pl.pallas_call(
        _dft_gen_kernel,
        grid=(na // ta, S // tc),
        in_specs=[pl.BlockSpec((ta, tc), lambda i, j: (i, j)),
                  pl.BlockSpec((ta, tc), lambda i, j: (i, j)),
                  pl.BlockSpec((DFT_ROW_SPLIT, tc), lambda i, j: (0, j)),
                  pl.BlockSpec((DFT_ROW_SPLIT, tc), lambda i, j: (0, j))],
        out_specs=[pl.BlockSpec((ta * DFT_ROW_SPLIT, tc), lambda i, j: (i, j)),
                   pl.BlockSpec((ta * DFT_ROW_SPLIT, tc), lambda i, j: (i, j))],
        out_shape=[jax.ShapeDtypeStruct((S, S), BF16), jax.ShapeDtypeStruct((S, S), BF16)],
        compiler_params=_cparams(("parallel", "parallel"), 40),
        name="dft_matrices",
    )(tac, tas, tbc, tbs)


def _seq_dft_kernel(cs_ref, sn_ref, a_ref, b_ref, y_ref, acc_s):
    kk = pl.program_id(3)

    @pl.when(kk == 0)
    def _():
        acc_s[...] = jnp.zeros(acc_s.shape, F32)

    acc_s[...] += (jnp.dot(cs_ref[...], a_ref[...], preferred_element_type=F32)
                   + jnp.dot(sn_ref[...], b_ref[...], preferred_element_type=F32))

    @pl.when(kk == pl.num_programs(3) - 1)
    def _():
        y_ref[...] = acc_s[...].astype(y_ref.dtype)


def _seq_dft(cs, sn, a, b, B, S):
    fw = a.shape[1]
    tm = _tile(S, 1024)
    tn = _tile(fw, 1024)
    tk = _tile(S, 1024)
    nm, nk = S // tm, S // tk
    return pl.pallas_call(
        _seq_dft_kernel,
        grid=(B, nm, fw // tn, nk),
        in_specs=[pl.BlockSpec((tm, tk), lambda bb, i, j, k: (i, k)),
                  pl.BlockSpec((tm, tk), lambda bb, i, j, k: (i, k)),
                  pl.BlockSpec((tk, tn), lambda bb, i, j, k: (bb * nk + k, j)),
                  pl.BlockSpec((tk, tn), lambda bb, i, j, k: (bb * nk + k, j))],
        out_specs=pl.BlockSpec((tm, tn), lambda bb, i, j, k: (bb * nm + i, j)),
        out_shape=jax.ShapeDtypeStruct((B * S, fw), BF16),
        scratch_shapes=[pltpu.VMEM((tm, tn), F32)],
        compiler_params=_cparams(("parallel", "parallel", "parallel", "arbitrary"), 48),
        name="seq_dft",
    )(cs, sn, a, b)


def _wo_kernel(ym_ref, yf_ref, gm_ref, gf_ref, w_ref, x_ref, mu_ref, rs_ref, lg_ref, lb_ref,
               pre_ref, mix_s, *, alpha, mw):
    @pl.when(pl.program_id(1) == 0)
    def _():
        ym = ym_ref[...].astype(F32)
        mix_s[:, :mw] = (ym * lax.rsqrt(jnp.mean(ym * ym, axis=-1, keepdims=True) + RMS_EPS)
                         * gm_ref[...]).astype(BF16)
        yf = yf_ref[...].astype(F32)
        mix_s[:, mw:] = (yf * lax.rsqrt(jnp.mean(yf * yf, axis=-1, keepdims=True) + RMS_EPS)
                         * gf_ref[...]).astype(BF16)

    h = (x_ref[...] - mu_ref[...]) * rs_ref[...] * lg_ref[...] + lb_ref[...]
    pre_ref[...] = alpha * h + jnp.dot(mix_s[...], w_ref[...], preferred_element_type=F32)


def _wo(ym, yf, gm, gf, w_o, x2, mu, rs, ln_g, ln_b, alpha):
    T, mw = ym.shape
    fw = yf.shape[1]
    D = w_o.shape[1]
    tm = _tile(T, 1024)
    tn = _tile(D, 512)
    return pl.pallas_call(
        functools.partial(_wo_kernel, alpha=alpha, mw=mw),
        grid=(T // tm, D // tn),
        in_specs=[pl.BlockSpec((tm, mw), lambda i, j: (i, 0)),
                  pl.BlockSpec((tm, fw), lambda i, j: (i, 0)),
                  pl.BlockSpec((1, mw), lambda i, j: (0, 0)),
                  pl.BlockSpec((1, fw), lambda i, j: (0, 0)),
                  pl.BlockSpec((mw + fw, tn), lambda i, j: (0, j)),
                  pl.BlockSpec((tm, tn), lambda i, j: (i, j)),
                  pl.BlockSpec((tm, 1), lambda i, j: (i, 0)),
                  pl.BlockSpec((tm, 1), lambda i, j: (i, 0)),
                  pl.BlockSpec((1, tn), lambda i, j: (0, j)),
                  pl.BlockSpec((1, tn), lambda i, j: (0, j))],
        out_specs=pl.BlockSpec((tm, tn), lambda i, j: (i, j)),
        out_shape=jax.ShapeDtypeStruct((T, D), F32),
        scratch_shapes=[pltpu.VMEM((tm, mw + fw), BF16)],
        compiler_params=_cparams(("parallel", "arbitrary"), 56),
        name="w_o_residual",
    )(ym, yf, gm.reshape(1, mw), gf.reshape(1, fw), w_o, x2, mu, rs,
      ln_g.reshape(1, D), ln_b.reshape(1, D))


def _pack_bf16_pair(lo, hi):
    lo_bits = pltpu.bitcast(lo.astype(BF16).astype(F32), jnp.uint32)
    hi_bits = pltpu.bitcast(hi.astype(BF16).astype(F32), jnp.uint32)
    return (lo_bits >> 16) | (hi_bits & jnp.uint32(0xFFFF0000))


def _unpack_bf16_pair(w):
    lo = pltpu.bitcast(w << 16, F32).astype(BF16)
    hi = pltpu.bitcast(w & jnp.uint32(0xFFFF0000), F32).astype(BF16)
    return lo, hi


def _ln1_router_kernel(pre_ref, g_ref, b_ref, rw_ref, rb_ref, x1_ref, x1p_ref, idx_ref, gate_ref):
    x = pre_ref[...]
    mu = jnp.mean(x, axis=-1, keepdims=True)
    xc = x - mu
    var = jnp.mean(xc * xc, axis=-1, keepdims=True)
    x1 = xc * lax.rsqrt(var + LN_EPS) * g_ref[...] + b_ref[...]
    x1_ref[...] = x1
    half = x1.shape[1] // 2
    x1p_ref[...] = _pack_bf16_pair(x1[:, :half], x1[:, half:])
    logits = lax.dot_general(rw_ref[...], x1, (((1,), (1,)), ((), ())),
                             precision=lax.Precision.HIGHEST, preferred_element_type=F32) + rb_ref[...]
    n_e = logits.shape[0]
    eidx = lax.broadcasted_iota(jnp.int32, logits.shape, 0)
    vals, idxs = [], []
    for _ in range(TOP_K):
        m = jnp.max(logits, axis=0, keepdims=True)
        sel = jnp.min(jnp.where(logits == m, eidx, n_e), axis=0, keepdims=True)
        logits = jnp.where(eidx == sel, -jnp.inf, logits)
        vals.append(m)
        idxs.append(sel)
    exps = [jnp.exp(v - vals[0]) for v in vals]
    denom = exps[0] + exps[1] + exps[2] + exps[3]
    for kk in range(TOP_K):
        idx_ref[kk:kk + 1, :] = idxs[kk]
        gate_ref[kk:kk + 1, :] = exps[kk] / denom


def _ln1_router(pre, g, b, router_w, router_b):
    T, D = pre.shape
    E = router_w.shape[1]
    tm = _tile(T, 256)
    return pl.pallas_call(
        _ln1_router_kernel,
        grid=(T // tm,),
        in_specs=[pl.BlockSpec((tm, D), lambda i: (i, 0)),
                  pl.BlockSpec((1, D), lambda i: (0, 0)),
                  pl.BlockSpec((1, D), lambda i: (0, 0)),
                  pl.BlockSpec((E, D), lambda i: (0, 0)),
                  pl.BlockSpec((E, 1), lambda i: (0, 0))],
        out_specs=[pl.BlockSpec((tm, D), lambda i: (i, 0)),
                   pl.BlockSpec((tm, D // 2), lambda i: (i, 0)),
                   pl.BlockSpec((TOP_K, tm), lambda i: (0, i)),
                   pl.BlockSpec((TOP_K, tm), lambda i: (0, i))],
        out_shape=[jax.ShapeDtypeStruct((T, D), F32),
                   jax.ShapeDtypeStruct((T, D // 2), jnp.uint32),
                   jax.ShapeDtypeStruct((TOP_K, T), jnp.int32),
                   jax.ShapeDtypeStruct((TOP_K, T), F32)],
        compiler_params=_cparams(("parallel",), 48),
        name="ln1_router",
    )(pre, g.reshape(1, D), b.reshape(1, D), router_w.T, router_b.reshape(E, 1))


def _row_copy(src_hbm, dst_ref, src_row, dst_row, sem):
    return pltpu.make_async_copy(src_hbm.at[pl.ds(src_row, 1)], dst_ref.at[pl.ds(dst_row, 1)], sem)


def _dispatch_kernel(dest_ref, pad_ref, npad_ref, nv_ref, x_hbm, o_hbm, zero_s, sem, zsem, *, n_assign):
    batch = DISPATCH_BATCH

    def wait_rows(n):
        pltpu.make_async_copy(x_hbm.at[pl.ds(0, n)], o_hbm.at[pl.ds(0, n)], sem).wait()

    b = pl.program_id(0)
    n_batches = n_assign // batch

    @pl.when(b < n_batches)
    def _():
        def issue(r, c):
            tok = b * (batch // TOP_K) + r
            for kk in range(TOP_K):
                _row_copy(x_hbm, o_hbm, tok, dest_ref[tok * TOP_K + kk], sem).start()
            return c
        lax.fori_loop(0, batch // TOP_K, issue, 0, unroll=2)

    @pl.when(b > 0)
    def _():
        wait_rows(batch)

    @pl.when(b == n_batches)
    def _():
        n_pad = npad_ref[0]

        def issue_pad(i, c):
            _row_copy(x_hbm, o_hbm, 0, pad_ref[i], sem).start()
            return c
        lax.fori_loop(0, n_pad, issue_pad, 0)

        def wait_pad(i, c):
            wait_rows(1)
            return c
        lax.fori_loop(0, n_pad, wait_pad, 0)

        zero_s[...] = jnp.zeros(zero_s.shape, zero_s.dtype)
        n_sub = MOE_CHUNK // MOE_SUB

        def zero_copy(i):
            return pltpu.make_async_copy(zero_s, o_hbm.at[pl.ds(pl.multiple_of(i * MOE_SUB, MOE_SUB), MOE_SUB)], zsem)

        def empty(i):
            return (i % n_sub) * MOE_SUB >= nv_ref[i // n_sub]

        def issue_zero(i, c):
            @pl.when(empty(i))
            def _():
                zero_copy(i).start()
            return c
        lax.fori_loop(0, nv_ref.shape[0] * n_sub, issue_zero, 0)

        def wait_zero(i, c):
            @pl.when(empty(i))
            def _():
                zero_copy(i).wait()
            return c
        lax.fori_loop(0, nv_ref.shape[0] * n_sub, wait_zero, 0)


def _dispatch(dest, pad_slots, n_pad, chunk_nv, x1p):
    n_assign = dest.shape[0]
    assert n_assign % DISPATCH_BATCH == 0
    grid_spec = pltpu.PrefetchScalarGridSpec(
        num_scalar_prefetch=4,
        grid=(n_assign // DISPATCH_BATCH + 1,),
        in_specs=[pl.BlockSpec(memory_space=pl.ANY)],
        out_specs=pl.BlockSpec(memory_space=pl.ANY),
        scratch_shapes=[pltpu.VMEM((MOE_SUB, x1p.shape[1]), x1p.dtype),
                        pltpu.SemaphoreType.DMA(()), pltpu.SemaphoreType.DMA(())],
    )
    return pl.pallas_call(
        functools.partial(_dispatch_kernel, n_assign=n_assign),
        grid_spec=grid_spec,
        out_shape=jax.ShapeDtypeStruct((chunk_nv.shape[0] * MOE_CHUNK, x1p.shape[1]), x1p.dtype),
        compiler_params=_cparams(("arbitrary",), 32),
        name="moe_dispatch",
    )(dest, pad_slots, n_pad, chunk_nv, x1p)


def _expert_up_kernel(ce_ref, nv_ref, nu_ref, x_ref, wg_ref, wu_ref, bg_ref, bu_ref, h_ref,
                      wgb_s, wub_s):
    c = pl.program_id(0)
    nv = nv_ref[c]

    @pl.when(nv > 0)
    def _():
        wgb_s[...] = wg_ref[...].astype(BF16)
        wub_s[...] = wu_ref[...].astype(BF16)

    for s in range(MOE_CHUNK // MOE_SUB):
        rows = slice(s * MOE_SUB, (s + 1) * MOE_SUB)
        active = s * MOE_SUB < nv

        @pl.when(active)
        def _():
            x_lo, x_hi = _unpack_bf16_pair(x_ref[rows, :])
            half = x_lo.shape[1]

            def proj(w_s):
                return (jnp.dot(x_lo, w_s[:half, :], preferred_element_type=F32)
                        + jnp.dot(x_hi, w_s[half:, :], preferred_element_type=F32))
            hg = jnp.minimum(proj(wgb_s) + bg_ref[...], SWIGLU_LIMIT)
            hu = jnp.clip(proj(wub_s) + bu_ref[...], -SWIGLU_LIMIT, SWIGLU_LIMIT)
            act = (hu + 1.0) * (hg * jax.nn.sigmoid(SWIGLU_ALPHA * hg))
            h_ref[rows, :] = act.astype(h_ref.dtype)

        @pl.when(jnp.logical_not(active))
        def _():
            h_ref[rows, :] = jnp.zeros((MOE_SUB, h_ref.shape[1]), h_ref.dtype)


def _expert_up(chunk_e, chunk_nv, n_used, xs, w_gate, w_up, b_gate, b_up, n_chunks):
    E, D, F = w_gate.shape
    tf = _tile(F, 256)
    nf = F // tf

    def used(c, nu):
        return jnp.minimum(c, nu[0] - 1)

    def jeff(c, j, nu):
        return jnp.where(c < nu[0], j, nf - 1)

    grid_spec = pltpu.PrefetchScalarGridSpec(
        num_scalar_prefetch=3,
        grid=(n_chunks, nf),
        in_specs=[pl.BlockSpec((MOE_CHUNK, D // 2), lambda c, j, ce, nv, nu: (used(c, nu), 0)),
                  pl.BlockSpec((None, D, tf), lambda c, j, ce, nv, nu: (ce[c], 0, jeff(c, j, nu))),
                  pl.BlockSpec((None, D, tf), lambda c, j, ce, nv, nu: (ce[c], 0, jeff(c, j, nu))),
                  pl.BlockSpec((None, 1, tf), lambda c, j, ce, nv, nu: (ce[c], 0, jeff(c, j, nu))),
                  pl.BlockSpec((None, 1, tf), lambda c, j, ce, nv, nu: (ce[c], 0, jeff(c, j, nu)))],
        out_specs=pl.BlockSpec((MOE_CHUNK, tf), lambda c, j, ce, nv, nu: (c, j)),
        scratch_shapes=[pltpu.VMEM((D, tf), BF16),
                        pltpu.VMEM((D, tf), BF16)],
    )
    return pl.pallas_call(
        _expert_up_kernel,
        grid_spec=grid_spec,
        out_shape=jax.ShapeDtypeStruct((n_chunks * MOE_CHUNK, F), BF16),
        compiler_params=_cparams(("arbitrary", "arbitrary"), 60),
        name="expert_gate_up",
    )(chunk_e, chunk_nv, n_used, xs, w_gate, w_up, b_gate.reshape(E, 1, F), b_up.reshape(E, 1, F))


def _expert_down_kernel(ce_ref, nv_ref, nu_ref, h_ref, wd_ref, bd_ref, y_ref, wdb_s):
    c = pl.program_id(0)
    nv = nv_ref[c]

    @pl.when(nv > 0)
    def _():
        wdb_s[...] = wd_ref[...].astype(BF16)

    for s in range(MOE_CHUNK // MOE_SUB):
        rows = slice(s * MOE_SUB, (s + 1) * MOE_SUB)
        active = s * MOE_SUB < nv

        @pl.when(active)
        def _():
            y_ref[rows, :] = jnp.dot(h_ref[rows, :], wdb_s[...], preferred_element_type=F32) + bd_ref[...]

        @pl.when(jnp.logical_not(active))
        def _():
            y_ref[rows, :] = jnp.zeros((MOE_SUB, y_ref.shape[1]), y_ref.dtype)


def _expert_down(chunk_e, chunk_nv, n_used, hmid, w_down, b_down, n_chunks):
    E, F, D = w_down.shape
    tn = _tile(D, 512)
    nn = D // tn

    def used(c, nu):
        return jnp.minimum(c, nu[0] - 1)

    def jeff(c, j, nu):
        return jnp.where(c < nu[0], j, nn - 1)

    grid_spec = pltpu.PrefetchScalarGridSpec(
        num_scalar_prefetch=3,
        grid=(n_chunks, nn),
        in_specs=[pl.BlockSpec((MOE_CHUNK, F), lambda c, j, ce, nv, nu: (used(c, nu), 0)),
                  pl.BlockSpec((None, F, tn), lambda c, j, ce, nv, nu: (ce[c], 0, jeff(c, j, nu))),
                  pl.BlockSpec((None, 1, tn), lambda c, j, ce, nv, nu: (ce[c], 0, jeff(c, j, nu)))],
        out_specs=pl.BlockSpec((MOE_CHUNK, tn), lambda c, j, ce, nv, nu: (c, j)),
        scratch_shapes=[pltpu.VMEM((F, tn), BF16)],
    )
    return pl.pallas_call(
        _expert_down_kernel,
        grid_spec=grid_spec,
        out_shape=jax.ShapeDtypeStruct((n_chunks * MOE_CHUNK, D), F32),
        compiler_params=_cparams(("arbitrary", "arbitrary"), 56),
        name="expert_down",
    )(chunk_e, chunk_nv, n_used, hmid, w_down, b_down.reshape(E, 1, D))


def _combine_kernel(dest_ref, y_hbm, x1_ref, gate_ref, g_ref, b_ref, o_ref, buf, sem, *, alpha):
    i = pl.program_id(0)
    n = pl.num_programs(0)
    tt = o_ref.shape[0]

    def issue_step(step, slot):
        def issue(r, carry):
            for kk in range(TOP_K):
                _row_copy(y_hbm, buf.at[slot, kk], dest_ref[(step * tt + r) * TOP_K + kk], r,
                          sem.at[slot]).start()
            return carry
        lax.fori_loop(0, tt, issue, 0, unroll=4)

    @pl.when(i == 0)
    def _():
        issue_step(0, 0)

    slot = i % 2

    @pl.when(i + 1 < n)
    def _():
        issue_step(i + 1, 1 - slot)

    for kk in range(TOP_K):
        pltpu.make_async_copy(y_hbm.at[pl.ds(0, tt)], buf.at[slot, kk], sem.at[slot]).wait()
    gates = gate_ref[...]
    y = gates[:, 0:1] * buf[slot, 0]
    for kk in range(1, TOP_K):
        y = y + gates[:, kk:kk + 1] * buf[slot, kk]
    z = alpha * x1_ref[...] + y
    mu = jnp.mean(z, axis=-1, keepdims=True)
    zc = z - mu
    var = jnp.mean(zc * zc, axis=-1, keepdims=True)
    o_ref[...] = zc * lax.rsqrt(var + LN_EPS) * g_ref[...] + b_ref[...]


def _combine(dest, yslots, x1, gates_tk, g, b, alpha):
    T, D = x1.shape
    tt = _tile(T, COMBINE_TOKENS)
    grid_spec = pltpu.PrefetchScalarGridSpec(
        num_scalar_prefetch=1,
        grid=(T // tt,),
        in_specs=[pl.BlockSpec(memory_space=pl.ANY),
                  pl.BlockSpec((tt, D), lambda i, d: (i, 0)),
                  pl.BlockSpec((tt, TOP_K), lambda i, d: (i, 0)),
                  pl.BlockSpec((1, D), lambda i, d: (0, 0)),
                  pl.BlockSpec((1, D), lambda i, d: (0, 0))],
        out_specs=pl.BlockSpec((tt, D), lambda i, d: (i, 0)),
        scratch_shapes=[pltpu.VMEM((2, TOP_K, tt, D), F32), pltpu.SemaphoreType.DMA((2,))],
    )
    return pl.pallas_call(
        functools.partial(_combine_kernel, alpha=alpha),
        grid_spec=grid_spec,
        out_shape=jax.ShapeDtypeStruct((T, D), F32),
        compiler_params=_cparams(("arbitrary",), 32),
        name="moe_combine_ln2",
    )(dest, yslots, x1, gates_tk, g.reshape(1, D), b.reshape(1, D))


def _routing_tables(top_idx, n_experts, n_chunks):
    T = top_idx.shape[1]
    M = T * TOP_K
    flat_e = top_idx.T.reshape(M)
    onehot = (flat_e[:, None] == jnp.arange(n_experts, dtype=jnp.int32)[None, :]).astype(jnp.int32)
    csum = jnp.cumsum(onehot, axis=0)
    rank = jnp.sum(csum * onehot, axis=1) - 1
    counts = csum[-1]
    chunks_e = (counts + MOE_CHUNK - 1) // MOE_CHUNK
    chunk_end = jnp.cumsum(chunks_e)
    chunk_start = chunk_end - chunks_e
    n_used = chunk_end[-1]
    dest = chunk_start[flat_e] * MOE_CHUNK + rank
    P = n_chunks * MOE_CHUNK
    pad_idx = counts[:, None] + jnp.arange(MOE_SUB, dtype=jnp.int32)[None, :]
    padded = (counts + MOE_SUB - 1) // MOE_SUB * MOE_SUB
    pad_slots = jnp.where(pad_idx < padded[:, None], chunk_start[:, None] * MOE_CHUNK + pad_idx, P)
    pad_slots = jnp.sort(pad_slots.reshape(-1)).astype(jnp.int32)
    n_pad = jnp.sum(padded - counts).astype(jnp.int32).reshape(1)
    cid = jnp.arange(n_chunks, dtype=jnp.int32)
    chunk_e = jnp.minimum(jnp.searchsorted(chunk_end, cid, side='right'), n_experts - 1).astype(jnp.int32)
    last_e = chunk_e[jnp.maximum(n_used - 1, 0)]
    chunk_e = jnp.where(cid < n_used, chunk_e, last_e)
    chunk_nv = jnp.where(cid < n_used,
                         jnp.clip(counts[chunk_e] - (cid - chunk_start[chunk_e]) * MOE_CHUNK, 0, MOE_CHUNK),
                         0).astype(jnp.int32)
    return (dest.astype(jnp.int32), pad_slots, n_pad, chunk_e, chunk_nv,
            n_used.astype(jnp.int32).reshape(1))


def kernel(x, positions, ln_in_g, ln_in_b, w_in, q_a_norm_g, w_q_b, kv_a_norm_g, w_kv_b, mla_out_norm_g, fourier_out_norm_g, w_o, ln1_g, ln1_b, router_w, router_b, w_gate, b_gate, w_up, b_up, w_down, b_down, ln2_g, ln2_b):
    B, S, D = x.shape
    T = B * S
    depth = w_in.shape[0]
    assert depth == 1, "single-layer trunk only"
    qr = q_a_norm_g.shape[1]
    kvr = kv_a_norm_g.shape[1]
    H = w_q_b.shape[2] // (QK_NOPE_DIM + QK_ROPE_DIM)
    fw = fourier_out_norm_g.shape[1]
    E = router_w.shape[2]
    assert H % HEADS_PER_TILE == 0 and (qr + kvr) % LANES == 0 and S % (8 * DFT_ROW_SPLIT) == 0
    alpha = (2.0 * depth) ** 0.25

    inv_freq = ROPE_THETA ** (-jnp.arange(0, QK_ROPE_DIM, 2, dtype=F32) / QK_ROPE_DIM)
    ang = positions.astype(F32)[..., None] * inv_freq
    cos4 = jnp.tile(jnp.cos(ang), (1, 1, 2 * LANES // QK_ROPE_DIM)).reshape(T, LANES)
    sin4 = jnp.tile(jnp.sin(ang), (1, 1, 2 * LANES // QK_ROPE_DIM)).reshape(T, LANES)

    rope_end = qr + kvr + QK_ROPE_DIM
    w_a = w_in[0, :, :qr + kvr + LANES].astype(BF16)
    w_f = w_in[0, :, rope_end:].astype(BF16)
    wq = w_q_b[0].reshape(qr, H // HEADS_PER_TILE, HEADS_PER_TILE, QK_NOPE_DIM + QK_ROPE_DIM)
    wq_perm = jnp.concatenate(
        [wq[..., :QK_NOPE_DIM].reshape(qr, H // HEADS_PER_TILE, HEADS_PER_TILE * QK_NOPE_DIM),
         wq[..., QK_NOPE_DIM:].reshape(qr, H // HEADS_PER_TILE, HEADS_PER_TILE * QK_ROPE_DIM)],
        axis=-1).reshape(qr, H * (QK_NOPE_DIM + QK_ROPE_DIM)).astype(BF16)
    w_kv = w_kv_b[0].astype(BF16)
    w_o_b = w_o[0].astype(BF16)
    ch = jnp.arange(FOURIER_GROUP_DIM, dtype=jnp.int32)
    ang_c = (2.0 * math.pi / FOURIER_GROUP_DIM) * ((ch[:, None] * ch[None, :]) % FOURIER_GROUP_DIM).astype(F32)
    cs_tab = jnp.concatenate([jnp.cos(ang_c), jnp.sin(ang_c)], axis=1).astype(BF16)

    x2 = x.reshape(T, D)
    hb, mu, rs = _ln_in(x2, ln_in_g, ln_in_b)

    cq, ckv, kpe2 = _inproj_a(hb, w_a, q_a_norm_g[0], kv_a_norm_g[0], cos4, sin4)
    qscale = (QK_NOPE_DIM + QK_ROPE_DIM) ** -0.5 * LOG2E
    q = _q_up(cq, wq_perm, cos4, sin4, qscale)
    k, v = _kv_up(ckv, w_kv, kpe2)
    y_mla = _attention(q, k, v, B, S, H)

    fa, fb = _inproj_f(hb, w_f, cs_tab)
    cs_mat, sn_mat = _dft_matrices(S, FOURIER_GROUP_DIM)
    y_f = _seq_dft(cs_mat, sn_mat, fa, fb, B, S)

    pre = _wo(y_mla, y_f, mla_out_norm_g[0], fourier_out_norm_g[0], w_o_b, x2, mu, rs,
              ln_in_g, ln_in_b, alpha)
    x1, x1p, top_idx, gates = _ln1_router(pre, ln1_g[0], ln1_b[0], router_w[0], router_b[0])

    n_chunks = -(-T * TOP_K // MOE_CHUNK) + E
    dest, pad_slots, n_pad, chunk_e, chunk_nv, n_used = _routing_tables(top_idx, E, n_chunks)
    xs = _dispatch(dest, pad_slots, n_pad, chunk_nv, x1p)
    hmid = _expert_up(chunk_e, chunk_nv, n_used, xs, w_gate[0], w_up[0], b_gate[0], b_up[0], n_chunks)
    yslots = _expert_down(chunk_e, chunk_nv, n_used, hmid, w_down[0], b_down[0], n_chunks)
    out = _combine(dest, yslots, x1, gates.T, ln2_g[0], ln2_b[0], alpha)
    return out.reshape(B, S, D)
```

```python
import functools
import math

import jax
import jax.numpy as jnp
from jax import lax
from jax.experimental import pallas as pl
from jax.experimental.pallas import tpu as pltpu

F32 = jnp.float32
BF16 = jnp.bfloat16

V_HEAD_DIM = 128
QK_NOPE_DIM = 128
QK_ROPE_DIM = 64
QK_PAD_DIM = 256
V_PAD_DIM = 256
ROPE_THETA = 10000.0
FOURIER_GROUP_DIM = 128
TOP_K = 4
SWIGLU_LIMIT = 7.0
SWIGLU_ALPHA = 1.702
LN_EPS = 1e-5
RMS_EPS = 1e-6
LOG2E = 1.4426950408889634

LANES = 128
V7X_VMEM_BYTES = 64 * 1024 * 1024
HEADS_PER_TILE = 4
DFT_ROW_SPLIT = 64
ATTN_TQ = 1024
ATTN_TKV = 1024

MOE_CHUNK = 1280
MOE_SUB = 256
DISPATCH_TOKENS = 512
COMBINE_TOKENS = 64


def _cparams(semantics, vmem_mb):
    return pltpu.CompilerParams(dimension_semantics=semantics,
                                vmem_limit_bytes=min(vmem_mb * 1024 * 1024, V7X_VMEM_BYTES - (4 << 20)))


def _tile(dim, pref):
    t = min(dim, pref)
    while dim % t:
        t //= 2
    return t


def _ln_in_kernel(x_ref, g_ref, b_ref, hb_ref, mu_ref, rs_ref):
    x = x_ref[...]
    mu = jnp.mean(x, axis=-1, keepdims=True)
    xc = x - mu
    var = jnp.mean(xc * xc, axis=-1, keepdims=True)
    rs = lax.rsqrt(var + LN_EPS)
    hb_ref[...] = (xc * rs * g_ref[...] + b_ref[...]).astype(BF16)
    mu_ref[...] = mu
    rs_ref[...] = rs


def _ln_in(x2, g, b):
    T, D = x2.shape
    tm = _tile(T, 256)
    return pl.pallas_call(
        _ln_in_kernel,
        grid=(T // tm,),
        in_specs=[pl.BlockSpec((tm, D), lambda i: (i, 0)),
                  pl.BlockSpec((1, D), lambda i: (0, 0)),
                  pl.BlockSpec((1, D), lambda i: (0, 0))],
        out_specs=[pl.BlockSpec((tm, D), lambda i: (i, 0)),
                   pl.BlockSpec((tm, 1), lambda i: (i, 0)),
                   pl.BlockSpec((tm, 1), lambda i: (i, 0))],
        out_shape=[jax.ShapeDtypeStruct((T, D), BF16),
                   jax.ShapeDtypeStruct((T, 1), F32),
                   jax.ShapeDtypeStruct((T, 1), F32)],
        compiler_params=_cparams(("parallel",), 40),
        name="ln_in",
    )(x2, g.reshape(1, D), b.reshape(1, D))


def _rope128(p, cos4, sin4):
    lane = lax.broadcasted_iota(jnp.int32, p.shape, 1)
    first_half = (lane % QK_ROPE_DIM) < (QK_ROPE_DIM // 2)
    rot = jnp.where(first_half, -pltpu.roll(p, LANES - QK_ROPE_DIM // 2, 1),
                    pltpu.roll(p, QK_ROPE_DIM // 2, 1))
    return p * cos4 + rot * sin4


def _inproj_a_kernel(h_ref, w_ref, gq_ref, gkv_ref, cos_ref, sin_ref,
                     cq_ref, ckv_ref, kpe_ref, *, qr, kvr):
    acc = jnp.dot(h_ref[...], w_ref[...], preferred_element_type=F32)
    cq = acc[:, :qr]
    cq_ref[...] = (cq * lax.rsqrt(jnp.mean(cq * cq, axis=-1, keepdims=True) + RMS_EPS)
                   * gq_ref[...]).astype(BF16)
    ckv = acc[:, qr:qr + kvr]
    ckv_ref[...] = (ckv * lax.rsqrt(jnp.mean(ckv * ckv, axis=-1, keepdims=True) + RMS_EPS)
                    * gkv_ref[...]).astype(BF16)
    roped = _rope128(acc[:, qr + kvr:qr + kvr + LANES], cos_ref[...], sin_ref[...])
    lane = lax.broadcasted_iota(jnp.int32, roped.shape, 1)
    even = jnp.where(lane < QK_ROPE_DIM, roped, 0.0)
    kpe_ref[:, :LANES] = even.astype(BF16)
    kpe_ref[:, LANES:] = pltpu.roll(even, QK_ROPE_DIM, 1).astype(BF16)


def _inproj_a(hb, w_a, gq, gkv, cos4, sin4):
    T, D = hb.shape
    qr, kvr = gq.shape[0], gkv.shape[0]
    wa = w_a.shape[1]
    tm = _tile(T, 512)
    return pl.pallas_call(
        functools.partial(_inproj_a_kernel, qr=qr, kvr=kvr),
        grid=(T // tm,),
        in_specs=[pl.BlockSpec((tm, D), lambda i: (i, 0)),
                  pl.BlockSpec((D, wa), lambda i: (0, 0)),
                  pl.BlockSpec((1, qr), lambda i: (0, 0)),
                  pl.BlockSpec((1, kvr), lambda i: (0, 0)),
                  pl.BlockSpec((tm, LANES), lambda i: (i, 0)),
                  pl.BlockSpec((tm, LANES), lambda i: (i, 0))],
        out_specs=[pl.BlockSpec((tm, qr), lambda i: (i, 0)),
                   pl.BlockSpec((tm, kvr), lambda i: (i, 0)),
                   pl.BlockSpec((tm, 2 * LANES), lambda i: (i, 0))],
        out_shape=[jax.ShapeDtypeStruct((T, qr), BF16),
                   jax.ShapeDtypeStruct((T, kvr), BF16),
                   jax.ShapeDtypeStruct((T, 2 * LANES), BF16)],
        compiler_params=_cparams(("parallel",), 56),
        name="inproj_mla",
    )(hb, w_a, gq.reshape(1, qr), gkv.reshape(1, kvr), cos4, sin4)


def _q_up_kernel(c_ref, w_ref, cos_ref, sin_ref, q_ref, *, qscale):
    acc = jnp.dot(c_ref[...], w_ref[...], preferred_element_type=F32)
    nope_w = HEADS_PER_TILE * QK_NOPE_DIM
    lane = lax.broadcasted_iota(jnp.int32, (acc.shape[0], LANES), 1)
    for pair in range(HEADS_PER_TILE // 2):
        roped = _rope128(acc[:, nope_w + pair * LANES:nope_w + (pair + 1) * LANES],
                         cos_ref[...], sin_ref[...]) * qscale
        for par in range(2):
            j = 2 * pair + par
            keep = (lane < QK_ROPE_DIM) if par == 0 else (lane >= QK_ROPE_DIM)
            base = j * QK_PAD_DIM
            q_ref[:, base:base + QK_NOPE_DIM] = (
                acc[:, j * QK_NOPE_DIM:(j + 1) * QK_NOPE_DIM] * qscale).astype(BF16)
            q_ref[:, base + QK_NOPE_DIM:base + QK_PAD_DIM] = jnp.where(keep, roped, 0.0).astype(BF16)


def _q_up(cq, wq_perm, cos4, sin4, qscale):
    T, qr = cq.shape
    n_tiles = wq_perm.shape[1] // (HEADS_PER_TILE * (QK_NOPE_DIM + QK_ROPE_DIM))
    tw = HEADS_PER_TILE * (QK_NOPE_DIM + QK_ROPE_DIM)
    to = HEADS_PER_TILE * QK_PAD_DIM
    tm = _tile(T, 1024)
    return pl.pallas_call(
        functools.partial(_q_up_kernel, qscale=qscale),
        grid=(T // tm, n_tiles),
        in_specs=[pl.BlockSpec((tm, qr), lambda i, j: (i, 0)),
                  pl.BlockSpec((qr, tw), lambda i, j: (0, j)),
                  pl.BlockSpec((tm, LANES), lambda i, j: (i, 0)),
                  pl.BlockSpec((tm, LANES), lambda i, j: (i, 0))],
        out_specs=pl.BlockSpec((tm, to), lambda i, j: (i, j)),
        out_shape=jax.ShapeDtypeStruct((T, n_tiles * to), BF16),
        compiler_params=_cparams(("parallel", "arbitrary"), 40),
        name="q_up",
    )(cq, wq_perm, cos4, sin4)


def _kv_up_kernel(c_ref, w_ref, kpe_ref, k_ref, v_ref):
    acc = jnp.dot(c_ref[...], w_ref[...], preferred_element_type=F32)
    for j in range(HEADS_PER_TILE):
        src = j * (QK_NOPE_DIM + V_HEAD_DIM)
        k_ref[:, j * QK_PAD_DIM:j * QK_PAD_DIM + QK_NOPE_DIM] = acc[:, src:src + QK_NOPE_DIM].astype(BF16)
        par = j % 2
        k_ref[:, j * QK_PAD_DIM + QK_NOPE_DIM:(j + 1) * QK_PAD_DIM] = kpe_ref[:, par * LANES:(par + 1) * LANES]
        v_ref[:, j * V_PAD_DIM:j * V_PAD_DIM + V_HEAD_DIM] = (
            acc[:, src + QK_NOPE_DIM:src + QK_NOPE_DIM + V_HEAD_DIM].astype(BF16))
        v_ref[:, j * V_PAD_DIM + V_HEAD_DIM:(j + 1) * V_PAD_DIM] = jnp.ones(
            (acc.shape[0], V_PAD_DIM - V_HEAD_DIM), BF16)


def _kv_up(ckv, w_kv, kpe2):
    T, kvr = ckv.shape
    tw = HEADS_PER_TILE * (QK_NOPE_DIM + V_HEAD_DIM)
    n_tiles = w_kv.shape[1] // tw
    tm = _tile(T, 1024)
    return pl.pallas_call(
        _kv_up_kernel,
        grid=(T // tm, n_tiles),
        in_specs=[pl.BlockSpec((tm, kvr), lambda i, j: (i, 0)),
                  pl.BlockSpec((kvr, tw), lambda i, j: (0, j)),
                  pl.BlockSpec((tm, 2 * LANES), lambda i, j: (i, 0))],
        out_specs=[pl.BlockSpec((tm, HEADS_PER_TILE * QK_PAD_DIM), lambda i, j: (i, j)),
                   pl.BlockSpec((tm, HEADS_PER_TILE * V_PAD_DIM), lambda i, j: (i, j))],
        out_shape=[jax.ShapeDtypeStruct((T, n_tiles * HEADS_PER_TILE * QK_PAD_DIM), BF16),
                   jax.ShapeDtypeStruct((T, n_tiles * HEADS_PER_TILE * V_PAD_DIM), BF16)],
        compiler_params=_cparams(("parallel", "arbitrary"), 40),
        name="kv_up",
    )(ckv, w_kv, kpe2)


def _lane_tile(x, reps):
    return jnp.concatenate([x] * reps, axis=1)


def _attn_kernel(q_ref, k_ref, v_ref, o_ref, m_s, acc_s, s0_s, s1_s, x0_s, x1_s, p0_s, p1_s, a0_s, a1_s,
                 *, tkv):
    n_kv = k_ref.shape[0] // tkv
    s_buf, x_buf, p_buf, a_buf = (s0_s, s1_s), (x0_s, x1_s), (p0_s, p1_s), (a0_s, a1_s)
    m_s[...] = jnp.full(m_s.shape, -jnp.inf, F32)
    acc_s[...] = jnp.zeros(acc_s.shape, F32)

    def scores(i, slot):
        off = pl.multiple_of(i * tkv, tkv)
        s = lax.dot_general(q_ref[...], k_ref[pl.ds(off, tkv), :], (((1,), (1,)), ((), ())),
                            preferred_element_type=F32)
        s_buf[slot][...] = s
        x_buf[slot][...] = jnp.broadcast_to(jnp.max(s, axis=1, keepdims=True), x_buf[slot].shape)

    def probs(slot):
        m_prev = m_s[...]
        m_new = jnp.maximum(m_prev, x_buf[slot][...])
        m_s[...] = m_new
        a_buf[slot][...] = jnp.exp2(m_prev - m_new)
        p_buf[slot][...] = jnp.exp2(s_buf[slot][...] - _lane_tile(m_new, tkv // LANES)).astype(BF16)

    def values(i, slot):
        off = pl.multiple_of(i * tkv, tkv)
        acc_s[...] = (_lane_tile(a_buf[slot][...], V_PAD_DIM // LANES) * acc_s[...]
                      + jnp.dot(p_buf[slot][...], v_ref[pl.ds(off, tkv), :], preferred_element_type=F32))

    def trip(i, slot):
        scores(i + 1, 1 - slot)
        values(i - 1, 1 - slot)
        probs(slot)

    def pair(t, carry):
        trip(2 * t + 1, 1)
        trip(2 * t + 2, 0)
        return carry

    scores(0, 0)
    scores(1, 1)
    probs(0)
    lax.fori_loop(0, (n_kv - 2) // 2, pair, 0)
    values(n_kv - 2, 0)
    probs(1)
    values(n_kv - 1, 1)
    acc = acc_s[...]
    o_ref[...] = (acc[:, :V_HEAD_DIM] / acc[:, V_HEAD_DIM:]).astype(o_ref.dtype)


def _attention(q, k, v, B, S, H):
    T = B * S
    tq = _tile(S, ATTN_TQ)
    tkv = _tile(S, min(ATTN_TKV, S // 2))
    assert (S // tkv) % 2 == 0, "the key-chunk pipeline is unrolled in pairs"
    nq = S // tq
    return pl.pallas_call(
        functools.partial(_attn_kernel, tkv=tkv),
        grid=(B, H, nq),
        in_specs=[pl.BlockSpec((tq, QK_PAD_DIM), lambda b, h, i: (b * nq + i, h)),
                  pl.BlockSpec((S, QK_PAD_DIM), lambda b, h, i: (b, h)),
                  pl.BlockSpec((S, V_PAD_DIM), lambda b, h, i: (b, h))],
        out_specs=pl.BlockSpec((tq, V_HEAD_DIM), lambda b, h, i: (b * nq + i, h)),
        out_shape=jax.ShapeDtypeStruct((T, H * V_HEAD_DIM), BF16),
        scratch_shapes=[pltpu.VMEM((tq, LANES), F32), pltpu.VMEM((tq, V_PAD_DIM), F32),
                        pltpu.VMEM((tq, tkv), F32), pltpu.VMEM((tq, tkv), F32),
                        pltpu.VMEM((tq, LANES), F32), pltpu.VMEM((tq, LANES), F32),
                        pltpu.VMEM((tq, tkv), BF16), pltpu.VMEM((tq, tkv), BF16),
                        pltpu.VMEM((tq, LANES), F32), pltpu.VMEM((tq, LANES), F32)],
        compiler_params=_cparams(("parallel", "parallel", "arbitrary"), 40),
        name="mla_attention",
    )(q, k, v)


def _inproj_f_kernel(h_ref, w_ref, cs_ref, a_ref, b_ref, *, groups):
    acc = jnp.dot(h_ref[...], w_ref[...], preferred_element_type=F32)
    C = FOURIER_GROUP_DIM
    for g in range(groups):
        ab = jnp.dot(acc[:, g * C:(g + 1) * C].astype(BF16), cs_ref[...], preferred_element_type=F32)
        a_ref[:, g * C:(g + 1) * C] = ab[:, :C].astype(BF16)
        b_ref[:, g * C:(g + 1) * C] = ab[:, C:].astype(BF16)


def _inproj_f(hb, w_f, cs_tab):
    T, D = hb.shape
    fw = w_f.shape[1]
    tm = _tile(T, 1024)
    tn = _tile(fw, 512)
    return pl.pallas_call(
        functools.partial(_inproj_f_kernel, groups=tn // FOURIER_GROUP_DIM),
        grid=(T // tm, fw // tn),
        in_specs=[pl.BlockSpec((tm, D), lambda i, j: (i, 0)),
                  pl.BlockSpec((D, tn), lambda i, j: (0, j)),
                  pl.BlockSpec((FOURIER_GROUP_DIM, 2 * FOURIER_GROUP_DIM), lambda i, j: (0, 0))],
        out_specs=[pl.BlockSpec((tm, tn), lambda i, j: (i, j)),
                   pl.BlockSpec((tm, tn), lambda i, j: (i, j))],
        out_shape=[jax.ShapeDtypeStruct((T, fw), BF16), jax.ShapeDtypeStruct((T, fw), BF16)],
        compiler_params=_cparams(("parallel", "arbitrary"), 48),
        name="inproj_fourier",
    )(hb, w_f, cs_tab)


def _dft_gen_kernel(tac_ref, tas_ref, tbc_ref, tbs_ref, cs_ref, sn_ref):
    tbc = tbc_ref[...]
    tbs = tbs_ref[...]
    for aa in range(tac_ref.shape[0]):
        ca = tac_ref[aa:aa + 1, :]
        sa = tas_ref[aa:aa + 1, :]
        rows = slice(aa * DFT_ROW_SPLIT, (aa + 1) * DFT_ROW_SPLIT)
        cs_ref[rows, :] = (ca * tbc - sa * tbs).astype(BF16)
        sn_ref[rows, :] = (-(sa * tbc + ca * tbs)).astype(BF16)


def _dft_matrices(S, n_chan):
    na = S // DFT_ROW_SPLIT
    scale = 1.0 / math.sqrt(S * n_chan)
    col = jnp.arange(S // 2, dtype=jnp.int32)[None, :]
    ang_a = (2.0 * math.pi / na) * ((jnp.arange(na, dtype=jnp.int32)[:, None] * col) % na).astype(F32)
    ang_b = (2.0 * math.pi / S) * ((jnp.arange(DFT_ROW_SPLIT, dtype=jnp.int32)[:, None] * col) % S).astype(F32)
    tac, tas = jnp.cos(ang_a), jnp.sin(ang_a)
    tbc, tbs = scale * jnp.cos(ang_b), scale * jnp.sin(ang_b)
    ta = 8
    tc = _tile(S // 2, 2048)
    return pl.pallas_call(
        _dft_gen_kernel,
        grid=(na // ta, S // 2 // tc),
        in_specs=[pl.BlockSpec((ta, tc), lambda i, j: (i, j)),
                  pl.BlockSpec((ta, tc), lambda i, j: (i, j)),
                  pl.BlockSpec((DFT_ROW_SPLIT, tc), lambda i, j: (0, j)),
                  pl.BlockSpec((DFT_ROW_SPLIT, tc), lambda i, j: (0, j))],
        out_specs=[pl.BlockSpec((ta * DFT_ROW_SPLIT, tc), lambda i, j: (i, j)),
                   pl.BlockSpec((ta * DFT_ROW_SPLIT, tc), lambda i, j: (i, j))],
        out_shape=[jax.ShapeDtypeStruct((S, S // 2), BF16), jax.ShapeDtypeStruct((S, S // 2), BF16)],
        compiler_params=_cparams(("parallel", "parallel"), 40),
        name="dft_matrices",
    )(tac, tas, tbc, tbs)


def _dft_fold_kernel(a_ref, am_ref, an_ref, b_ref, bm_ref, bn_ref, ae_ref, bo_ref):
    tm = a_ref.shape[0]
    r = lax.broadcasted_iota(jnp.int32, (tm, tm), 0)
    c = lax.broadcasted_iota(jnp.int32, (tm, tm), 1)
    rev = jnp.where(r + c == tm, 1.0, 0.0).astype(BF16)
    has_next = (pl.program_id(1) > 0).astype(F32)
    first = (jnp.where(r + c == 0, 1.0, 0.0) * has_next).astype(BF16)

    def mirrored(m_ref, n_ref):
        return (jnp.dot(rev, m_ref[...], preferred_element_type=F32)
                + jnp.dot(first, n_ref[...], preferred_element_type=F32))

    ae_ref[...] = (a_ref[...].astype(F32) + mirrored(am_ref, an_ref)).astype(BF16)
    bo_ref[...] = (b_ref[...].astype(F32) - mirrored(bm_ref, bn_ref)).astype(BF16)


def _dft_fold(a, b, B, S):
    fw = a.shape[1]
    tm = _tile(S // 2, 256)
    nb = S // tm
    nh = nb // 2

    def own(bb, i):
        return (bb * nb + i, 0)

    def mirror(bb, i):
        return (bb * nb + nb - 1 - i, 0)

    def mirror_next(bb, i):
        return (bb * nb + jnp.minimum(nb - i, nb - 1), 0)

    spec = lambda f: pl.BlockSpec((tm, fw), f)
    return pl.pallas_call(
        _dft_fold_kernel,
        grid=(B, nh),
        in_specs=[spec(own), spec(mirror), spec(mirror_next), spec(own), spec(mirror), spec(mirror_next)],
        out_specs=[pl.BlockSpec((tm, fw), lambda bb, i: (bb * nh + i, 0)),
                   pl.BlockSpec((tm, fw), lambda bb, i: (bb * nh + i, 0))],
        out_shape=[jax.ShapeDtypeStruct((B * S // 2, fw), BF16),
                   jax.ShapeDtypeStruct((B * S // 2, fw), BF16)],
        compiler_params=_cparams(("parallel", "arbitrary"), 40),
        name="dft_fold",
    )(a, a, a, b, b, b)


def _seq_dft_kernel(cs_ref, sn_ref, ae_ref, bo_ref, mid_ref, y_ref, acc_s, *, scale):
    kk = pl.program_id(3)

    @pl.when(kk == 0)
    def _():
        acc_s[...] = jnp.zeros(acc_s.shape, F32)

    acc_s[...] += (jnp.dot(cs_ref[...], ae_ref[...], preferred_element_type=F32)
                   + jnp.dot(sn_ref[...], bo_ref[...], preferred_element_type=F32))

    @pl.when(kk == pl.num_programs(3) - 1)
    def _():
        row = lax.broadcasted_iota(jnp.int32, acc_s.shape, 0)
        sign = jnp.where(row % 2 == 0, scale, -scale)
        y_ref[...] = (acc_s[...] + sign * mid_ref[0:1, :].astype(F32)).astype(y_ref.dtype)


def _seq_dft(cs, sn, ae, bo, a, B, S, scale):
    fw = a.shape[1]
    tm = _tile(S, 1024)
    tn = _tile(fw, 1024)
    tk = _tile(S // 2, 1024)
    nm, nk = S // tm, S // 2 // tk
    mid_rows = 16
    return pl.pallas_call(
        functools.partial(_seq_dft_kernel, scale=scale),
        grid=(B, nm, fw // tn, nk),
        in_specs=[pl.BlockSpec((tm, tk), lambda bb, i, j, k: (i, k)),
                  pl.BlockSpec((tm, tk), lambda bb, i, j, k: (i, k)),
                  pl.BlockSpec((tk, tn), lambda bb, i, j, k: (bb * nk + k, j)),
                  pl.BlockSpec((tk, tn), lambda bb, i, j, k: (bb * nk + k, j)),
                  pl.BlockSpec((mid_rows, tn),
                               lambda bb, i, j, k: ((bb * S + S // 2) // mid_rows, j))],
        out_specs=pl.BlockSpec((tm, tn), lambda bb, i, j, k: (bb * nm + i, j)),
        out_shape=jax.ShapeDtypeStruct((B * S, fw), BF16),
        scratch_shapes=[pltpu.VMEM((tm, tn), F32)],
        compiler_params=_cparams(("parallel", "parallel", "parallel", "arbitrary"), 48),
        name="seq_dft",
    )(cs, sn, ae, bo, a)


def _wo_kernel(ym_ref, yf_ref, gm_ref, gf_ref, w_ref, x_ref, mu_ref, rs_ref, lg_ref, lb_ref,
               pre_ref, mix_s, *, alpha, mw):
    @pl.when(pl.program_id(1) == 0)
    def _():
        ym = ym_ref[...].astype(F32)
        mix_s[:, :mw] = (ym * lax.rsqrt(jnp.mean(ym * ym, axis=-1, keepdims=True) + RMS_EPS)
                         * gm_ref[...]).astype(BF16)
        yf = yf_ref[...].astype(F32)
        mix_s[:, mw:] = (yf * lax.rsqrt(jnp.mean(yf * yf, axis=-1, keepdims=True) + RMS_EPS)
                         * gf_ref[...]).astype(BF16)

    h = (x_ref[...] - mu_ref[...]) * rs_ref[...] * lg_ref[...] + lb_ref[...]
    pre_ref[...] = alpha * h + jnp.dot(mix_s[...], w_ref[...], preferred_element_type=F32)


def _wo(ym, yf, gm, gf, w_o, x2, mu, rs, ln_g, ln_b, alpha):
    T, mw = ym.shape
    fw = yf.shape[1]
    D = w_o.shape[1]
    tm = _tile(T, 1024)
    tn = _tile(D, 512)
    return pl.pallas_call(
        functools.partial(_wo_kernel, alpha=alpha, mw=mw),
        grid=(T // tm, D // tn),
        in_specs=[pl.BlockSpec((tm, mw), lambda i, j: (i, 0)),
                  pl.BlockSpec((tm, fw), lambda i, j: (i, 0)),
                  pl.BlockSpec((1, mw), lambda i, j: (0, 0)),
                  pl.BlockSpec((1, fw), lambda i, j: (0, 0)),
                  pl.BlockSpec((mw + fw, tn), lambda i, j: (0, j)),
                  pl.BlockSpec((tm, tn), lambda i, j: (i, j)),
                  pl.BlockSpec((tm, 1), lambda i, j: (i, 0)),
                  pl.BlockSpec((tm, 1), lambda i, j: (i, 0)),
                  pl.BlockSpec((1, tn), lambda i, j: (0, j)),
                  pl.BlockSpec((1, tn), lambda i, j: (0, j))],
        out_specs=pl.BlockSpec((tm, tn), lambda i, j: (i, j)),
        out_shape=jax.ShapeDtypeStruct((T, D), F32),
        scratch_shapes=[pltpu.VMEM((tm, mw + fw), BF16)],
        compiler_params=_cparams(("parallel", "arbitrary"), 56),
        name="w_o_residual",
    )(ym, yf, gm.reshape(1, mw), gf.reshape(1, fw), w_o, x2, mu, rs,
      ln_g.reshape(1, D), ln_b.reshape(1, D))


def _pack_bf16_pair(lo, hi):
    lo_bits = pltpu.bitcast(lo.astype(BF16).astype(F32), jnp.uint32)
    hi_bits = pltpu.bitcast(hi.astype(BF16).astype(F32), jnp.uint32)
    return (lo_bits >> 16) | (hi_bits & jnp.uint32(0xFFFF0000))


def _unpack_bf16_pair(w):
    lo = pltpu.bitcast(w << 16, F32).astype(BF16)
    hi = pltpu.bitcast(w & jnp.uint32(0xFFFF0000), F32).astype(BF16)
    return lo, hi


def _ln1_router_kernel(pre_ref, g_ref, b_ref, rw_ref, rb_ref, x1_ref, x1p_ref, idx_ref, gate_ref):
    x = pre_ref[...]
    mu = jnp.mean(x, axis=-1, keepdims=True)
    xc = x - mu
    var = jnp.mean(xc * xc, axis=-1, keepdims=True)
    x1 = xc * lax.rsqrt(var + LN_EPS) * g_ref[...] + b_ref[...]
    x1_ref[...] = x1
    half = x1.shape[1] // 2
    x1p_ref[...] = _pack_bf16_pair(x1[:, :half], x1[:, half:])
    logits = lax.dot_general(rw_ref[...], x1, (((1,), (1,)), ((), ())),
                             precision=lax.Precision.HIGHEST, preferred_element_type=F32) + rb_ref[...]
    n_e = logits.shape[0]
    eidx = lax.broadcasted_iota(jnp.int32, logits.shape, 0)
    vals, idxs = [], []
    for _ in range(TOP_K):
        m = jnp.max(logits, axis=0, keepdims=True)
        sel = jnp.min(jnp.where(logits == m, eidx, n_e), axis=0, keepdims=True)
        logits = jnp.where(eidx == sel, -jnp.inf, logits)
        vals.append(m)
        idxs.append(sel)
    exps = [jnp.exp(v - vals[0]) for v in vals]
    denom = exps[0] + exps[1] + exps[2] + exps[3]
    for kk in range(TOP_K):
        idx_ref[kk:kk + 1, :] = idxs[kk]
        gate_ref[kk:kk + 1, :] = exps[kk] / denom


def _ln1_router(pre, g, b, router_w, router_b):
    T, D = pre.shape
    E = router_w.shape[1]
    tm = _tile(T, 256)
    return pl.pallas_call(
        _ln1_router_kernel,
        grid=(T // tm,),
        in_specs=[pl.BlockSpec((tm, D), lambda i: (i, 0)),
                  pl.BlockSpec((1, D), lambda i: (0, 0)),
                  pl.BlockSpec((1, D), lambda i: (0, 0)),
                  pl.BlockSpec((E, D), lambda i: (0, 0)),
                  pl.BlockSpec((E, 1), lambda i: (0, 0))],
        out_specs=[pl.BlockSpec((tm, D), lambda i: (i, 0)),
                   pl.BlockSpec((tm, D // 2), lambda i: (i, 0)),
                   pl.BlockSpec((TOP_K, tm), lambda i: (0, i)),
                   pl.BlockSpec((TOP_K, tm), lambda i: (0, i))],
        out_shape=[jax.ShapeDtypeStruct((T, D), F32),
                   jax.ShapeDtypeStruct((T, D // 2), jnp.uint32),
                   jax.ShapeDtypeStruct((TOP_K, T), jnp.int32),
                   jax.ShapeDtypeStruct((TOP_K, T), F32)],
        compiler_params=_cparams(("parallel",), 48),
        name="ln1_router",
    )(pre, g.reshape(1, D), b.reshape(1, D), router_w.T, router_b.reshape(E, 1))


def _row_copy(src_hbm, dst_ref, src_row, dst_row, sem):
    return pltpu.make_async_copy(src_hbm.at[pl.ds(src_row, 1)], dst_ref.at[pl.ds(dst_row, 1)], sem)


def _dispatch_kernel(dest_ref, pad_ref, npad_ref, nv_ref, x_ref, o_hbm, zero_s, sem, zsem, *, n_steps):
    b = pl.program_id(0)
    tt = x_ref.shape[0]

    def wait_rows(n):
        pltpu.make_async_copy(o_hbm.at[pl.ds(0, n)], o_hbm.at[pl.ds(0, n)], sem).wait()

    @pl.when(b < n_steps)
    def _():
        def issue(r, c):
            for kk in range(TOP_K):
                _row_copy(x_ref, o_hbm, r, dest_ref[(b * tt + r) * TOP_K + kk], sem).start()
            return c
        lax.fori_loop(0, tt, issue, 0, unroll=2)
        wait_rows(tt * TOP_K)

    @pl.when(b == n_steps)
    def _():
        n_pad = npad_ref[0]

        def issue_pad(i, c):
            _row_copy(x_ref, o_hbm, 0, pad_ref[i], sem).start()
            return c
        lax.fori_loop(0, n_pad, issue_pad, 0)

        def wait_pad(i, c):
            wait_rows(1)
            return c
        lax.fori_loop(0, n_pad, wait_pad, 0)

        zero_s[...] = jnp.zeros(zero_s.shape, zero_s.dtype)
        n_sub = MOE_CHUNK // MOE_SUB

        def zero_copy(i):
            return pltpu.make_async_copy(zero_s, o_hbm.at[pl.ds(pl.multiple_of(i * MOE_SUB, MOE_SUB), MOE_SUB)], zsem)

        def empty(i):
            return (i % n_sub) * MOE_SUB >= nv_ref[i // n_sub]

        def issue_zero(i, c):
            @pl.when(empty(i))
            def _():
                zero_copy(i).start()
            return c
        lax.fori_loop(0, nv_ref.shape[0] * n_sub, issue_zero, 0)

        def wait_zero(i, c):
            @pl.when(empty(i))
            def _():
                zero_copy(i).wait()
            return c
        lax.fori_loop(0, nv_ref.shape[0] * n_sub, wait_zero, 0)


def _dispatch(dest, pad_slots, n_pad, chunk_nv, x1p):
    T, W = x1p.shape
    tt = _tile(T, DISPATCH_TOKENS)
    n_steps = T // tt
    grid_spec = pltpu.PrefetchScalarGridSpec(
        num_scalar_prefetch=4,
        grid=(n_steps + 1,),
        in_specs=[pl.BlockSpec((tt, W), lambda b, d, p, n, v: (jnp.minimum(b, n_steps - 1), 0))],
        out_specs=pl.BlockSpec(memory_space=pl.ANY),
        scratch_shapes=[pltpu.VMEM((MOE_SUB, W), x1p.dtype),
                        pltpu.SemaphoreType.DMA(()), pltpu.SemaphoreType.DMA(())],
    )
    return pl.pallas_call(
        functools.partial(_dispatch_kernel, n_steps=n_steps),
        grid_spec=grid_spec,
        out_shape=jax.ShapeDtypeStruct((chunk_nv.shape[0] * MOE_CHUNK, W), x1p.dtype),
        compiler_params=_cparams(("arbitrary",), 32),
        name="moe_dispatch",
    )(dest, pad_slots, n_pad, chunk_nv, x1p)


def _expert_up_kernel(ce_ref, nv_ref, nu_ref, x_ref, wg_ref, wu_ref, bg_ref, bu_ref, h_ref,
                      wgb_s, wub_s):
    c = pl.program_id(0)
    nv = nv_ref[c]

    @pl.when(nv > 0)
    def _():
        wgb_s[...] = wg_ref[...].astype(BF16)
        wub_s[...] = wu_ref[...].astype(BF16)

    for s in range(MOE_CHUNK // MOE_SUB):
        rows = slice(s * MOE_SUB, (s + 1) * MOE_SUB)
        active = s * MOE_SUB < nv

        @pl.when(active)
        def _():
            x_lo, x_hi = _unpack_bf16_pair(x_ref[rows, :])
            half = x_lo.shape[1]

            def proj(w_s):
                return (jnp.dot(x_lo, w_s[:half, :], preferred_element_type=F32)
                        + jnp.dot(x_hi, w_s[half:, :], preferred_element_type=F32))
            hg = jnp.minimum(proj(wgb_s) + bg_ref[...], SWIGLU_LIMIT)
            hu = jnp.clip(proj(wub_s) + bu_ref[...], -SWIGLU_LIMIT, SWIGLU_LIMIT)
            act = (hu + 1.0) * (hg * jax.nn.sigmoid(SWIGLU_ALPHA * hg))
            h_ref[rows, :] = act.astype(h_ref.dtype)

        @pl.when(jnp.logical_not(active))
        def _():
            h_ref[rows, :] = jnp.zeros((MOE_SUB, h_ref.shape[1]), h_ref.dtype)


def _expert_up(chunk_e, chunk_nv, n_used, xs, w_gate, w_up, b_gate, b_up, n_chunks):
    E, D, F = w_gate.shape
    tf = _tile(F, 256)
    nf = F // tf

    def used(c, nu):
        return jnp.minimum(c, nu[0] - 1)

    def jeff(c, j, nu):
        return jnp.where(c < nu[0], j, nf - 1)

    grid_spec = pltpu.PrefetchScalarGridSpec(
        num_scalar_prefetch=3,
        grid=(n_chunks, nf),
        in_specs=[pl.BlockSpec((MOE_CHUNK, D // 2), lambda c, j, ce, nv, nu: (used(c, nu), 0)),
                  pl.BlockSpec((None, D, tf), lambda c, j, ce, nv, nu: (ce[c], 0, jeff(c, j, nu))),
                  pl.BlockSpec((None, D, tf), lambda c, j, ce, nv, nu: (ce[c], 0, jeff(c, j, nu))),
                  pl.BlockSpec((None, 1, tf), lambda c, j, ce, nv, nu: (ce[c], 0, jeff(c, j, nu))),
                  pl.BlockSpec((None, 1, tf), lambda c, j, ce, nv, nu: (ce[c], 0, jeff(c, j, nu)))],
        out_specs=pl.BlockSpec((MOE_CHUNK, tf), lambda c, j, ce, nv, nu: (c, j)),
        scratch_shapes=[pltpu.VMEM((D, tf), BF16),
                        pltpu.VMEM((D, tf), BF16)],
    )
    return pl.pallas_call(
        _expert_up_kernel,
        grid_spec=grid_spec,
        out_shape=jax.ShapeDtypeStruct((n_chunks * MOE_CHUNK, F), BF16),
        compiler_params=_cparams(("arbitrary", "arbitrary"), 60),
        name="expert_gate_up",
    )(chunk_e, chunk_nv, n_used, xs, w_gate, w_up, b_gate.reshape(E, 1, F), b_up.reshape(E, 1, F))


def _expert_down_kernel(ce_ref, nv_ref, nu_ref, h_ref, wd_ref, bd_ref, y_ref, wdb_s):
    c = pl.program_id(0)
    nv = nv_ref[c]

    @pl.when(nv > 0)
    def _():
        wdb_s[...] = wd_ref[...].astype(BF16)

    for s in range(MOE_CHUNK // MOE_SUB):
        rows = slice(s * MOE_SUB, (s + 1) * MOE_SUB)
        active = s * MOE_SUB < nv

        @pl.when(active)
        def _():
            y_ref[rows, :] = jnp.dot(h_ref[rows, :], wdb_s[...], preferred_element_type=F32) + bd_ref[...]

        @pl.when(jnp.logical_not(active))
        def _():
            y_ref[rows, :] = jnp.zeros((MOE_SUB, y_ref.shape[1]), y_ref.dtype)


def _expert_down(chunk_e, chunk_nv, n_used, hmid, w_down, b_down, n_chunks):
    E, F, D = w_down.shape
    tn = _tile(D, 512)
    nn = D // tn

    def used(c, nu):
        return jnp.minimum(c, nu[0] - 1)

    def jeff(c, j, nu):
        return jnp.where(c < nu[0], j, nn - 1)

    grid_spec = pltpu.PrefetchScalarGridSpec(
        num_scalar_prefetch=3,
        grid=(n_chunks, nn),
        in_specs=[pl.BlockSpec((MOE_CHUNK, F), lambda c, j, ce, nv, nu: (used(c, nu), 0)),
                  pl.BlockSpec((None, F, tn), lambda c, j, ce, nv, nu: (ce[c], 0, jeff(c, j, nu))),
                  pl.BlockSpec((None, 1, tn), lambda c, j, ce, nv, nu: (ce[c], 0, jeff(c, j, nu)))],
        out_specs=pl.BlockSpec((MOE_CHUNK, tn), lambda c, j, ce, nv, nu: (c, j)),
        scratch_shapes=[pltpu.VMEM((F, tn), BF16)],
    )
    return pl.pallas_call(
        _expert_down_kernel,
        grid_spec=grid_spec,
        out_shape=jax.ShapeDtypeStruct((n_chunks * MOE_CHUNK, D), F32),
        compiler_params=_cparams(("arbitrary", "arbitrary"), 56),
        name="expert_down",
    )(chunk_e, chunk_nv, n_used, hmid, w_down, b_down.reshape(E, 1, D))


def _combine_kernel(dest_ref, y_hbm, x1_ref, gate_ref, g_ref, b_ref, o_ref, buf, sem, *, alpha):
    i = pl.program_id(0)
    n = pl.num_programs(0)
    tt = o_ref.shape[0]

    def issue_step(step, slot):
        def issue(r, carry):
            for kk in range(TOP_K):
                _row_copy(y_hbm, buf.at[slot, kk], dest_ref[(step * tt + r) * TOP_K + kk], r,
                          sem.at[slot]).start()
            return carry
        lax.fori_loop(0, tt, issue, 0, unroll=4)

    @pl.when(i == 0)
    def _():
        issue_step(0, 0)

    slot = i % 2

    @pl.when(i + 1 < n)
    def _():
        issue_step(i + 1, 1 - slot)

    for kk in range(TOP_K):
        pltpu.make_async_copy(y_hbm.at[pl.ds(0, tt)], buf.at[slot, kk], sem.at[slot]).wait()
    gates = gate_ref[...]
    y = gates[:, 0:1] * buf[slot, 0]
    for kk in range(1, TOP_K):
        y = y + gates[:, kk:kk + 1] * buf[slot, kk]
    z = alpha * x1_ref[...] + y
    mu = jnp.mean(z, axis=-1, keepdims=True)
    zc = z - mu
    var = jnp.mean(zc * zc, axis=-1, keepdims=True)
    o_ref[...] = zc * lax.rsqrt(var + LN_EPS) * g_ref[...] + b_ref[...]


def _combine(dest, yslots, x1, gates_tk, g, b, alpha):
    T, D = x1.shape
    tt = _tile(T, COMBINE_TOKENS)
    grid_spec = pltpu.PrefetchScalarGridSpec(
        num_scalar_prefetch=1,
        grid=(T // tt,),
        in_specs=[pl.BlockSpec(memory_space=pl.ANY),
                  pl.BlockSpec((tt, D), lambda i, d: (i, 0)),
                  pl.BlockSpec((tt, TOP_K), lambda i, d: (i, 0)),
                  pl.BlockSpec((1, D), lambda i, d: (0, 0)),
                  pl.BlockSpec((1, D), lambda i, d: (0, 0))],
        out_specs=pl.BlockSpec((tt, D), lambda i, d: (i, 0)),
        scratch_shapes=[pltpu.VMEM((2, TOP_K, tt, D), F32), pltpu.SemaphoreType.DMA((2,))],
    )
    return pl.pallas_call(
        functools.partial(_combine_kernel, alpha=alpha),
        grid_spec=grid_spec,
        out_shape=jax.ShapeDtypeStruct((T, D), F32),
        compiler_params=_cparams(("arbitrary",), 32),
        name="moe_combine_ln2",
    )(dest, yslots, x1, gates_tk, g.reshape(1, D), b.reshape(1, D))


def _routing_tables(top_idx, n_experts, n_chunks):
    T = top_idx.shape[1]
    M = T * TOP_K
    flat_e = top_idx.T.reshape(M)
    onehot = (flat_e[:, None] == jnp.arange(n_experts, dtype=jnp.int32)[None, :]).astype(jnp.int32)
    csum = jnp.cumsum(onehot, axis=0)
    rank = jnp.sum(csum * onehot, axis=1) - 1
    counts = csum[-1]
    chunks_e = (counts + MOE_CHUNK - 1) // MOE_CHUNK
    chunk_end = jnp.cumsum(chunks_e)
    chunk_start = chunk_end - chunks_e
    n_used = chunk_end[-1]
    dest = chunk_start[flat_e] * MOE_CHUNK + rank
    P = n_chunks * MOE_CHUNK
    pad_idx = counts[:, None] + jnp.arange(MOE_SUB, dtype=jnp.int32)[None, :]
    padded = (counts + MOE_SUB - 1) // MOE_SUB * MOE_SUB
    pad_slots = jnp.where(pad_idx < padded[:, None], chunk_start[:, None] * MOE_CHUNK + pad_idx, P)
    pad_slots = jnp.sort(pad_slots.reshape(-1)).astype(jnp.int32)
    n_pad = jnp.sum(padded - counts).astype(jnp.int32).reshape(1)
    cid = jnp.arange(n_chunks, dtype=jnp.int32)
    chunk_e = jnp.minimum(jnp.searchsorted(chunk_end, cid, side='right'), n_experts - 1).astype(jnp.int32)
    last_e = chunk_e[jnp.maximum(n_used - 1, 0)]
    chunk_e = jnp.where(cid < n_used, chunk_e, last_e)
    chunk_nv = jnp.where(cid < n_used,
                         jnp.clip(counts[chunk_e] - (cid - chunk_start[chunk_e]) * MOE_CHUNK, 0, MOE_CHUNK),
                         0).astype(jnp.int32)
    return (dest.astype(jnp.int32), pad_slots, n_pad, chunk_e, chunk_nv,
            n_used.astype(jnp.int32).reshape(1))


def kernel(x, positions, ln_in_g, ln_in_b, w_in, q_a_norm_g, w_q_b, kv_a_norm_g, w_kv_b, mla_out_norm_g, fourier_out_norm_g, w_o, ln1_g, ln1_b, router_w, router_b, w_gate, b_gate, w_up, b_up, w_down, b_down, ln2_g, ln2_b):
    B, S, D = x.shape
    T = B * S
    depth = w_in.shape[0]
    assert depth == 1, "single-layer trunk only"
    qr = q_a_norm_g.shape[1]
    kvr = kv_a_norm_g.shape[1]
    H = w_q_b.shape[2] // (QK_NOPE_DIM + QK_ROPE_DIM)
    fw = fourier_out_norm_g.shape[1]
    E = router_w.shape[2]
    assert H % HEADS_PER_TILE == 0 and (qr + kvr) % LANES == 0 and S % (8 * DFT_ROW_SPLIT) == 0
    alpha = (2.0 * depth) ** 0.25

    inv_freq = ROPE_THETA ** (-jnp.arange(0, QK_ROPE_DIM, 2, dtype=F32) / QK_ROPE_DIM)
    ang = positions.astype(F32)[..., None] * inv_freq
    cos4 = jnp.tile(jnp.cos(ang), (1, 1, 2 * LANES // QK_ROPE_DIM)).reshape(T, LANES)
    sin4 = jnp.tile(jnp.sin(ang), (1, 1, 2 * LANES // QK_ROPE_DIM)).reshape(T, LANES)

    rope_end = qr + kvr + QK_ROPE_DIM
    w_a = w_in[0, :, :qr + kvr + LANES].astype(BF16)
    w_f = w_in[0, :, rope_end:].astype(BF16)
    wq = w_q_b[0].reshape(qr, H // HEADS_PER_TILE, HEADS_PER_TILE, QK_NOPE_DIM + QK_ROPE_DIM)
    wq_perm = jnp.concatenate(
        [wq[..., :QK_NOPE_DIM].reshape(qr, H // HEADS_PER_TILE, HEADS_PER_TILE * QK_NOPE_DIM),
         wq[..., QK_NOPE_DIM:].reshape(qr, H // HEADS_PER_TILE, HEADS_PER_TILE * QK_ROPE_DIM)],
        axis=-1).reshape(qr, H * (QK_NOPE_DIM + QK_ROPE_DIM)).astype(BF16)
    w_kv = w_kv_b[0].astype(BF16)
    w_o_b = w_o[0].astype(BF16)
    ch = jnp.arange(FOURIER_GROUP_DIM, dtype=jnp.int32)
    ang_c = (2.0 * math.pi / FOURIER_GROUP_DIM) * ((ch[:, None] * ch[None, :]) % FOURIER_GROUP_DIM).astype(F32)
    cs_tab = jnp.concatenate([jnp.cos(ang_c), jnp.sin(ang_c)], axis=1).astype(BF16)

    x2 = x.reshape(T, D)
    hb, mu, rs = _ln_in(x2, ln_in_g, ln_in_b)

    cq, ckv, kpe2 = _inproj_a(hb, w_a, q_a_norm_g[0], kv_a_norm_g[0], cos4, sin4)
    qscale = (QK_NOPE_DIM + QK_ROPE_DIM) ** -0.5 * LOG2E
    q = _q_up(cq, wq_perm, cos4, sin4, qscale)
    k, v = _kv_up(ckv, w_kv, kpe2)
    y_mla = _attention(q, k, v, B, S, H)

    fa, fb = _inproj_f(hb, w_f, cs_tab)
    cs_mat, sn_mat = _dft_matrices(S, FOURIER_GROUP_DIM)
    fae, fbo = _dft_fold(fa, fb, B, S)
    y_f = _seq_dft(cs_mat, sn_mat, fae, fbo, fa, B, S, 1.0 / math.sqrt(S * FOURIER_GROUP_DIM))

    pre = _wo(y_mla, y_f, mla_out_norm_g[0], fourier_out_norm_g[0], w_o_b, x2, mu, rs,
              ln_in_g, ln_in_b, alpha)
    x1, x1p, top_idx, gates = _ln1_router(pre, ln1_g[0], ln1_b[0], router_w[0], router_b[0])

    n_chunks = -(-T * TOP_K // MOE_CHUNK) + E
    dest, pad_slots, n_pad, chunk_e, chunk_nv, n_used = _routing_tables(top_idx, E, n_chunks)
    xs = _dispatch(dest, pad_slots, n_pad, chunk_nv, x1p)
    hmid = _expert_up(chunk_e, chunk_nv, n_used, xs, w_gate[0], w_up[0], b_gate[0], b_up[0], n_chunks)
    yslots = _expert_down(chunk_e, chunk_nv, n_used, hmid, w_down[0], b_down[0], n_chunks)
    out = _combine(dest, yslots, x1, gates.T, ln2_g[0], ln2_b[0], alpha)
    return out.reshape(B, S, D)
```

```python
import functools
import math

import jax
import jax.numpy as jnp
from jax import lax
from jax.experimental import pallas as pl
from jax.experimental.pallas import tpu as pltpu

F32 = jnp.float32
BF16 = jnp.bfloat16

V_HEAD_DIM = 128
QK_NOPE_DIM = 128
QK_ROPE_DIM = 64
QK_PAD_DIM = 256
V_PAD_DIM = 256
ROPE_THETA = 10000.0
FOURIER_GROUP_DIM = 128
TOP_K = 4
SWIGLU_LIMIT = 7.0
SWIGLU_ALPHA = 1.702
LN_EPS = 1e-5
RMS_EPS = 1e-6
LOG2E = 1.4426950408889634

LANES = 128
V7X_VMEM_BYTES = 64 * 1024 * 1024
HEADS_PER_TILE = 4
DFT_ROW_SPLIT = 64
ATTN_TQ = 1024
ATTN_TKV = 1024

MOE_CHUNK = 1280
MOE_SUB = 256
DISPATCH_TOKENS = 512
COMBINE_TOKENS = 64


def _cparams(semantics, vmem_mb):
    return pltpu.CompilerParams(dimension_semantics=semantics,
                                vmem_limit_bytes=min(vmem_mb * 1024 * 1024, V7X_VMEM_BYTES - (4 << 20)))


def _tile(dim, pref):
    t = min(dim, pref)
    while dim % t:
        t //= 2
    return t


def _ln_in_kernel(x_ref, g_ref, b_ref, hb_ref, mu_ref, rs_ref):
    x = x_ref[...]
    mu = jnp.mean(x, axis=-1, keepdims=True)
    xc = x - mu
    var = jnp.mean(xc * xc, axis=-1, keepdims=True)
    rs = lax.rsqrt(var + LN_EPS)
    hb_ref[...] = (xc * rs * g_ref[...] + b_ref[...]).astype(BF16)
    mu_ref[...] = mu
    rs_ref[...] = rs


def _ln_in(x2, g, b):
    T, D = x2.shape
    tm = _tile(T, 256)
    return pl.pallas_call(
        _ln_in_kernel,
        grid=(T // tm,),
        in_specs=[pl.BlockSpec((tm, D), lambda i: (i, 0)),
                  pl.BlockSpec((1, D), lambda i: (0, 0)),
                  pl.BlockSpec((1, D), lambda i: (0, 0))],
        out_specs=[pl.BlockSpec((tm, D), lambda i: (i, 0)),
                   pl.BlockSpec((tm, 1), lambda i: (i, 0)),
                   pl.BlockSpec((tm, 1), lambda i: (i, 0))],
        out_shape=[jax.ShapeDtypeStruct((T, D), BF16),
                   jax.ShapeDtypeStruct((T, 1), F32),
                   jax.ShapeDtypeStruct((T, 1), F32)],
        compiler_params=_cparams(("parallel",), 40),
        name="ln_in",
    )(x2, g.reshape(1, D), b.reshape(1, D))


def _rope128(p, cos4, sin4):
    lane = lax.broadcasted_iota(jnp.int32, p.shape, 1)
    first_half = (lane % QK_ROPE_DIM) < (QK_ROPE_DIM // 2)
    rot = jnp.where(first_half, -pltpu.roll(p, LANES - QK_ROPE_DIM // 2, 1),
                    pltpu.roll(p, QK_ROPE_DIM // 2, 1))
    return p * cos4 + rot * sin4


def _inproj_a_kernel(h_ref, w_ref, gq_ref, gkv_ref, cos_ref, sin_ref,
                     cq_ref, ckv_ref, kpe_ref, *, qr, kvr):
    acc = jnp.dot(h_ref[...], w_ref[...], preferred_element_type=F32)
    cq = acc[:, :qr]
    cq_ref[...] = (cq * lax.rsqrt(jnp.mean(cq * cq, axis=-1, keepdims=True) + RMS_EPS)
                   * gq_ref[...]).astype(BF16)
    ckv = acc[:, qr:qr + kvr]
    ckv_ref[...] = (ckv * lax.rsqrt(jnp.mean(ckv * ckv, axis=-1, keepdims=True) + RMS_EPS)
                    * gkv_ref[...]).astype(BF16)
    roped = _rope128(acc[:, qr + kvr:qr + kvr + LANES], cos_ref[...], sin_ref[...])
    lane = lax.broadcasted_iota(jnp.int32, roped.shape, 1)
    even = jnp.where(lane < QK_ROPE_DIM, roped, 0.0)
    kpe_ref[:, :LANES] = even.astype(BF16)
    kpe_ref[:, LANES:] = pltpu.roll(even, QK_ROPE_DIM, 1).astype(BF16)


def _inproj_a(hb, w_a, gq, gkv, cos4, sin4):
    T, D = hb.shape
    qr, kvr = gq.shape[0], gkv.shape[0]
    wa = w_a.shape[1]
    tm = _tile(T, 512)
    return pl.pallas_call(
        functools.partial(_inproj_a_kernel, qr=qr, kvr=kvr),
        grid=(T // tm,),
        in_specs=[pl.BlockSpec((tm, D), lambda i: (i, 0)),
                  pl.BlockSpec((D, wa), lambda i: (0, 0)),
                  pl.BlockSpec((1, qr), lambda i: (0, 0)),
                  pl.BlockSpec((1, kvr), lambda i: (0, 0)),
                  pl.BlockSpec((tm, LANES), lambda i: (i, 0)),
                  pl.BlockSpec((tm, LANES), lambda i: (i, 0))],
        out_specs=[pl.BlockSpec((tm, qr), lambda i: (i, 0)),
                   pl.BlockSpec((tm, kvr), lambda i: (i, 0)),
                   pl.BlockSpec((tm, 2 * LANES), lambda i: (i, 0))],
        out_shape=[jax.ShapeDtypeStruct((T, qr), BF16),
                   jax.ShapeDtypeStruct((T, kvr), BF16),
                   jax.ShapeDtypeStruct((T, 2 * LANES), BF16)],
        compiler_params=_cparams(("parallel",), 56),
        name="inproj_mla",
    )(hb, w_a, gq.reshape(1, qr), gkv.reshape(1, kvr), cos4, sin4)


def _q_up_kernel(c_ref, w_ref, cos_ref, sin_ref, q_ref, *, qscale):
    acc = jnp.dot(c_ref[...], w_ref[...], preferred_element_type=F32)
    nope_w = HEADS_PER_TILE * QK_NOPE_DIM
    lane = lax.broadcasted_iota(jnp.int32, (acc.shape[0], LANES), 1)
    for pair in range(HEADS_PER_TILE // 2):
        roped = _rope128(acc[:, nope_w + pair * LANES:nope_w + (pair + 1) * LANES],
                         cos_ref[...], sin_ref[...]) * qscale
        for par in range(2):
            j = 2 * pair + par
            keep = (lane < QK_ROPE_DIM) if par == 0 else (lane >= QK_ROPE_DIM)
            base = j * QK_PAD_DIM
            q_ref[:, base:base + QK_NOPE_DIM] = (
                acc[:, j * QK_NOPE_DIM:(j + 1) * QK_NOPE_DIM] * qscale).astype(BF16)
            q_ref[:, base + QK_NOPE_DIM:base + QK_PAD_DIM] = jnp.where(keep, roped, 0.0).astype(BF16)


def _q_up(cq, wq_perm, cos4, sin4, qscale):
    T, qr = cq.shape
    n_tiles = wq_perm.shape[1] // (HEADS_PER_TILE * (QK_NOPE_DIM + QK_ROPE_DIM))
    tw = HEADS_PER_TILE * (QK_NOPE_DIM + QK_ROPE_DIM)
    to = HEADS_PER_TILE * QK_PAD_DIM
    tm = _tile(T, 1024)
    return pl.pallas_call(
        functools.partial(_q_up_kernel, qscale=qscale),
        grid=(T // tm, n_tiles),
        in_specs=[pl.BlockSpec((tm, qr), lambda i, j: (i, 0)),
                  pl.BlockSpec((qr, tw), lambda i, j: (0, j)),
                  pl.BlockSpec((tm, LANES), lambda i, j: (i, 0)),
                  pl.BlockSpec((tm, LANES), lambda i, j: (i, 0))],
        out_specs=pl.BlockSpec((tm, to), lambda i, j: (i, j)),
        out_shape=jax.ShapeDtypeStruct((T, n_tiles * to), BF16),
        compiler_params=_cparams(("parallel", "arbitrary"), 40),
        name="q_up",
    )(cq, wq_perm, cos4, sin4)


def _kv_up_kernel(c_ref, w_ref, kpe_ref, k_ref, v_ref):
    acc = jnp.dot(c_ref[...], w_ref[...], preferred_element_type=F32)
    for j in range(HEADS_PER_TILE):
        src = j * (QK_NOPE_DIM + V_HEAD_DIM)
        k_ref[:, j * QK_PAD_DIM:j * QK_PAD_DIM + QK_NOPE_DIM] = acc[:, src:src + QK_NOPE_DIM].astype(BF16)
        par = j % 2
        k_ref[:, j * QK_PAD_DIM + QK_NOPE_DIM:(j + 1) * QK_PAD_DIM] = kpe_ref[:, par * LANES:(par + 1) * LANES]
        v_ref[:, j * V_PAD_DIM:j * V_PAD_DIM + V_HEAD_DIM] = (
            acc[:, src + QK_NOPE_DIM:src + QK_NOPE_DIM + V_HEAD_DIM].astype(BF16))
        v_ref[:, j * V_PAD_DIM + V_HEAD_DIM:(j + 1) * V_PAD_DIM] = jnp.ones(
            (acc.shape[0], V_PAD_DIM - V_HEAD_DIM), BF16)


def _kv_up(ckv, w_kv, kpe2):
    T, kvr = ckv.shape
    tw = HEADS_PER_TILE * (QK_NOPE_DIM + V_HEAD_DIM)
    n_tiles = w_kv.shape[1] // tw
    tm = _tile(T, 1024)
    return pl.pallas_call(
        _kv_up_kernel,
        grid=(T // tm, n_tiles),
        in_specs=[pl.BlockSpec((tm, kvr), lambda i, j: (i, 0)),
                  pl.BlockSpec((kvr, tw), lambda i, j: (0, j)),
                  pl.BlockSpec((tm, 2 * LANES), lambda i, j: (i, 0))],
        out_specs=[pl.BlockSpec((tm, HEADS_PER_TILE * QK_PAD_DIM), lambda i, j: (i, j)),
                   pl.BlockSpec((tm, HEADS_PER_TILE * V_PAD_DIM), lambda i, j: (i, j))],
        out_shape=[jax.ShapeDtypeStruct((T, n_tiles * HEADS_PER_TILE * QK_PAD_DIM), BF16),
                   jax.ShapeDtypeStruct((T, n_tiles * HEADS_PER_TILE * V_PAD_DIM), BF16)],
        compiler_params=_cparams(("parallel", "arbitrary"), 40),
        name="kv_up",
    )(ckv, w_kv, kpe2)


def _lane_tile(x, reps):
    return jnp.concatenate([x] * reps, axis=1)


def _attn_kernel(q_ref, k_ref, v_ref, o_ref, m_s, acc_s, s0_s, s1_s, x0_s, x1_s, p0_s, p1_s, a0_s, a1_s,
                 *, tkv):
    n_kv = k_ref.shape[0] // tkv
    s_buf, x_buf, p_buf, a_buf = (s0_s, s1_s), (x0_s, x1_s), (p0_s, p1_s), (a0_s, a1_s)
    m_s[...] = jnp.full(m_s.shape, -jnp.inf, F32)
    acc_s[...] = jnp.zeros(acc_s.shape, F32)

    def scores(i, slot):
        off = pl.multiple_of(i * tkv, tkv)
        s = lax.dot_general(q_ref[...], k_ref[pl.ds(off, tkv), :], (((1,), (1,)), ((), ())),
                            preferred_element_type=F32)
        s_buf[slot][...] = s
        x_buf[slot][...] = jnp.broadcast_to(jnp.max(s, axis=1, keepdims=True), x_buf[slot].shape)

    def probs(slot):
        m_prev = m_s[...]
        m_new = jnp.maximum(m_prev, x_buf[slot][...])
        m_s[...] = m_new
        a_buf[slot][...] = jnp.exp2(m_prev - m_new)
        p_buf[slot][...] = jnp.exp2(s_buf[slot][...] - _lane_tile(m_new, tkv // LANES)).astype(BF16)

    def values(i, slot):
        off = pl.multiple_of(i * tkv, tkv)
        acc_s[...] = (_lane_tile(a_buf[slot][...], V_PAD_DIM // LANES) * acc_s[...]
                      + jnp.dot(p_buf[slot][...], v_ref[pl.ds(off, tkv), :], preferred_element_type=F32))

    def trip(i, slot):
        scores(i + 1, 1 - slot)
        values(i - 1, 1 - slot)
        probs(slot)

    scores(0, 0)
    scores(1, 1)
    probs(0)
    for i in range(1, n_kv - 1):
        trip(i, i % 2)
    values(n_kv - 2, 0)
    probs(1)
    values(n_kv - 1, 1)
    acc = acc_s[...]
    o_ref[...] = (acc[:, :V_HEAD_DIM] / acc[:, V_HEAD_DIM:]).astype(o_ref.dtype)


def _attention(q, k, v, B, S, H):
    T = B * S
    tq = _tile(S, ATTN_TQ)
    tkv = _tile(S, min(ATTN_TKV, S // 2))
    assert (S // tkv) % 2 == 0, "the key-chunk pipeline is unrolled in pairs"
    nq = S // tq
    return pl.pallas_call(
        functools.partial(_attn_kernel, tkv=tkv),
        grid=(B, H, nq),
        in_specs=[pl.BlockSpec((tq, QK_PAD_DIM), lambda b, h, i: (b * nq + i, h)),
                  pl.BlockSpec((S, QK_PAD_DIM), lambda b, h, i: (b, h)),
                  pl.BlockSpec((S, V_PAD_DIM), lambda b, h, i: (b, h))],
        out_specs=pl.BlockSpec((tq, V_HEAD_DIM), lambda b, h, i: (b * nq + i, h)),
        out_shape=jax.ShapeDtypeStruct((T, H * V_HEAD_DIM), BF16),
        scratch_shapes=[pltpu.VMEM((tq, LANES), F32), pltpu.VMEM((tq, V_PAD_DIM), F32),
                        pltpu.VMEM((tq, tkv), F32), pltpu.VMEM((tq, tkv), F32),
                        pltpu.VMEM((tq, LANES), F32), pltpu.VMEM((tq, LANES), F32),
                        pltpu.VMEM((tq, tkv), BF16), pltpu.VMEM((tq, tkv), BF16),
                        pltpu.VMEM((tq, LANES), F32), pltpu.VMEM((tq, LANES), F32)],
        compiler_params=_cparams(("parallel", "parallel", "arbitrary"), 40),
        name="mla_attention",
    )(q, k, v)


def _inproj_f_kernel(h_ref, w_ref, cs_ref, a_ref, b_ref, *, groups):
    acc = jnp.dot(h_ref[...], w_ref[...], preferred_element_type=F32)
    C = FOURIER_GROUP_DIM
    for g in range(groups):
        ab = jnp.dot(acc[:, g * C:(g + 1) * C].astype(BF16), cs_ref[...], preferred_element_type=F32)
        a_ref[:, g * C:(g + 1) * C] = ab[:, :C].astype(BF16)
        b_ref[:, g * C:(g + 1) * C] = ab[:, C:].astype(BF16)


def _inproj_f(hb, w_f, cs_tab):
    T, D = hb.shape
    fw = w_f.shape[1]
    tm = _tile(T, 1024)
    tn = _tile(fw, 512)
    return pl.pallas_call(
        functools.partial(_inproj_f_kernel, groups=tn // FOURIER_GROUP_DIM),
        grid=(T // tm, fw // tn),
        in_specs=[pl.BlockSpec((tm, D), lambda i, j: (i, 0)),
                  pl.BlockSpec((D, tn), lambda i, j: (0, j)),
                  pl.BlockSpec((FOURIER_GROUP_DIM, 2 * FOURIER_GROUP_DIM), lambda i, j: (0, 0))],
        out_specs=[pl.BlockSpec((tm, tn), lambda i, j: (i, j)),
                   pl.BlockSpec((tm, tn), lambda i, j: (i, j))],
        out_shape=[jax.ShapeDtypeStruct((T, fw), BF16), jax.ShapeDtypeStruct((T, fw), BF16)],
        compiler_params=_cparams(("parallel", "arbitrary"), 48),
        name="inproj_fourier",
    )(hb, w_f, cs_tab)


def _dft_gen_kernel(tac_ref, tas_ref, tbc_ref, tbs_ref, cs_ref, sn_ref):
    tbc = tbc_ref[...]
    tbs = tbs_ref[...]
    for aa in range(tac_ref.shape[0]):
        ca = tac_ref[aa:aa + 1, :]
        sa = tas_ref[aa:aa + 1, :]
        rows = slice(aa * DFT_ROW_SPLIT, (aa + 1) * DFT_ROW_SPLIT)
        cs_ref[rows, :] = (ca * tbc - sa * tbs).astype(BF16)
        sn_ref[rows, :] = (-(sa * tbc + ca * tbs)).astype(BF16)


def _dft_matrices(S, n_chan):
    na = S // DFT_ROW_SPLIT
    scale = 1.0 / math.sqrt(S * n_chan)
    col = jnp.arange(S // 2, dtype=jnp.int32)[None, :]
    ang_a = (2.0 * math.pi / na) * ((jnp.arange(na, dtype=jnp.int32)[:, None] * col) % na).astype(F32)
    ang_b = (2.0 * math.pi / S) * ((jnp.arange(DFT_ROW_SPLIT, dtype=jnp.int32)[:, None] * col) % S).astype(F32)
    tac, tas = jnp.cos(ang_a), jnp.sin(ang_a)
    tbc, tbs = scale * jnp.cos(ang_b), scale * jnp.sin(ang_b)
    ta = 8
    tc = _tile(S // 2, 2048)
    return pl.pallas_call(
        _dft_gen_kernel,
        grid=(na // ta, S // 2 // tc),
        in_specs=[pl.BlockSpec((ta, tc), lambda i, j: (i, j)),
                  pl.BlockSpec((ta, tc), lambda i, j: (i, j)),
                  pl.BlockSpec((DFT_ROW_SPLIT, tc), lambda i, j: (0, j)),
                  pl.BlockSpec((DFT_ROW_SPLIT, tc), lambda i, j: (0, j))],
        out_specs=[pl.BlockSpec((ta * DFT_ROW_SPLIT, tc), lambda i, j: (i, j)),
                   pl.BlockSpec((ta * DFT_ROW_SPLIT, tc), lambda i, j: (i, j))],
        out_shape=[jax.ShapeDtypeStruct((S, S // 2), BF16), jax.ShapeDtypeStruct((S, S // 2), BF16)],
        compiler_params=_cparams(("parallel", "parallel"), 40),
        name="dft_matrices",
    )(tac, tas, tbc, tbs)


def _dft_fold_kernel(a_ref, am_ref, an_ref, b_ref, bm_ref, bn_ref, ae_ref, bo_ref):
    tm = a_ref.shape[0]
    r = lax.broadcasted_iota(jnp.int32, (tm, tm), 0)
    c = lax.broadcasted_iota(jnp.int32, (tm, tm), 1)
    rev = jnp.where(r + c == tm, 1.0, 0.0).astype(BF16)
    has_next = (pl.program_id(1) > 0).astype(F32)
    first = (jnp.where(r + c == 0, 1.0, 0.0) * has_next).astype(BF16)

    def mirrored(m_ref, n_ref):
        return (jnp.dot(rev, m_ref[...], preferred_element_type=F32)
                + jnp.dot(first, n_ref[...], preferred_element_type=F32))

    ae_ref[...] = (a_ref[...].astype(F32) + mirrored(am_ref, an_ref)).astype(BF16)
    bo_ref[...] = (b_ref[...].astype(F32) - mirrored(bm_ref, bn_ref)).astype(BF16)


def _dft_fold(a, b, B, S):
    fw = a.shape[1]
    tm = _tile(S // 2, 256)
    nb = S // tm
    nh = nb // 2

    def own(bb, i):
        return (bb * nb + i, 0)

    def mirror(bb, i):
        return (bb * nb + nb - 1 - i, 0)

    def mirror_next(bb, i):
        return (bb * nb + jnp.minimum(nb - i, nb - 1), 0)

    spec = lambda f: pl.BlockSpec((tm, fw), f)
    return pl.pallas_call(
        _dft_fold_kernel,
        grid=(B, nh),
        in_specs=[spec(own), spec(mirror), spec(mirror_next), spec(own), spec(mirror), spec(mirror_next)],
        out_specs=[pl.BlockSpec((tm, fw), lambda bb, i: (bb * nh + i, 0)),
                   pl.BlockSpec((tm, fw), lambda bb, i: (bb * nh + i, 0))],
        out_shape=[jax.ShapeDtypeStruct((B * S // 2, fw), BF16),
                   jax.ShapeDtypeStruct((B * S // 2, fw), BF16)],
        compiler_params=_cparams(("parallel", "arbitrary"), 40),
        name="dft_fold",
    )(a, a, a, b, b, b)


def _seq_dft_kernel(cs_ref, sn_ref, ae_ref, bo_ref, mid_ref, y_ref, acc_s, *, scale):
    kk = pl.program_id(3)

    @pl.when(kk == 0)
    def _():
        acc_s[...] = jnp.zeros(acc_s.shape, F32)

    acc_s[...] += (jnp.dot(cs_ref[...], ae_ref[...], preferred_element_type=F32)
                   + jnp.dot(sn_ref[...], bo_ref[...], preferred_element_type=F32))

    @pl.when(kk == pl.num_programs(3) - 1)
    def _():
        row = lax.broadcasted_iota(jnp.int32, acc_s.shape, 0)
        sign = jnp.where(row % 2 == 0, scale, -scale)
        y_ref[...] = (acc_s[...] + sign * mid_ref[0:1, :].astype(F32)).astype(y_ref.dtype)


def _seq_dft(cs, sn, ae, bo, a, B, S, scale):
    fw = a.shape[1]
    tm = _tile(S, 1024)
    tn = _tile(fw, 1024)
    tk = _tile(S // 2, 1024)
    nm, nk = S // tm, S // 2 // tk
    mid_rows = 16
    return pl.pallas_call(
        functools.partial(_seq_dft_kernel, scale=scale),
        grid=(B, nm, fw // tn, nk),
        in_specs=[pl.BlockSpec((tm, tk), lambda bb, i, j, k: (i, k)),
                  pl.BlockSpec((tm, tk), lambda bb, i, j, k: (i, k)),
                  pl.BlockSpec((tk, tn), lambda bb, i, j, k: (bb * nk + k, j)),
                  pl.BlockSpec((tk, tn), lambda bb, i, j, k: (bb * nk + k, j)),
                  pl.BlockSpec((mid_rows, tn),
                               lambda bb, i, j, k: ((bb * S + S // 2) // mid_rows, j))],
        out_specs=pl.BlockSpec((tm, tn), lambda bb, i, j, k: (bb * nm + i, j)),
        out_shape=jax.ShapeDtypeStruct((B * S, fw), BF16),
        scratch_shapes=[pltpu.VMEM((tm, tn), F32)],
        compiler_params=_cparams(("parallel", "parallel", "parallel", "arbitrary"), 48),
        name="seq_dft",
    )(cs, sn, ae, bo, a)


def _wo_kernel(ym_ref, yf_ref, gm_ref, gf_ref, w_ref, x_ref, mu_ref, rs_ref, lg_ref, lb_ref,
               pre_ref, mix_s, *, alpha, mw):
    @pl.when(pl.program_id(1) == 0)
    def _():
        ym = ym_ref[...].astype(F32)
        mix_s[:, :mw] = (ym * lax.rsqrt(jnp.mean(ym * ym, axis=-1, keepdims=True) + RMS_EPS)
                         * gm_ref[...]).astype(BF16)
        yf = yf_ref[...].astype(F32)
        mix_s[:, mw:] = (yf * lax.rsqrt(jnp.mean(yf * yf, axis=-1, keepdims=True) + RMS_EPS)
                         * gf_ref[...]).astype(BF16)

    h = (x_ref[...] - mu_ref[...]) * rs_ref[...] * lg_ref[...] + lb_ref[...]
    pre_ref[...] = alpha * h + jnp.dot(mix_s[...], w_ref[...], preferred_element_type=F32)


def _wo(ym, yf, gm, gf, w_o, x2, mu, rs, ln_g, ln_b, alpha):
    T, mw = ym.shape
    fw = yf.shape[1]
    D = w_o.shape[1]
    tm = _tile(T, 1024)
    tn = _tile(D, 512)
    return pl.pallas_call(
        functools.partial(_wo_kernel, alpha=alpha, mw=mw),
        grid=(T // tm, D // tn),
        in_specs=[pl.BlockSpec((tm, mw), lambda i, j: (i, 0)),
                  pl.BlockSpec((tm, fw), lambda i, j: (i, 0)),
                  pl.BlockSpec((1, mw), lambda i, j: (0, 0)),
                  pl.BlockSpec((1, fw), lambda i, j: (0, 0)),
                  pl.BlockSpec((mw + fw, tn), lambda i, j: (0, j)),
                  pl.BlockSpec((tm, tn), lambda i, j: (i, j)),
                  pl.BlockSpec((tm, 1), lambda i, j: (i, 0)),
                  pl.BlockSpec((tm, 1), lambda i, j: (i, 0)),
                  pl.BlockSpec((1, tn), lambda i, j: (0, j)),
                  pl.BlockSpec((1, tn), lambda i, j: (0, j))],
        out_specs=pl.BlockSpec((tm, tn), lambda i, j: (i, j)),
        out_shape=jax.ShapeDtypeStruct((T, D), F32),
        scratch_shapes=[pltpu.VMEM((tm, mw + fw), BF16)],
        compiler_params=_cparams(("parallel", "arbitrary"), 56),
        name="w_o_residual",
    )(ym, yf, gm.reshape(1, mw), gf.reshape(1, fw), w_o, x2, mu, rs,
      ln_g.reshape(1, D), ln_b.reshape(1, D))


def _pack_bf16_pair(lo, hi):
    lo_bits = pltpu.bitcast(lo.astype(BF16).astype(F32), jnp.uint32)
    hi_bits = pltpu.bitcast(hi.astype(BF16).astype(F32), jnp.uint32)
    return (lo_bits >> 16) | (hi_bits & jnp.uint32(0xFFFF0000))


def _unpack_bf16_pair(w):
    lo = pltpu.bitcast(w << 16, F32).astype(BF16)
    hi = pltpu.bitcast(w & jnp.uint32(0xFFFF0000), F32).astype(BF16)
    return lo, hi


def _ln1_router_kernel(pre_ref, g_ref, b_ref, rw_ref, rb_ref, x1_ref, x1p_ref, idx_ref, gate_ref):
    x = pre_ref[...]
    mu = jnp.mean(x, axis=-1, keepdims=True)
    xc = x - mu
    var = jnp.mean(xc * xc, axis=-1, keepdims=True)
    x1 = xc * lax.rsqrt(var + LN_EPS) * g_ref[...] + b_ref[...]
    x1_ref[...] = x1
    half = x1.shape[1] // 2
    x1p_ref[...] = _pack_bf16_pair(x1[:, :half], x1[:, half:])
    logits = lax.dot_general(rw_ref[...], x1, (((1,), (1,)), ((), ())),
                             precision=lax.Precision.HIGHEST, preferred_element_type=F32) + rb_ref[...]
    n_e = logits.shape[0]
    eidx = lax.broadcasted_iota(jnp.int32, logits.shape, 0)
    vals, idxs = [], []
    for _ in range(TOP_K):
        m = jnp.max(logits, axis=0, keepdims=True)
        sel = jnp.min(jnp.where(logits == m, eidx, n_e), axis=0, keepdims=True)
        logits = jnp.where(eidx == sel, -jnp.inf, logits)
        vals.append(m)
        idxs.append(sel)
    exps = [jnp.exp(v - vals[0]) for v in vals]
    denom = exps[0] + exps[1] + exps[2] + exps[3]
    for kk in range(TOP_K):
        idx_ref[kk:kk + 1, :] = idxs[kk]
        gate_ref[kk:kk + 1, :] = exps[kk] / denom


def _ln1_router(pre, g, b, router_w, router_b):
    T, D = pre.shape
    E = router_w.shape[1]
    tm = _tile(T, 256)
    return pl.pallas_call(
        _ln1_router_kernel,
        grid=(T // tm,),
        in_specs=[pl.BlockSpec((tm, D), lambda i: (i, 0)),
                  pl.BlockSpec((1, D), lambda i: (0, 0)),
                  pl.BlockSpec((1, D), lambda i: (0, 0)),
                  pl.BlockSpec((E, D), lambda i: (0, 0)),
                  pl.BlockSpec((E, 1), lambda i: (0, 0))],
        out_specs=[pl.BlockSpec((tm, D), lambda i: (i, 0)),
                   pl.BlockSpec((tm, D // 2), lambda i: (i, 0)),
                   pl.BlockSpec((TOP_K, tm), lambda i: (0, i)),
                   pl.BlockSpec((TOP_K, tm), lambda i: (0, i))],
        out_shape=[jax.ShapeDtypeStruct((T, D), F32),
                   jax.ShapeDtypeStruct((T, D // 2), jnp.uint32),
                   jax.ShapeDtypeStruct((TOP_K, T), jnp.int32),
                   jax.ShapeDtypeStruct((TOP_K, T), F32)],
        compiler_params=_cparams(("parallel",), 48),
        name="ln1_router",
    )(pre, g.reshape(1, D), b.reshape(1, D), router_w.T, router_b.reshape(E, 1))


def _row_copy(src_hbm, dst_ref, src_row, dst_row, sem):
    return pltpu.make_async_copy(src_hbm.at[pl.ds(src_row, 1)], dst_ref.at[pl.ds(dst_row, 1)], sem)


def _dispatch_kernel(dest_ref, pad_ref, npad_ref, nv_ref, x_ref, o_hbm, zero_s, sem, zsem, *, n_steps):
    b = pl.program_id(0)
    tt = x_ref.shape[0]

    def wait_rows(n):
        pltpu.make_async_copy(o_hbm.at[pl.ds(0, n)], o_hbm.at[pl.ds(0, n)], sem).wait()

    @pl.when(b < n_steps)
    def _():
        def issue(r, c):
            for kk in range(TOP_K):
                _row_copy(x_ref, o_hbm, r, dest_ref[(b * tt + r) * TOP_K + kk], sem).start()
            return c
        lax.fori_loop(0, tt, issue, 0, unroll=2)
        wait_rows(tt * TOP_K)

    @pl.when(b == n_steps)
    def _():
        n_pad = npad_ref[0]

        def issue_pad(i, c):
            _row_copy(x_ref, o_hbm, 0, pad_ref[i], sem).start()
            return c
        lax.fori_loop(0, n_pad, issue_pad, 0)

        def wait_pad(i, c):
            wait_rows(1)
            return c
        lax.fori_loop(0, n_pad, wait_pad, 0)

        zero_s[...] = jnp.zeros(zero_s.shape, zero_s.dtype)
        n_sub = MOE_CHUNK // MOE_SUB

        def zero_copy(i):
            return pltpu.make_async_copy(zero_s, o_hbm.at[pl.ds(pl.multiple_of(i * MOE_SUB, MOE_SUB), MOE_SUB)], zsem)

        def empty(i):
            return (i % n_sub) * MOE_SUB >= nv_ref[i // n_sub]

        def issue_zero(i, c):
            @pl.when(empty(i))
            def _():
                zero_copy(i).start()
            return c
        lax.fori_loop(0, nv_ref.shape[0] * n_sub, issue_zero, 0)

        def wait_zero(i, c):
            @pl.when(empty(i))
            def _():
                zero_copy(i).wait()
            return c
        lax.fori_loop(0, nv_ref.shape[0] * n_sub, wait_zero, 0)


def _dispatch(dest, pad_slots, n_pad, chunk_nv, x1p):
    T, W = x1p.shape
    tt = _tile(T, DISPATCH_TOKENS)
    n_steps = T // tt
    grid_spec = pltpu.PrefetchScalarGridSpec(
        num_scalar_prefetch=4,
        grid=(n_steps + 1,),
        in_specs=[pl.BlockSpec((tt, W), lambda b, d, p, n, v: (jnp.minimum(b, n_steps - 1), 0))],
        out_specs=pl.BlockSpec(memory_space=pl.ANY),
        scratch_shapes=[pltpu.VMEM((MOE_SUB, W), x1p.dtype),
                        pltpu.SemaphoreType.DMA(()), pltpu.SemaphoreType.DMA(())],
    )
    return pl.pallas_call(
        functools.partial(_dispatch_kernel, n_steps=n_steps),
        grid_spec=grid_spec,
        out_shape=jax.ShapeDtypeStruct((chunk_nv.shape[0] * MOE_CHUNK, W), x1p.dtype),
        compiler_params=_cparams(("arbitrary",), 32),
        name="moe_dispatch",
    )(dest, pad_slots, n_pad, chunk_nv, x1p)


def _expert_up_kernel(ce_ref, nv_ref, nu_ref, x_ref, wg_ref, wu_ref, bg_ref, bu_ref, h_ref):
    c = pl.program_id(0)
    nv = nv_ref[c]

    n_sub = MOE_CHUNK // MOE_SUB
    n_act = (nv + MOE_SUB - 1) // MOE_SUB
    for k in range(n_sub + 1):
        @pl.when(n_act == k)
        def _(k=k):
            if k > 0:
                x_lo, x_hi = _unpack_bf16_pair(x_ref[:k * MOE_SUB, :])
                half = x_lo.shape[1]

                def proj(w_ref):
                    return (jnp.dot(x_lo, w_ref[:half, :].astype(BF16), preferred_element_type=F32)
                            + jnp.dot(x_hi, w_ref[half:, :].astype(BF16), preferred_element_type=F32))
                hg = jnp.minimum(proj(wg_ref) + bg_ref[...], SWIGLU_LIMIT)
                hu = jnp.clip(proj(wu_ref) + bu_ref[...], -SWIGLU_LIMIT, SWIGLU_LIMIT)
                act = (hu + 1.0) * (hg * jax.nn.sigmoid(SWIGLU_ALPHA * hg))
                h_ref[:k * MOE_SUB, :] = act.astype(h_ref.dtype)
            if k < n_sub:
                h_ref[k * MOE_SUB:, :] = jnp.zeros(((n_sub - k) * MOE_SUB, h_ref.shape[1]), h_ref.dtype)


def _expert_up(chunk_e, chunk_nv, n_used, xs, w_gate, w_up, b_gate, b_up, n_chunks):
    E, D, F = w_gate.shape
    tf = _tile(F, 256)
    nf = F // tf

    def used(c, nu):
        return jnp.minimum(c, nu[0] - 1)

    def jeff(c, j, nu):
        return jnp.where(c < nu[0], j, nf - 1)

    grid_spec = pltpu.PrefetchScalarGridSpec(
        num_scalar_prefetch=3,
        grid=(n_chunks, nf),
        in_specs=[pl.BlockSpec((MOE_CHUNK, D // 2), lambda c, j, ce, nv, nu: (used(c, nu), 0)),
                  pl.BlockSpec((None, D, tf), lambda c, j, ce, nv, nu: (ce[c], 0, jeff(c, j, nu))),
                  pl.BlockSpec((None, D, tf), lambda c, j, ce, nv, nu: (ce[c], 0, jeff(c, j, nu))),
                  pl.BlockSpec((None, 1, tf), lambda c, j, ce, nv, nu: (ce[c], 0, jeff(c, j, nu))),
                  pl.BlockSpec((None, 1, tf), lambda c, j, ce, nv, nu: (ce[c], 0, jeff(c, j, nu)))],
        out_specs=pl.BlockSpec((MOE_CHUNK, tf), lambda c, j, ce, nv, nu: (c, j)),
    )
    return pl.pallas_call(
        _expert_up_kernel,
        grid_spec=grid_spec,
        out_shape=jax.ShapeDtypeStruct((n_chunks * MOE_CHUNK, F), BF16),
        compiler_params=_cparams(("arbitrary", "arbitrary"), 60),
        name="expert_gate_up",
    )(chunk_e, chunk_nv, n_used, xs, w_gate, w_up, b_gate.reshape(E, 1, F), b_up.reshape(E, 1, F))


def _expert_down_kernel(ce_ref, nv_ref, nu_ref, h_ref, wd_ref, bd_ref, y_ref):
    c = pl.program_id(0)
    nv = nv_ref[c]

    n_sub = MOE_CHUNK // MOE_SUB
    n_act = (nv + MOE_SUB - 1) // MOE_SUB
    for k in range(n_sub + 1):
        @pl.when(n_act == k)
        def _(k=k):
            if k > 0:
                y_ref[:k * MOE_SUB, :] = (jnp.dot(h_ref[:k * MOE_SUB, :], wd_ref[...].astype(BF16),
                                                  preferred_element_type=F32) + bd_ref[...])
            if k < n_sub:
                y_ref[k * MOE_SUB:, :] = jnp.zeros(((n_sub - k) * MOE_SUB, y_ref.shape[1]), y_ref.dtype)


def _expert_down(chunk_e, chunk_nv, n_used, hmid, w_down, b_down, n_chunks):
    E, F, D = w_down.shape
    tn = _tile(D, 1024)
    nn = D // tn

    def used(c, nu):
        return jnp.minimum(c, nu[0] - 1)

    def jeff(c, j, nu):
        return jnp.where(c < nu[0], j, nn - 1)

    grid_spec = pltpu.PrefetchScalarGridSpec(
        num_scalar_prefetch=3,
        grid=(n_chunks, nn),
        in_specs=[pl.BlockSpec((MOE_CHUNK, F), lambda c, j, ce, nv, nu: (used(c, nu), 0)),
                  pl.BlockSpec((None, F, tn), lambda c, j, ce, nv, nu: (ce[c], 0, jeff(c, j, nu))),
                  pl.BlockSpec((None, 1, tn), lambda c, j, ce, nv, nu: (ce[c], 0, jeff(c, j, nu)))],
        out_specs=pl.BlockSpec((MOE_CHUNK, tn), lambda c, j, ce, nv, nu: (c, j)),
    )
    return pl.pallas_call(
        _expert_down_kernel,
        grid_spec=grid_spec,
        out_shape=jax.ShapeDtypeStruct((n_chunks * MOE_CHUNK, D), F32),
        compiler_params=_cparams(("arbitrary", "arbitrary"), 56),
        name="expert_down",
    )(chunk_e, chunk_nv, n_used, hmid, w_down, b_down.reshape(E, 1, D))


def _combine_kernel(dest_ref, y_hbm, x1_ref, gate_ref, g_ref, b_ref, o_ref, buf, sem, *, alpha):
    i = pl.program_id(0)
    n = pl.num_programs(0)
    tt = o_ref.shape[0]

    def issue_step(step, slot):
        def issue(r, carry):
            for kk in range(TOP_K):
                _row_copy(y_hbm, buf.at[slot, kk], dest_ref[(step * tt + r) * TOP_K + kk], r,
                          sem.at[slot]).start()
            return carry
        lax.fori_loop(0, tt, issue, 0, unroll=4)

    @pl.when(i == 0)
    def _():
        issue_step(0, 0)

    slot = i % 2

    @pl.when(i + 1 < n)
    def _():
        issue_step(i + 1, 1 - slot)

    for kk in range(TOP_K):
        pltpu.make_async_copy(y_hbm.at[pl.ds(0, tt)], buf.at[slot, kk], sem.at[slot]).wait()
    gates = gate_ref[...]
    y = gates[:, 0:1] * buf[slot, 0]
    for kk in range(1, TOP_K):
        y = y + gates[:, kk:kk + 1] * buf[slot, kk]
    z = alpha * x1_ref[...] + y
    mu = jnp.mean(z, axis=-1, keepdims=True)
    zc = z - mu
    var = jnp.mean(zc * zc, axis=-1, keepdims=True)
    o_ref[...] = zc * lax.rsqrt(var + LN_EPS) * g_ref[...] + b_ref[...]


def _combine(dest, yslots, x1, gates_tk, g, b, alpha):
    T, D = x1.shape
    tt = _tile(T, COMBINE_TOKENS)
    grid_spec = pltpu.PrefetchScalarGridSpec(
        num_scalar_prefetch=1,
        grid=(T // tt,),
        in_specs=[pl.BlockSpec(memory_space=pl.ANY),
                  pl.BlockSpec((tt, D), lambda i, d: (i, 0)),
                  pl.BlockSpec((tt, TOP_K), lambda i, d: (i, 0)),
                  pl.BlockSpec((1, D), lambda i, d: (0, 0)),
                  pl.BlockSpec((1, D), lambda i, d: (0, 0))],
        out_specs=pl.BlockSpec((tt, D), lambda i, d: (i, 0)),
        scratch_shapes=[pltpu.VMEM((2, TOP_K, tt, D), F32), pltpu.SemaphoreType.DMA((2,))],
    )
    return pl.pallas_call(
        functools.partial(_combine_kernel, alpha=alpha),
        grid_spec=grid_spec,
        out_shape=jax.ShapeDtypeStruct((T, D), F32),
        compiler_params=_cparams(("arbitrary",), 32),
        name="moe_combine_ln2",
    )(dest, yslots, x1, gates_tk, g.reshape(1, D), b.reshape(1, D))


def _routing_tables(top_idx, n_experts, n_chunks):
    T = top_idx.shape[1]
    M = T * TOP_K
    flat_e = top_idx.T.reshape(M)
    onehot = (flat_e[:, None] == jnp.arange(n_experts, dtype=jnp.int32)[None, :]).astype(jnp.int32)
    csum = jnp.cumsum(onehot, axis=0)
    rank = jnp.sum(csum * onehot, axis=1) - 1
    counts = csum[-1]
    chunks_e = (counts + MOE_CHUNK - 1) // MOE_CHUNK
    chunk_end = jnp.cumsum(chunks_e)
    chunk_start = chunk_end - chunks_e
    n_used = chunk_end[-1]
    dest = chunk_start[flat_e] * MOE_CHUNK + rank
    P = n_chunks * MOE_CHUNK
    pad_idx = counts[:, None] + jnp.arange(MOE_SUB, dtype=jnp.int32)[None, :]
    padded = (counts + MOE_SUB - 1) // MOE_SUB * MOE_SUB
    pad_slots = jnp.where(pad_idx < padded[:, None], chunk_start[:, None] * MOE_CHUNK + pad_idx, P)
    pad_slots = jnp.sort(pad_slots.reshape(-1)).astype(jnp.int32)
    n_pad = jnp.sum(padded - counts).astype(jnp.int32).reshape(1)
    cid = jnp.arange(n_chunks, dtype=jnp.int32)
    chunk_e = jnp.minimum(jnp.searchsorted(chunk_end, cid, side='right'), n_experts - 1).astype(jnp.int32)
    last_e = chunk_e[jnp.maximum(n_used - 1, 0)]
    chunk_e = jnp.where(cid < n_used, chunk_e, last_e)
    chunk_nv = jnp.where(cid < n_used,
                         jnp.clip(counts[chunk_e] - (cid - chunk_start[chunk_e]) * MOE_CHUNK, 0, MOE_CHUNK),
                         0).astype(jnp.int32)
    return (dest.astype(jnp.int32), pad_slots, n_pad, chunk_e, chunk_nv,
            n_used.astype(jnp.int32).reshape(1))


def kernel(x, positions, ln_in_g, ln_in_b, w_in, q_a_norm_g, w_q_b, kv_a_norm_g, w_kv_b, mla_out_norm_g, fourier_out_norm_g, w_o, ln1_g, ln1_b, router_w, router_b, w_gate, b_gate, w_up, b_up, w_down, b_down, ln2_g, ln2_b):
    B, S, D = x.shape
    T = B * S
    depth = w_in.shape[0]
    assert depth == 1, "single-layer trunk only"
    qr = q_a_norm_g.shape[1]
    kvr = kv_a_norm_g.shape[1]
    H = w_q_b.shape[2] // (QK_NOPE_DIM + QK_ROPE_DIM)
    fw = fourier_out_norm_g.shape[1]
    E = router_w.shape[2]
    assert H % HEADS_PER_TILE == 0 and (qr + kvr) % LANES == 0 and S % (8 * DFT_ROW_SPLIT) == 0
    alpha = (2.0 * depth) ** 0.25

    inv_freq = ROPE_THETA ** (-jnp.arange(0, QK_ROPE_DIM, 2, dtype=F32) / QK_ROPE_DIM)
    ang = positions.astype(F32)[..., None] * inv_freq
    cos4 = jnp.tile(jnp.cos(ang), (1, 1, 2 * LANES // QK_ROPE_DIM)).reshape(T, LANES)
    sin4 = jnp.tile(jnp.sin(ang), (1, 1, 2 * LANES // QK_ROPE_DIM)).reshape(T, LANES)

    rope_end = qr + kvr + QK_ROPE_DIM
    w_a = w_in[0, :, :qr + kvr + LANES].astype(BF16)
    w_f = w_in[0, :, rope_end:].astype(BF16)
    wq = w_q_b[0].reshape(qr, H // HEADS_PER_TILE, HEADS_PER_TILE, QK_NOPE_DIM + QK_ROPE_DIM)
    wq_perm = jnp.concatenate(
        [wq[..., :QK_NOPE_DIM].reshape(qr, H // HEADS_PER_TILE, HEADS_PER_TILE * QK_NOPE_DIM),
         wq[..., QK_NOPE_DIM:].reshape(qr, H // HEADS_PER_TILE, HEADS_PER_TILE * QK_ROPE_DIM)],
        axis=-1).reshape(qr, H * (QK_NOPE_DIM + QK_ROPE_DIM)).astype(BF16)
    w_kv = w_kv_b[0].astype(BF16)
    w_o_b = w_o[0].astype(BF16)
    ch = jnp.arange(FOURIER_GROUP_DIM, dtype=jnp.int32)
    ang_c = (2.0 * math.pi / FOURIER_GROUP_DIM) * ((ch[:, None] * ch[None, :]) % FOURIER_GROUP_DIM).astype(F32)
    cs_tab = jnp.concatenate([jnp.cos(ang_c), jnp.sin(ang_c)], axis=1).astype(BF16)

    x2 = x.reshape(T, D)
    hb, mu, rs = _ln_in(x2, ln_in_g, ln_in_b)

    cq, ckv, kpe2 = _inproj_a(hb, w_a, q_a_norm_g[0], kv_a_norm_g[0], cos4, sin4)
    qscale = (QK_NOPE_DIM + QK_ROPE_DIM) ** -0.5 * LOG2E
    q = _q_up(cq, wq_perm, cos4, sin4, qscale)
    k, v = _kv_up(ckv, w_kv, kpe2)
    y_mla = _attention(q, k, v, B, S, H)

    fa, fb = _inproj_f(hb, w_f, cs_tab)
    cs_mat, sn_mat = _dft_matrices(S, FOURIER_GROUP_DIM)
    fae, fbo = _dft_fold(fa, fb, B, S)
    y_f = _seq_dft(cs_mat, sn_mat, fae, fbo, fa, B, S, 1.0 / math.sqrt(S * FOURIER_GROUP_DIM))

    pre = _wo(y_mla, y_f, mla_out_norm_g[0], fourier_out_norm_g[0], w_o_b, x2, mu, rs,
              ln_in_g, ln_in_b, alpha)
    x1, x1p, top_idx, gates = _ln1_router(pre, ln1_g[0], ln1_b[0], router_w[0], router_b[0])

    n_chunks = -(-T * TOP_K // MOE_CHUNK) + E
    dest, pad_slots, n_pad, chunk_e, chunk_nv, n_used = _routing_tables(top_idx, E, n_chunks)
    xs = _dispatch(dest, pad_slots, n_pad, chunk_nv, x1p)
    hmid = _expert_up(chunk_e, chunk_nv, n_used, xs, w_gate[0], w_up[0], b_gate[0], b_up[0], n_chunks)
    yslots = _expert_down(chunk_e, chunk_nv, n_used, hmid, w_down[0], b_down[0], n_chunks)
    out = _combine(dest, yslots, x1, gates.T, ln2_g[0], ln2_b[0], alpha)
    return out.reshape(B, S, D)
```

```python
import functools
import math

import jax
import jax.numpy as jnp
from jax import lax
from jax.experimental import pallas as pl
from jax.experimental.pallas import tpu as pltpu

F32 = jnp.float32
BF16 = jnp.bfloat16

V_HEAD_DIM = 128
QK_NOPE_DIM = 128
QK_ROPE_DIM = 64
QK_PAD_DIM = 256
V_PAD_DIM = 256
ROPE_THETA = 10000.0
FOURIER_GROUP_DIM = 128
TOP_K = 4
SWIGLU_LIMIT = 7.0
SWIGLU_ALPHA = 1.702
LN_EPS = 1e-5
RMS_EPS = 1e-6
LOG2E = 1.4426950408889634

LANES = 128
V7X_VMEM_BYTES = 64 * 1024 * 1024
HEADS_PER_TILE = 4
DFT_ROW_SPLIT = 64
ATTN_TQ = 1024
ATTN_TKV = 1024

MOE_CHUNK = 1280
MOE_SUB = 256
DISPATCH_TOKENS = 512
COMBINE_TOKENS = 64
COMBINE_BUFS = 4
COMBINE_AHEAD = 2
MOE_DOWN_TN = 1024


def _cparams(semantics, vmem_mb):
    return pltpu.CompilerParams(dimension_semantics=semantics,
                                vmem_limit_bytes=min(vmem_mb * 1024 * 1024, V7X_VMEM_BYTES - (4 << 20)))


def _tile(dim, pref):
    t = min(dim, pref)
    while dim % t:
        t //= 2
    return t


def _ln_in_kernel(x_ref, g_ref, b_ref, hb_ref, mu_ref, rs_ref):
    x = x_ref[...]
    mu = jnp.mean(x, axis=-1, keepdims=True)
    xc = x - mu
    var = jnp.mean(xc * xc, axis=-1, keepdims=True)
    rs = lax.rsqrt(var + LN_EPS)
    hb_ref[...] = (xc * rs * g_ref[...] + b_ref[...]).astype(BF16)
    mu_ref[...] = mu
    rs_ref[...] = rs


def _ln_in(x2, g, b):
    T, D = x2.shape
    tm = _tile(T, 256)
    return pl.pallas_call(
        _ln_in_kernel,
        grid=(T // tm,),
        in_specs=[pl.BlockSpec((tm, D), lambda i: (i, 0)),
                  pl.BlockSpec((1, D), lambda i: (0, 0)),
                  pl.BlockSpec((1, D), lambda i: (0, 0))],
        out_specs=[pl.BlockSpec((tm, D), lambda i: (i, 0)),
                   pl.BlockSpec((tm, 1), lambda i: (i, 0)),
                   pl.BlockSpec((tm, 1), lambda i: (i, 0))],
        out_shape=[jax.ShapeDtypeStruct((T, D), BF16),
                   jax.ShapeDtypeStruct((T, 1), F32),
                   jax.ShapeDtypeStruct((T, 1), F32)],
        compiler_params=_cparams(("parallel",), 40),
        name="ln_in",
    )(x2, g.reshape(1, D), b.reshape(1, D))


def _rope128(p, cos4, sin4):
    lane = lax.broadcasted_iota(jnp.int32, p.shape, 1)
    first_half = (lane % QK_ROPE_DIM) < (QK_ROPE_DIM // 2)
    rot = jnp.where(first_half, -pltpu.roll(p, LANES - QK_ROPE_DIM // 2, 1),
                    pltpu.roll(p, QK_ROPE_DIM // 2, 1))
    return p * cos4 + rot * sin4


def _inproj_a_kernel(h_ref, w_ref, gq_ref, gkv_ref, cos_ref, sin_ref,
                     cq_ref, ckv_ref, kpe_ref, *, qr, kvr):
    acc = jnp.dot(h_ref[...], w_ref[...], preferred_element_type=F32)
    cq = acc[:, :qr]
    cq_ref[...] = (cq * lax.rsqrt(jnp.mean(cq * cq, axis=-1, keepdims=True) + RMS_EPS)
                   * gq_ref[...]).astype(BF16)
    ckv = acc[:, qr:qr + kvr]
    ckv_ref[...] = (ckv * lax.rsqrt(jnp.mean(ckv * ckv, axis=-1, keepdims=True) + RMS_EPS)
                    * gkv_ref[...]).astype(BF16)
    roped = _rope128(acc[:, qr + kvr:qr + kvr + LANES], cos_ref[...], sin_ref[...])
    lane = lax.broadcasted_iota(jnp.int32, roped.shape, 1)
    even = jnp.where(lane < QK_ROPE_DIM, roped, 0.0)
    kpe_ref[:, :LANES] = even.astype(BF16)
    kpe_ref[:, LANES:] = pltpu.roll(even, QK_ROPE_DIM, 1).astype(BF16)


def _inproj_a(hb, w_a, gq, gkv, cos4, sin4):
    T, D = hb.shape
    qr, kvr = gq.shape[0], gkv.shape[0]
    wa = w_a.shape[1]
    tm = _tile(T, 512)
    return pl.pallas_call(
        functools.partial(_inproj_a_kernel, qr=qr, kvr=kvr),
        grid=(T // tm,),
        in_specs=[pl.BlockSpec((tm, D), lambda i: (i, 0)),
                  pl.BlockSpec((D, wa), lambda i: (0, 0)),
                  pl.BlockSpec((1, qr), lambda i: (0, 0)),
                  pl.BlockSpec((1, kvr), lambda i: (0, 0)),
                  pl.BlockSpec((tm, LANES), lambda i: (i, 0)),
                  pl.BlockSpec((tm, LANES), lambda i: (i, 0))],
        out_specs=[pl.BlockSpec((tm, qr), lambda i: (i, 0)),
                   pl.BlockSpec((tm, kvr), lambda i: (i, 0)),
                   pl.BlockSpec((tm, 2 * LANES), lambda i: (i, 0))],
        out_shape=[jax.ShapeDtypeStruct((T, qr), BF16),
                   jax.ShapeDtypeStruct((T, kvr), BF16),
                   jax.ShapeDtypeStruct((T, 2 * LANES), BF16)],
        compiler_params=_cparams(("parallel",), 56),
        name="inproj_mla",
    )(hb, w_a, gq.reshape(1, qr), gkv.reshape(1, kvr), cos4, sin4)


def _q_up_kernel(c_ref, w_ref, cos_ref, sin_ref, q_ref, *, qscale):
    acc = jnp.dot(c_ref[...], w_ref[...], preferred_element_type=F32)
    nope_w = HEADS_PER_TILE * QK_NOPE_DIM
    lane = lax.broadcasted_iota(jnp.int32, (acc.shape[0], LANES), 1)
    for pair in range(HEADS_PER_TILE // 2):
        roped = _rope128(acc[:, nope_w + pair * LANES:nope_w + (pair + 1) * LANES],
                         cos_ref[...], sin_ref[...]) * qscale
        for par in range(2):
            j = 2 * pair + par
            keep = (lane < QK_ROPE_DIM) if par == 0 else (lane >= QK_ROPE_DIM)
            base = j * QK_PAD_DIM
            q_ref[:, base:base + QK_NOPE_DIM] = (
                acc[:, j * QK_NOPE_DIM:(j + 1) * QK_NOPE_DIM] * qscale).astype(BF16)
            q_ref[:, base + QK_NOPE_DIM:base + QK_PAD_DIM] = jnp.where(keep, roped, 0.0).astype(BF16)


def _q_up(cq, wq_perm, cos4, sin4, qscale):
    T, qr = cq.shape
    n_tiles = wq_perm.shape[1] // (HEADS_PER_TILE * (QK_NOPE_DIM + QK_ROPE_DIM))
    tw = HEADS_PER_TILE * (QK_NOPE_DIM + QK_ROPE_DIM)
    to = HEADS_PER_TILE * QK_PAD_DIM
    tm = _tile(T, 1024)
    return pl.pallas_call(
        functools.partial(_q_up_kernel, qscale=qscale),
        grid=(T // tm, n_tiles),
        in_specs=[pl.BlockSpec((tm, qr), lambda i, j: (i, 0)),
                  pl.BlockSpec((qr, tw), lambda i, j: (0, j)),
                  pl.BlockSpec((tm, LANES), lambda i, j: (i, 0)),
                  pl.BlockSpec((tm, LANES), lambda i, j: (i, 0))],
        out_specs=pl.BlockSpec((tm, to), lambda i, j: (i, j)),
        out_shape=jax.ShapeDtypeStruct((T, n_tiles * to), BF16),
        compiler_params=_cparams(("parallel", "arbitrary"), 40),
        name="q_up",
    )(cq, wq_perm, cos4, sin4)


def _kv_up_kernel(c_ref, w_ref, kpe_ref, k_ref, v_ref):
    acc = jnp.dot(c_ref[...], w_ref[...], preferred_element_type=F32)
    for j in range(HEADS_PER_TILE):
        src = j * (QK_NOPE_DIM + V_HEAD_DIM)
        k_ref[:, j * QK_PAD_DIM:j * QK_PAD_DIM + QK_NOPE_DIM] = acc[:, src:src + QK_NOPE_DIM].astype(BF16)
        par = j % 2
        k_ref[:, j * QK_PAD_DIM + QK_NOPE_DIM:(j + 1) * QK_PAD_DIM] = kpe_ref[:, par * LANES:(par + 1) * LANES]
        v_ref[:, j * V_PAD_DIM:j * V_PAD_DIM + V_HEAD_DIM] = (
            acc[:, src + QK_NOPE_DIM:src + QK_NOPE_DIM + V_HEAD_DIM].astype(BF16))
        v_ref[:, j * V_PAD_DIM + V_HEAD_DIM:(j + 1) * V_PAD_DIM] = jnp.ones(
            (acc.shape[0], V_PAD_DIM - V_HEAD_DIM), BF16)


def _kv_up(ckv, w_kv, kpe2):
    T, kvr = ckv.shape
    tw = HEADS_PER_TILE * (QK_NOPE_DIM + V_HEAD_DIM)
    n_tiles = w_kv.shape[1] // tw
    tm = _tile(T, 1024)
    return pl.pallas_call(
        _kv_up_kernel,
        grid=(T // tm, n_tiles),
        in_specs=[pl.BlockSpec((tm, kvr), lambda i, j: (i, 0)),
                  pl.BlockSpec((kvr, tw), lambda i, j: (0, j)),
                  pl.BlockSpec((tm, 2 * LANES), lambda i, j: (i, 0))],
        out_specs=[pl.BlockSpec((tm, HEADS_PER_TILE * QK_PAD_DIM), lambda i, j: (i, j)),
                   pl.BlockSpec((tm, HEADS_PER_TILE * V_PAD_DIM), lambda i, j: (i, j))],
        out_shape=[jax.ShapeDtypeStruct((T, n_tiles * HEADS_PER_TILE * QK_PAD_DIM), BF16),
                   jax.ShapeDtypeStruct((T, n_tiles * HEADS_PER_TILE * V_PAD_DIM), BF16)],
        compiler_params=_cparams(("parallel", "arbitrary"), 40),
        name="kv_up",
    )(ckv, w_kv, kpe2)


def _lane_tile(x, reps):
    return jnp.concatenate([x] * reps, axis=1)


def _attn_kernel(q_ref, k_ref, v_ref, o_ref, m_s, acc_s, s0_s, s1_s, x0_s, x1_s, p0_s, p1_s, a0_s, a1_s,
                 *, tkv):
    n_kv = k_ref.shape[0] // tkv
    s_buf, x_buf, p_buf, a_buf = (s0_s, s1_s), (x0_s, x1_s), (p0_s, p1_s), (a0_s, a1_s)
    m_s[...] = jnp.full(m_s.shape, -jnp.inf, F32)
    acc_s[...] = jnp.zeros(acc_s.shape, F32)

    def scores(i, slot):
        off = pl.multiple_of(i * tkv, tkv)
        s = lax.dot_general(q_ref[...], k_ref[pl.ds(off, tkv), :], (((1,), (1,)), ((), ())),
                            preferred_element_type=F32)
        s_buf[slot][...] = s
        x_buf[slot][...] = jnp.broadcast_to(jnp.max(s, axis=1, keepdims=True), x_buf[slot].shape)

    def probs(slot):
        m_prev = m_s[...]
        m_new = jnp.maximum(m_prev, x_buf[slot][...])
        m_s[...] = m_new
        a_buf[slot][...] = jnp.exp2(m_prev - m_new)
        p_buf[slot][...] = jnp.exp2(s_buf[slot][...] - _lane_tile(m_new, tkv // LANES)).astype(BF16)

    def values(i, slot):
        off = pl.multiple_of(i * tkv, tkv)
        acc_s[...] = (_lane_tile(a_buf[slot][...], V_PAD_DIM // LANES) * acc_s[...]
                      + jnp.dot(p_buf[slot][...], v_ref[pl.ds(off, tkv), :], preferred_element_type=F32))

    def trip(i, slot):
        scores(i + 1, 1 - slot)
        values(i - 1, 1 - slot)
        probs(slot)

    scores(0, 0)
    scores(1, 1)
    probs(0)
    for i in range(1, n_kv - 1):
        trip(i, i % 2)
    values(n_kv - 2, 0)
    probs(1)
    values(n_kv - 1, 1)
    acc = acc_s[...]
    o_ref[...] = (acc[:, :V_HEAD_DIM] / acc[:, V_HEAD_DIM:]).astype(o_ref.dtype)


def _attention(q, k, v, B, S, H):
    T = B * S
    tq = _tile(S, ATTN_TQ)
    tkv = _tile(S, min(ATTN_TKV, S // 2))
    assert (S // tkv) % 2 == 0, "the key-chunk pipeline is unrolled in pairs"
    nq = S // tq
    return pl.pallas_call(
        functools.partial(_attn_kernel, tkv=tkv),
        grid=(B, H, nq),
        in_specs=[pl.BlockSpec((tq, QK_PAD_DIM), lambda b, h, i: (b * nq + i, h)),
                  pl.BlockSpec((S, QK_PAD_DIM), lambda b, h, i: (b, h)),
                  pl.BlockSpec((S, V_PAD_DIM), lambda b, h, i: (b, h))],
        out_specs=pl.BlockSpec((tq, V_HEAD_DIM), lambda b, h, i: (b * nq + i, h)),
        out_shape=jax.ShapeDtypeStruct((T, H * V_HEAD_DIM), BF16),
        scratch_shapes=[pltpu.VMEM((tq, LANES), F32), pltpu.VMEM((tq, V_PAD_DIM), F32),
                        pltpu.VMEM((tq, tkv), F32), pltpu.VMEM((tq, tkv), F32),
                        pltpu.VMEM((tq, LANES), F32), pltpu.VMEM((tq, LANES), F32),
                        pltpu.VMEM((tq, tkv), BF16), pltpu.VMEM((tq, tkv), BF16),
                        pltpu.VMEM((tq, LANES), F32), pltpu.VMEM((tq, LANES), F32)],
        compiler_params=_cparams(("parallel", "parallel", "arbitrary"), 40),
        name="mla_attention",
    )(q, k, v)


def _inproj_f_kernel(h_ref, w_ref, cs_ref, a_ref, b_ref, *, groups):
    acc = jnp.dot(h_ref[...], w_ref[...], preferred_element_type=F32)
    C = FOURIER_GROUP_DIM
    for g in range(groups):
        ab = jnp.dot(acc[:, g * C:(g + 1) * C].astype(BF16), cs_ref[...], preferred_element_type=F32)
        a_ref[:, g * C:(g + 1) * C] = ab[:, :C].astype(BF16)
        b_ref[:, g * C:(g + 1) * C] = ab[:, C:].astype(BF16)


def _inproj_f(hb, w_f, cs_tab):
    T, D = hb.shape
    fw = w_f.shape[1]
    tm = _tile(T, 1024)
    tn = _tile(fw, 512)
    return pl.pallas_call(
        functools.partial(_inproj_f_kernel, groups=tn // FOURIER_GROUP_DIM),
        grid=(T // tm, fw // tn),
        in_specs=[pl.BlockSpec((tm, D), lambda i, j: (i, 0)),
                  pl.BlockSpec((D, tn), lambda i, j: (0, j)),
                  pl.BlockSpec((FOURIER_GROUP_DIM, 2 * FOURIER_GROUP_DIM), lambda i, j: (0, 0))],
        out_specs=[pl.BlockSpec((tm, tn), lambda i, j: (i, j)),
                   pl.BlockSpec((tm, tn), lambda i, j: (i, j))],
        out_shape=[jax.ShapeDtypeStruct((T, fw), BF16), jax.ShapeDtypeStruct((T, fw), BF16)],
        compiler_params=_cparams(("parallel", "arbitrary"), 48),
        name="inproj_fourier",
    )(hb, w_f, cs_tab)


def _dft_gen_kernel(tac_ref, tas_ref, tbc_ref, tbs_ref, cs_ref, sn_ref):
    tbc = tbc_ref[...]
    tbs = tbs_ref[...]
    for aa in range(tac_ref.shape[0]):
        ca = tac_ref[aa:aa + 1, :]
        sa = tas_ref[aa:aa + 1, :]
        rows = slice(aa * DFT_ROW_SPLIT, (aa + 1) * DFT_ROW_SPLIT)
        cs_ref[rows, :] = (ca * tbc - sa * tbs).astype(BF16)
        sn_ref[rows, :] = (-(sa * tbc + ca * tbs)).astype(BF16)


def _dft_matrices(S, n_chan):
    na = S // DFT_ROW_SPLIT
    scale = 1.0 / math.sqrt(S * n_chan)
    col = jnp.arange(S // 2, dtype=jnp.int32)[None, :]
    ang_a = (2.0 * math.pi / na) * ((jnp.arange(na, dtype=jnp.int32)[:, None] * col) % na).astype(F32)
    ang_b = (2.0 * math.pi / S) * ((jnp.arange(DFT_ROW_SPLIT, dtype=jnp.int32)[:, None] * col) % S).astype(F32)
    tac, tas = jnp.cos(ang_a), jnp.sin(ang_a)
    tbc, tbs = scale * jnp.cos(ang_b), scale * jnp.sin(ang_b)
    ta = 8
    tc = _tile(S // 2, 2048)
    return pl.pallas_call(
        _dft_gen_kernel,
        grid=(na // ta, S // 2 // tc),
        in_specs=[pl.BlockSpec((ta, tc), lambda i, j: (i, j)),
                  pl.BlockSpec((ta, tc), lambda i, j: (i, j)),
                  pl.BlockSpec((DFT_ROW_SPLIT, tc), lambda i, j: (0, j)),
                  pl.BlockSpec((DFT_ROW_SPLIT, tc), lambda i, j: (0, j))],
        out_specs=[pl.BlockSpec((ta * DFT_ROW_SPLIT, tc), lambda i, j: (i, j)),
                   pl.BlockSpec((ta * DFT_ROW_SPLIT, tc), lambda i, j: (i, j))],
        out_shape=[jax.ShapeDtypeStruct((S, S // 2), BF16), jax.ShapeDtypeStruct((S, S // 2), BF16)],
        compiler_params=_cparams(("parallel", "parallel"), 40),
        name="dft_matrices",
    )(tac, tas, tbc, tbs)


def _dft_fold_kernel(a_ref, am_ref, an_ref, b_ref, bm_ref, bn_ref, ae_ref, bo_ref):
    tm = a_ref.shape[0]
    r = lax.broadcasted_iota(jnp.int32, (tm, tm), 0)
    c = lax.broadcasted_iota(jnp.int32, (tm, tm), 1)
    rev = jnp.where(r + c == tm, 1.0, 0.0).astype(BF16)
    has_next = (pl.program_id(1) > 0).astype(F32)
    first = (jnp.where(r + c == 0, 1.0, 0.0) * has_next).astype(BF16)

    def mirrored(m_ref, n_ref):
        return (jnp.dot(rev, m_ref[...], preferred_element_type=F32)
                + jnp.dot(first, n_ref[...], preferred_element_type=F32))

    ae_ref[...] = (a_ref[...].astype(F32) + mirrored(am_ref, an_ref)).astype(BF16)
    bo_ref[...] = (b_ref[...].astype(F32) - mirrored(bm_ref, bn_ref)).astype(BF16)


def _dft_fold(a, b, B, S):
    fw = a.shape[1]
    tm = _tile(S // 2, 256)
    nb = S // tm
    nh = nb // 2

    def own(bb, i):
        return (bb * nb + i, 0)

    def mirror(bb, i):
        return (bb * nb + nb - 1 - i, 0)

    def mirror_next(bb, i):
        return (bb * nb + jnp.minimum(nb - i, nb - 1), 0)

    spec = lambda f: pl.BlockSpec((tm, fw), f)
    return pl.pallas_call(
        _dft_fold_kernel,
        grid=(B, nh),
        in_specs=[spec(own), spec(mirror), spec(mirror_next), spec(own), spec(mirror), spec(mirror_next)],
        out_specs=[pl.BlockSpec((tm, fw), lambda bb, i: (bb * nh + i, 0)),
                   pl.BlockSpec((tm, fw), lambda bb, i: (bb * nh + i, 0))],
        out_shape=[jax.ShapeDtypeStruct((B * S // 2, fw), BF16),
                   jax.ShapeDtypeStruct((B * S // 2, fw), BF16)],
        compiler_params=_cparams(("parallel", "arbitrary"), 40),
        name="dft_fold",
    )(a, a, a, b, b, b)


def _seq_dft_kernel(cs_ref, sn_ref, ae_ref, bo_ref, mid_ref, y_ref, acc_s, *, scale):
    kk = pl.program_id(3)

    @pl.when(kk == 0)
    def _():
        acc_s[...] = jnp.zeros(acc_s.shape, F32)

    acc_s[...] += (jnp.dot(cs_ref[...], ae_ref[...], preferred_element_type=F32)
                   + jnp.dot(sn_ref[...], bo_ref[...], preferred_element_type=F32))

    @pl.when(kk == pl.num_programs(3) - 1)
    def _():
        row = lax.broadcasted_iota(jnp.int32, acc_s.shape, 0)
        sign = jnp.where(row % 2 == 0, scale, -scale)
        y_ref[...] = (acc_s[...] + sign * mid_ref[0:1, :].astype(F32)).astype(y_ref.dtype)


def _seq_dft(cs, sn, ae, bo, a, B, S, scale):
    fw = a.shape[1]
    tm = _tile(S, 1024)
    tn = _tile(fw, 1024)
    tk = _tile(S // 2, 1024)
    nm, nk = S // tm, S // 2 // tk
    mid_rows = 16
    return pl.pallas_call(
        functools.partial(_seq_dft_kernel, scale=scale),
        grid=(B, nm, fw // tn, nk),
        in_specs=[pl.BlockSpec((tm, tk), lambda bb, i, j, k: (i, k)),
                  pl.BlockSpec((tm, tk), lambda bb, i, j, k: (i, k)),
                  pl.BlockSpec((tk, tn), lambda bb, i, j, k: (bb * nk + k, j)),
                  pl.BlockSpec((tk, tn), lambda bb, i, j, k: (bb * nk + k, j)),
                  pl.BlockSpec((mid_rows, tn),
                               lambda bb, i, j, k: ((bb * S + S // 2) // mid_rows, j))],
        out_specs=pl.BlockSpec((tm, tn), lambda bb, i, j, k: (bb * nm + i, j)),
        out_shape=jax.ShapeDtypeStruct((B * S, fw), BF16),
        scratch_shapes=[pltpu.VMEM((tm, tn), F32)],
        compiler_params=_cparams(("parallel", "parallel", "parallel", "arbitrary"), 48),
        name="seq_dft",
    )(cs, sn, ae, bo, a)


def _wo_kernel(ym_ref, yf_ref, gm_ref, gf_ref, w_ref, x_ref, mu_ref, rs_ref, lg_ref, lb_ref,
               pre_ref, mix_s, *, alpha, mw):
    @pl.when(pl.program_id(1) == 0)
    def _():
        ym = ym_ref[...].astype(F32)
        mix_s[:, :mw] = (ym * lax.rsqrt(jnp.mean(ym * ym, axis=-1, keepdims=True) + RMS_EPS)
                         * gm_ref[...]).astype(BF16)
        yf = yf_ref[...].astype(F32)
        mix_s[:, mw:] = (yf * lax.rsqrt(jnp.mean(yf * yf, axis=-1, keepdims=True) + RMS_EPS)
                         * gf_ref[...]).astype(BF16)

    h = (x_ref[...] - mu_ref[...]) * rs_ref[...] * lg_ref[...] + lb_ref[...]
    pre_ref[...] = alpha * h + jnp.dot(mix_s[...], w_ref[...], preferred_element_type=F32)


def _wo(ym, yf, gm, gf, w_o, x2, mu, rs, ln_g, ln_b, alpha):
    T, mw = ym.shape
    fw = yf.shape[1]
    D = w_o.shape[1]
    tm = _tile(T, 1024)
    tn = _tile(D, 512)
    return pl.pallas_call(
        functools.partial(_wo_kernel, alpha=alpha, mw=mw),
        grid=(T // tm, D // tn),
        in_specs=[pl.BlockSpec((tm, mw), lambda i, j: (i, 0)),
                  pl.BlockSpec((tm, fw), lambda i, j: (i, 0)),
                  pl.BlockSpec((1, mw), lambda i, j: (0, 0)),
                  pl.BlockSpec((1, fw), lambda i, j: (0, 0)),
                  pl.BlockSpec((mw + fw, tn), lambda i, j: (0, j)),
                  pl.BlockSpec((tm, tn), lambda i, j: (i, j)),
                  pl.BlockSpec((tm, 1), lambda i, j: (i, 0)),
                  pl.BlockSpec((tm, 1), lambda i, j: (i, 0)),
                  pl.BlockSpec((1, tn), lambda i, j: (0, j)),
                  pl.BlockSpec((1, tn), lambda i, j: (0, j))],
        out_specs=pl.BlockSpec((tm, tn), lambda i, j: (i, j)),
        out_shape=jax.ShapeDtypeStruct((T, D), F32),
        scratch_shapes=[pltpu.VMEM((tm, mw + fw), BF16)],
        compiler_params=_cparams(("parallel", "arbitrary"), 56),
        name="w_o_residual",
    )(ym, yf, gm.reshape(1, mw), gf.reshape(1, fw), w_o, x2, mu, rs,
      ln_g.reshape(1, D), ln_b.reshape(1, D))


def _pack_bf16_pair(lo, hi):
    lo_bits = pltpu.bitcast(lo.astype(BF16).astype(F32), jnp.uint32)
    hi_bits = pltpu.bitcast(hi.astype(BF16).astype(F32), jnp.uint32)
    return (lo_bits >> 16) | (hi_bits & jnp.uint32(0xFFFF0000))


def _unpack_bf16_pair(w):
    lo = pltpu.bitcast(w << 16, F32).astype(BF16)
    hi = pltpu.bitcast(w & jnp.uint32(0xFFFF0000), F32).astype(BF16)
    return lo, hi


def _ln1_router_kernel(pre_ref, g_ref, b_ref, rw_ref, rb_ref, x1_ref, x1p_ref, idx_ref, gate_ref):
    x = pre_ref[...]
    mu = jnp.mean(x, axis=-1, keepdims=True)
    xc = x - mu
    var = jnp.mean(xc * xc, axis=-1, keepdims=True)
    x1 = xc * lax.rsqrt(var + LN_EPS) * g_ref[...] + b_ref[...]
    x1_ref[...] = x1
    half = x1.shape[1] // 2
    x1p_ref[...] = _pack_bf16_pair(x1[:, :half], x1[:, half:])
    logits = lax.dot_general(rw_ref[...], x1, (((1,), (1,)), ((), ())),
                             precision=lax.Precision.HIGHEST, preferred_element_type=F32) + rb_ref[...]
    n_e = logits.shape[0]
    eidx = lax.broadcasted_iota(jnp.int32, logits.shape, 0)
    vals, idxs = [], []
    for _ in range(TOP_K):
        m = jnp.max(logits, axis=0, keepdims=True)
        sel = jnp.min(jnp.where(logits == m, eidx, n_e), axis=0, keepdims=True)
        logits = jnp.where(eidx == sel, -jnp.inf, logits)
        vals.append(m)
        idxs.append(sel)
    exps = [jnp.exp(v - vals[0]) for v in vals]
    denom = exps[0] + exps[1] + exps[2] + exps[3]
    for kk in range(TOP_K):
        idx_ref[kk:kk + 1, :] = idxs[kk]
        gate_ref[kk:kk + 1, :] = exps[kk] / denom


def _ln1_router(pre, g, b, router_w, router_b):
    T, D = pre.shape
    E = router_w.shape[1]
    tm = _tile(T, 256)
    return pl.pallas_call(
        _ln1_router_kernel,
        grid=(T // tm,),
        in_specs=[pl.BlockSpec((tm, D), lambda i: (i, 0)),
                  pl.BlockSpec((1, D), lambda i: (0, 0)),
                  pl.BlockSpec((1, D), lambda i: (0, 0)),
                  pl.BlockSpec((E, D), lambda i: (0, 0)),
                  pl.BlockSpec((E, 1), lambda i: (0, 0))],
        out_specs=[pl.BlockSpec((tm, D), lambda i: (i, 0)),
                   pl.BlockSpec((tm, D // 2), lambda i: (i, 0)),
                   pl.BlockSpec((TOP_K, tm), lambda i: (0, i)),
                   pl.BlockSpec((TOP_K, tm), lambda i: (0, i))],
        out_shape=[jax.ShapeDtypeStruct((T, D), F32),
                   jax.ShapeDtypeStruct((T, D // 2), jnp.uint32),
                   jax.ShapeDtypeStruct((TOP_K, T), jnp.int32),
                   jax.ShapeDtypeStruct((TOP_K, T), F32)],
        compiler_params=_cparams(("parallel",), 48),
        name="ln1_router",
    )(pre, g.reshape(1, D), b.reshape(1, D), router_w.T, router_b.reshape(E, 1))


def _row_copy(src_hbm, dst_ref, src_row, dst_row, sem):
    return pltpu.make_async_copy(src_hbm.at[pl.ds(src_row, 1)], dst_ref.at[pl.ds(dst_row, 1)], sem)


def _dispatch_kernel(dest_ref, pad_ref, npad_ref, nv_ref, x_ref, o_hbm, zero_s, sem, zsem, *, n_steps):
    b = pl.program_id(0)
    tt = x_ref.shape[0]

    def wait_rows(n):
        pltpu.make_async_copy(o_hbm.at[pl.ds(0, n)], o_hbm.at[pl.ds(0, n)], sem).wait()

    @pl.when(b < n_steps)
    def _():
        def issue(r, c):
            for kk in range(TOP_K):
                _row_copy(x_ref, o_hbm, r, dest_ref[(b * tt + r) * TOP_K + kk], sem).start()
            return c
        lax.fori_loop(0, tt, issue, 0, unroll=2)
        wait_rows(tt * TOP_K)

    @pl.when(b == n_steps)
    def _():
        n_pad = npad_ref[0]

        def issue_pad(i, c):
            _row_copy(x_ref, o_hbm, 0, pad_ref[i], sem).start()
            return c
        lax.fori_loop(0, n_pad, issue_pad, 0)

        def wait_pad(i, c):
            wait_rows(1)
            return c
        lax.fori_loop(0, n_pad, wait_pad, 0)

        zero_s[...] = jnp.zeros(zero_s.shape, zero_s.dtype)
        n_sub = MOE_CHUNK // MOE_SUB

        def zero_copy(i):
            return pltpu.make_async_copy(zero_s, o_hbm.at[pl.ds(pl.multiple_of(i * MOE_SUB, MOE_SUB), MOE_SUB)], zsem)

        def empty(i):
            return (i % n_sub) * MOE_SUB >= nv_ref[i // n_sub]

        def issue_zero(i, c):
            @pl.when(empty(i))
            def _():
                zero_copy(i).start()
            return c
        lax.fori_loop(0, nv_ref.shape[0] * n_sub, issue_zero, 0)

        def wait_zero(i, c):
            @pl.when(empty(i))
            def _():
                zero_copy(i).wait()
            return c
        lax.fori_loop(0, nv_ref.shape[0] * n_sub, wait_zero, 0)


def _dispatch(dest, pad_slots, n_pad, chunk_nv, x1p):
    T, W = x1p.shape
    tt = _tile(T, DISPATCH_TOKENS)
    n_steps = T // tt
    grid_spec = pltpu.PrefetchScalarGridSpec(
        num_scalar_prefetch=4,
        grid=(n_steps + 1,),
        in_specs=[pl.BlockSpec((tt, W), lambda b, d, p, n, v: (jnp.minimum(b, n_steps - 1), 0))],
        out_specs=pl.BlockSpec(memory_space=pl.ANY),
        scratch_shapes=[pltpu.VMEM((MOE_SUB, W), x1p.dtype),
                        pltpu.SemaphoreType.DMA(()), pltpu.SemaphoreType.DMA(())],
    )
    return pl.pallas_call(
        functools.partial(_dispatch_kernel, n_steps=n_steps),
        grid_spec=grid_spec,
        out_shape=jax.ShapeDtypeStruct((chunk_nv.shape[0] * MOE_CHUNK, W), x1p.dtype),
        compiler_params=_cparams(("arbitrary",), 32),
        name="moe_dispatch",
    )(dest, pad_slots, n_pad, chunk_nv, x1p)


def _expert_up_kernel(ce_ref, nv_ref, nu_ref, x_ref, wg_ref, wu_ref, bg_ref, bu_ref, h_ref):
    c = pl.program_id(0)
    nv = nv_ref[c]

    n_sub = MOE_CHUNK // MOE_SUB
    n_act = (nv + MOE_SUB - 1) // MOE_SUB
    for k in range(n_sub + 1):
        @pl.when(n_act == k)
        def _(k=k):
            if k > 0:
                x_lo, x_hi = _unpack_bf16_pair(x_ref[:k * MOE_SUB, :])
                half = x_lo.shape[1]

                def proj(w_ref):
                    return (jnp.dot(x_lo, w_ref[:half, :].astype(BF16), preferred_element_type=F32)
                            + jnp.dot(x_hi, w_ref[half:, :].astype(BF16), preferred_element_type=F32))
                hg = jnp.minimum(proj(wg_ref) + bg_ref[...], SWIGLU_LIMIT)
                hu = jnp.clip(proj(wu_ref) + bu_ref[...], -SWIGLU_LIMIT, SWIGLU_LIMIT)
                act = (hu + 1.0) * (hg * jax.nn.sigmoid(SWIGLU_ALPHA * hg))
                h_ref[:k * MOE_SUB, :] = act.astype(h_ref.dtype)
            if k < n_sub:
                h_ref[k * MOE_SUB:, :] = jnp.zeros(((n_sub - k) * MOE_SUB, h_ref.shape[1]), h_ref.dtype)


def _expert_up(chunk_e, chunk_nv, n_used, xs, w_gate, w_up, b_gate, b_up, n_chunks):
    E, D, F = w_gate.shape
    tf = _tile(F, 256)
    nf = F // tf

    def used(c, nu):
        return jnp.minimum(c, nu[0] - 1)

    def jeff(c, j, nu):
        return jnp.where(c < nu[0], j, nf - 1)

    grid_spec = pltpu.PrefetchScalarGridSpec(
        num_scalar_prefetch=3,
        grid=(n_chunks, nf),
        in_specs=[pl.BlockSpec((MOE_CHUNK, D // 2), lambda c, j, ce, nv, nu: (used(c, nu), 0)),
                  pl.BlockSpec((None, D, tf), lambda c, j, ce, nv, nu: (ce[c], 0, jeff(c, j, nu))),
                  pl.BlockSpec((None, D, tf), lambda c, j, ce, nv, nu: (ce[c], 0, jeff(c, j, nu))),
                  pl.BlockSpec((None, 1, tf), lambda c, j, ce, nv, nu: (ce[c], 0, jeff(c, j, nu))),
                  pl.BlockSpec((None, 1, tf), lambda c, j, ce, nv, nu: (ce[c], 0, jeff(c, j, nu)))],
        out_specs=pl.BlockSpec((MOE_CHUNK, tf), lambda c, j, ce, nv, nu: (c, j)),
    )
    return pl.pallas_call(
        _expert_up_kernel,
        grid_spec=grid_spec,
        out_shape=jax.ShapeDtypeStruct((n_chunks * MOE_CHUNK, F), BF16),
        compiler_params=_cparams(("arbitrary", "arbitrary"), 60),
        name="expert_gate_up",
    )(chunk_e, chunk_nv, n_used, xs, w_gate, w_up, b_gate.reshape(E, 1, F), b_up.reshape(E, 1, F))


def _expert_down_kernel(ce_ref, nv_ref, nu_ref, h_ref, wd_ref, bd_ref, y_ref):
    c = pl.program_id(0)
    nv = nv_ref[c]

    n_sub = MOE_CHUNK // MOE_SUB
    n_act = (nv + MOE_SUB - 1) // MOE_SUB
    for k in range(n_sub + 1):
        @pl.when(n_act == k)
        def _(k=k):
            if k > 0:
                out = jnp.dot(h_ref[:k * MOE_SUB, :], wd_ref[...].astype(BF16),
                              preferred_element_type=F32) + bd_ref[...]
                hw = out.shape[1] // 2
                y_ref[:k * MOE_SUB, :] = _pack_bf16_pair(out[:, :hw], out[:, hw:])
            if k < n_sub:
                y_ref[k * MOE_SUB:, :] = jnp.zeros(((n_sub - k) * MOE_SUB, y_ref.shape[1]), y_ref.dtype)


def _expert_down(chunk_e, chunk_nv, n_used, hmid, w_down, b_down, n_chunks):
    E, F, D = w_down.shape
    tn = _tile(D, MOE_DOWN_TN)
    nn = D // tn

    def used(c, nu):
        return jnp.minimum(c, nu[0] - 1)

    def jeff(c, j, nu):
        return jnp.where(c < nu[0], j, nn - 1)

    grid_spec = pltpu.PrefetchScalarGridSpec(
        num_scalar_prefetch=3,
        grid=(n_chunks, nn),
        in_specs=[pl.BlockSpec((MOE_CHUNK, F), lambda c, j, ce, nv, nu: (used(c, nu), 0)),
                  pl.BlockSpec((None, F, tn), lambda c, j, ce, nv, nu: (ce[c], 0, jeff(c, j, nu))),
                  pl.BlockSpec((None, 1, tn), lambda c, j, ce, nv, nu: (ce[c], 0, jeff(c, j, nu)))],
        out_specs=pl.BlockSpec((MOE_CHUNK, tn // 2), lambda c, j, ce, nv, nu: (c, j)),
    )
    return pl.pallas_call(
        _expert_down_kernel,
        grid_spec=grid_spec,
        out_shape=jax.ShapeDtypeStruct((n_chunks * MOE_CHUNK, D // 2), jnp.uint32),
        compiler_params=_cparams(("arbitrary", "arbitrary"), 56),
        name="expert_down",
    )(chunk_e, chunk_nv, n_used, hmid, w_down, b_down.reshape(E, 1, D))


def _combine_kernel(dest_ref, y_hbm, x1_ref, gate_ref, g_ref, b_ref, o_ref, *scratch, alpha, tn):
    i = pl.program_id(0)
    n = pl.num_programs(0)
    nb = COMBINE_BUFS
    tt = o_ref.shape[0] // nb
    bufs, sem = scratch[:nb], scratch[nb]

    def issue_tile(tile, slot):
        base = jnp.minimum(tile, nb * n - 1) * tt
        for r in range(tt):
            for kk in range(TOP_K):
                _row_copy(y_hbm, bufs[slot].at[kk], dest_ref[(base + r) * TOP_K + kk], r, sem.at[slot]).start()

    def wait_tile(slot):
        for kk in range(TOP_K):
            pltpu.make_async_copy(y_hbm.at[pl.ds(0, tt)], bufs[slot].at[kk], sem.at[slot]).wait()

    def finish_tile(slot):
        rows = slice(slot * tt, (slot + 1) * tt)
        gates = gate_ref[rows, :]
        y_lo = y_hi = None
        for kk in range(TOP_K):
            w = bufs[slot][kk]
            g_k = gates[:, kk:kk + 1]
            lo = g_k * pltpu.bitcast(w << 16, F32)
            hi = g_k * pltpu.bitcast(w & jnp.uint32(0xFFFF0000), F32)
            y_lo = lo if y_lo is None else y_lo + lo
            y_hi = hi if y_hi is None else y_hi + hi
        hw = tn // 2
        pieces = []
        for j in range(x1_ref.shape[1] // tn):
            pieces.append((slice(j * tn, j * tn + hw), y_lo[:, j * hw:(j + 1) * hw]))
            pieces.append((slice(j * tn + hw, (j + 1) * tn), y_hi[:, j * hw:(j + 1) * hw]))
        zs = [alpha * x1_ref[rows, cols] + y for cols, y in pieces]
        d = x1_ref.shape[1]
        mu = sum(jnp.sum(z, axis=-1, keepdims=True) for z in zs) / d
        zcs = [z - mu for z in zs]
        var = sum(jnp.sum(zc * zc, axis=-1, keepdims=True) for zc in zcs) / d
        rs = lax.rsqrt(var + LN_EPS)
        for (cols, _), zc in zip(pieces, zcs):
            o_ref[rows, cols] = zc * rs * g_ref[:, cols] + b_ref[:, cols]

    @pl.when(i == 0)
    def _():
        for t in range(COMBINE_AHEAD):
            issue_tile(t, t)

    for t in range(nb):
        wait_tile(t)
        issue_tile(nb * i + t + COMBINE_AHEAD, (t + COMBINE_AHEAD) % nb)
        finish_tile(t)

    @pl.when(i == n - 1)
    def _():
        for t in range(COMBINE_AHEAD):
            wait_tile(t)


def _combine(dest, yslots, x1, gates_tk, g, b, alpha):
    T, D = x1.shape
    nb = COMBINE_BUFS
    tt = _tile(T // nb, COMBINE_TOKENS)
    grid_spec = pltpu.PrefetchScalarGridSpec(
        num_scalar_prefetch=1,
        grid=(T // (nb * tt),),
        in_specs=[pl.BlockSpec(memory_space=pl.ANY),
                  pl.BlockSpec((nb * tt, D), lambda i, d: (i, 0)),
                  pl.BlockSpec((nb * tt, TOP_K), lambda i, d: (i, 0)),
                  pl.BlockSpec((1, D), lambda i, d: (0, 0)),
                  pl.BlockSpec((1, D), lambda i, d: (0, 0))],
        out_specs=pl.BlockSpec((nb * tt, D), lambda i, d: (i, 0)),
        scratch_shapes=[pltpu.VMEM((TOP_K, tt, D // 2), jnp.uint32) for _ in range(nb)]
        + [pltpu.SemaphoreType.DMA((nb,))],
    )
    return pl.pallas_call(
        functools.partial(_combine_kernel, alpha=alpha, tn=_tile(D, MOE_DOWN_TN)),
        grid_spec=grid_spec,
        out_shape=jax.ShapeDtypeStruct((T, D), F32),
        compiler_params=_cparams(("arbitrary",), 48),
        name="moe_combine_ln2",
    )(dest, yslots, x1, gates_tk, g.reshape(1, D), b.reshape(1, D))


def _routing_tables(top_idx, n_experts, n_chunks):
    T = top_idx.shape[1]
    M = T * TOP_K
    flat_e = top_idx.T.reshape(M)
    onehot = (flat_e[:, None] == jnp.arange(n_experts, dtype=jnp.int32)[None, :]).astype(jnp.int32)
    csum = jnp.cumsum(onehot, axis=0)
    rank = jnp.sum(csum * onehot, axis=1) - 1
    counts = csum[-1]
    chunks_e = (counts + MOE_CHUNK - 1) // MOE_CHUNK
    chunk_end = jnp.cumsum(chunks_e)
    chunk_start = chunk_end - chunks_e
    n_used = chunk_end[-1]
    dest = chunk_start[flat_e] * MOE_CHUNK + rank
    P = n_chunks * MOE_CHUNK
    pad_idx = counts[:, None] + jnp.arange(MOE_SUB, dtype=jnp.int32)[None, :]
    padded = (counts + MOE_SUB - 1) // MOE_SUB * MOE_SUB
    pad_slots = jnp.where(pad_idx < padded[:, None], chunk_start[:, None] * MOE_CHUNK + pad_idx, P)
    pad_slots = jnp.sort(pad_slots.reshape(-1)).astype(jnp.int32)
    n_pad = jnp.sum(padded - counts).astype(jnp.int32).reshape(1)
    cid = jnp.arange(n_chunks, dtype=jnp.int32)
    chunk_e = jnp.minimum(jnp.searchsorted(chunk_end, cid, side='right'), n_experts - 1).astype(jnp.int32)
    last_e = chunk_e[jnp.maximum(n_used - 1, 0)]
    chunk_e = jnp.where(cid < n_used, chunk_e, last_e)
    chunk_nv = jnp.where(cid < n_used,
                         jnp.clip(counts[chunk_e] - (cid - chunk_start[chunk_e]) * MOE_CHUNK, 0, MOE_CHUNK),
                         0).astype(jnp.int32)
    return (dest.astype(jnp.int32), pad_slots, n_pad, chunk_e, chunk_nv,
            n_used.astype(jnp.int32).reshape(1))


def kernel(x, positions, ln_in_g, ln_in_b, w_in, q_a_norm_g, w_q_b, kv_a_norm_g, w_kv_b, mla_out_norm_g, fourier_out_norm_g, w_o, ln1_g, ln1_b, router_w, router_b, w_gate, b_gate, w_up, b_up, w_down, b_down, ln2_g, ln2_b):
    B, S, D = x.shape
    T = B * S
    depth = w_in.shape[0]
    assert depth == 1, "single-layer trunk only"
    qr = q_a_norm_g.shape[1]
    kvr = kv_a_norm_g.shape[1]
    H = w_q_b.shape[2] // (QK_NOPE_DIM + QK_ROPE_DIM)
    fw = fourier_out_norm_g.shape[1]
    E = router_w.shape[2]
    assert H % HEADS_PER_TILE == 0 and (qr + kvr) % LANES == 0 and S % (8 * DFT_ROW_SPLIT) == 0
    alpha = (2.0 * depth) ** 0.25

    inv_freq = ROPE_THETA ** (-jnp.arange(0, QK_ROPE_DIM, 2, dtype=F32) / QK_ROPE_DIM)
    ang = positions.astype(F32)[..., None] * inv_freq
    cos4 = jnp.tile(jnp.cos(ang), (1, 1, 2 * LANES // QK_ROPE_DIM)).reshape(T, LANES)
    sin4 = jnp.tile(jnp.sin(ang), (1, 1, 2 * LANES // QK_ROPE_DIM)).reshape(T, LANES)

    rope_end = qr + kvr + QK_ROPE_DIM
    w_a = w_in[0, :, :qr + kvr + LANES].astype(BF16)
    w_f = w_in[0, :, rope_end:].astype(BF16)
    wq = w_q_b[0].reshape(qr, H // HEADS_PER_TILE, HEADS_PER_TILE, QK_NOPE_DIM + QK_ROPE_DIM)
    wq_perm = jnp.concatenate(
        [wq[..., :QK_NOPE_DIM].reshape(qr, H // HEADS_PER_TILE, HEADS_PER_TILE * QK_NOPE_DIM),
         wq[..., QK_NOPE_DIM:].reshape(qr, H // HEADS_PER_TILE, HEADS_PER_TILE * QK_ROPE_DIM)],
        axis=-1).reshape(qr, H * (QK_NOPE_DIM + QK_ROPE_DIM)).astype(BF16)
    w_kv = w_kv_b[0].astype(BF16)
    w_o_b = w_o[0].astype(BF16)
    ch = jnp.arange(FOURIER_GROUP_DIM, dtype=jnp.int32)
    ang_c = (2.0 * math.pi / FOURIER_GROUP_DIM) * ((ch[:, None] * ch[None, :]) % FOURIER_GROUP_DIM).astype(F32)
    cs_tab = jnp.concatenate([jnp.cos(ang_c), jnp.sin(ang_c)], axis=1).astype(BF16)

    x2 = x.reshape(T, D)
    hb, mu, rs = _ln_in(x2, ln_in_g, ln_in_b)

    cq, ckv, kpe2 = _inproj_a(hb, w_a, q_a_norm_g[0], kv_a_norm_g[0], cos4, sin4)
    qscale = (QK_NOPE_DIM + QK_ROPE_DIM) ** -0.5 * LOG2E
    q = _q_up(cq, wq_perm, cos4, sin4, qscale)
    k, v = _kv_up(ckv, w_kv, kpe2)
    y_mla = _attention(q, k, v, B, S, H)

    fa, fb = _inproj_f(hb, w_f, cs_tab)
    cs_mat, sn_mat = _dft_matrices(S, FOURIER_GROUP_DIM)
    fae, fbo = _dft_fold(fa, fb, B, S)
    y_f = _seq_dft(cs_mat, sn_mat, fae, fbo, fa, B, S, 1.0 / math.sqrt(S * FOURIER_GROUP_DIM))

    pre = _wo(y_mla, y_f, mla_out_norm_g[0], fourier_out_norm_g[0], w_o_b, x2, mu, rs,
              ln_in_g, ln_in_b, alpha)
    x1, x1p, top_idx, gates = _ln1_router(pre, ln1_g[0], ln1_b[0], router_w[0], router_b[0])

    n_chunks = -(-T * TOP_K // MOE_CHUNK) + E
    dest, pad_slots, n_pad, chunk_e, chunk_nv, n_used = _routing_tables(top_idx, E, n_chunks)
    xs = _dispatch(dest, pad_slots, n_pad, chunk_nv, x1p)
    hmid = _expert_up(chunk_e, chunk_nv, n_used, xs, w_gate[0], w_up[0], b_gate[0], b_up[0], n_chunks)
    yslots = _expert_down(chunk_e, chunk_nv, n_used, hmid, w_down[0], b_down[0], n_chunks)
    out = _combine(dest, yslots, x1, gates.T, ln2_g[0], ln2_b[0], alpha)
    return out.reshape(B, S, D)
```

```python
import functools
import math

import jax
import jax.numpy as jnp
from jax import lax
from jax.experimental import pallas as pl
from jax.experimental.pallas import tpu as pltpu

F32 = jnp.float32
BF16 = jnp.bfloat16

V_HEAD_DIM = 128
QK_NOPE_DIM = 128
QK_ROPE_DIM = 64
QK_PAD_DIM = 256
V_PAD_DIM = 256
ROPE_THETA = 10000.0
FOURIER_GROUP_DIM = 128
TOP_K = 4
SWIGLU_LIMIT = 7.0
SWIGLU_ALPHA = 1.702
LN_EPS = 1e-5
RMS_EPS = 1e-6
LOG2E = 1.4426950408889634

LANES = 128
V7X_VMEM_BYTES = 64 * 1024 * 1024
HEADS_PER_TILE = 4
DFT_ROW_SPLIT = 64
ATTN_TQ = 1024
ATTN_TKV = 1024

MOE_CHUNK = 1280
MOE_SUB = 256
DISPATCH_TOKENS = 512
COMBINE_TOKENS = 64
COMBINE_BUFS = 4
COMBINE_AHEAD = 2
MOE_DOWN_TN = 1024


def _cparams(semantics, vmem_mb):
    return pltpu.CompilerParams(dimension_semantics=semantics,
                                vmem_limit_bytes=min(vmem_mb * 1024 * 1024, V7X_VMEM_BYTES - (4 << 20)))


def _tile(dim, pref):
    t = min(dim, pref)
    while dim % t:
        t //= 2
    return t


def _ln_in_kernel(x_ref, g_ref, b_ref, hb_ref, mu_ref, rs_ref):
    x = x_ref[...]
    mu = jnp.mean(x, axis=-1, keepdims=True)
    xc = x - mu
    var = jnp.mean(xc * xc, axis=-1, keepdims=True)
    rs = lax.rsqrt(var + LN_EPS)
    hb_ref[...] = (xc * rs * g_ref[...] + b_ref[...]).astype(BF16)
    mu_ref[...] = mu
    rs_ref[...] = rs


def _ln_in(x2, g, b):
    T, D = x2.shape
    tm = _tile(T, 256)
    return pl.pallas_call(
        _ln_in_kernel,
        grid=(T // tm,),
        in_specs=[pl.BlockSpec((tm, D), lambda i: (i, 0)),
                  pl.BlockSpec((1, D), lambda i: (0, 0)),
                  pl.BlockSpec((1, D), lambda i: (0, 0))],
        out_specs=[pl.BlockSpec((tm, D), lambda i: (i, 0)),
                   pl.BlockSpec((tm, 1), lambda i: (i, 0)),
                   pl.BlockSpec((tm, 1), lambda i: (i, 0))],
        out_shape=[jax.ShapeDtypeStruct((T, D), BF16),
                   jax.ShapeDtypeStruct((T, 1), F32),
                   jax.ShapeDtypeStruct((T, 1), F32)],
        compiler_params=_cparams(("parallel",), 40),
        name="ln_in",
    )(x2, g.reshape(1, D), b.reshape(1, D))


def _rope128(p, cos4, sin4):
    lane = lax.broadcasted_iota(jnp.int32, p.shape, 1)
    first_half = (lane % QK_ROPE_DIM) < (QK_ROPE_DIM // 2)
    rot = jnp.where(first_half, -pltpu.roll(p, LANES - QK_ROPE_DIM // 2, 1),
                    pltpu.roll(p, QK_ROPE_DIM // 2, 1))
    return p * cos4 + rot * sin4


def _inproj_a_kernel(h_ref, w_ref, gq_ref, gkv_ref, cos_ref, sin_ref,
                     cq_ref, ckv_ref, kpe_ref, *, qr, kvr):
    acc = jnp.dot(h_ref[...], w_ref[...], preferred_element_type=F32)
    cq = acc[:, :qr]
    cq_ref[...] = (cq * lax.rsqrt(jnp.mean(cq * cq, axis=-1, keepdims=True) + RMS_EPS)
                   * gq_ref[...]).astype(BF16)
    ckv = acc[:, qr:qr + kvr]
    ckv_ref[...] = (ckv * lax.rsqrt(jnp.mean(ckv * ckv, axis=-1, keepdims=True) + RMS_EPS)
                    * gkv_ref[...]).astype(BF16)
    roped = _rope128(acc[:, qr + kvr:qr + kvr + LANES], cos_ref[...], sin_ref[...])
    lane = lax.broadcasted_iota(jnp.int32, roped.shape, 1)
    even = jnp.where(lane < QK_ROPE_DIM, roped, 0.0)
    kpe_ref[:, :LANES] = even.astype(BF16)
    kpe_ref[:, LANES:] = pltpu.roll(even, QK_ROPE_DIM, 1).astype(BF16)


def _inproj_a(hb, w_a, gq, gkv, cos4, sin4):
    T, D = hb.shape
    qr, kvr = gq.shape[0], gkv.shape[0]
    wa = w_a.shape[1]
    tm = _tile(T, 512)
    return pl.pallas_call(
        functools.partial(_inproj_a_kernel, qr=qr, kvr=kvr),
        grid=(T // tm,),
        in_specs=[pl.BlockSpec((tm, D), lambda i: (i, 0)),
                  pl.BlockSpec((D, wa), lambda i: (0, 0)),
                  pl.BlockSpec((1, qr), lambda i: (0, 0)),
                  pl.BlockSpec((1, kvr), lambda i: (0, 0)),
                  pl.BlockSpec((tm, LANES), lambda i: (i, 0)),
                  pl.BlockSpec((tm, LANES), lambda i: (i, 0))],
        out_specs=[pl.BlockSpec((tm, qr), lambda i: (i, 0)),
                   pl.BlockSpec((tm, kvr), lambda i: (i, 0)),
                   pl.BlockSpec((tm, 2 * LANES), lambda i: (i, 0))],
        out_shape=[jax.ShapeDtypeStruct((T, qr), BF16),
                   jax.ShapeDtypeStruct((T, kvr), BF16),
                   jax.ShapeDtypeStruct((T, 2 * LANES), BF16)],
        compiler_params=_cparams(("parallel",), 56),
        name="inproj_mla",
    )(hb, w_a, gq.reshape(1, qr), gkv.reshape(1, kvr), cos4, sin4)


def _q_up_kernel(c_ref, w_ref, cos_ref, sin_ref, q_ref, *, qscale):
    acc = jnp.dot(c_ref[...], w_ref[...], preferred_element_type=F32)
    nope_w = HEADS_PER_TILE * QK_NOPE_DIM
    lane = lax.broadcasted_iota(jnp.int32, (acc.shape[0], LANES), 1)
    for pair in range(HEADS_PER_TILE // 2):
        roped = _rope128(acc[:, nope_w + pair * LANES:nope_w + (pair + 1) * LANES],
                         cos_ref[...], sin_ref[...]) * qscale
        for par in range(2):
            j = 2 * pair + par
            keep = (lane < QK_ROPE_DIM) if par == 0 else (lane >= QK_ROPE_DIM)
            base = j * QK_PAD_DIM
            q_ref[:, base:base + QK_NOPE_DIM] = (
                acc[:, j * QK_NOPE_DIM:(j + 1) * QK_NOPE_DIM] * qscale).astype(BF16)
            q_ref[:, base + QK_NOPE_DIM:base + QK_PAD_DIM] = jnp.where(keep, roped, 0.0).astype(BF16)


def _q_up(cq, wq_perm, cos4, sin4, qscale):
    T, qr = cq.shape
    n_tiles = wq_perm.shape[1] // (HEADS_PER_TILE * (QK_NOPE_DIM + QK_ROPE_DIM))
    tw = HEADS_PER_TILE * (QK_NOPE_DIM + QK_ROPE_DIM)
    to = HEADS_PER_TILE * QK_PAD_DIM
    tm = _tile(T, 1024)
    return pl.pallas_call(
        functools.partial(_q_up_kernel, qscale=qscale),
        grid=(T // tm, n_tiles),
        in_specs=[pl.BlockSpec((tm, qr), lambda i, j: (i, 0)),
                  pl.BlockSpec((qr, tw), lambda i, j: (0, j)),
                  pl.BlockSpec((tm, LANES), lambda i, j: (i, 0)),
                  pl.BlockSpec((tm, LANES), lambda i, j: (i, 0))],
        out_specs=pl.BlockSpec((tm, to), lambda i, j: (i, j)),
        out_shape=jax.ShapeDtypeStruct((T, n_tiles * to), BF16),
        compiler_params=_cparams(("parallel", "arbitrary"), 40),
        name="q_up",
    )(cq, wq_perm, cos4, sin4)


def _kv_up_kernel(c_ref, w_ref, kpe_ref, k_ref, v_ref):
    acc = jnp.dot(c_ref[...], w_ref[...], preferred_element_type=F32)
    for j in range(HEADS_PER_TILE):
        src = j * (QK_NOPE_DIM + V_HEAD_DIM)
        k_ref[:, j * QK_PAD_DIM:j * QK_PAD_DIM + QK_NOPE_DIM] = acc[:, src:src + QK_NOPE_DIM].astype(BF16)
        par = j % 2
        k_ref[:, j * QK_PAD_DIM + QK_NOPE_DIM:(j + 1) * QK_PAD_DIM] = kpe_ref[:, par * LANES:(par + 1) * LANES]
        v_ref[:, j * V_PAD_DIM:j * V_PAD_DIM + V_HEAD_DIM] = (
            acc[:, src + QK_NOPE_DIM:src + QK_NOPE_DIM + V_HEAD_DIM].astype(BF16))
        v_ref[:, j * V_PAD_DIM + V_HEAD_DIM:(j + 1) * V_PAD_DIM] = jnp.ones(
            (acc.shape[0], V_PAD_DIM - V_HEAD_DIM), BF16)


def _kv_up(ckv, w_kv, kpe2):
    T, kvr = ckv.shape
    tw = HEADS_PER_TILE * (QK_NOPE_DIM + V_HEAD_DIM)
    n_tiles = w_kv.shape[1] // tw
    tm = _tile(T, 1024)
    return pl.pallas_call(
        _kv_up_kernel,
        grid=(T // tm, n_tiles),
        in_specs=[pl.BlockSpec((tm, kvr), lambda i, j: (i, 0)),
                  pl.BlockSpec((kvr, tw), lambda i, j: (0, j)),
                  pl.BlockSpec((tm, 2 * LANES), lambda i, j: (i, 0))],
        out_specs=[pl.BlockSpec((tm, HEADS_PER_TILE * QK_PAD_DIM), lambda i, j: (i, j)),
                   pl.BlockSpec((tm, HEADS_PER_TILE * V_PAD_DIM), lambda i, j: (i, j))],
        out_shape=[jax.ShapeDtypeStruct((T, n_tiles * HEADS_PER_TILE * QK_PAD_DIM), BF16),
                   jax.ShapeDtypeStruct((T, n_tiles * HEADS_PER_TILE * V_PAD_DIM), BF16)],
        compiler_params=_cparams(("parallel", "arbitrary"), 40),
        name="kv_up",
    )(ckv, w_kv, kpe2)


def _lane_tile(x, reps):
    return jnp.concatenate([x] * reps, axis=1)


def _attn_kernel(q_ref, k_ref, v_ref, o_ref, m_s, acc_s, s0_s, s1_s, x0_s, x1_s, p0_s, p1_s, a0_s, a1_s,
                 *, tkv):
    n_kv = k_ref.shape[0] // tkv
    s_buf, x_buf, p_buf, a_buf = (s0_s, s1_s), (x0_s, x1_s), (p0_s, p1_s), (a0_s, a1_s)
    m_s[...] = jnp.full(m_s.shape, -jnp.inf, F32)
    acc_s[...] = jnp.zeros(acc_s.shape, F32)

    def scores(i, slot):
        off = pl.multiple_of(i * tkv, tkv)
        s = lax.dot_general(q_ref[...], k_ref[pl.ds(off, tkv), :], (((1,), (1,)), ((), ())),
                            preferred_element_type=F32)
        s_buf[slot][...] = s
        x_buf[slot][...] = jnp.broadcast_to(jnp.max(s, axis=1, keepdims=True), x_buf[slot].shape)

    def probs(slot):
        m_prev = m_s[...]
        m_new = jnp.maximum(m_prev, x_buf[slot][...])
        m_s[...] = m_new
        a_buf[slot][...] = jnp.exp2(m_prev - m_new)
        p_buf[slot][...] = jnp.exp2(s_buf[slot][...] - _lane_tile(m_new, tkv // LANES)).astype(BF16)

    def values(i, slot):
        off = pl.multiple_of(i * tkv, tkv)
        acc_s[...] = (_lane_tile(a_buf[slot][...], V_PAD_DIM // LANES) * acc_s[...]
                      + jnp.dot(p_buf[slot][...], v_ref[pl.ds(off, tkv), :], preferred_element_type=F32))

    def trip(i, slot):
        scores(i + 1, 1 - slot)
        values(i - 1, 1 - slot)
        probs(slot)

    scores(0, 0)
    scores(1, 1)
    probs(0)
    for i in range(1, n_kv - 1):
        trip(i, i % 2)
    values(n_kv - 2, 0)
    probs(1)
    values(n_kv - 1, 1)
    acc = acc_s[...]
    o_ref[...] = (acc[:, :V_HEAD_DIM] / acc[:, V_HEAD_DIM:]).astype(o_ref.dtype)


def _attention(q, k, v, B, S, H):
    T = B * S
    tq = _tile(S, ATTN_TQ)
    tkv = _tile(S, min(ATTN_TKV, S // 2))
    assert (S // tkv) % 2 == 0, "the key-chunk pipeline is unrolled in pairs"
    nq = S // tq
    return pl.pallas_call(
        functools.partial(_attn_kernel, tkv=tkv),
        grid=(B, H, nq),
        in_specs=[pl.BlockSpec((tq, QK_PAD_DIM), lambda b, h, i: (b * nq + i, h)),
                  pl.BlockSpec((S, QK_PAD_DIM), lambda b, h, i: (b, h)),
                  pl.BlockSpec((S, V_PAD_DIM), lambda b, h, i: (b, h))],
        out_specs=pl.BlockSpec((tq, V_HEAD_DIM), lambda b, h, i: (b * nq + i, h)),
        out_shape=jax.ShapeDtypeStruct((T, H * V_HEAD_DIM), BF16),
        scratch_shapes=[pltpu.VMEM((tq, LANES), F32), pltpu.VMEM((tq, V_PAD_DIM), F32),
                        pltpu.VMEM((tq, tkv), F32), pltpu.VMEM((tq, tkv), F32),
                        pltpu.VMEM((tq, LANES), F32), pltpu.VMEM((tq, LANES), F32),
                        pltpu.VMEM((tq, tkv), BF16), pltpu.VMEM((tq, tkv), BF16),
                        pltpu.VMEM((tq, LANES), F32), pltpu.VMEM((tq, LANES), F32)],
        compiler_params=_cparams(("parallel", "parallel", "arbitrary"), 40),
        name="mla_attention",
    )(q, k, v)


def _inproj_f_kernel(h_ref, w_ref, cs_ref, a_ref, b_ref, *, groups):
    acc = jnp.dot(h_ref[...], w_ref[...], preferred_element_type=F32)
    C = FOURIER_GROUP_DIM
    for g in range(groups):
        ab = jnp.dot(acc[:, g * C:(g + 1) * C].astype(BF16), cs_ref[...], preferred_element_type=F32)
        a_ref[:, g * C:(g + 1) * C] = ab[:, :C].astype(BF16)
        b_ref[:, g * C:(g + 1) * C] = ab[:, C:].astype(BF16)


def _inproj_f(hb, w_f, cs_tab):
    T, D = hb.shape
    fw = w_f.shape[1]
    tm = _tile(T, 1024)
    tn = _tile(fw, 512)
    return pl.pallas_call(
        functools.partial(_inproj_f_kernel, groups=tn // FOURIER_GROUP_DIM),
        grid=(T // tm, fw // tn),
        in_specs=[pl.BlockSpec((tm, D), lambda i, j: (i, 0)),
                  pl.BlockSpec((D, tn), lambda i, j: (0, j)),
                  pl.BlockSpec((FOURIER_GROUP_DIM, 2 * FOURIER_GROUP_DIM), lambda i, j: (0, 0))],
        out_specs=[pl.BlockSpec((tm, tn), lambda i, j: (i, j)),
                   pl.BlockSpec((tm, tn), lambda i, j: (i, j))],
        out_shape=[jax.ShapeDtypeStruct((T, fw), BF16), jax.ShapeDtypeStruct((T, fw), BF16)],
        compiler_params=_cparams(("parallel", "arbitrary"), 48),
        name="inproj_fourier",
    )(hb, w_f, cs_tab)


def _dft_gen_kernel(tac_ref, tas_ref, tbc_ref, tbs_ref, cs_ref, sn_ref):
    tbc = tbc_ref[...]
    tbs = tbs_ref[...]
    for aa in range(tac_ref.shape[0]):
        ca = tac_ref[aa:aa + 1, :]
        sa = tas_ref[aa:aa + 1, :]
        rows = slice(aa * DFT_ROW_SPLIT, (aa + 1) * DFT_ROW_SPLIT)
        cs_ref[rows, :] = (ca * tbc - sa * tbs).astype(BF16)
        sn_ref[rows, :] = (-(sa * tbc + ca * tbs)).astype(BF16)


def _dft_matrices(S, n_chan):
    na = S // 2 // DFT_ROW_SPLIT
    scale = 1.0 / math.sqrt(S * n_chan)
    col = jnp.arange(S // 2, dtype=jnp.int32)[None, :]
    period = S // DFT_ROW_SPLIT
    ang_a = (2.0 * math.pi / period) * ((jnp.arange(na, dtype=jnp.int32)[:, None] * col) % period).astype(F32)
    ang_b = (2.0 * math.pi / S) * ((jnp.arange(DFT_ROW_SPLIT, dtype=jnp.int32)[:, None] * col) % S).astype(F32)
    tac, tas = jnp.cos(ang_a), jnp.sin(ang_a)
    tbc, tbs = scale * jnp.cos(ang_b), scale * jnp.sin(ang_b)
    ta = min(8, na)
    assert na % ta == 0
    tc = _tile(S // 2, 2048)
    return pl.pallas_call(
        _dft_gen_kernel,
        grid=(na // ta, S // 2 // tc),
        in_specs=[pl.BlockSpec((ta, tc), lambda i, j: (i, j)),
                  pl.BlockSpec((ta, tc), lambda i, j: (i, j)),
                  pl.BlockSpec((DFT_ROW_SPLIT, tc), lambda i, j: (0, j)),
                  pl.BlockSpec((DFT_ROW_SPLIT, tc), lambda i, j: (0, j))],
        out_specs=[pl.BlockSpec((ta * DFT_ROW_SPLIT, tc), lambda i, j: (i, j)),
                   pl.BlockSpec((ta * DFT_ROW_SPLIT, tc), lambda i, j: (i, j))],
        out_shape=[jax.ShapeDtypeStruct((S // 2, S // 2), BF16), jax.ShapeDtypeStruct((S // 2, S // 2), BF16)],
        compiler_params=_cparams(("parallel", "parallel"), 40),
        name="dft_matrices",
    )(tac, tas, tbc, tbs)


def _dft_fold_kernel(a_ref, am_ref, an_ref, b_ref, bm_ref, bn_ref, ae_ref, bo_ref, alt_ref):
    tm = a_ref.shape[0]
    r = lax.broadcasted_iota(jnp.int32, (tm, tm), 0)
    c = lax.broadcasted_iota(jnp.int32, (tm, tm), 1)
    rev = jnp.where(r + c == tm, 1.0, 0.0).astype(BF16)
    has_next = (pl.program_id(1) > 0).astype(F32)
    first = (jnp.where(r + c == 0, 1.0, 0.0) * has_next).astype(BF16)

    def mirrored(m_ref, n_ref):
        return (jnp.dot(rev, m_ref[...], preferred_element_type=F32)
                + jnp.dot(first, n_ref[...], preferred_element_type=F32))

    ae_ref[...] = (a_ref[...].astype(F32) + mirrored(am_ref, an_ref)).astype(BF16)
    bo_ref[...] = (b_ref[...].astype(F32) - mirrored(bm_ref, bn_ref)).astype(BF16)

    @pl.when(pl.program_id(1) == 0)
    def _():
        alt_ref[...] = jnp.zeros(alt_ref.shape, F32)

    row = lax.broadcasted_iota(jnp.int32, a_ref.shape, 0)
    both = a_ref[...].astype(F32) + am_ref[...].astype(F32)
    alt_ref[0:1, :] += jnp.sum(jnp.where(row % 2 == 0, both, -both), axis=0, keepdims=True)


def _dft_fold(a, b, B, S):
    fw = a.shape[1]
    tm = _tile(S // 2, 256)
    nb = S // tm
    nh = nb // 2

    def own(bb, i):
        return (bb * nb + i, 0)

    def mirror(bb, i):
        return (bb * nb + nb - 1 - i, 0)

    def mirror_next(bb, i):
        return (bb * nb + jnp.minimum(nb - i, nb - 1), 0)

    spec = lambda f: pl.BlockSpec((tm, fw), f)
    return pl.pallas_call(
        _dft_fold_kernel,
        grid=(B, nh),
        in_specs=[spec(own), spec(mirror), spec(mirror_next), spec(own), spec(mirror), spec(mirror_next)],
        out_specs=[pl.BlockSpec((tm, fw), lambda bb, i: (bb * nh + i, 0)),
                   pl.BlockSpec((tm, fw), lambda bb, i: (bb * nh + i, 0)),
                   pl.BlockSpec((8, fw), lambda bb, i: (bb, 0))],
        out_shape=[jax.ShapeDtypeStruct((B * S // 2, fw), BF16),
                   jax.ShapeDtypeStruct((B * S // 2, fw), BF16),
                   jax.ShapeDtypeStruct((B * 8, fw), F32)],
        compiler_params=_cparams(("parallel", "arbitrary"), 40),
        name="dft_fold",
    )(a, a, a, b, b, b)


def _seq_dft_kernel(cs_ref, sn_ref, ae_ref, bo_ref, mid_ref, y_ref, d_ref, p_s, q_s, *, scale):
    kk = pl.program_id(3)

    @pl.when(kk == 0)
    def _():
        p_s[...] = jnp.zeros(p_s.shape, F32)
        q_s[...] = jnp.zeros(q_s.shape, F32)

    p_s[...] += jnp.dot(cs_ref[...], ae_ref[...], preferred_element_type=F32)
    q_s[...] += jnp.dot(sn_ref[...], bo_ref[...], preferred_element_type=F32)

    @pl.when(kk == pl.num_programs(3) - 1)
    def _():
        row = lax.broadcasted_iota(jnp.int32, p_s.shape, 0)
        pm = p_s[...] + jnp.where(row % 2 == 0, scale, -scale) * mid_ref[0:1, :].astype(F32)
        y_ref[...] = (pm + q_s[...]).astype(y_ref.dtype)
        d_ref[...] = (pm - q_s[...]).astype(d_ref.dtype)


def _seq_dft(cs, sn, ae, bo, a, B, S, scale):
    fw = a.shape[1]
    half = S // 2
    tm = _tile(half, 1024)
    tn = _tile(fw, 1024)
    tk = _tile(half, 1024)
    nm, nk = half // tm, half // tk
    mid_rows = 16
    return pl.pallas_call(
        functools.partial(_seq_dft_kernel, scale=scale),
        grid=(B, nm, fw // tn, nk),
        in_specs=[pl.BlockSpec((tm, tk), lambda bb, i, j, k: (i, k)),
                  pl.BlockSpec((tm, tk), lambda bb, i, j, k: (i, k)),
                  pl.BlockSpec((tk, tn), lambda bb, i, j, k: (bb * nk + k, j)),
                  pl.BlockSpec((tk, tn), lambda bb, i, j, k: (bb * nk + k, j)),
                  pl.BlockSpec((mid_rows, tn),
                               lambda bb, i, j, k: ((bb * S + half) // mid_rows, j))],
        out_specs=[pl.BlockSpec((tm, tn), lambda bb, i, j, k: (bb * nm + i, j)),
                   pl.BlockSpec((tm, tn), lambda bb, i, j, k: (bb * nm + i, j))],
        out_shape=[jax.ShapeDtypeStruct((B * half, fw), BF16),
                   jax.ShapeDtypeStruct((B * half, fw), BF16)],
        scratch_shapes=[pltpu.VMEM((tm, tn), F32), pltpu.VMEM((tm, tn), F32)],
        compiler_params=_cparams(("parallel", "parallel", "parallel", "arbitrary"), 48),
        name="seq_dft",
    )(cs, sn, ae, bo, a)


def _dft_mirror_kernel(yh_ref, dm_ref, dn_ref, alt_ref, y_ref, *, scale, nh):
    j = pl.program_id(1)

    @pl.when(j < nh)
    def _():
        y_ref[...] = yh_ref[...]

    @pl.when(j >= nh)
    def _():
        tm = dm_ref.shape[0]
        r = lax.broadcasted_iota(jnp.int32, (tm, tm), 0)
        c = lax.broadcasted_iota(jnp.int32, (tm, tm), 1)
        rev = jnp.where(r + c == tm, 1.0, 0.0).astype(BF16)
        first = (jnp.where(r + c == 0, 1.0, 0.0) * (j > nh).astype(F32)).astype(BF16)
        out = (jnp.dot(rev, dm_ref[...], preferred_element_type=F32)
               + jnp.dot(first, dn_ref[...], preferred_element_type=F32))
        row = lax.broadcasted_iota(jnp.int32, out.shape, 0)
        nyquist = jnp.where(row == 0, scale * (j == nh).astype(F32), 0.0) * alt_ref[0:1, :]
        y_ref[...] = (out + nyquist).astype(y_ref.dtype)


def _dft_mirror(y_half, d, alt, B, S, scale):
    fw = d.shape[1]
    half = S // 2
    tm = _tile(half, 256)
    nh = half // tm

    def mirror(bb, j):
        return (bb * nh + nh - 1 - jnp.maximum(j - nh, 0), 0)

    def mirror_next(bb, j):
        return (bb * nh + jnp.minimum(nh - jnp.maximum(j - nh, 0), nh - 1), 0)

    return pl.pallas_call(
        functools.partial(_dft_mirror_kernel, scale=scale, nh=nh),
        grid=(B, 2 * nh),
        in_specs=[pl.BlockSpec((tm, fw), lambda bb, j: (bb * nh + jnp.minimum(j, nh - 1), 0)),
                  pl.BlockSpec((tm, fw), mirror),
                  pl.BlockSpec((tm, fw), mirror_next),
                  pl.BlockSpec((8, fw), lambda bb, j: (bb, 0))],
        out_specs=pl.BlockSpec((tm, fw), lambda bb, j: (bb * 2 * nh + j, 0)),
        out_shape=jax.ShapeDtypeStruct((B * S, fw), d.dtype),
        compiler_params=_cparams(("parallel", "arbitrary"), 40),
        name="dft_mirror",
    )(y_half, d, d, alt)


def _wo_kernel(ym_ref, yf_ref, gm_ref, gf_ref, w_ref, x_ref, mu_ref, rs_ref, lg_ref, lb_ref,
               pre_ref, mix_s, *, alpha, mw):
    @pl.when(pl.program_id(1) == 0)
    def _():
        ym = ym_ref[...].astype(F32)
        mix_s[:, :mw] = (ym * lax.rsqrt(jnp.mean(ym * ym, axis=-1, keepdims=True) + RMS_EPS)
                         * gm_ref[...]).astype(BF16)
        yf = yf_ref[...].astype(F32)
        mix_s[:, mw:] = (yf * lax.rsqrt(jnp.mean(yf * yf, axis=-1, keepdims=True) + RMS_EPS)
                         * gf_ref[...]).astype(BF16)

    h = (x_ref[...] - mu_ref[...]) * rs_ref[...] * lg_ref[...] + lb_ref[...]
    pre_ref[...] = alpha * h + jnp.dot(mix_s[...], w_ref[...], preferred_element_type=F32)


def _wo(ym, yf, gm, gf, w_o, x2, mu, rs, ln_g, ln_b, alpha):
    T, mw = ym.shape
    fw = yf.shape[1]
    D = w_o.shape[1]
    tm = _tile(T, 1024)
    tn = _tile(D, 512)
    return pl.pallas_call(
        functools.partial(_wo_kernel, alpha=alpha, mw=mw),
        grid=(T // tm, D // tn),
        in_specs=[pl.BlockSpec((tm, mw), lambda i, j: (i, 0)),
                  pl.BlockSpec((tm, fw), lambda i, j: (i, 0)),
                  pl.BlockSpec((1, mw), lambda i, j: (0, 0)),
                  pl.BlockSpec((1, fw), lambda i, j: (0, 0)),
                  pl.BlockSpec((mw + fw, tn), lambda i, j: (0, j)),
                  pl.BlockSpec((tm, tn), lambda i, j: (i, j)),
                  pl.BlockSpec((tm, 1), lambda i, j: (i, 0)),
                  pl.BlockSpec((tm, 1), lambda i, j: (i, 0)),
                  pl.BlockSpec((1, tn), lambda i, j: (0, j)),
                  pl.BlockSpec((1, tn), lambda i, j: (0, j))],
        out_specs=pl.BlockSpec((tm, tn), lambda i, j: (i, j)),
        out_shape=jax.ShapeDtypeStruct((T, D), F32),
        scratch_shapes=[pltpu.VMEM((tm, mw + fw), BF16)],
        compiler_params=_cparams(("parallel", "arbitrary"), 56),
        name="w_o_residual",
    )(ym, yf, gm.reshape(1, mw), gf.reshape(1, fw), w_o, x2, mu, rs,
      ln_g.reshape(1, D), ln_b.reshape(1, D))


def _pack_bf16_pair(lo, hi):
    lo_bits = pltpu.bitcast(lo.astype(BF16).astype(F32), jnp.uint32)
    hi_bits = pltpu.bitcast(hi.astype(BF16).astype(F32), jnp.uint32)
    return (lo_bits >> 16) | (hi_bits & jnp.uint32(0xFFFF0000))


def _unpack_bf16_pair(w):
    lo = pltpu.bitcast(w << 16, F32).astype(BF16)
    hi = pltpu.bitcast(w & jnp.uint32(0xFFFF0000), F32).astype(BF16)
    return lo, hi


def _ln1_router_kernel(pre_ref, g_ref, b_ref, rw_ref, rb_ref, x1_ref, x1p_ref, idx_ref, gate_ref):
    x = pre_ref[...]
    mu = jnp.mean(x, axis=-1, keepdims=True)
    xc = x - mu
    var = jnp.mean(xc * xc, axis=-1, keepdims=True)
    x1 = xc * lax.rsqrt(var + LN_EPS) * g_ref[...] + b_ref[...]
    x1_ref[...] = x1
    half = x1.shape[1] // 2
    x1p_ref[...] = _pack_bf16_pair(x1[:, :half], x1[:, half:])
    logits = lax.dot_general(rw_ref[...], x1, (((1,), (1,)), ((), ())),
                             precision=lax.Precision.HIGHEST, preferred_element_type=F32) + rb_ref[...]
    n_e = logits.shape[0]
    eidx = lax.broadcasted_iota(jnp.int32, logits.shape, 0)
    vals, idxs = [], []
    for _ in range(TOP_K):
        m = jnp.max(logits, axis=0, keepdims=True)
        sel = jnp.min(jnp.where(logits == m, eidx, n_e), axis=0, keepdims=True)
        logits = jnp.where(eidx == sel, -jnp.inf, logits)
        vals.append(m)
        idxs.append(sel)
    exps = [jnp.exp(v - vals[0]) for v in vals]
    denom = exps[0] + exps[1] + exps[2] + exps[3]
    for kk in range(TOP_K):
        idx_ref[kk:kk + 1, :] = idxs[kk]
        gate_ref[kk:kk + 1, :] = exps[kk] / denom


def _ln1_router(pre, g, b, router_w, router_b):
    T, D = pre.shape
    E = router_w.shape[1]
    tm = _tile(T, 256)
    return pl.pallas_call(
        _ln1_router_kernel,
        grid=(T // tm,),
        in_specs=[pl.BlockSpec((tm, D), lambda i: (i, 0)),
                  pl.BlockSpec((1, D), lambda i: (0, 0)),
                  pl.BlockSpec((1, D), lambda i: (0, 0)),
                  pl.BlockSpec((E, D), lambda i: (0, 0)),
                  pl.BlockSpec((E, 1), lambda i: (0, 0))],
        out_specs=[pl.BlockSpec((tm, D), lambda i: (i, 0)),
                   pl.BlockSpec((tm, D // 2), lambda i: (i, 0)),
                   pl.BlockSpec((TOP_K, tm), lambda i: (0, i)),
                   pl.BlockSpec((TOP_K, tm), lambda i: (0, i))],
        out_shape=[jax.ShapeDtypeStruct((T, D), F32),
                   jax.ShapeDtypeStruct((T, D // 2), jnp.uint32),
                   jax.ShapeDtypeStruct((TOP_K, T), jnp.int32),
                   jax.ShapeDtypeStruct((TOP_K, T), F32)],
        compiler_params=_cparams(("parallel",), 48),
        name="ln1_router",
    )(pre, g.reshape(1, D), b.reshape(1, D), router_w.T, router_b.reshape(E, 1))


def _row_copy(src_hbm, dst_ref, src_row, dst_row, sem):
    return pltpu.make_async_copy(src_hbm.at[pl.ds(src_row, 1)], dst_ref.at[pl.ds(dst_row, 1)], sem)


def _dispatch_kernel(dest_ref, pad_ref, npad_ref, nv_ref, x_ref, o_hbm, zero_s, sem, zsem, *, n_steps):
    b = pl.program_id(0)
    tt = x_ref.shape[0]

    def wait_rows(n):
        pltpu.make_async_copy(o_hbm.at[pl.ds(0, n)], o_hbm.at[pl.ds(0, n)], sem).wait()

    @pl.when(b < n_steps)
    def _():
        def issue(r, c):
            for kk in range(TOP_K):
                _row_copy(x_ref, o_hbm, r, dest_ref[(b * tt + r) * TOP_K + kk], sem).start()
            return c
        lax.fori_loop(0, tt, issue, 0, unroll=2)
        wait_rows(tt * TOP_K)

    @pl.when(b == n_steps)
    def _():
        n_pad = npad_ref[0]

        def issue_pad(i, c):
            _row_copy(x_ref, o_hbm, 0, pad_ref[i], sem).start()
            return c
        lax.fori_loop(0, n_pad, issue_pad, 0)

        def wait_pad(i, c):
            wait_rows(1)
            return c
        lax.fori_loop(0, n_pad, wait_pad, 0)

        zero_s[...] = jnp.zeros(zero_s.shape, zero_s.dtype)
        n_sub = MOE_CHUNK // MOE_SUB

        def zero_copy(i):
            return pltpu.make_async_copy(zero_s, o_hbm.at[pl.ds(pl.multiple_of(i * MOE_SUB, MOE_SUB), MOE_SUB)], zsem)

        def empty(i):
            return (i % n_sub) * MOE_SUB >= nv_ref[i // n_sub]

        def issue_zero(i, c):
            @pl.when(empty(i))
            def _():
                zero_copy(i).start()
            return c
        lax.fori_loop(0, nv_ref.shape[0] * n_sub, issue_zero, 0)

        def wait_zero(i, c):
            @pl.when(empty(i))
            def _():
                zero_copy(i).wait()
            return c
        lax.fori_loop(0, nv_ref.shape[0] * n_sub, wait_zero, 0)


def _dispatch(dest, pad_slots, n_pad, chunk_nv, x1p):
    T, W = x1p.shape
    tt = _tile(T, DISPATCH_TOKENS)
    n_steps = T // tt
    grid_spec = pltpu.PrefetchScalarGridSpec(
        num_scalar_prefetch=4,
        grid=(n_steps + 1,),
        in_specs=[pl.BlockSpec((tt, W), lambda b, d, p, n, v: (jnp.minimum(b, n_steps - 1), 0))],
        out_specs=pl.BlockSpec(memory_space=pl.ANY),
        scratch_shapes=[pltpu.VMEM((MOE_SUB, W), x1p.dtype),
                        pltpu.SemaphoreType.DMA(()), pltpu.SemaphoreType.DMA(())],
    )
    return pl.pallas_call(
        functools.partial(_dispatch_kernel, n_steps=n_steps),
        grid_spec=grid_spec,
        out_shape=jax.ShapeDtypeStruct((chunk_nv.shape[0] * MOE_CHUNK, W), x1p.dtype),
        compiler_params=_cparams(("arbitrary",), 32),
        name="moe_dispatch",
    )(dest, pad_slots, n_pad, chunk_nv, x1p)


def _expert_up_kernel(ce_ref, nv_ref, nu_ref, x_ref, wg_ref, wu_ref, bg_ref, bu_ref, h_ref):
    c = pl.program_id(0)
    nv = nv_ref[c]
    brow = ce_ref[c] * pl.num_programs(1) + pl.program_id(1)

    n_sub = MOE_CHUNK // MOE_SUB
    n_act = (nv + MOE_SUB - 1) // MOE_SUB
    for k in range(n_sub + 1):
        @pl.when(n_act == k)
        def _(k=k):
            if k > 0:
                x_lo, x_hi = _unpack_bf16_pair(x_ref[:k * MOE_SUB, :])
                half = x_lo.shape[1]

                def proj(w_ref):
                    return (jnp.dot(x_lo, w_ref[:half, :].astype(BF16), preferred_element_type=F32)
                            + jnp.dot(x_hi, w_ref[half:, :].astype(BF16), preferred_element_type=F32))
                hg = jnp.minimum(proj(wg_ref) + bg_ref[pl.ds(brow, 1), :], SWIGLU_LIMIT)
                hu = jnp.clip(proj(wu_ref) + bu_ref[pl.ds(brow, 1), :], -SWIGLU_LIMIT, SWIGLU_LIMIT)
                act = (hu + 1.0) * (hg * jax.nn.sigmoid(SWIGLU_ALPHA * hg))
                h_ref[:k * MOE_SUB, :] = act.astype(h_ref.dtype)
            if k < n_sub:
                h_ref[k * MOE_SUB:, :] = jnp.zeros(((n_sub - k) * MOE_SUB, h_ref.shape[1]), h_ref.dtype)


def _expert_up(chunk_e, chunk_nv, n_used, xs, w_gate, w_up, b_gate, b_up, n_chunks):
    E, D, F = w_gate.shape
    tf = _tile(F, 256)
    nf = F // tf

    def used(c, nu):
        return jnp.minimum(c, nu[0] - 1)

    def jeff(c, j, nu):
        return jnp.where(c < nu[0], j, nf - 1)

    grid_spec = pltpu.PrefetchScalarGridSpec(
        num_scalar_prefetch=3,
        grid=(n_chunks, nf),
        in_specs=[pl.BlockSpec((MOE_CHUNK, D // 2), lambda c, j, ce, nv, nu: (used(c, nu), 0)),
                  pl.BlockSpec((None, D, tf), lambda c, j, ce, nv, nu: (ce[c], 0, jeff(c, j, nu))),
                  pl.BlockSpec((None, D, tf), lambda c, j, ce, nv, nu: (ce[c], 0, jeff(c, j, nu))),
                  pl.BlockSpec((E * nf, tf), lambda c, j, ce, nv, nu: (0, 0)),
                  pl.BlockSpec((E * nf, tf), lambda c, j, ce, nv, nu: (0, 0))],
        out_specs=pl.BlockSpec((MOE_CHUNK, tf), lambda c, j, ce, nv, nu: (c, j)),
    )
    return pl.pallas_call(
        _expert_up_kernel,
        grid_spec=grid_spec,
        out_shape=jax.ShapeDtypeStruct((n_chunks * MOE_CHUNK, F), BF16),
        compiler_params=_cparams(("arbitrary", "arbitrary"), 60),
        name="expert_gate_up",
    )(chunk_e, chunk_nv, n_used, xs, w_gate, w_up, b_gate.reshape(E * nf, tf), b_up.reshape(E * nf, tf))


def _expert_down_kernel(ce_ref, nv_ref, nu_ref, h_ref, wd_ref, bd_ref, y_ref):
    c = pl.program_id(0)
    nv = nv_ref[c]
    brow = ce_ref[c] * pl.num_programs(1) + pl.program_id(1)

    n_sub = MOE_CHUNK // MOE_SUB
    n_act = (nv + MOE_SUB - 1) // MOE_SUB
    for k in range(n_sub + 1):
        @pl.when(n_act == k)
        def _(k=k):
            if k > 0:
                out = jnp.dot(h_ref[:k * MOE_SUB, :], wd_ref[...].astype(BF16),
                              preferred_element_type=F32) + bd_ref[pl.ds(brow, 1), :]
                hw = out.shape[1] // 2
                y_ref[:k * MOE_SUB, :] = _pack_bf16_pair(out[:, :hw], out[:, hw:])
            if k < n_sub:
                y_ref[k * MOE_SUB:, :] = jnp.zeros(((n_sub - k) * MOE_SUB, y_ref.shape[1]), y_ref.dtype)


def _expert_down(chunk_e, chunk_nv, n_used, hmid, w_down, b_down, n_chunks):
    E, F, D = w_down.shape
    tn = _tile(D, MOE_DOWN_TN)
    nn = D // tn

    def used(c, nu):
        return jnp.minimum(c, nu[0] - 1)

    def jeff(c, j, nu):
        return jnp.where(c < nu[0], j, nn - 1)

    grid_spec = pltpu.PrefetchScalarGridSpec(
        num_scalar_prefetch=3,
        grid=(n_chunks, nn),
        in_specs=[pl.BlockSpec((MOE_CHUNK, F), lambda c, j, ce, nv, nu: (used(c, nu), 0)),
                  pl.BlockSpec((None, F, tn), lambda c, j, ce, nv, nu: (ce[c], 0, jeff(c, j, nu))),
                  pl.BlockSpec((E * nn, tn), lambda c, j, ce, nv, nu: (0, 0))],
        out_specs=pl.BlockSpec((MOE_CHUNK, tn // 2), lambda c, j, ce, nv, nu: (c, j)),
    )
    return pl.pallas_call(
        _expert_down_kernel,
        grid_spec=grid_spec,
        out_shape=jax.ShapeDtypeStruct((n_chunks * MOE_CHUNK, D // 2), jnp.uint32),
        compiler_params=_cparams(("arbitrary", "arbitrary"), 56),
        name="expert_down",
    )(chunk_e, chunk_nv, n_used, hmid, w_down, b_down.reshape(E * nn, tn))


def _combine_kernel(dest_ref, y_hbm, x1_ref, gate_ref, g_ref, b_ref, o_ref, *scratch, alpha, tn):
    i = pl.program_id(0)
    n = pl.num_programs(0)
    nb = COMBINE_BUFS
    tt = o_ref.shape[0] // nb
    bufs, sem = scratch[:nb], scratch[nb]

    def issue_tile(tile, slot):
        base = jnp.minimum(tile, nb * n - 1) * tt
        for r in range(tt):
            for kk in range(TOP_K):
                _row_copy(y_hbm, bufs[slot].at[kk], dest_ref[(base + r) * TOP_K + kk], r, sem.at[slot]).start()

    def wait_tile(slot):
        for kk in range(TOP_K):
            pltpu.make_async_copy(y_hbm.at[pl.ds(0, tt)], bufs[slot].at[kk], sem.at[slot]).wait()

    def finish_tile(slot):
        rows = slice(slot * tt, (slot + 1) * tt)
        gates = gate_ref[rows, :]
        y_lo = y_hi = None
        for kk in range(TOP_K):
            w = bufs[slot][kk]
            g_k = gates[:, kk:kk + 1]
            lo = g_k * pltpu.bitcast(w << 16, F32)
            hi = g_k * pltpu.bitcast(w & jnp.uint32(0xFFFF0000), F32)
            y_lo = lo if y_lo is None else y_lo + lo
            y_hi = hi if y_hi is None else y_hi + hi
        hw = tn // 2
        pieces = []
        for j in range(x1_ref.shape[1] // tn):
            pieces.append((slice(j * tn, j * tn + hw), y_lo[:, j * hw:(j + 1) * hw]))
            pieces.append((slice(j * tn + hw, (j + 1) * tn), y_hi[:, j * hw:(j + 1) * hw]))
        zs = [alpha * x1_ref[rows, cols] + y for cols, y in pieces]
        d = x1_ref.shape[1]
        mu = sum(jnp.sum(z, axis=-1, keepdims=True) for z in zs) / d
        zcs = [z - mu for z in zs]
        var = sum(jnp.sum(zc * zc, axis=-1, keepdims=True) for zc in zcs) / d
        rs = lax.rsqrt(var + LN_EPS)
        for (cols, _), zc in zip(pieces, zcs):
            o_ref[rows, cols] = zc * rs * g_ref[:, cols] + b_ref[:, cols]

    @pl.when(i == 0)
    def _():
        for t in range(COMBINE_AHEAD):
            issue_tile(t, t)

    for t in range(nb):
        wait_tile(t)
        issue_tile(nb * i + t + COMBINE_AHEAD, (t + COMBINE_AHEAD) % nb)
        finish_tile(t)

    @pl.when(i == n - 1)
    def _():
        for t in range(COMBINE_AHEAD):
            wait_tile(t)


def _combine(dest, yslots, x1, gates_tk, g, b, alpha):
    T, D = x1.shape
    nb = COMBINE_BUFS
    tt = _tile(T // nb, COMBINE_TOKENS)
    grid_spec = pltpu.PrefetchScalarGridSpec(
        num_scalar_prefetch=1,
        grid=(T // (nb * tt),),
        in_specs=[pl.BlockSpec(memory_space=pl.ANY),
                  pl.BlockSpec((nb * tt, D), lambda i, d: (i, 0)),
                  pl.BlockSpec((nb * tt, TOP_K), lambda i, d: (i, 0)),
                  pl.BlockSpec((1, D), lambda i, d: (0, 0)),
                  pl.BlockSpec((1, D), lambda i, d: (0, 0))],
        out_specs=pl.BlockSpec((nb * tt, D), lambda i, d: (i, 0)),
        scratch_shapes=[pltpu.VMEM((TOP_K, tt, D // 2), jnp.uint32) for _ in range(nb)]
        + [pltpu.SemaphoreType.DMA((nb,))],
    )
    return pl.pallas_call(
        functools.partial(_combine_kernel, alpha=alpha, tn=_tile(D, MOE_DOWN_TN)),
        grid_spec=grid_spec,
        out_shape=jax.ShapeDtypeStruct((T, D), F32),
        compiler_params=_cparams(("arbitrary",), 48),
        name="moe_combine_ln2",
    )(dest, yslots, x1, gates_tk, g.reshape(1, D), b.reshape(1, D))


def _routing_tables(top_idx, n_experts, n_chunks):
    T = top_idx.shape[1]
    M = T * TOP_K
    flat_e = top_idx.T.reshape(M)
    onehot = (flat_e[:, None] == jnp.arange(n_experts, dtype=jnp.int32)[None, :]).astype(jnp.int32)
    csum = jnp.cumsum(onehot, axis=0)
    rank = jnp.sum(csum * onehot, axis=1) - 1
    counts = csum[-1]
    chunks_e = (counts + MOE_CHUNK - 1) // MOE_CHUNK
    chunk_end = jnp.cumsum(chunks_e)
    chunk_start = chunk_end - chunks_e
    n_used = chunk_end[-1]
    dest = chunk_start[flat_e] * MOE_CHUNK + rank
    P = n_chunks * MOE_CHUNK
    pad_idx = counts[:, None] + jnp.arange(MOE_SUB, dtype=jnp.int32)[None, :]
    padded = (counts + MOE_SUB - 1) // MOE_SUB * MOE_SUB
    pad_slots = jnp.where(pad_idx < padded[:, None], chunk_start[:, None] * MOE_CHUNK + pad_idx, P)
    pad_slots = jnp.sort(pad_slots.reshape(-1)).astype(jnp.int32)
    n_pad = jnp.sum(padded - counts).astype(jnp.int32).reshape(1)
    cid = jnp.arange(n_chunks, dtype=jnp.int32)
    chunk_e = jnp.minimum(jnp.searchsorted(chunk_end, cid, side='right'), n_experts - 1).astype(jnp.int32)
    last_e = chunk_e[jnp.maximum(n_used - 1, 0)]
    chunk_e = jnp.where(cid < n_used, chunk_e, last_e)
    chunk_nv = jnp.where(cid < n_used,
                         jnp.clip(counts[chunk_e] - (cid - chunk_start[chunk_e]) * MOE_CHUNK, 0, MOE_CHUNK),
                         0).astype(jnp.int32)
    return (dest.astype(jnp.int32), pad_slots, n_pad, chunk_e, chunk_nv,
            n_used.astype(jnp.int32).reshape(1))


def kernel(x, positions, ln_in_g, ln_in_b, w_in, q_a_norm_g, w_q_b, kv_a_norm_g, w_kv_b, mla_out_norm_g, fourier_out_norm_g, w_o, ln1_g, ln1_b, router_w, router_b, w_gate, b_gate, w_up, b_up, w_down, b_down, ln2_g, ln2_b):
    B, S, D = x.shape
    T = B * S
    depth = w_in.shape[0]
    assert depth == 1, "single-layer trunk only"
    qr = q_a_norm_g.shape[1]
    kvr = kv_a_norm_g.shape[1]
    H = w_q_b.shape[2] // (QK_NOPE_DIM + QK_ROPE_DIM)
    fw = fourier_out_norm_g.shape[1]
    E = router_w.shape[2]
    assert H % HEADS_PER_TILE == 0 and (qr + kvr) % LANES == 0 and S % (2 * DFT_ROW_SPLIT) == 0
    alpha = (2.0 * depth) ** 0.25

    inv_freq = ROPE_THETA ** (-jnp.arange(0, QK_ROPE_DIM, 2, dtype=F32) / QK_ROPE_DIM)
    ang = positions.astype(F32)[..., None] * inv_freq
    cos4 = jnp.tile(jnp.cos(ang), (1, 1, 2 * LANES // QK_ROPE_DIM)).reshape(T, LANES)
    sin4 = jnp.tile(jnp.sin(ang), (1, 1, 2 * LANES // QK_ROPE_DIM)).reshape(T, LANES)

    rope_end = qr + kvr + QK_ROPE_DIM
    w_a = w_in[0, :, :qr + kvr + LANES].astype(BF16)
    w_f = w_in[0, :, rope_end:].astype(BF16)
    wq = w_q_b[0].reshape(qr, H // HEADS_PER_TILE, HEADS_PER_TILE, QK_NOPE_DIM + QK_ROPE_DIM)
    wq_perm = jnp.concatenate(
        [wq[..., :QK_NOPE_DIM].reshape(qr, H // HEADS_PER_TILE, HEADS_PER_TILE * QK_NOPE_DIM),
         wq[..., QK_NOPE_DIM:].reshape(qr, H // HEADS_PER_TILE, HEADS_PER_TILE * QK_ROPE_DIM)],
        axis=-1).reshape(qr, H * (QK_NOPE_DIM + QK_ROPE_DIM)).astype(BF16)
    w_kv = w_kv_b[0].astype(BF16)
    w_o_b = w_o[0].astype(BF16)
    ch = jnp.arange(FOURIER_GROUP_DIM, dtype=jnp.int32)
    ang_c = (2.0 * math.pi / FOURIER_GROUP_DIM) * ((ch[:, None] * ch[None, :]) % FOURIER_GROUP_DIM).astype(F32)
    cs_tab = jnp.concatenate([jnp.cos(ang_c), jnp.sin(ang_c)], axis=1).astype(BF16)

    x2 = x.reshape(T, D)
    hb, mu, rs = _ln_in(x2, ln_in_g, ln_in_b)

    cq, ckv, kpe2 = _inproj_a(hb, w_a, q_a_norm_g[0], kv_a_norm_g[0], cos4, sin4)
    qscale = (QK_NOPE_DIM + QK_ROPE_DIM) ** -0.5 * LOG2E
    q = _q_up(cq, wq_perm, cos4, sin4, qscale)
    k, v = _kv_up(ckv, w_kv, kpe2)
    y_mla = _attention(q, k, v, B, S, H)

    fa, fb = _inproj_f(hb, w_f, cs_tab)
    cs_mat, sn_mat = _dft_matrices(S, FOURIER_GROUP_DIM)
    fae, fbo, falt = _dft_fold(fa, fb, B, S)
    dft_scale = 1.0 / math.sqrt(S * FOURIER_GROUP_DIM)
    y_half, y_diff = _seq_dft(cs_mat, sn_mat, fae, fbo, fa, B, S, dft_scale)
    y_f = _dft_mirror(y_half, y_diff, falt, B, S, dft_scale)

    pre = _wo(y_mla, y_f, mla_out_norm_g[0], fourier_out_norm_g[0], w_o_b, x2, mu, rs,
              ln_in_g, ln_in_b, alpha)
    x1, x1p, top_idx, gates = _ln1_router(pre, ln1_g[0], ln1_b[0], router_w[0], router_b[0])

    n_chunks = -(-T * TOP_K // MOE_CHUNK) + E
    dest, pad_slots, n_pad, chunk_e, chunk_nv, n_used = _routing_tables(top_idx, E, n_chunks)
    xs = _dispatch(dest, pad_slots, n_pad, chunk_nv, x1p)
    hmid = _expert_up(chunk_e, chunk_nv, n_used, xs, w_gate[0], w_up[0], b_gate[0], b_up[0], n_chunks)
    yslots = _expert_down(chunk_e, chunk_nv, n_used, hmid, w_down[0], b_down[0], n_chunks)
    out = _combine(dest, yslots, x1, gates.T, ln2_g[0], ln2_b[0], alpha)
    return out.reshape(B, S, D)
```

```python
import functools
import math

import jax
import jax.numpy as jnp
from jax import lax
from jax.experimental import pallas as pl
from jax.experimental.pallas import tpu as pltpu

F32 = jnp.float32
BF16 = jnp.bfloat16

V_HEAD_DIM = 128
QK_NOPE_DIM = 128
QK_ROPE_DIM = 64
QK_PAD_DIM = 256
V_PAD_DIM = 256
ROPE_THETA = 10000.0
FOURIER_GROUP_DIM = 128
TOP_K = 4
SWIGLU_LIMIT = 7.0
SWIGLU_ALPHA = 1.702
LN_EPS = 1e-5
RMS_EPS = 1e-6
LOG2E = 1.4426950408889634

LANES = 128
V7X_VMEM_BYTES = 64 * 1024 * 1024
HEADS_PER_TILE = 4
DFT_ROW_SPLIT = 64
ATTN_TQ = 1024
ATTN_TKV = 1024

MOE_CHUNK = 1280
MOE_SUB = 256
DISPATCH_TOKENS = 512
COMBINE_TOKENS = 64
COMBINE_BUFS = 4
COMBINE_AHEAD = 2
MOE_DOWN_TN = 1024


def _cparams(semantics, vmem_mb):
    return pltpu.CompilerParams(dimension_semantics=semantics,
                                vmem_limit_bytes=min(vmem_mb * 1024 * 1024, V7X_VMEM_BYTES - (4 << 20)))


def _tile(dim, pref):
    t = min(dim, pref)
    while dim % t:
        t //= 2
    return t


def _ln_in_kernel(x_ref, g_ref, b_ref, hb_ref, mu_ref, rs_ref):
    x = x_ref[...]
    mu = jnp.mean(x, axis=-1, keepdims=True)
    xc = x - mu
    var = jnp.mean(xc * xc, axis=-1, keepdims=True)
    rs = lax.rsqrt(var + LN_EPS)
    hb_ref[...] = (xc * rs * g_ref[...] + b_ref[...]).astype(BF16)
    mu_ref[...] = mu
    rs_ref[...] = rs


def _ln_in(x2, g, b):
    T, D = x2.shape
    tm = _tile(T, 256)
    return pl.pallas_call(
        _ln_in_kernel,
        grid=(T // tm,),
        in_specs=[pl.BlockSpec((tm, D), lambda i: (i, 0)),
                  pl.BlockSpec((1, D), lambda i: (0, 0)),
                  pl.BlockSpec((1, D), lambda i: (0, 0))],
        out_specs=[pl.BlockSpec((tm, D), lambda i: (i, 0)),
                   pl.BlockSpec((tm, 1), lambda i: (i, 0)),
                   pl.BlockSpec((tm, 1), lambda i: (i, 0))],
        out_shape=[jax.ShapeDtypeStruct((T, D), BF16),
                   jax.ShapeDtypeStruct((T, 1), F32),
                   jax.ShapeDtypeStruct((T, 1), F32)],
        compiler_params=_cparams(("parallel",), 40),
        name="ln_in",
    )(x2, g.reshape(1, D), b.reshape(1, D))


def _rope128(p, cos4, sin4):
    lane = lax.broadcasted_iota(jnp.int32, p.shape, 1)
    first_half = (lane % QK_ROPE_DIM) < (QK_ROPE_DIM // 2)
    rot = jnp.where(first_half, -pltpu.roll(p, LANES - QK_ROPE_DIM // 2, 1),
                    pltpu.roll(p, QK_ROPE_DIM // 2, 1))
    return p * cos4 + rot * sin4


def _inproj_a_kernel(h_ref, w_ref, gq_ref, gkv_ref, cos_ref, sin_ref,
                     cq_ref, ckv_ref, kpe_ref, *, qr, kvr):
    acc = jnp.dot(h_ref[...], w_ref[...], preferred_element_type=F32)
    cq = acc[:, :qr]
    cq_ref[...] = (cq * lax.rsqrt(jnp.mean(cq * cq, axis=-1, keepdims=True) + RMS_EPS)
                   * gq_ref[...]).astype(BF16)
    ckv = acc[:, qr:qr + kvr]
    ckv_ref[...] = (ckv * lax.rsqrt(jnp.mean(ckv * ckv, axis=-1, keepdims=True) + RMS_EPS)
                    * gkv_ref[...]).astype(BF16)
    roped = _rope128(acc[:, qr + kvr:qr + kvr + LANES], cos_ref[...], sin_ref[...])
    lane = lax.broadcasted_iota(jnp.int32, roped.shape, 1)
    even = jnp.where(lane < QK_ROPE_DIM, roped, 0.0)
    kpe_ref[:, :LANES] = even.astype(BF16)
    kpe_ref[:, LANES:] = pltpu.roll(even, QK_ROPE_DIM, 1).astype(BF16)


def _inproj_a(hb, w_a, gq, gkv, cos4, sin4):
    T, D = hb.shape
    qr, kvr = gq.shape[0], gkv.shape[0]
    wa = w_a.shape[1]
    tm = _tile(T, 512)
    return pl.pallas_call(
        functools.partial(_inproj_a_kernel, qr=qr, kvr=kvr),
        grid=(T // tm,),
        in_specs=[pl.BlockSpec((tm, D), lambda i: (i, 0)),
                  pl.BlockSpec((D, wa), lambda i: (0, 0)),
                  pl.BlockSpec((1, qr), lambda i: (0, 0)),
                  pl.BlockSpec((1, kvr), lambda i: (0, 0)),
                  pl.BlockSpec((tm, LANES), lambda i: (i, 0)),
                  pl.BlockSpec((tm, LANES), lambda i: (i, 0))],
        out_specs=[pl.BlockSpec((tm, qr), lambda i: (i, 0)),
                   pl.BlockSpec((tm, kvr), lambda i: (i, 0)),
                   pl.BlockSpec((tm, 2 * LANES), lambda i: (i, 0))],
        out_shape=[jax.ShapeDtypeStruct((T, qr), BF16),
                   jax.ShapeDtypeStruct((T, kvr), BF16),
                   jax.ShapeDtypeStruct((T, 2 * LANES), BF16)],
        compiler_params=_cparams(("parallel",), 56),
        name="inproj_mla",
    )(hb, w_a, gq.reshape(1, qr), gkv.reshape(1, kvr), cos4, sin4)


def _q_up_kernel(c_ref, w_ref, cos_ref, sin_ref, q_ref, *, qscale):
    acc = jnp.dot(c_ref[...], w_ref[...], preferred_element_type=F32)
    nope_w = HEADS_PER_TILE * QK_NOPE_DIM
    lane = lax.broadcasted_iota(jnp.int32, (acc.shape[0], LANES), 1)
    for pair in range(HEADS_PER_TILE // 2):
        roped = _rope128(acc[:, nope_w + pair * LANES:nope_w + (pair + 1) * LANES],
                         cos_ref[...], sin_ref[...]) * qscale
        for par in range(2):
            j = 2 * pair + par
            keep = (lane < QK_ROPE_DIM) if par == 0 else (lane >= QK_ROPE_DIM)
            base = j * QK_PAD_DIM
            q_ref[:, base:base + QK_NOPE_DIM] = (
                acc[:, j * QK_NOPE_DIM:(j + 1) * QK_NOPE_DIM] * qscale).astype(BF16)
            q_ref[:, base + QK_NOPE_DIM:base + QK_PAD_DIM] = jnp.where(keep, roped, 0.0).astype(BF16)


def _q_up(cq, wq_perm, cos4, sin4, qscale):
    T, qr = cq.shape
    n_tiles = wq_perm.shape[1] // (HEADS_PER_TILE * (QK_NOPE_DIM + QK_ROPE_DIM))
    tw = HEADS_PER_TILE * (QK_NOPE_DIM + QK_ROPE_DIM)
    to = HEADS_PER_TILE * QK_PAD_DIM
    tm = _tile(T, 1024)
    return pl.pallas_call(
        functools.partial(_q_up_kernel, qscale=qscale),
        grid=(T // tm, n_tiles),
        in_specs=[pl.BlockSpec((tm, qr), lambda i, j: (i, 0)),
                  pl.BlockSpec((qr, tw), lambda i, j: (0, j)),
                  pl.BlockSpec((tm, LANES), lambda i, j: (i, 0)),
                  pl.BlockSpec((tm, LANES), lambda i, j: (i, 0))],
        out_specs=pl.BlockSpec((tm, to), lambda i, j: (i, j)),
        out_shape=jax.ShapeDtypeStruct((T, n_tiles * to), BF16),
        compiler_params=_cparams(("parallel", "arbitrary"), 40),
        name="q_up",
    )(cq, wq_perm, cos4, sin4)


def _kv_up_kernel(c_ref, w_ref, kpe_ref, k_ref, v_ref):
    acc = jnp.dot(c_ref[...], w_ref[...], preferred_element_type=F32)
    for j in range(HEADS_PER_TILE):
        src = j * (QK_NOPE_DIM + V_HEAD_DIM)
        k_ref[:, j * QK_PAD_DIM:j * QK_PAD_DIM + QK_NOPE_DIM] = acc[:, src:src + QK_NOPE_DIM].astype(BF16)
        par = j % 2
        k_ref[:, j * QK_PAD_DIM + QK_NOPE_DIM:(j + 1) * QK_PAD_DIM] = kpe_ref[:, par * LANES:(par + 1) * LANES]
        v_ref[:, j * V_PAD_DIM:j * V_PAD_DIM + V_HEAD_DIM] = (
            acc[:, src + QK_NOPE_DIM:src + QK_NOPE_DIM + V_HEAD_DIM].astype(BF16))
        v_ref[:, j * V_PAD_DIM + V_HEAD_DIM:(j + 1) * V_PAD_DIM] = jnp.ones(
            (acc.shape[0], V_PAD_DIM - V_HEAD_DIM), BF16)


def _kv_up(ckv, w_kv, kpe2):
    T, kvr = ckv.shape
    tw = HEADS_PER_TILE * (QK_NOPE_DIM + V_HEAD_DIM)
    n_tiles = w_kv.shape[1] // tw
    tm = _tile(T, 1024)
    return pl.pallas_call(
        _kv_up_kernel,
        grid=(T // tm, n_tiles),
        in_specs=[pl.BlockSpec((tm, kvr), lambda i, j: (i, 0)),
                  pl.BlockSpec((kvr, tw), lambda i, j: (0, j)),
                  pl.BlockSpec((tm, 2 * LANES), lambda i, j: (i, 0))],
        out_specs=[pl.BlockSpec((tm, HEADS_PER_TILE * QK_PAD_DIM), lambda i, j: (i, j)),
                   pl.BlockSpec((tm, HEADS_PER_TILE * V_PAD_DIM), lambda i, j: (i, j))],
        out_shape=[jax.ShapeDtypeStruct((T, n_tiles * HEADS_PER_TILE * QK_PAD_DIM), BF16),
                   jax.ShapeDtypeStruct((T, n_tiles * HEADS_PER_TILE * V_PAD_DIM), BF16)],
        compiler_params=_cparams(("parallel", "arbitrary"), 40),
        name="kv_up",
    )(ckv, w_kv, kpe2)


def _lane_tile(x, reps):
    return jnp.concatenate([x] * reps, axis=1)


def _attn_kernel(q_ref, k_ref, v_ref, o_ref, m_s, acc_s, s0_s, s1_s, x0_s, x1_s, p0_s, p1_s, a0_s, a1_s,
                 *, tkv):
    n_kv = k_ref.shape[0] // tkv
    s_buf, x_buf, p_buf, a_buf = (s0_s, s1_s), (x0_s, x1_s), (p0_s, p1_s), (a0_s, a1_s)
    m_s[...] = jnp.full(m_s.shape, -jnp.inf, F32)
    acc_s[...] = jnp.zeros(acc_s.shape, F32)

    def scores(i, slot):
        off = pl.multiple_of(i * tkv, tkv)
        s = lax.dot_general(q_ref[...], k_ref[pl.ds(off, tkv), :], (((1,), (1,)), ((), ())),
                            preferred_element_type=F32)
        s_buf[slot][...] = s
        x_buf[slot][...] = jnp.broadcast_to(jnp.max(s, axis=1, keepdims=True), x_buf[slot].shape)

    def probs(slot):
        m_prev = m_s[...]
        m_new = jnp.maximum(m_prev, x_buf[slot][...])
        m_s[...] = m_new
        a_buf[slot][...] = jnp.exp2(m_prev - m_new)
        p_buf[slot][...] = jnp.exp2(s_buf[slot][...] - _lane_tile(m_new, tkv // LANES)).astype(BF16)

    def values(i, slot):
        off = pl.multiple_of(i * tkv, tkv)
        acc_s[...] = (_lane_tile(a_buf[slot][...], V_PAD_DIM // LANES) * acc_s[...]
                      + jnp.dot(p_buf[slot][...], v_ref[pl.ds(off, tkv), :], preferred_element_type=F32))

    def trip(i, slot):
        scores(i + 1, 1 - slot)
        values(i - 1, 1 - slot)
        probs(slot)

    scores(0, 0)
    scores(1, 1)
    probs(0)
    for i in range(1, n_kv - 1):
        trip(i, i % 2)
    values(n_kv - 2, 0)
    probs(1)
    values(n_kv - 1, 1)
    acc = acc_s[...]
    o_ref[...] = (acc[:, :V_HEAD_DIM] / acc[:, V_HEAD_DIM:]).astype(o_ref.dtype)


def _attention(q, k, v, B, S, H):
    T = B * S
    tq = _tile(S, ATTN_TQ)
    tkv = _tile(S, min(ATTN_TKV, S // 2))
    assert (S // tkv) % 2 == 0, "the key-chunk pipeline is unrolled in pairs"
    nq = S // tq
    return pl.pallas_call(
        functools.partial(_attn_kernel, tkv=tkv),
        grid=(B, H, nq),
        in_specs=[pl.BlockSpec((tq, QK_PAD_DIM), lambda b, h, i: (b * nq + i, h)),
                  pl.BlockSpec((S, QK_PAD_DIM), lambda b, h, i: (b, h)),
                  pl.BlockSpec((S, V_PAD_DIM), lambda b, h, i: (b, h))],
        out_specs=pl.BlockSpec((tq, V_HEAD_DIM), lambda b, h, i: (b * nq + i, h)),
        out_shape=jax.ShapeDtypeStruct((T, H * V_HEAD_DIM), BF16),
        scratch_shapes=[pltpu.VMEM((tq, LANES), F32), pltpu.VMEM((tq, V_PAD_DIM), F32),
                        pltpu.VMEM((tq, tkv), F32), pltpu.VMEM((tq, tkv), F32),
                        pltpu.VMEM((tq, LANES), F32), pltpu.VMEM((tq, LANES), F32),
                        pltpu.VMEM((tq, tkv), BF16), pltpu.VMEM((tq, tkv), BF16),
                        pltpu.VMEM((tq, LANES), F32), pltpu.VMEM((tq, LANES), F32)],
        compiler_params=_cparams(("parallel", "parallel", "arbitrary"), 40),
        name="mla_attention",
    )(q, k, v)


def _inproj_f_kernel(h_ref, w_ref, cs_ref, a_ref, b_ref, *, groups):
    acc = jnp.dot(h_ref[...], w_ref[...], preferred_element_type=F32)
    C = FOURIER_GROUP_DIM
    for g in range(groups):
        ab = jnp.dot(acc[:, g * C:(g + 1) * C].astype(BF16), cs_ref[...], preferred_element_type=F32)
        a_ref[:, g * C:(g + 1) * C] = ab[:, :C].astype(BF16)
        b_ref[:, g * C:(g + 1) * C] = ab[:, C:].astype(BF16)


def _inproj_f(hb, w_f, cs_tab):
    T, D = hb.shape
    fw = w_f.shape[1]
    tm = _tile(T, 1024)
    tn = _tile(fw, 512)
    return pl.pallas_call(
        functools.partial(_inproj_f_kernel, groups=tn // FOURIER_GROUP_DIM),
        grid=(T // tm, fw // tn),
        in_specs=[pl.BlockSpec((tm, D), lambda i, j: (i, 0)),
                  pl.BlockSpec((D, tn), lambda i, j: (0, j)),
                  pl.BlockSpec((FOURIER_GROUP_DIM, 2 * FOURIER_GROUP_DIM), lambda i, j: (0, 0))],
        out_specs=[pl.BlockSpec((tm, tn), lambda i, j: (i, j)),
                   pl.BlockSpec((tm, tn), lambda i, j: (i, j))],
        out_shape=[jax.ShapeDtypeStruct((T, fw), BF16), jax.ShapeDtypeStruct((T, fw), BF16)],
        compiler_params=_cparams(("parallel", "arbitrary"), 48),
        name="inproj_fourier",
    )(hb, w_f, cs_tab)


def _dft_gen_kernel(tac_ref, tas_ref, tbc_ref, tbs_ref, cs_ref, sn_ref):
    tbc = tbc_ref[...]
    tbs = tbs_ref[...]
    for aa in range(tac_ref.shape[0]):
        ca = tac_ref[aa:aa + 1, :]
        sa = tas_ref[aa:aa + 1, :]
        rows = slice(aa * DFT_ROW_SPLIT, (aa + 1) * DFT_ROW_SPLIT)
        cs_ref[rows, :] = (ca * tbc - sa * tbs).astype(BF16)
        sn_ref[rows, :] = (-(sa * tbc + ca * tbs)).astype(BF16)


def _dft_matrices(S, n_chan):
    na = S // 2 // DFT_ROW_SPLIT
    scale = 1.0 / math.sqrt(S * n_chan)
    col = jnp.arange(S // 2, dtype=jnp.int32)[None, :]
    period = S // DFT_ROW_SPLIT
    ang_a = (2.0 * math.pi / period) * ((jnp.arange(na, dtype=jnp.int32)[:, None] * col) % period).astype(F32)
    ang_b = (2.0 * math.pi / S) * ((jnp.arange(DFT_ROW_SPLIT, dtype=jnp.int32)[:, None] * col) % S).astype(F32)
    tac, tas = jnp.cos(ang_a), jnp.sin(ang_a)
    tbc, tbs = scale * jnp.cos(ang_b), scale * jnp.sin(ang_b)
    ta = min(8, na)
    assert na % ta == 0
    tc = _tile(S // 2, 2048)
    return pl.pallas_call(
        _dft_gen_kernel,
        grid=(na // ta, S // 2 // tc),
        in_specs=[pl.BlockSpec((ta, tc), lambda i, j: (i, j)),
                  pl.BlockSpec((ta, tc), lambda i, j: (i, j)),
                  pl.BlockSpec((DFT_ROW_SPLIT, tc), lambda i, j: (0, j)),
                  pl.BlockSpec((DFT_ROW_SPLIT, tc), lambda i, j: (0, j))],
        out_specs=[pl.BlockSpec((ta * DFT_ROW_SPLIT, tc), lambda i, j: (i, j)),
                   pl.BlockSpec((ta * DFT_ROW_SPLIT, tc), lambda i, j: (i, j))],
        out_shape=[jax.ShapeDtypeStruct((S // 2, S // 2), BF16), jax.ShapeDtypeStruct((S // 2, S // 2), BF16)],
        compiler_params=_cparams(("parallel", "parallel"), 40),
        name="dft_matrices",
    )(tac, tas, tbc, tbs)


def _dft_fold_kernel(a_ref, am_ref, an_ref, b_ref, bm_ref, bn_ref, ae_ref, bo_ref, alt_ref):
    tm = a_ref.shape[0]
    r = lax.broadcasted_iota(jnp.int32, (tm, tm), 0)
    c = lax.broadcasted_iota(jnp.int32, (tm, tm), 1)
    rev = jnp.where(r + c == tm, 1.0, 0.0).astype(BF16)
    has_next = (pl.program_id(1) > 0).astype(F32)
    first = (jnp.where(r + c == 0, 1.0, 0.0) * has_next).astype(BF16)

    def mirrored(m_ref, n_ref):
        return (jnp.dot(rev, m_ref[...], preferred_element_type=F32)
                + jnp.dot(first, n_ref[...], preferred_element_type=F32))

    ae_ref[...] = (a_ref[...].astype(F32) + mirrored(am_ref, an_ref)).astype(BF16)
    bo_ref[...] = (b_ref[...].astype(F32) - mirrored(bm_ref, bn_ref)).astype(BF16)

    @pl.when(pl.program_id(1) == 0)
    def _():
        alt_ref[...] = jnp.zeros(alt_ref.shape, F32)

    row = lax.broadcasted_iota(jnp.int32, a_ref.shape, 0)
    both = a_ref[...].astype(F32) + am_ref[...].astype(F32)
    alt_ref[0:1, :] += jnp.sum(jnp.where(row % 2 == 0, both, -both), axis=0, keepdims=True)


def _dft_fold(a, b, B, S):
    fw = a.shape[1]
    tm = _tile(S // 2, 256)
    nb = S // tm
    nh = nb // 2

    def own(bb, i):
        return (bb * nb + i, 0)

    def mirror(bb, i):
        return (bb * nb + nb - 1 - i, 0)

    def mirror_next(bb, i):
        return (bb * nb + jnp.minimum(nb - i, nb - 1), 0)

    spec = lambda f: pl.BlockSpec((tm, fw), f)
    return pl.pallas_call(
        _dft_fold_kernel,
        grid=(B, nh),
        in_specs=[spec(own), spec(mirror), spec(mirror_next), spec(own), spec(mirror), spec(mirror_next)],
        out_specs=[pl.BlockSpec((tm, fw), lambda bb, i: (bb * nh + i, 0)),
                   pl.BlockSpec((tm, fw), lambda bb, i: (bb * nh + i, 0)),
                   pl.BlockSpec((8, fw), lambda bb, i: (bb, 0))],
        out_shape=[jax.ShapeDtypeStruct((B * S // 2, fw), BF16),
                   jax.ShapeDtypeStruct((B * S // 2, fw), BF16),
                   jax.ShapeDtypeStruct((B * 8, fw), F32)],
        compiler_params=_cparams(("parallel", "arbitrary"), 40),
        name="dft_fold",
    )(a, a, a, b, b, b)


def _seq_dft_kernel(cs_ref, sn_ref, ae_ref, bo_ref, mid_ref, y_ref, d_ref, p_s, q_s, *, scale):
    kk = pl.program_id(3)

    @pl.when(kk == 0)
    def _():
        p_s[...] = jnp.zeros(p_s.shape, F32)
        q_s[...] = jnp.zeros(q_s.shape, F32)

    p_s[...] += jnp.dot(cs_ref[...], ae_ref[...], preferred_element_type=F32)
    q_s[...] += jnp.dot(sn_ref[...], bo_ref[...], preferred_element_type=F32)

    @pl.when(kk == pl.num_programs(3) - 1)
    def _():
        row = lax.broadcasted_iota(jnp.int32, p_s.shape, 0)
        pm = p_s[...] + jnp.where(row % 2 == 0, scale, -scale) * mid_ref[0:1, :].astype(F32)
        y_ref[...] = (pm + q_s[...]).astype(y_ref.dtype)
        d_ref[...] = (pm - q_s[...]).astype(d_ref.dtype)


def _seq_dft(cs, sn, ae, bo, a, B, S, scale):
    fw = a.shape[1]
    half = S // 2
    tm = _tile(half, 1024)
    tn = _tile(fw, 1024)
    tk = _tile(half, 1024)
    nm, nk = half // tm, half // tk
    mid_rows = 16
    return pl.pallas_call(
        functools.partial(_seq_dft_kernel, scale=scale),
        grid=(B, nm, fw // tn, nk),
        in_specs=[pl.BlockSpec((tm, tk), lambda bb, i, j, k: (i, k)),
                  pl.BlockSpec((tm, tk), lambda bb, i, j, k: (i, k)),
                  pl.BlockSpec((tk, tn), lambda bb, i, j, k: (bb * nk + k, j)),
                  pl.BlockSpec((tk, tn), lambda bb, i, j, k: (bb * nk + k, j)),
                  pl.BlockSpec((mid_rows, tn),
                               lambda bb, i, j, k: ((bb * S + half) // mid_rows, j))],
        out_specs=[pl.BlockSpec((tm, tn), lambda bb, i, j, k: (bb * nm + i, j)),
                   pl.BlockSpec((tm, tn), lambda bb, i, j, k: (bb * nm + i, j))],
        out_shape=[jax.ShapeDtypeStruct((B * half, fw), BF16),
                   jax.ShapeDtypeStruct((B * half, fw), BF16)],
        scratch_shapes=[pltpu.VMEM((tm, tn), F32), pltpu.VMEM((tm, tn), F32)],
        compiler_params=_cparams(("parallel", "parallel", "parallel", "arbitrary"), 48),
        name="seq_dft",
    )(cs, sn, ae, bo, a)


def _dft_mirror_kernel(yh_ref, dm_ref, dn_ref, alt_ref, y_ref, *, scale, nh):
    j = pl.program_id(1)

    @pl.when(j < nh)
    def _():
        y_ref[...] = yh_ref[...]

    @pl.when(j >= nh)
    def _():
        tm = dm_ref.shape[0]
        r = lax.broadcasted_iota(jnp.int32, (tm, tm), 0)
        c = lax.broadcasted_iota(jnp.int32, (tm, tm), 1)
        rev = jnp.where(r + c == tm, 1.0, 0.0).astype(BF16)
        first = (jnp.where(r + c == 0, 1.0, 0.0) * (j > nh).astype(F32)).astype(BF16)
        out = (jnp.dot(rev, dm_ref[...], preferred_element_type=F32)
               + jnp.dot(first, dn_ref[...], preferred_element_type=F32))
        row = lax.broadcasted_iota(jnp.int32, out.shape, 0)
        nyquist = jnp.where(row == 0, scale * (j == nh).astype(F32), 0.0) * alt_ref[0:1, :]
        y_ref[...] = (out + nyquist).astype(y_ref.dtype)


def _dft_mirror(y_half, d, alt, B, S, scale):
    fw = d.shape[1]
    half = S // 2
    tm = _tile(half, 256)
    nh = half // tm

    def mirror(bb, j):
        return (bb * nh + nh - 1 - jnp.maximum(j - nh, 0), 0)

    def mirror_next(bb, j):
        return (bb * nh + jnp.minimum(nh - jnp.maximum(j - nh, 0), nh - 1), 0)

    return pl.pallas_call(
        functools.partial(_dft_mirror_kernel, scale=scale, nh=nh),
        grid=(B, 2 * nh),
        in_specs=[pl.BlockSpec((tm, fw), lambda bb, j: (bb * nh + jnp.minimum(j, nh - 1), 0)),
                  pl.BlockSpec((tm, fw), mirror),
                  pl.BlockSpec((tm, fw), mirror_next),
                  pl.BlockSpec((8, fw), lambda bb, j: (bb, 0))],
        out_specs=pl.BlockSpec((tm, fw), lambda bb, j: (bb * 2 * nh + j, 0)),
        out_shape=jax.ShapeDtypeStruct((B * S, fw), d.dtype),
        compiler_params=_cparams(("parallel", "arbitrary"), 40),
        name="dft_mirror",
    )(y_half, d, d, alt)


def _wo_kernel(ym_ref, yf_ref, gm_ref, gf_ref, w_ref, x_ref, mu_ref, rs_ref, lg_ref, lb_ref,
               pre_ref, mix_s, *, alpha, mw):
    @pl.when(pl.program_id(1) == 0)
    def _():
        ym = ym_ref[...].astype(F32)
        mix_s[:, :mw] = (ym * lax.rsqrt(jnp.mean(ym * ym, axis=-1, keepdims=True) + RMS_EPS)
                         * gm_ref[...]).astype(BF16)
        yf = yf_ref[...].astype(F32)
        mix_s[:, mw:] = (yf * lax.rsqrt(jnp.mean(yf * yf, axis=-1, keepdims=True) + RMS_EPS)
                         * gf_ref[...]).astype(BF16)

    h = (x_ref[...] - mu_ref[...]) * rs_ref[...] * lg_ref[...] + lb_ref[...]
    pre_ref[...] = alpha * h + jnp.dot(mix_s[...], w_ref[...], preferred_element_type=F32)


def _wo(ym, yf, gm, gf, w_o, x2, mu, rs, ln_g, ln_b, alpha):
    T, mw = ym.shape
    fw = yf.shape[1]
    D = w_o.shape[1]
    tm = _tile(T, 1024)
    tn = _tile(D, 512)
    return pl.pallas_call(
        functools.partial(_wo_kernel, alpha=alpha, mw=mw),
        grid=(T // tm, D // tn),
        in_specs=[pl.BlockSpec((tm, mw), lambda i, j: (i, 0)),
                  pl.BlockSpec((tm, fw), lambda i, j: (i, 0)),
                  pl.BlockSpec((1, mw), lambda i, j: (0, 0)),
                  pl.BlockSpec((1, fw), lambda i, j: (0, 0)),
                  pl.BlockSpec((mw + fw, tn), lambda i, j: (0, j)),
                  pl.BlockSpec((tm, tn), lambda i, j: (i, j)),
                  pl.BlockSpec((tm, 1), lambda i, j: (i, 0)),
                  pl.BlockSpec((tm, 1), lambda i, j: (i, 0)),
                  pl.BlockSpec((1, tn), lambda i, j: (0, j)),
                  pl.BlockSpec((1, tn), lambda i, j: (0, j))],
        out_specs=pl.BlockSpec((tm, tn), lambda i, j: (i, j)),
        out_shape=jax.ShapeDtypeStruct((T, D), F32),
        scratch_shapes=[pltpu.VMEM((tm, mw + fw), BF16)],
        compiler_params=_cparams(("parallel", "arbitrary"), 56),
        name="w_o_residual",
    )(ym, yf, gm.reshape(1, mw), gf.reshape(1, fw), w_o, x2, mu, rs,
      ln_g.reshape(1, D), ln_b.reshape(1, D))


def _pack_bf16_pair(lo, hi):
    lo_bits = pltpu.bitcast(lo.astype(BF16).astype(F32), jnp.uint32)
    hi_bits = pltpu.bitcast(hi.astype(BF16).astype(F32), jnp.uint32)
    return (lo_bits >> 16) | (hi_bits & jnp.uint32(0xFFFF0000))


def _unpack_bf16_pair(w):
    lo = pltpu.bitcast(w << 16, F32).astype(BF16)
    hi = pltpu.bitcast(w & jnp.uint32(0xFFFF0000), F32).astype(BF16)
    return lo, hi


def _ln1_router_kernel(pre_ref, g_ref, b_ref, rw_ref, rb_ref, x1_ref, x1p_ref, idx_ref, gate_ref):
    x = pre_ref[...]
    mu = jnp.mean(x, axis=-1, keepdims=True)
    xc = x - mu
    var = jnp.mean(xc * xc, axis=-1, keepdims=True)
    x1 = xc * lax.rsqrt(var + LN_EPS) * g_ref[...] + b_ref[...]
    x1_ref[...] = x1
    half = x1.shape[1] // 2
    x1p_ref[...] = _pack_bf16_pair(x1[:, :half], x1[:, half:])
    logits = lax.dot_general(rw_ref[...], x1, (((1,), (1,)), ((), ())),
                             precision=lax.Precision.HIGHEST, preferred_element_type=F32) + rb_ref[...]
    n_e = logits.shape[0]
    eidx = lax.broadcasted_iota(jnp.int32, logits.shape, 0)
    vals, idxs = [], []
    for _ in range(TOP_K):
        m = jnp.max(logits, axis=0, keepdims=True)
        sel = jnp.min(jnp.where(logits == m, eidx, n_e), axis=0, keepdims=True)
        logits = jnp.where(eidx == sel, -jnp.inf, logits)
        vals.append(m)
        idxs.append(sel)
    exps = [jnp.exp(v - vals[0]) for v in vals]
    denom = exps[0] + exps[1] + exps[2] + exps[3]
    for kk in range(TOP_K):
        idx_ref[kk:kk + 1, :] = idxs[kk]
        gate_ref[kk:kk + 1, :] = exps[kk] / denom


def _ln1_router(pre, g, b, router_w, router_b):
    T, D = pre.shape
    E = router_w.shape[1]
    tm = _tile(T, 256)
    return pl.pallas_call(
        _ln1_router_kernel,
        grid=(T // tm,),
        in_specs=[pl.BlockSpec((tm, D), lambda i: (i, 0)),
                  pl.BlockSpec((1, D), lambda i: (0, 0)),
                  pl.BlockSpec((1, D), lambda i: (0, 0)),
                  pl.BlockSpec((E, D), lambda i: (0, 0)),
                  pl.BlockSpec((E, 1), lambda i: (0, 0))],
        out_specs=[pl.BlockSpec((tm, D), lambda i: (i, 0)),
                   pl.BlockSpec((tm, D // 2), lambda i: (i, 0)),
                   pl.BlockSpec((TOP_K, tm), lambda i: (0, i)),
                   pl.BlockSpec((TOP_K, tm), lambda i: (0, i))],
        out_shape=[jax.ShapeDtypeStruct((T, D), F32),
                   jax.ShapeDtypeStruct((T, D // 2), jnp.uint32),
                   jax.ShapeDtypeStruct((TOP_K, T), jnp.int32),
                   jax.ShapeDtypeStruct((TOP_K, T), F32)],
        compiler_params=_cparams(("parallel",), 48),
        name="ln1_router",
    )(pre, g.reshape(1, D), b.reshape(1, D), router_w.T, router_b.reshape(E, 1))


def _row_copy(src_hbm, dst_ref, src_row, dst_row, sem):
    return pltpu.make_async_copy(src_hbm.at[pl.ds(src_row, 1)], dst_ref.at[pl.ds(dst_row, 1)], sem)


def _dispatch_kernel(dest_ref, pad_ref, npad_ref, nv_ref, x_ref, o_hbm, zero_s, sem, zsem, *, n_steps):
    b = pl.program_id(0)
    tt = x_ref.shape[0]

    def wait_rows(n):
        pltpu.make_async_copy(o_hbm.at[pl.ds(0, n)], o_hbm.at[pl.ds(0, n)], sem).wait()

    @pl.when(b < n_steps)
    def _():
        def issue(r, c):
            for kk in range(TOP_K):
                _row_copy(x_ref, o_hbm, r, dest_ref[(b * tt + r) * TOP_K + kk], sem).start(priority=kk % 2)
            return c
        lax.fori_loop(0, tt, issue, 0, unroll=2)
        wait_rows(tt * TOP_K)

    @pl.when(b == n_steps)
    def _():
        n_pad = npad_ref[0]

        def issue_pad(i, c):
            _row_copy(x_ref, o_hbm, 0, pad_ref[i], sem).start()
            return c
        lax.fori_loop(0, n_pad, issue_pad, 0)

        def wait_pad(i, c):
            wait_rows(1)
            return c
        lax.fori_loop(0, n_pad, wait_pad, 0)

        zero_s[...] = jnp.zeros(zero_s.shape, zero_s.dtype)
        n_sub = MOE_CHUNK // MOE_SUB

        def zero_copy(i):
            return pltpu.make_async_copy(zero_s, o_hbm.at[pl.ds(pl.multiple_of(i * MOE_SUB, MOE_SUB), MOE_SUB)], zsem)

        def empty(i):
            return (i % n_sub) * MOE_SUB >= nv_ref[i // n_sub]

        def issue_zero(i, c):
            @pl.when(empty(i))
            def _():
                zero_copy(i).start()
            return c
        lax.fori_loop(0, nv_ref.shape[0] * n_sub, issue_zero, 0)

        def wait_zero(i, c):
            @pl.when(empty(i))
            def _():
                zero_copy(i).wait()
            return c
        lax.fori_loop(0, nv_ref.shape[0] * n_sub, wait_zero, 0)


def _dispatch(dest, pad_slots, n_pad, chunk_nv, x1p):
    T, W = x1p.shape
    tt = _tile(T, DISPATCH_TOKENS)
    n_steps = T // tt
    grid_spec = pltpu.PrefetchScalarGridSpec(
        num_scalar_prefetch=4,
        grid=(n_steps + 1,),
        in_specs=[pl.BlockSpec((tt, W), lambda b, d, p, n, v: (jnp.minimum(b, n_steps - 1), 0))],
        out_specs=pl.BlockSpec(memory_space=pl.ANY),
        scratch_shapes=[pltpu.VMEM((MOE_SUB, W), x1p.dtype),
                        pltpu.SemaphoreType.DMA(()), pltpu.SemaphoreType.DMA(())],
    )
    return pl.pallas_call(
        functools.partial(_dispatch_kernel, n_steps=n_steps),
        grid_spec=grid_spec,
        out_shape=jax.ShapeDtypeStruct((chunk_nv.shape[0] * MOE_CHUNK, W), x1p.dtype),
        compiler_params=_cparams(("arbitrary",), 32),
        name="moe_dispatch",
    )(dest, pad_slots, n_pad, chunk_nv, x1p)


def _expert_up_kernel(ce_ref, nv_ref, nu_ref, x_ref, wg_ref, wu_ref, bg_ref, bu_ref, h_ref):
    c = pl.program_id(0)
    nv = nv_ref[c]
    brow = ce_ref[c] * pl.num_programs(1) + pl.program_id(1)

    n_sub = MOE_CHUNK // MOE_SUB
    n_act = (nv + MOE_SUB - 1) // MOE_SUB
    for k in range(n_sub + 1):
        @pl.when(n_act == k)
        def _(k=k):
            if k > 0:
                x_lo, x_hi = _unpack_bf16_pair(x_ref[:k * MOE_SUB, :])
                half = x_lo.shape[1]

                def proj(w_ref):
                    return (jnp.dot(x_lo, w_ref[:half, :].astype(BF16), preferred_element_type=F32)
                            + jnp.dot(x_hi, w_ref[half:, :].astype(BF16), preferred_element_type=F32))
                hg = jnp.minimum(proj(wg_ref) + bg_ref[pl.ds(brow, 1), :], SWIGLU_LIMIT)
                hu = jnp.clip(proj(wu_ref) + bu_ref[pl.ds(brow, 1), :], -SWIGLU_LIMIT, SWIGLU_LIMIT)
                act = (hu + 1.0) * (hg * jax.nn.sigmoid(SWIGLU_ALPHA * hg))
                h_ref[:k * MOE_SUB, :] = act.astype(h_ref.dtype)
            if k < n_sub:
                h_ref[k * MOE_SUB:, :] = jnp.zeros(((n_sub - k) * MOE_SUB, h_ref.shape[1]), h_ref.dtype)


def _expert_up(chunk_e, chunk_nv, n_used, xs, w_gate, w_up, b_gate, b_up, n_chunks):
    E, D, F = w_gate.shape
    tf = _tile(F, 256)
    nf = F // tf

    def used(c, nu):
        return jnp.minimum(c, nu[0] - 1)

    def jeff(c, j, nu):
        return jnp.where(c < nu[0], j, nf - 1)

    grid_spec = pltpu.PrefetchScalarGridSpec(
        num_scalar_prefetch=3,
        grid=(n_chunks, nf),
        in_specs=[pl.BlockSpec((MOE_CHUNK, D // 2), lambda c, j, ce, nv, nu: (used(c, nu), 0)),
                  pl.BlockSpec((None, D, tf), lambda c, j, ce, nv, nu: (ce[c], 0, jeff(c, j, nu))),
                  pl.BlockSpec((None, D, tf), lambda c, j, ce, nv, nu: (ce[c], 0, jeff(c, j, nu))),
                  pl.BlockSpec((E * nf, tf), lambda c, j, ce, nv, nu: (0, 0)),
                  pl.BlockSpec((E * nf, tf), lambda c, j, ce, nv, nu: (0, 0))],
        out_specs=pl.BlockSpec((MOE_CHUNK, tf), lambda c, j, ce, nv, nu: (c, j)),
    )
    return pl.pallas_call(
        _expert_up_kernel,
        grid_spec=grid_spec,
        out_shape=jax.ShapeDtypeStruct((n_chunks * MOE_CHUNK, F), BF16),
        compiler_params=_cparams(("arbitrary", "arbitrary"), 60),
        name="expert_gate_up",
    )(chunk_e, chunk_nv, n_used, xs, w_gate, w_up, b_gate.reshape(E * nf, tf), b_up.reshape(E * nf, tf))


def _expert_down_kernel(ce_ref, nv_ref, nu_ref, h_ref, wd_ref, bd_ref, y_ref):
    c = pl.program_id(0)
    nv = nv_ref[c]
    brow = ce_ref[c] * pl.num_programs(1) + pl.program_id(1)

    n_sub = MOE_CHUNK // MOE_SUB
    n_act = (nv + MOE_SUB - 1) // MOE_SUB
    for k in range(n_sub + 1):
        @pl.when(n_act == k)
        def _(k=k):
            if k > 0:
                out = jnp.dot(h_ref[:k * MOE_SUB, :], wd_ref[...].astype(BF16),
                              preferred_element_type=F32) + bd_ref[pl.ds(brow, 1), :]
                hw = out.shape[1] // 2
                y_ref[:k * MOE_SUB, :] = _pack_bf16_pair(out[:, :hw], out[:, hw:])
            if k < n_sub:
                y_ref[k * MOE_SUB:, :] = jnp.zeros(((n_sub - k) * MOE_SUB, y_ref.shape[1]), y_ref.dtype)


def _expert_down(chunk_e, chunk_nv, n_used, hmid, w_down, b_down, n_chunks):
    E, F, D = w_down.shape
    tn = _tile(D, MOE_DOWN_TN)
    nn = D // tn

    def used(c, nu):
        return jnp.minimum(c, nu[0] - 1)

    def jeff(c, j, nu):
        return jnp.where(c < nu[0], j, nn - 1)

    grid_spec = pltpu.PrefetchScalarGridSpec(
        num_scalar_prefetch=3,
        grid=(n_chunks, nn),
        in_specs=[pl.BlockSpec((MOE_CHUNK, F), lambda c, j, ce, nv, nu: (used(c, nu), 0)),
                  pl.BlockSpec((None, F, tn), lambda c, j, ce, nv, nu: (ce[c], 0, jeff(c, j, nu))),
                  pl.BlockSpec((E * nn, tn), lambda c, j, ce, nv, nu: (0, 0))],
        out_specs=pl.BlockSpec((MOE_CHUNK, tn // 2), lambda c, j, ce, nv, nu: (c, j)),
    )
    return pl.pallas_call(
        _expert_down_kernel,
        grid_spec=grid_spec,
        out_shape=jax.ShapeDtypeStruct((n_chunks * MOE_CHUNK, D // 2), jnp.uint32),
        compiler_params=_cparams(("arbitrary", "arbitrary"), 56),
        name="expert_down",
    )(chunk_e, chunk_nv, n_used, hmid, w_down, b_down.reshape(E * nn, tn))


def _combine_kernel(dest_ref, y_hbm, x1_ref, gate_ref, g_ref, b_ref, o_ref, *scratch, alpha, tn):
    i = pl.program_id(0)
    n = pl.num_programs(0)
    nb = COMBINE_BUFS
    tt = o_ref.shape[0] // nb
    bufs, sem = scratch[:nb], scratch[nb]

    def issue_tile(tile, slot):
        base = jnp.minimum(tile, nb * n - 1) * tt
        for r in range(tt):
            for kk in range(TOP_K):
                _row_copy(y_hbm, bufs[slot].at[kk], dest_ref[(base + r) * TOP_K + kk], r,
                          sem.at[slot]).start(priority=kk % 2)

    def wait_tile(slot):
        for kk in range(TOP_K):
            pltpu.make_async_copy(y_hbm.at[pl.ds(0, tt)], bufs[slot].at[kk], sem.at[slot]).wait()

    def finish_tile(slot):
        rows = slice(slot * tt, (slot + 1) * tt)
        gates = gate_ref[rows, :]
        y_lo = y_hi = None
        for kk in range(TOP_K):
            w = bufs[slot][kk]
            g_k = gates[:, kk:kk + 1]
            lo = g_k * pltpu.bitcast(w << 16, F32)
            hi = g_k * pltpu.bitcast(w & jnp.uint32(0xFFFF0000), F32)
            y_lo = lo if y_lo is None else y_lo + lo
            y_hi = hi if y_hi is None else y_hi + hi
        hw = tn // 2
        pieces = []
        for j in range(x1_ref.shape[1] // tn):
            pieces.append((slice(j * tn, j * tn + hw), y_lo[:, j * hw:(j + 1) * hw]))
            pieces.append((slice(j * tn + hw, (j + 1) * tn), y_hi[:, j * hw:(j + 1) * hw]))
        zs = [alpha * x1_ref[rows, cols] + y for cols, y in pieces]
        d = x1_ref.shape[1]
        mu = sum(jnp.sum(z, axis=-1, keepdims=True) for z in zs) / d
        zcs = [z - mu for z in zs]
        var = sum(jnp.sum(zc * zc, axis=-1, keepdims=True) for zc in zcs) / d
        rs = lax.rsqrt(var + LN_EPS)
        for (cols, _), zc in zip(pieces, zcs):
            o_ref[rows, cols] = zc * rs * g_ref[:, cols] + b_ref[:, cols]

    @pl.when(i == 0)
    def _():
        for t in range(COMBINE_AHEAD):
            issue_tile(t, t)

    for t in range(nb):
        wait_tile(t)
        issue_tile(nb * i + t + COMBINE_AHEAD, (t + COMBINE_AHEAD) % nb)
        finish_tile(t)

    @pl.when(i == n - 1)
    def _():
        for t in range(COMBINE_AHEAD):
            wait_tile(t)


def _combine(dest, yslots, x1, gates_tk, g, b, alpha):
    T, D = x1.shape
    nb = COMBINE_BUFS
    tt = _tile(T // nb, COMBINE_TOKENS)
    grid_spec = pltpu.PrefetchScalarGridSpec(
        num_scalar_prefetch=1,
        grid=(T // (nb * tt),),
        in_specs=[pl.BlockSpec(memory_space=pl.ANY),
                  pl.BlockSpec((nb * tt, D), lambda i, d: (i, 0)),
                  pl.BlockSpec((nb * tt, TOP_K), lambda i, d: (i, 0)),
                  pl.BlockSpec((1, D), lambda i, d: (0, 0)),
                  pl.BlockSpec((1, D), lambda i, d: (0, 0))],
        out_specs=pl.BlockSpec((nb * tt, D), lambda i, d: (i, 0)),
        scratch_shapes=[pltpu.VMEM((TOP_K, tt, D // 2), jnp.uint32) for _ in range(nb)]
        + [pltpu.SemaphoreType.DMA((nb,))],
    )
    return pl.pallas_call(
        functools.partial(_combine_kernel, alpha=alpha, tn=_tile(D, MOE_DOWN_TN)),
        grid_spec=grid_spec,
        out_shape=jax.ShapeDtypeStruct((T, D), F32),
        compiler_params=_cparams(("arbitrary",), 48),
        name="moe_combine_ln2",
    )(dest, yslots, x1, gates_tk, g.reshape(1, D), b.reshape(1, D))


def _routing_tables(top_idx, n_experts, n_chunks):
    T = top_idx.shape[1]
    M = T * TOP_K
    flat_e = top_idx.T.reshape(M)
    onehot = (flat_e[:, None] == jnp.arange(n_experts, dtype=jnp.int32)[None, :]).astype(jnp.int32)
    csum = jnp.cumsum(onehot, axis=0)
    rank = jnp.sum(csum * onehot, axis=1) - 1
    counts = csum[-1]
    chunks_e = (counts + MOE_CHUNK - 1) // MOE_CHUNK
    chunk_end = jnp.cumsum(chunks_e)
    chunk_start = chunk_end - chunks_e
    n_used = chunk_end[-1]
    dest = chunk_start[flat_e] * MOE_CHUNK + rank
    P = n_chunks * MOE_CHUNK
    pad_idx = counts[:, None] + jnp.arange(MOE_SUB, dtype=jnp.int32)[None, :]
    padded = (counts + MOE_SUB - 1) // MOE_SUB * MOE_SUB
    pad_slots = jnp.where(pad_idx < padded[:, None], chunk_start[:, None] * MOE_CHUNK + pad_idx, P)
    pad_slots = jnp.sort(pad_slots.reshape(-1)).astype(jnp.int32)
    n_pad = jnp.sum(padded - counts).astype(jnp.int32).reshape(1)
    cid = jnp.arange(n_chunks, dtype=jnp.int32)
    chunk_e = jnp.minimum(jnp.searchsorted(chunk_end, cid, side='right'), n_experts - 1).astype(jnp.int32)
    last_e = chunk_e[jnp.maximum(n_used - 1, 0)]
    chunk_e = jnp.where(cid < n_used, chunk_e, last_e)
    chunk_nv = jnp.where(cid < n_used,
                         jnp.clip(counts[chunk_e] - (cid - chunk_start[chunk_e]) * MOE_CHUNK, 0, MOE_CHUNK),
                         0).astype(jnp.int32)
    return (dest.astype(jnp.int32), pad_slots, n_pad, chunk_e, chunk_nv,
            n_used.astype(jnp.int32).reshape(1))


def kernel(x, positions, ln_in_g, ln_in_b, w_in, q_a_norm_g, w_q_b, kv_a_norm_g, w_kv_b, mla_out_norm_g, fourier_out_norm_g, w_o, ln1_g, ln1_b, router_w, router_b, w_gate, b_gate, w_up, b_up, w_down, b_down, ln2_g, ln2_b):
    B, S, D = x.shape
    T = B * S
    depth = w_in.shape[0]
    assert depth == 1, "single-layer trunk only"
    qr = q_a_norm_g.shape[1]
    kvr = kv_a_norm_g.shape[1]
    H = w_q_b.shape[2] // (QK_NOPE_DIM + QK_ROPE_DIM)
    fw = fourier_out_norm_g.shape[1]
    E = router_w.shape[2]
    assert H % HEADS_PER_TILE == 0 and (qr + kvr) % LANES == 0 and S % (2 * DFT_ROW_SPLIT) == 0
    alpha = (2.0 * depth) ** 0.25

    inv_freq = ROPE_THETA ** (-jnp.arange(0, QK_ROPE_DIM, 2, dtype=F32) / QK_ROPE_DIM)
    ang = positions.astype(F32)[..., None] * inv_freq
    cos4 = jnp.tile(jnp.cos(ang), (1, 1, 2 * LANES // QK_ROPE_DIM)).reshape(T, LANES)
    sin4 = jnp.tile(jnp.sin(ang), (1, 1, 2 * LANES // QK_ROPE_DIM)).reshape(T, LANES)

    rope_end = qr + kvr + QK_ROPE_DIM
    w_a = w_in[0, :, :qr + kvr + LANES].astype(BF16)
    w_f = w_in[0, :, rope_end:].astype(BF16)
    wq = w_q_b[0].reshape(qr, H // HEADS_PER_TILE, HEADS_PER_TILE, QK_NOPE_DIM + QK_ROPE_DIM)
    wq_perm = jnp.concatenate(
        [wq[..., :QK_NOPE_DIM].reshape(qr, H // HEADS_PER_TILE, HEADS_PER_TILE * QK_NOPE_DIM),
         wq[..., QK_NOPE_DIM:].reshape(qr, H // HEADS_PER_TILE, HEADS_PER_TILE * QK_ROPE_DIM)],
        axis=-1).reshape(qr, H * (QK_NOPE_DIM + QK_ROPE_DIM)).astype(BF16)
    w_kv = w_kv_b[0].astype(BF16)
    w_o_b = w_o[0].astype(BF16)
    ch = jnp.arange(FOURIER_GROUP_DIM, dtype=jnp.int32)
    ang_c = (2.0 * math.pi / FOURIER_GROUP_DIM) * ((ch[:, None] * ch[None, :]) % FOURIER_GROUP_DIM).astype(F32)
    cs_tab = jnp.concatenate([jnp.cos(ang_c), jnp.sin(ang_c)], axis=1).astype(BF16)

    x2 = x.reshape(T, D)
    hb, mu, rs = _ln_in(x2, ln_in_g, ln_in_b)

    cq, ckv, kpe2 = _inproj_a(hb, w_a, q_a_norm_g[0], kv_a_norm_g[0], cos4, sin4)
    qscale = (QK_NOPE_DIM + QK_ROPE_DIM) ** -0.5 * LOG2E
    q = _q_up(cq, wq_perm, cos4, sin4, qscale)
    k, v = _kv_up(ckv, w_kv, kpe2)
    y_mla = _attention(q, k, v, B, S, H)

    fa, fb = _inproj_f(hb, w_f, cs_tab)
    cs_mat, sn_mat = _dft_matrices(S, FOURIER_GROUP_DIM)
    fae, fbo, falt = _dft_fold(fa, fb, B, S)
    dft_scale = 1.0 / math.sqrt(S * FOURIER_GROUP_DIM)
    y_half, y_diff = _seq_dft(cs_mat, sn_mat, fae, fbo, fa, B, S, dft_scale)
    y_f = _dft_mirror(y_half, y_diff, falt, B, S, dft_scale)

    pre = _wo(y_mla, y_f, mla_out_norm_g[0], fourier_out_norm_g[0], w_o_b, x2, mu, rs,
              ln_in_g, ln_in_b, alpha)
    x1, x1p, top_idx, gates = _ln1_router(pre, ln1_g[0], ln1_b[0], router_w[0], router_b[0])

    n_chunks = -(-T * TOP_K // MOE_CHUNK) + E
    dest, pad_slots, n_pad, chunk_e, chunk_nv, n_used = _routing_tables(top_idx, E, n_chunks)
    xs = _dispatch(dest, pad_slots, n_pad, chunk_nv, x1p)
    hmid = _expert_up(chunk_e, chunk_nv, n_used, xs, w_gate[0], w_up[0], b_gate[0], b_up[0], n_chunks)
    yslots = _expert_down(chunk_e, chunk_nv, n_used, hmid, w_down[0], b_down[0], n_chunks)
    out = _combine(dest, yslots, x1, gates.T, ln2_g[0], ln2_b[0], alpha)
    return out.reshape(B, S, D)
```

```python
import functools
import math

import jax
import jax.numpy as jnp
from jax import lax
from jax.experimental import pallas as pl
from jax.experimental.pallas import tpu as pltpu

F32 = jnp.float32
BF16 = jnp.bfloat16

V_HEAD_DIM = 128
QK_NOPE_DIM = 128
QK_ROPE_DIM = 64
QK_PAD_DIM = 256
V_PAD_DIM = 256
ROPE_THETA = 10000.0
FOURIER_GROUP_DIM = 128
TOP_K = 4
SWIGLU_LIMIT = 7.0
SWIGLU_ALPHA = 1.702
LN_EPS = 1e-5
RMS_EPS = 1e-6
LOG2E = 1.4426950408889634

LANES = 128
V7X_VMEM_BYTES = 64 * 1024 * 1024
HEADS_PER_TILE = 4
DFT_ROW_SPLIT = 64
ATTN_TQ = 1024
ATTN_TKV = 1024

MOE_CHUNK = 1280
MOE_SUB = 128
DISPATCH_TOKENS = 512
COMBINE_TOKENS = 64
COMBINE_BUFS = 4
COMBINE_AHEAD = 2
MOE_DOWN_TN = 1024


def _cparams(semantics, vmem_mb):
    return pltpu.CompilerParams(dimension_semantics=semantics,
                                vmem_limit_bytes=min(vmem_mb * 1024 * 1024, V7X_VMEM_BYTES - (4 << 20)))


def _tile(dim, pref):
    t = min(dim, pref)
    while dim % t:
        t //= 2
    return t


def _ln_in_kernel(x_ref, g_ref, b_ref, hb_ref, mu_ref, rs_ref):
    x = x_ref[...]
    mu = jnp.mean(x, axis=-1, keepdims=True)
    xc = x - mu
    var = jnp.mean(xc * xc, axis=-1, keepdims=True)
    rs = lax.rsqrt(var + LN_EPS)
    hb_ref[...] = (xc * rs * g_ref[...] + b_ref[...]).astype(BF16)
    mu_ref[...] = mu
    rs_ref[...] = rs


def _ln_in(x2, g, b):
    T, D = x2.shape
    tm = _tile(T, 256)
    return pl.pallas_call(
        _ln_in_kernel,
        grid=(T // tm,),
        in_specs=[pl.BlockSpec((tm, D), lambda i: (i, 0)),
                  pl.BlockSpec((1, D), lambda i: (0, 0)),
                  pl.BlockSpec((1, D), lambda i: (0, 0))],
        out_specs=[pl.BlockSpec((tm, D), lambda i: (i, 0)),
                   pl.BlockSpec((tm, 1), lambda i: (i, 0)),
                   pl.BlockSpec((tm, 1), lambda i: (i, 0))],
        out_shape=[jax.ShapeDtypeStruct((T, D), BF16),
                   jax.ShapeDtypeStruct((T, 1), F32),
                   jax.ShapeDtypeStruct((T, 1), F32)],
        compiler_params=_cparams(("parallel",), 40),
        name="ln_in",
    )(x2, g.reshape(1, D), b.reshape(1, D))


def _rope128(p, cos4, sin4):
    lane = lax.broadcasted_iota(jnp.int32, p.shape, 1)
    first_half = (lane % QK_ROPE_DIM) < (QK_ROPE_DIM // 2)
    rot = jnp.where(first_half, -pltpu.roll(p, LANES - QK_ROPE_DIM // 2, 1),
                    pltpu.roll(p, QK_ROPE_DIM // 2, 1))
    return p * cos4 + rot * sin4


def _inproj_a_kernel(h_ref, w_ref, gq_ref, gkv_ref, cos_ref, sin_ref,
                     cq_ref, ckv_ref, kpe_ref, *, qr, kvr):
    acc = jnp.dot(h_ref[...], w_ref[...], preferred_element_type=F32)
    cq = acc[:, :qr]
    cq_ref[...] = (cq * lax.rsqrt(jnp.mean(cq * cq, axis=-1, keepdims=True) + RMS_EPS)
                   * gq_ref[...]).astype(BF16)
    ckv = acc[:, qr:qr + kvr]
    ckv_ref[...] = (ckv * lax.rsqrt(jnp.mean(ckv * ckv, axis=-1, keepdims=True) + RMS_EPS)
                    * gkv_ref[...]).astype(BF16)
    roped = _rope128(acc[:, qr + kvr:qr + kvr + LANES], cos_ref[...], sin_ref[...])
    lane = lax.broadcasted_iota(jnp.int32, roped.shape, 1)
    even = jnp.where(lane < QK_ROPE_DIM, roped, 0.0)
    kpe_ref[:, :LANES] = even.astype(BF16)
    kpe_ref[:, LANES:] = pltpu.roll(even, QK_ROPE_DIM, 1).astype(BF16)


def _inproj_a(hb, w_a, gq, gkv, cos4, sin4):
    T, D = hb.shape
    qr, kvr = gq.shape[0], gkv.shape[0]
    wa = w_a.shape[1]
    tm = _tile(T, 512)
    return pl.pallas_call(
        functools.partial(_inproj_a_kernel, qr=qr, kvr=kvr),
        grid=(T // tm,),
        in_specs=[pl.BlockSpec((tm, D), lambda i: (i, 0)),
                  pl.BlockSpec((D, wa), lambda i: (0, 0)),
                  pl.BlockSpec((1, qr), lambda i: (0, 0)),
                  pl.BlockSpec((1, kvr), lambda i: (0, 0)),
                  pl.BlockSpec((tm, LANES), lambda i: (i, 0)),
                  pl.BlockSpec((tm, LANES), lambda i: (i, 0))],
        out_specs=[pl.BlockSpec((tm, qr), lambda i: (i, 0)),
                   pl.BlockSpec((tm, kvr), lambda i: (i, 0)),
                   pl.BlockSpec((tm, 2 * LANES), lambda i: (i, 0))],
        out_shape=[jax.ShapeDtypeStruct((T, qr), BF16),
                   jax.ShapeDtypeStruct((T, kvr), BF16),
                   jax.ShapeDtypeStruct((T, 2 * LANES), BF16)],
        compiler_params=_cparams(("parallel",), 56),
        name="inproj_mla",
    )(hb, w_a, gq.reshape(1, qr), gkv.reshape(1, kvr), cos4, sin4)


def _q_up_kernel(c_ref, w_ref, cos_ref, sin_ref, q_ref, *, qscale):
    acc = jnp.dot(c_ref[...], w_ref[...], preferred_element_type=F32)
    nope_w = HEADS_PER_TILE * QK_NOPE_DIM
    lane = lax.broadcasted_iota(jnp.int32, (acc.shape[0], LANES), 1)
    for pair in range(HEADS_PER_TILE // 2):
        roped = _rope128(acc[:, nope_w + pair * LANES:nope_w + (pair + 1) * LANES],
                         cos_ref[...], sin_ref[...]) * qscale
        for par in range(2):
            j = 2 * pair + par
            keep = (lane < QK_ROPE_DIM) if par == 0 else (lane >= QK_ROPE_DIM)
            base = j * QK_PAD_DIM
            q_ref[:, base:base + QK_NOPE_DIM] = (
                acc[:, j * QK_NOPE_DIM:(j + 1) * QK_NOPE_DIM] * qscale).astype(BF16)
            q_ref[:, base + QK_NOPE_DIM:base + QK_PAD_DIM] = jnp.where(keep, roped, 0.0).astype(BF16)


def _q_up(cq, wq_perm, cos4, sin4, qscale):
    T, qr = cq.shape
    n_tiles = wq_perm.shape[1] // (HEADS_PER_TILE * (QK_NOPE_DIM + QK_ROPE_DIM))
    tw = HEADS_PER_TILE * (QK_NOPE_DIM + QK_ROPE_DIM)
    to = HEADS_PER_TILE * QK_PAD_DIM
    tm = _tile(T, 1024)
    return pl.pallas_call(
        functools.partial(_q_up_kernel, qscale=qscale),
        grid=(T // tm, n_tiles),
        in_specs=[pl.BlockSpec((tm, qr), lambda i, j: (i, 0)),
                  pl.BlockSpec((qr, tw), lambda i, j: (0, j)),
                  pl.BlockSpec((tm, LANES), lambda i, j: (i, 0)),
                  pl.BlockSpec((tm, LANES), lambda i, j: (i, 0))],
        out_specs=pl.BlockSpec((tm, to), lambda i, j: (i, j)),
        out_shape=jax.ShapeDtypeStruct((T, n_tiles * to), BF16),
        compiler_params=_cparams(("parallel", "arbitrary"), 40),
        name="q_up",
    )(cq, wq_perm, cos4, sin4)


def _kv_up_kernel(c_ref, w_ref, kpe_ref, k_ref, v_ref):
    acc = jnp.dot(c_ref[...], w_ref[...], preferred_element_type=F32)
    for j in range(HEADS_PER_TILE):
        src = j * (QK_NOPE_DIM + V_HEAD_DIM)
        k_ref[:, j * QK_PAD_DIM:j * QK_PAD_DIM + QK_NOPE_DIM] = acc[:, src:src + QK_NOPE_DIM].astype(BF16)
        par = j % 2
        k_ref[:, j * QK_PAD_DIM + QK_NOPE_DIM:(j + 1) * QK_PAD_DIM] = kpe_ref[:, par * LANES:(par + 1) * LANES]
        v_ref[:, j * V_PAD_DIM:j * V_PAD_DIM + V_HEAD_DIM] = (
            acc[:, src + QK_NOPE_DIM:src + QK_NOPE_DIM + V_HEAD_DIM].astype(BF16))
        v_ref[:, j * V_PAD_DIM + V_HEAD_DIM:(j + 1) * V_PAD_DIM] = jnp.ones(
            (acc.shape[0], V_PAD_DIM - V_HEAD_DIM), BF16)


def _kv_up(ckv, w_kv, kpe2):
    T, kvr = ckv.shape
    tw = HEADS_PER_TILE * (QK_NOPE_DIM + V_HEAD_DIM)
    n_tiles = w_kv.shape[1] // tw
    tm = _tile(T, 1024)
    return pl.pallas_call(
        _kv_up_kernel,
        grid=(T // tm, n_tiles),
        in_specs=[pl.BlockSpec((tm, kvr), lambda i, j: (i, 0)),
                  pl.BlockSpec((kvr, tw), lambda i, j: (0, j)),
                  pl.BlockSpec((tm, 2 * LANES), lambda i, j: (i, 0))],
        out_specs=[pl.BlockSpec((tm, HEADS_PER_TILE * QK_PAD_DIM), lambda i, j: (i, j)),
                   pl.BlockSpec((tm, HEADS_PER_TILE * V_PAD_DIM), lambda i, j: (i, j))],
        out_shape=[jax.ShapeDtypeStruct((T, n_tiles * HEADS_PER_TILE * QK_PAD_DIM), BF16),
                   jax.ShapeDtypeStruct((T, n_tiles * HEADS_PER_TILE * V_PAD_DIM), BF16)],
        compiler_params=_cparams(("parallel", "arbitrary"), 40),
        name="kv_up",
    )(ckv, w_kv, kpe2)


def _lane_tile(x, reps):
    return jnp.concatenate([x] * reps, axis=1)


def _attn_kernel(q_ref, k_ref, v_ref, o_ref, m_s, acc_s, s0_s, s1_s, x0_s, x1_s, p0_s, p1_s, a0_s, a1_s,
                 *, tkv):
    n_kv = k_ref.shape[0] // tkv
    s_buf, x_buf, p_buf, a_buf = (s0_s, s1_s), (x0_s, x1_s), (p0_s, p1_s), (a0_s, a1_s)
    m_s[...] = jnp.full(m_s.shape, -jnp.inf, F32)
    acc_s[...] = jnp.zeros(acc_s.shape, F32)

    def scores(i, slot):
        off = pl.multiple_of(i * tkv, tkv)
        s = lax.dot_general(q_ref[...], k_ref[pl.ds(off, tkv), :], (((1,), (1,)), ((), ())),
                            preferred_element_type=F32)
        s_buf[slot][...] = s
        x_buf[slot][...] = jnp.broadcast_to(jnp.max(s, axis=1, keepdims=True), x_buf[slot].shape)

    def probs(slot):
        m_prev = m_s[...]
        m_new = jnp.maximum(m_prev, x_buf[slot][...])
        m_s[...] = m_new
        a_buf[slot][...] = jnp.exp2(m_prev - m_new)
        p_buf[slot][...] = jnp.exp2(s_buf[slot][...] - _lane_tile(m_new, tkv // LANES)).astype(BF16)

    def values(i, slot):
        off = pl.multiple_of(i * tkv, tkv)
        acc_s[...] = (_lane_tile(a_buf[slot][...], V_PAD_DIM // LANES) * acc_s[...]
                      + jnp.dot(p_buf[slot][...], v_ref[pl.ds(off, tkv), :], preferred_element_type=F32))

    def trip(i, slot):
        scores(i + 1, 1 - slot)
        values(i - 1, 1 - slot)
        probs(slot)

    scores(0, 0)
    scores(1, 1)
    probs(0)
    for i in range(1, n_kv - 1):
        trip(i, i % 2)
    values(n_kv - 2, 0)
    probs(1)
    values(n_kv - 1, 1)
    acc = acc_s[...]
    o_ref[...] = (acc[:, :V_HEAD_DIM] / acc[:, V_HEAD_DIM:]).astype(o_ref.dtype)


def _attention(q, k, v, B, S, H):
    T = B * S
    tq = _tile(S, ATTN_TQ)
    tkv = _tile(S, min(ATTN_TKV, S // 2))
    assert (S // tkv) % 2 == 0, "the key-chunk pipeline is unrolled in pairs"
    nq = S // tq
    return pl.pallas_call(
        functools.partial(_attn_kernel, tkv=tkv),
        grid=(B, H, nq),
        in_specs=[pl.BlockSpec((tq, QK_PAD_DIM), lambda b, h, i: (b * nq + i, h)),
                  pl.BlockSpec((S, QK_PAD_DIM), lambda b, h, i: (b, h)),
                  pl.BlockSpec((S, V_PAD_DIM), lambda b, h, i: (b, h))],
        out_specs=pl.BlockSpec((tq, V_HEAD_DIM), lambda b, h, i: (b * nq + i, h)),
        out_shape=jax.ShapeDtypeStruct((T, H * V_HEAD_DIM), BF16),
        scratch_shapes=[pltpu.VMEM((tq, LANES), F32), pltpu.VMEM((tq, V_PAD_DIM), F32),
                        pltpu.VMEM((tq, tkv), F32), pltpu.VMEM((tq, tkv), F32),
                        pltpu.VMEM((tq, LANES), F32), pltpu.VMEM((tq, LANES), F32),
                        pltpu.VMEM((tq, tkv), BF16), pltpu.VMEM((tq, tkv), BF16),
                        pltpu.VMEM((tq, LANES), F32), pltpu.VMEM((tq, LANES), F32)],
        compiler_params=_cparams(("parallel", "parallel", "arbitrary"), 40),
        name="mla_attention",
    )(q, k, v)


def _inproj_f_kernel(h_ref, w_ref, cs_ref, a_ref, b_ref, *, groups):
    acc = jnp.dot(h_ref[...], w_ref[...], preferred_element_type=F32)
    C = FOURIER_GROUP_DIM
    for g in range(groups):
        ab = jnp.dot(acc[:, g * C:(g + 1) * C].astype(BF16), cs_ref[...], preferred_element_type=F32)
        a_ref[:, g * C:(g + 1) * C] = ab[:, :C].astype(BF16)
        b_ref[:, g * C:(g + 1) * C] = ab[:, C:].astype(BF16)


def _inproj_f(hb, w_f, cs_tab):
    T, D = hb.shape
    fw = w_f.shape[1]
    tm = _tile(T, 1024)
    tn = _tile(fw, 512)
    return pl.pallas_call(
        functools.partial(_inproj_f_kernel, groups=tn // FOURIER_GROUP_DIM),
        grid=(T // tm, fw // tn),
        in_specs=[pl.BlockSpec((tm, D), lambda i, j: (i, 0)),
                  pl.BlockSpec((D, tn), lambda i, j: (0, j)),
                  pl.BlockSpec((FOURIER_GROUP_DIM, 2 * FOURIER_GROUP_DIM), lambda i, j: (0, 0))],
        out_specs=[pl.BlockSpec((tm, tn), lambda i, j: (i, j)),
                   pl.BlockSpec((tm, tn), lambda i, j: (i, j))],
        out_shape=[jax.ShapeDtypeStruct((T, fw), BF16), jax.ShapeDtypeStruct((T, fw), BF16)],
        compiler_params=_cparams(("parallel", "arbitrary"), 48),
        name="inproj_fourier",
    )(hb, w_f, cs_tab)


def _dft_gen_kernel(tac_ref, tas_ref, tbc_ref, tbs_ref, cs_ref, sn_ref):
    tbc = tbc_ref[...]
    tbs = tbs_ref[...]
    for aa in range(tac_ref.shape[0]):
        ca = tac_ref[aa:aa + 1, :]
        sa = tas_ref[aa:aa + 1, :]
        rows = slice(aa * DFT_ROW_SPLIT, (aa + 1) * DFT_ROW_SPLIT)
        cs_ref[rows, :] = (ca * tbc - sa * tbs).astype(BF16)
        sn_ref[rows, :] = (-(sa * tbc + ca * tbs)).astype(BF16)


def _dft_matrices(S, n_chan):
    na = S // 2 // DFT_ROW_SPLIT
    scale = 1.0 / math.sqrt(S * n_chan)
    col = jnp.arange(S // 2, dtype=jnp.int32)[None, :]
    period = S // DFT_ROW_SPLIT
    ang_a = (2.0 * math.pi / period) * ((jnp.arange(na, dtype=jnp.int32)[:, None] * col) % period).astype(F32)
    ang_b = (2.0 * math.pi / S) * ((jnp.arange(DFT_ROW_SPLIT, dtype=jnp.int32)[:, None] * col) % S).astype(F32)
    tac, tas = jnp.cos(ang_a), jnp.sin(ang_a)
    tbc, tbs = scale * jnp.cos(ang_b), scale * jnp.sin(ang_b)
    ta = min(8, na)
    assert na % ta == 0
    tc = _tile(S // 2, 2048)
    return pl.pallas_call(
        _dft_gen_kernel,
        grid=(na // ta, S // 2 // tc),
        in_specs=[pl.BlockSpec((ta, tc), lambda i, j: (i, j)),
                  pl.BlockSpec((ta, tc), lambda i, j: (i, j)),
                  pl.BlockSpec((DFT_ROW_SPLIT, tc), lambda i, j: (0, j)),
                  pl.BlockSpec((DFT_ROW_SPLIT, tc), lambda i, j: (0, j))],
        out_specs=[pl.BlockSpec((ta * DFT_ROW_SPLIT, tc), lambda i, j: (i, j)),
                   pl.BlockSpec((ta * DFT_ROW_SPLIT, tc), lambda i, j: (i, j))],
        out_shape=[jax.ShapeDtypeStruct((S // 2, S // 2), BF16), jax.ShapeDtypeStruct((S // 2, S // 2), BF16)],
        compiler_params=_cparams(("parallel", "parallel"), 40),
        name="dft_matrices",
    )(tac, tas, tbc, tbs)


def _dft_fold_kernel(a_ref, am_ref, an_ref, b_ref, bm_ref, bn_ref, ae_ref, bo_ref, alt_ref):
    tm = a_ref.shape[0]
    r = lax.broadcasted_iota(jnp.int32, (tm, tm), 0)
    c = lax.broadcasted_iota(jnp.int32, (tm, tm), 1)
    rev = jnp.where(r + c == tm, 1.0, 0.0).astype(BF16)
    has_next = (pl.program_id(1) > 0).astype(F32)
    first = (jnp.where(r + c == 0, 1.0, 0.0) * has_next).astype(BF16)

    def mirrored(m_ref, n_ref):
        return (jnp.dot(rev, m_ref[...], preferred_element_type=F32)
                + jnp.dot(first, n_ref[...], preferred_element_type=F32))

    ae_ref[...] = (a_ref[...].astype(F32) + mirrored(am_ref, an_ref)).astype(BF16)
    bo_ref[...] = (b_ref[...].astype(F32) - mirrored(bm_ref, bn_ref)).astype(BF16)

    @pl.when(pl.program_id(1) == 0)
    def _():
        alt_ref[...] = jnp.zeros(alt_ref.shape, F32)

    row = lax.broadcasted_iota(jnp.int32, a_ref.shape, 0)
    both = a_ref[...].astype(F32) + am_ref[...].astype(F32)
    alt_ref[0:1, :] += jnp.sum(jnp.where(row % 2 == 0, both, -both), axis=0, keepdims=True)


def _dft_fold(a, b, B, S):
    fw = a.shape[1]
    tm = _tile(S // 2, 256)
    nb = S // tm
    nh = nb // 2

    def own(bb, i):
        return (bb * nb + i, 0)

    def mirror(bb, i):
        return (bb * nb + nb - 1 - i, 0)

    def mirror_next(bb, i):
        return (bb * nb + jnp.minimum(nb - i, nb - 1), 0)

    spec = lambda f: pl.BlockSpec((tm, fw), f)
    return pl.pallas_call(
        _dft_fold_kernel,
        grid=(B, nh),
        in_specs=[spec(own), spec(mirror), spec(mirror_next), spec(own), spec(mirror), spec(mirror_next)],
        out_specs=[pl.BlockSpec((tm, fw), lambda bb, i: (bb * nh + i, 0)),
                   pl.BlockSpec((tm, fw), lambda bb, i: (bb * nh + i, 0)),
                   pl.BlockSpec((8, fw), lambda bb, i: (bb, 0))],
        out_shape=[jax.ShapeDtypeStruct((B * S // 2, fw), BF16),
                   jax.ShapeDtypeStruct((B * S // 2, fw), BF16),
                   jax.ShapeDtypeStruct((B * 8, fw), F32)],
        compiler_params=_cparams(("parallel", "arbitrary"), 40),
        name="dft_fold",
    )(a, a, a, b, b, b)


def _seq_dft_kernel(cs_ref, sn_ref, ae_ref, bo_ref, mid_ref, y_ref, d_ref, p_s, q_s, *, scale):
    kk = pl.program_id(3)

    @pl.when(kk == 0)
    def _():
        p_s[...] = jnp.zeros(p_s.shape, F32)
        q_s[...] = jnp.zeros(q_s.shape, F32)

    p_s[...] += jnp.dot(cs_ref[...], ae_ref[...], preferred_element_type=F32)
    q_s[...] += jnp.dot(sn_ref[...], bo_ref[...], preferred_element_type=F32)

    @pl.when(kk == pl.num_programs(3) - 1)
    def _():
        row = lax.broadcasted_iota(jnp.int32, p_s.shape, 0)
        pm = p_s[...] + jnp.where(row % 2 == 0, scale, -scale) * mid_ref[0:1, :].astype(F32)
        y_ref[...] = (pm + q_s[...]).astype(y_ref.dtype)
        d_ref[...] = (pm - q_s[...]).astype(d_ref.dtype)


def _seq_dft(cs, sn, ae, bo, a, B, S, scale):
    fw = a.shape[1]
    half = S // 2
    tm = _tile(half, 1024)
    tn = _tile(fw, 1024)
    tk = _tile(half, 1024)
    nm, nk = half // tm, half // tk
    mid_rows = 16
    return pl.pallas_call(
        functools.partial(_seq_dft_kernel, scale=scale),
        grid=(B, nm, fw // tn, nk),
        in_specs=[pl.BlockSpec((tm, tk), lambda bb, i, j, k: (i, k)),
                  pl.BlockSpec((tm, tk), lambda bb, i, j, k: (i, k)),
                  pl.BlockSpec((tk, tn), lambda bb, i, j, k: (bb * nk + k, j)),
                  pl.BlockSpec((tk, tn), lambda bb, i, j, k: (bb * nk + k, j)),
                  pl.BlockSpec((mid_rows, tn),
                               lambda bb, i, j, k: ((bb * S + half) // mid_rows, j))],
        out_specs=[pl.BlockSpec((tm, tn), lambda bb, i, j, k: (bb * nm + i, j)),
                   pl.BlockSpec((tm, tn), lambda bb, i, j, k: (bb * nm + i, j))],
        out_shape=[jax.ShapeDtypeStruct((B * half, fw), BF16),
                   jax.ShapeDtypeStruct((B * half, fw), BF16)],
        scratch_shapes=[pltpu.VMEM((tm, tn), F32), pltpu.VMEM((tm, tn), F32)],
        compiler_params=_cparams(("parallel", "parallel", "parallel", "arbitrary"), 48),
        name="seq_dft",
    )(cs, sn, ae, bo, a)


def _dft_mirror_kernel(yh_ref, dm_ref, dn_ref, alt_ref, y_ref, *, scale, nh):
    j = pl.program_id(1)

    @pl.when(j < nh)
    def _():
        y_ref[...] = yh_ref[...]

    @pl.when(j >= nh)
    def _():
        tm = dm_ref.shape[0]
        r = lax.broadcasted_iota(jnp.int32, (tm, tm), 0)
        c = lax.broadcasted_iota(jnp.int32, (tm, tm), 1)
        rev = jnp.where(r + c == tm, 1.0, 0.0).astype(BF16)
        first = (jnp.where(r + c == 0, 1.0, 0.0) * (j > nh).astype(F32)).astype(BF16)
        out = (jnp.dot(rev, dm_ref[...], preferred_element_type=F32)
               + jnp.dot(first, dn_ref[...], preferred_element_type=F32))
        row = lax.broadcasted_iota(jnp.int32, out.shape, 0)
        nyquist = jnp.where(row == 0, scale * (j == nh).astype(F32), 0.0) * alt_ref[0:1, :]
        y_ref[...] = (out + nyquist).astype(y_ref.dtype)


def _dft_mirror(y_half, d, alt, B, S, scale):
    fw = d.shape[1]
    half = S // 2
    tm = _tile(half, 256)
    nh = half // tm

    def mirror(bb, j):
        return (bb * nh + nh - 1 - jnp.maximum(j - nh, 0), 0)

    def mirror_next(bb, j):
        return (bb * nh + jnp.minimum(nh - jnp.maximum(j - nh, 0), nh - 1), 0)

    return pl.pallas_call(
        functools.partial(_dft_mirror_kernel, scale=scale, nh=nh),
        grid=(B, 2 * nh),
        in_specs=[pl.BlockSpec((tm, fw), lambda bb, j: (bb * nh + jnp.minimum(j, nh - 1), 0)),
                  pl.BlockSpec((tm, fw), mirror),
                  pl.BlockSpec((tm, fw), mirror_next),
                  pl.BlockSpec((8, fw), lambda bb, j: (bb, 0))],
        out_specs=pl.BlockSpec((tm, fw), lambda bb, j: (bb * 2 * nh + j, 0)),
        out_shape=jax.ShapeDtypeStruct((B * S, fw), d.dtype),
        compiler_params=_cparams(("parallel", "arbitrary"), 40),
        name="dft_mirror",
    )(y_half, d, d, alt)


def _wo_kernel(ym_ref, yf_ref, gm_ref, gf_ref, w_ref, x_ref, mu_ref, rs_ref, lg_ref, lb_ref,
               pre_ref, mix_s, *, alpha, mw):
    @pl.when(pl.program_id(1) == 0)
    def _():
        ym = ym_ref[...].astype(F32)
        mix_s[:, :mw] = (ym * lax.rsqrt(jnp.mean(ym * ym, axis=-1, keepdims=True) + RMS_EPS)
                         * gm_ref[...]).astype(BF16)
        yf = yf_ref[...].astype(F32)
        mix_s[:, mw:] = (yf * lax.rsqrt(jnp.mean(yf * yf, axis=-1, keepdims=True) + RMS_EPS)
                         * gf_ref[...]).astype(BF16)

    h = (x_ref[...] - mu_ref[...]) * rs_ref[...] * lg_ref[...] + lb_ref[...]
    pre_ref[...] = alpha * h + jnp.dot(mix_s[...], w_ref[...], preferred_element_type=F32)


def _wo(ym, yf, gm, gf, w_o, x2, mu, rs, ln_g, ln_b, alpha):
    T, mw = ym.shape
    fw = yf.shape[1]
    D = w_o.shape[1]
    tm = _tile(T, 1024)
    tn = _tile(D, 512)
    return pl.pallas_call(
        functools.partial(_wo_kernel, alpha=alpha, mw=mw),
        grid=(T // tm, D // tn),
        in_specs=[pl.BlockSpec((tm, mw), lambda i, j: (i, 0)),
                  pl.BlockSpec((tm, fw), lambda i, j: (i, 0)),
                  pl.BlockSpec((1, mw), lambda i, j: (0, 0)),
                  pl.BlockSpec((1, fw), lambda i, j: (0, 0)),
                  pl.BlockSpec((mw + fw, tn), lambda i, j: (0, j)),
                  pl.BlockSpec((tm, tn), lambda i, j: (i, j)),
                  pl.BlockSpec((tm, 1), lambda i, j: (i, 0)),
                  pl.BlockSpec((tm, 1), lambda i, j: (i, 0)),
                  pl.BlockSpec((1, tn), lambda i, j: (0, j)),
                  pl.BlockSpec((1, tn), lambda i, j: (0, j))],
        out_specs=pl.BlockSpec((tm, tn), lambda i, j: (i, j)),
        out_shape=jax.ShapeDtypeStruct((T, D), F32),
        scratch_shapes=[pltpu.VMEM((tm, mw + fw), BF16)],
        compiler_params=_cparams(("parallel", "arbitrary"), 56),
        name="w_o_residual",
    )(ym, yf, gm.reshape(1, mw), gf.reshape(1, fw), w_o, x2, mu, rs,
      ln_g.reshape(1, D), ln_b.reshape(1, D))


def _pack_bf16_pair(lo, hi):
    lo_bits = pltpu.bitcast(lo.astype(BF16).astype(F32), jnp.uint32)
    hi_bits = pltpu.bitcast(hi.astype(BF16).astype(F32), jnp.uint32)
    return (lo_bits >> 16) | (hi_bits & jnp.uint32(0xFFFF0000))


def _unpack_bf16_pair(w):
    lo = pltpu.bitcast(w << 16, F32).astype(BF16)
    hi = pltpu.bitcast(w & jnp.uint32(0xFFFF0000), F32).astype(BF16)
    return lo, hi


def _ln1_router_kernel(pre_ref, g_ref, b_ref, rw_ref, rb_ref, x1_ref, x1p_ref, idx_ref, gate_ref):
    x = pre_ref[...]
    mu = jnp.mean(x, axis=-1, keepdims=True)
    xc = x - mu
    var = jnp.mean(xc * xc, axis=-1, keepdims=True)
    x1 = xc * lax.rsqrt(var + LN_EPS) * g_ref[...] + b_ref[...]
    x1_ref[...] = x1
    half = x1.shape[1] // 2
    x1p_ref[...] = _pack_bf16_pair(x1[:, :half], x1[:, half:])
    logits = lax.dot_general(rw_ref[...], x1, (((1,), (1,)), ((), ())),
                             precision=lax.Precision.HIGHEST, preferred_element_type=F32) + rb_ref[...]
    n_e = logits.shape[0]
    eidx = lax.broadcasted_iota(jnp.int32, logits.shape, 0)
    vals, idxs = [], []
    for _ in range(TOP_K):
        m = jnp.max(logits, axis=0, keepdims=True)
        sel = jnp.min(jnp.where(logits == m, eidx, n_e), axis=0, keepdims=True)
        logits = jnp.where(eidx == sel, -jnp.inf, logits)
        vals.append(m)
        idxs.append(sel)
    exps = [jnp.exp(v - vals[0]) for v in vals]
    denom = exps[0] + exps[1] + exps[2] + exps[3]
    for kk in range(TOP_K):
        idx_ref[kk:kk + 1, :] = idxs[kk]
        gate_ref[kk:kk + 1, :] = exps[kk] / denom


def _ln1_router(pre, g, b, router_w, router_b):
    T, D = pre.shape
    E = router_w.shape[1]
    tm = _tile(T, 256)
    return pl.pallas_call(
        _ln1_router_kernel,
        grid=(T // tm,),
        in_specs=[pl.BlockSpec((tm, D), lambda i: (i, 0)),
                  pl.BlockSpec((1, D), lambda i: (0, 0)),
                  pl.BlockSpec((1, D), lambda i: (0, 0)),
                  pl.BlockSpec((E, D), lambda i: (0, 0)),
                  pl.BlockSpec((E, 1), lambda i: (0, 0))],
        out_specs=[pl.BlockSpec((tm, D), lambda i: (i, 0)),
                   pl.BlockSpec((tm, D // 2), lambda i: (i, 0)),
                   pl.BlockSpec((TOP_K, tm), lambda i: (0, i)),
                   pl.BlockSpec((TOP_K, tm), lambda i: (0, i))],
        out_shape=[jax.ShapeDtypeStruct((T, D), F32),
                   jax.ShapeDtypeStruct((T, D // 2), jnp.uint32),
                   jax.ShapeDtypeStruct((TOP_K, T), jnp.int32),
                   jax.ShapeDtypeStruct((TOP_K, T), F32)],
        compiler_params=_cparams(("parallel",), 48),
        name="ln1_router",
    )(pre, g.reshape(1, D), b.reshape(1, D), router_w.T, router_b.reshape(E, 1))


def _row_copy(src_hbm, dst_ref, src_row, dst_row, sem):
    return pltpu.make_async_copy(src_hbm.at[pl.ds(src_row, 1)], dst_ref.at[pl.ds(dst_row, 1)], sem)


def _dispatch_kernel(dest_ref, pad_ref, npad_ref, nv_ref, x_ref, o_hbm, zero_s, sem, zsem, *, n_steps):
    b = pl.program_id(0)
    tt = x_ref.shape[0]

    def wait_rows(n):
        pltpu.make_async_copy(o_hbm.at[pl.ds(0, n)], o_hbm.at[pl.ds(0, n)], sem).wait()

    @pl.when(b < n_steps)
    def _():
        def issue(r, c):
            for kk in range(TOP_K):
                _row_copy(x_ref, o_hbm, r, dest_ref[(b * tt + r) * TOP_K + kk], sem).start(priority=kk % 2)
            return c
        lax.fori_loop(0, tt, issue, 0, unroll=2)
        wait_rows(tt * TOP_K)

    @pl.when(b == n_steps)
    def _():
        n_pad = npad_ref[0]

        def issue_pad(i, c):
            _row_copy(x_ref, o_hbm, 0, pad_ref[i], sem).start()
            return c
        lax.fori_loop(0, n_pad, issue_pad, 0)

        def wait_pad(i, c):
            wait_rows(1)
            return c
        lax.fori_loop(0, n_pad, wait_pad, 0)

        zero_s[...] = jnp.zeros(zero_s.shape, zero_s.dtype)
        n_sub = MOE_CHUNK // MOE_SUB

        def zero_copy(i):
            return pltpu.make_async_copy(zero_s, o_hbm.at[pl.ds(pl.multiple_of(i * MOE_SUB, MOE_SUB), MOE_SUB)], zsem)

        def empty(i):
            return (i % n_sub) * MOE_SUB >= nv_ref[i // n_sub]

        def issue_zero(i, c):
            @pl.when(empty(i))
            def _():
                zero_copy(i).start()
            return c
        lax.fori_loop(0, nv_ref.shape[0] * n_sub, issue_zero, 0)

        def wait_zero(i, c):
            @pl.when(empty(i))
            def _():
                zero_copy(i).wait()
            return c
        lax.fori_loop(0, nv_ref.shape[0] * n_sub, wait_zero, 0)


def _dispatch(dest, pad_slots, n_pad, chunk_nv, x1p):
    T, W = x1p.shape
    tt = _tile(T, DISPATCH_TOKENS)
    n_steps = T // tt
    grid_spec = pltpu.PrefetchScalarGridSpec(
        num_scalar_prefetch=4,
        grid=(n_steps + 1,),
        in_specs=[pl.BlockSpec((tt, W), lambda b, d, p, n, v: (jnp.minimum(b, n_steps - 1), 0))],
        out_specs=pl.BlockSpec(memory_space=pl.ANY),
        scratch_shapes=[pltpu.VMEM((MOE_SUB, W), x1p.dtype),
                        pltpu.SemaphoreType.DMA(()), pltpu.SemaphoreType.DMA(())],
    )
    return pl.pallas_call(
        functools.partial(_dispatch_kernel, n_steps=n_steps),
        grid_spec=grid_spec,
        out_shape=jax.ShapeDtypeStruct((chunk_nv.shape[0] * MOE_CHUNK, W), x1p.dtype),
        compiler_params=_cparams(("arbitrary",), 32),
        name="moe_dispatch",
    )(dest, pad_slots, n_pad, chunk_nv, x1p)


def _expert_up_kernel(ce_ref, nv_ref, nu_ref, x_ref, wg_ref, wu_ref, bg_ref, bu_ref, h_ref):
    c = pl.program_id(0)
    nv = nv_ref[c]
    brow = ce_ref[c] * pl.num_programs(1) + pl.program_id(1)

    n_sub = MOE_CHUNK // MOE_SUB
    n_act = (nv + MOE_SUB - 1) // MOE_SUB
    for k in range(n_sub + 1):
        @pl.when(n_act == k)
        def _(k=k):
            if k > 0:
                x_lo, x_hi = _unpack_bf16_pair(x_ref[:k * MOE_SUB, :])
                half = x_lo.shape[1]

                def proj(w_ref):
                    return (jnp.dot(x_lo, w_ref[:half, :].astype(BF16), preferred_element_type=F32)
                            + jnp.dot(x_hi, w_ref[half:, :].astype(BF16), preferred_element_type=F32))
                hg = jnp.minimum(proj(wg_ref) + bg_ref[pl.ds(brow, 1), :], SWIGLU_LIMIT)
                hu = jnp.clip(proj(wu_ref) + bu_ref[pl.ds(brow, 1), :], -SWIGLU_LIMIT, SWIGLU_LIMIT)
                act = (hu + 1.0) * (hg * jax.nn.sigmoid(SWIGLU_ALPHA * hg))
                h_ref[:k * MOE_SUB, :] = act.astype(h_ref.dtype)
            if k < n_sub:
                h_ref[k * MOE_SUB:, :] = jnp.zeros(((n_sub - k) * MOE_SUB, h_ref.shape[1]), h_ref.dtype)


def _expert_up(chunk_e, chunk_nv, n_used, xs, w_gate, w_up, b_gate, b_up, n_chunks):
    E, D, F = w_gate.shape
    tf = _tile(F, 256)
    nf = F // tf

    def used(c, nu):
        return jnp.minimum(c, nu[0] - 1)

    def jeff(c, j, nu):
        return jnp.where(c < nu[0], j, nf - 1)

    grid_spec = pltpu.PrefetchScalarGridSpec(
        num_scalar_prefetch=3,
        grid=(n_chunks, nf),
        in_specs=[pl.BlockSpec((MOE_CHUNK, D // 2), lambda c, j, ce, nv, nu: (used(c, nu), 0)),
                  pl.BlockSpec((None, D, tf), lambda c, j, ce, nv, nu: (ce[c], 0, jeff(c, j, nu))),
                  pl.BlockSpec((None, D, tf), lambda c, j, ce, nv, nu: (ce[c], 0, jeff(c, j, nu))),
                  pl.BlockSpec((E * nf, tf), lambda c, j, ce, nv, nu: (0, 0)),
                  pl.BlockSpec((E * nf, tf), lambda c, j, ce, nv, nu: (0, 0))],
        out_specs=pl.BlockSpec((MOE_CHUNK, tf), lambda c, j, ce, nv, nu: (c, j)),
    )
    return pl.pallas_call(
        _expert_up_kernel,
        grid_spec=grid_spec,
        out_shape=jax.ShapeDtypeStruct((n_chunks * MOE_CHUNK, F), BF16),
        compiler_params=_cparams(("arbitrary", "arbitrary"), 60),
        name="expert_gate_up",
    )(chunk_e, chunk_nv, n_used, xs, w_gate, w_up, b_gate.reshape(E * nf, tf), b_up.reshape(E * nf, tf))


def _expert_down_kernel(ce_ref, nv_ref, nu_ref, h_ref, wd_ref, bd_ref, y_ref):
    c = pl.program_id(0)
    nv = nv_ref[c]
    brow = ce_ref[c] * pl.num_programs(1) + pl.program_id(1)

    n_sub = MOE_CHUNK // MOE_SUB
    n_act = (nv + MOE_SUB - 1) // MOE_SUB
    for k in range(n_sub + 1):
        @pl.when(n_act == k)
        def _(k=k):
            if k > 0:
                out = jnp.dot(h_ref[:k * MOE_SUB, :], wd_ref[...].astype(BF16),
                              preferred_element_type=F32) + bd_ref[pl.ds(brow, 1), :]
                hw = out.shape[1] // 2
                y_ref[:k * MOE_SUB, :] = _pack_bf16_pair(out[:, :hw], out[:, hw:])
            if k < n_sub:
                y_ref[k * MOE_SUB:, :] = jnp.zeros(((n_sub - k) * MOE_SUB, y_ref.shape[1]), y_ref.dtype)


def _expert_down(chunk_e, chunk_nv, n_used, hmid, w_down, b_down, n_chunks):
    E, F, D = w_down.shape
    tn = _tile(D, MOE_DOWN_TN)
    nn = D // tn

    def used(c, nu):
        return jnp.minimum(c, nu[0] - 1)

    def jeff(c, j, nu):
        return jnp.where(c < nu[0], j, nn - 1)

    grid_spec = pltpu.PrefetchScalarGridSpec(
        num_scalar_prefetch=3,
        grid=(n_chunks, nn),
        in_specs=[pl.BlockSpec((MOE_CHUNK, F), lambda c, j, ce, nv, nu: (used(c, nu), 0)),
                  pl.BlockSpec((None, F, tn), lambda c, j, ce, nv, nu: (ce[c], 0, jeff(c, j, nu))),
                  pl.BlockSpec((E * nn, tn), lambda c, j, ce, nv, nu: (0, 0))],
        out_specs=pl.BlockSpec((MOE_CHUNK, tn // 2), lambda c, j, ce, nv, nu: (c, j)),
    )
    return pl.pallas_call(
        _expert_down_kernel,
        grid_spec=grid_spec,
        out_shape=jax.ShapeDtypeStruct((n_chunks * MOE_CHUNK, D // 2), jnp.uint32),
        compiler_params=_cparams(("arbitrary", "arbitrary"), 56),
        name="expert_down",
    )(chunk_e, chunk_nv, n_used, hmid, w_down, b_down.reshape(E * nn, tn))


def _combine_kernel(dest_ref, y_hbm, x1_ref, gate_ref, g_ref, b_ref, o_ref, *scratch, alpha, tn):
    i = pl.program_id(0)
    n = pl.num_programs(0)
    nb = COMBINE_BUFS
    tt = o_ref.shape[0] // nb
    bufs, sem = scratch[:nb], scratch[nb]

    def issue_tile(tile, slot):
        base = jnp.minimum(tile, nb * n - 1) * tt
        for r in range(tt):
            for kk in range(TOP_K):
                _row_copy(y_hbm, bufs[slot].at[kk], dest_ref[(base + r) * TOP_K + kk], r,
                          sem.at[slot]).start(priority=kk % 2)

    def wait_tile(slot):
        for kk in range(TOP_K):
            pltpu.make_async_copy(y_hbm.at[pl.ds(0, tt)], bufs[slot].at[kk], sem.at[slot]).wait()

    def finish_tile(slot):
        rows = slice(slot * tt, (slot + 1) * tt)
        gates = gate_ref[rows, :]
        y_lo = y_hi = None
        for kk in range(TOP_K):
            w = bufs[slot][kk]
            g_k = gates[:, kk:kk + 1]
            lo = g_k * pltpu.bitcast(w << 16, F32)
            hi = g_k * pltpu.bitcast(w & jnp.uint32(0xFFFF0000), F32)
            y_lo = lo if y_lo is None else y_lo + lo
            y_hi = hi if y_hi is None else y_hi + hi
        hw = tn // 2
        pieces = []
        for j in range(x1_ref.shape[1] // tn):
            pieces.append((slice(j * tn, j * tn + hw), y_lo[:, j * hw:(j + 1) * hw]))
            pieces.append((slice(j * tn + hw, (j + 1) * tn), y_hi[:, j * hw:(j + 1) * hw]))
        zs = [alpha * x1_ref[rows, cols] + y for cols, y in pieces]
        d = x1_ref.shape[1]
        mu = sum(jnp.sum(z, axis=-1, keepdims=True) for z in zs) / d
        zcs = [z - mu for z in zs]
        var = sum(jnp.sum(zc * zc, axis=-1, keepdims=True) for zc in zcs) / d
        rs = lax.rsqrt(var + LN_EPS)
        for (cols, _), zc in zip(pieces, zcs):
            o_ref[rows, cols] = zc * rs * g_ref[:, cols] + b_ref[:, cols]

    @pl.when(i == 0)
    def _():
        for t in range(COMBINE_AHEAD):
            issue_tile(t, t)

    for t in range(nb):
        wait_tile(t)
        issue_tile(nb * i + t + COMBINE_AHEAD, (t + COMBINE_AHEAD) % nb)
        finish_tile(t)

    @pl.when(i == n - 1)
    def _():
        for t in range(COMBINE_AHEAD):
            wait_tile(t)


def _combine(dest, yslots, x1, gates_tk, g, b, alpha):
    T, D = x1.shape
    nb = COMBINE_BUFS
    tt = _tile(T // nb, COMBINE_TOKENS)
    grid_spec = pltpu.PrefetchScalarGridSpec(
        num_scalar_prefetch=1,
        grid=(T // (nb * tt),),
        in_specs=[pl.BlockSpec(memory_space=pl.ANY),
                  pl.BlockSpec((nb * tt, D), lambda i, d: (i, 0)),
                  pl.BlockSpec((nb * tt, TOP_K), lambda i, d: (i, 0)),
                  pl.BlockSpec((1, D), lambda i, d: (0, 0)),
                  pl.BlockSpec((1, D), lambda i, d: (0, 0))],
        out_specs=pl.BlockSpec((nb * tt, D), lambda i, d: (i, 0)),
        scratch_shapes=[pltpu.VMEM((TOP_K, tt, D // 2), jnp.uint32) for _ in range(nb)]
        + [pltpu.SemaphoreType.DMA((nb,))],
    )
    return pl.pallas_call(
        functools.partial(_combine_kernel, alpha=alpha, tn=_tile(D, MOE_DOWN_TN)),
        grid_spec=grid_spec,
        out_shape=jax.ShapeDtypeStruct((T, D), F32),
        compiler_params=_cparams(("arbitrary",), 48),
        name="moe_combine_ln2",
    )(dest, yslots, x1, gates_tk, g.reshape(1, D), b.reshape(1, D))


def _routing_tables(top_idx, n_experts, n_chunks):
    T = top_idx.shape[1]
    M = T * TOP_K
    flat_e = top_idx.T.reshape(M)
    onehot = (flat_e[:, None] == jnp.arange(n_experts, dtype=jnp.int32)[None, :]).astype(jnp.int32)
    csum = jnp.cumsum(onehot, axis=0)
    rank = jnp.sum(csum * onehot, axis=1) - 1
    counts = csum[-1]
    chunks_e = (counts + MOE_CHUNK - 1) // MOE_CHUNK
    chunk_end = jnp.cumsum(chunks_e)
    chunk_start = chunk_end - chunks_e
    n_used = chunk_end[-1]
    dest = chunk_start[flat_e] * MOE_CHUNK + rank
    P = n_chunks * MOE_CHUNK
    pad_idx = counts[:, None] + jnp.arange(MOE_SUB, dtype=jnp.int32)[None, :]
    padded = (counts + MOE_SUB - 1) // MOE_SUB * MOE_SUB
    pad_slots = jnp.where(pad_idx < padded[:, None], chunk_start[:, None] * MOE_CHUNK + pad_idx, P)
    pad_slots = jnp.sort(pad_slots.reshape(-1)).astype(jnp.int32)
    n_pad = jnp.sum(padded - counts).astype(jnp.int32).reshape(1)
    cid = jnp.arange(n_chunks, dtype=jnp.int32)
    chunk_e = jnp.minimum(jnp.searchsorted(chunk_end, cid, side='right'), n_experts - 1).astype(jnp.int32)
    last_e = chunk_e[jnp.maximum(n_used - 1, 0)]
    chunk_e = jnp.where(cid < n_used, chunk_e, last_e)
    chunk_nv = jnp.where(cid < n_used,
                         jnp.clip(counts[chunk_e] - (cid - chunk_start[chunk_e]) * MOE_CHUNK, 0, MOE_CHUNK),
                         0).astype(jnp.int32)
    return (dest.astype(jnp.int32), pad_slots, n_pad, chunk_e, chunk_nv,
            n_used.astype(jnp.int32).reshape(1))


def kernel(x, positions, ln_in_g, ln_in_b, w_in, q_a_norm_g, w_q_b, kv_a_norm_g, w_kv_b, mla_out_norm_g, fourier_out_norm_g, w_o, ln1_g, ln1_b, router_w, router_b, w_gate, b_gate, w_up, b_up, w_down, b_down, ln2_g, ln2_b):
    B, S, D = x.shape
    T = B * S
    depth = w_in.shape[0]
    assert depth == 1, "single-layer trunk only"
    qr = q_a_norm_g.shape[1]
    kvr = kv_a_norm_g.shape[1]
    H = w_q_b.shape[2] // (QK_NOPE_DIM + QK_ROPE_DIM)
    fw = fourier_out_norm_g.shape[1]
    E = router_w.shape[2]
    assert H % HEADS_PER_TILE == 0 and (qr + kvr) % LANES == 0 and S % (2 * DFT_ROW_SPLIT) == 0
    alpha = (2.0 * depth) ** 0.25

    inv_freq = ROPE_THETA ** (-jnp.arange(0, QK_ROPE_DIM, 2, dtype=F32) / QK_ROPE_DIM)
    ang = positions.astype(F32)[..., None] * inv_freq
    cos4 = jnp.tile(jnp.cos(ang), (1, 1, 2 * LANES // QK_ROPE_DIM)).reshape(T, LANES)
    sin4 = jnp.tile(jnp.sin(ang), (1, 1, 2 * LANES // QK_ROPE_DIM)).reshape(T, LANES)

    rope_end = qr + kvr + QK_ROPE_DIM
    w_a = w_in[0, :, :qr + kvr + LANES].astype(BF16)
    w_f = w_in[0, :, rope_end:].astype(BF16)
    wq = w_q_b[0].reshape(qr, H // HEADS_PER_TILE, HEADS_PER_TILE, QK_NOPE_DIM + QK_ROPE_DIM)
    wq_perm = jnp.concatenate(
        [wq[..., :QK_NOPE_DIM].reshape(qr, H // HEADS_PER_TILE, HEADS_PER_TILE * QK_NOPE_DIM),
         wq[..., QK_NOPE_DIM:].reshape(qr, H // HEADS_PER_TILE, HEADS_PER_TILE * QK_ROPE_DIM)],
        axis=-1).reshape(qr, H * (QK_NOPE_DIM + QK_ROPE_DIM)).astype(BF16)
    w_kv = w_kv_b[0].astype(BF16)
    w_o_b = w_o[0].astype(BF16)
    ch = jnp.arange(FOURIER_GROUP_DIM, dtype=jnp.int32)
    ang_c = (2.0 * math.pi / FOURIER_GROUP_DIM) * ((ch[:, None] * ch[None, :]) % FOURIER_GROUP_DIM).astype(F32)
    cs_tab = jnp.concatenate([jnp.cos(ang_c), jnp.sin(ang_c)], axis=1).astype(BF16)

    x2 = x.reshape(T, D)
    hb, mu, rs = _ln_in(x2, ln_in_g, ln_in_b)

    cq, ckv, kpe2 = _inproj_a(hb, w_a, q_a_norm_g[0], kv_a_norm_g[0], cos4, sin4)
    qscale = (QK_NOPE_DIM + QK_ROPE_DIM) ** -0.5 * LOG2E
    q = _q_up(cq, wq_perm, cos4, sin4, qscale)
    k, v = _kv_up(ckv, w_kv, kpe2)
    y_mla = _attention(q, k, v, B, S, H)

    fa, fb = _inproj_f(hb, w_f, cs_tab)
    cs_mat, sn_mat = _dft_matrices(S, FOURIER_GROUP_DIM)
    fae, fbo, falt = _dft_fold(fa, fb, B, S)
    dft_scale = 1.0 / math.sqrt(S * FOURIER_GROUP_DIM)
    y_half, y_diff = _seq_dft(cs_mat, sn_mat, fae, fbo, fa, B, S, dft_scale)
    y_f = _dft_mirror(y_half, y_diff, falt, B, S, dft_scale)

    pre = _wo(y_mla, y_f, mla_out_norm_g[0], fourier_out_norm_g[0], w_o_b, x2, mu, rs,
              ln_in_g, ln_in_b, alpha)
    x1, x1p, top_idx, gates = _ln1_router(pre, ln1_g[0], ln1_b[0], router_w[0], router_b[0])

    n_chunks = -(-T * TOP_K // MOE_CHUNK) + E
    dest, pad_slots, n_pad, chunk_e, chunk_nv, n_used = _routing_tables(top_idx, E, n_chunks)
    xs = _dispatch(dest, pad_slots, n_pad, chunk_nv, x1p)
    hmid = _expert_up(chunk_e, chunk_nv, n_used, xs, w_gate[0], w_up[0], b_gate[0], b_up[0], n_chunks)
    yslots = _expert_down(chunk_e, chunk_nv, n_used, hmid, w_down[0], b_down[0], n_chunks)
    out = _combine(dest, yslots, x1, gates.T, ln2_g[0], ln2_b[0], alpha)
    return out.reshape(B, S, D)
```

```python
import functools
import math

import jax
import jax.numpy as jnp
from jax import lax
from jax.experimental import pallas as pl
from jax.experimental.pallas import tpu as pltpu

F32 = jnp.float32
BF16 = jnp.bfloat16

V_HEAD_DIM = 128
QK_NOPE_DIM = 128
QK_ROPE_DIM = 64
QK_PAD_DIM = 256
V_PAD_DIM = 256
ROPE_THETA = 10000.0
FOURIER_GROUP_DIM = 128
TOP_K = 4
SWIGLU_LIMIT = 7.0
SWIGLU_ALPHA = 1.702
LN_EPS = 1e-5
RMS_EPS = 1e-6
LOG2E = 1.4426950408889634

LANES = 128
V7X_VMEM_BYTES = 64 * 1024 * 1024
HEADS_PER_TILE = 4
DFT_ROW_SPLIT = 64
ATTN_TQ = 1024
ATTN_TKV = 2048

MOE_CHUNK = 1280
MOE_SUB = 128
DISPATCH_TOKENS = 512
COMBINE_TOKENS = 64
COMBINE_BUFS = 4
COMBINE_AHEAD = 2
MOE_DOWN_TN = 1024


def _cparams(semantics, vmem_mb):
    return pltpu.CompilerParams(dimension_semantics=semantics,
                                vmem_limit_bytes=min(vmem_mb * 1024 * 1024, V7X_VMEM_BYTES - (4 << 20)))


def _tile(dim, pref):
    t = min(dim, pref)
    while dim % t:
        t //= 2
    return t


def _ln_in_kernel(x_ref, g_ref, b_ref, hb_ref, mu_ref, rs_ref):
    x = x_ref[...]
    mu = jnp.mean(x, axis=-1, keepdims=True)
    xc = x - mu
    var = jnp.mean(xc * xc, axis=-1, keepdims=True)
    rs = lax.rsqrt(var + LN_EPS)
    hb_ref[...] = (xc * rs * g_ref[...] + b_ref[...]).astype(BF16)
    mu_ref[...] = mu
    rs_ref[...] = rs


def _ln_in(x2, g, b):
    T, D = x2.shape
    tm = _tile(T, 256)
    return pl.pallas_call(
        _ln_in_kernel,
        grid=(T // tm,),
        in_specs=[pl.BlockSpec((tm, D), lambda i: (i, 0)),
                  pl.BlockSpec((1, D), lambda i: (0, 0)),
                  pl.BlockSpec((1, D), lambda i: (0, 0))],
        out_specs=[pl.BlockSpec((tm, D), lambda i: (i, 0)),
                   pl.BlockSpec((tm, 1), lambda i: (i, 0)),
                   pl.BlockSpec((tm, 1), lambda i: (i, 0))],
        out_shape=[jax.ShapeDtypeStruct((T, D), BF16),
                   jax.ShapeDtypeStruct((T, 1), F32),
                   jax.ShapeDtypeStruct((T, 1), F32)],
        compiler_params=_cparams(("parallel",), 40),
        name="ln_in",
    )(x2, g.reshape(1, D), b.reshape(1, D))


def _rope128(p, cos4, sin4):
    lane = lax.broadcasted_iota(jnp.int32, p.shape, 1)
    first_half = (lane % QK_ROPE_DIM) < (QK_ROPE_DIM // 2)
    rot = jnp.where(first_half, -pltpu.roll(p, LANES - QK_ROPE_DIM // 2, 1),
                    pltpu.roll(p, QK_ROPE_DIM // 2, 1))
    return p * cos4 + rot * sin4


def _inproj_a_kernel(h_ref, w_ref, gq_ref, gkv_ref, cos_ref, sin_ref,
                     cq_ref, ckv_ref, kpe_ref, *, qr, kvr):
    acc = jnp.dot(h_ref[...], w_ref[...], preferred_element_type=F32)
    cq = acc[:, :qr]
    cq_ref[...] = (cq * lax.rsqrt(jnp.mean(cq * cq, axis=-1, keepdims=True) + RMS_EPS)
                   * gq_ref[...]).astype(BF16)
    ckv = acc[:, qr:qr + kvr]
    ckv_ref[...] = (ckv * lax.rsqrt(jnp.mean(ckv * ckv, axis=-1, keepdims=True) + RMS_EPS)
                    * gkv_ref[...]).astype(BF16)
    roped = _rope128(acc[:, qr + kvr:qr + kvr + LANES], cos_ref[...], sin_ref[...])
    lane = lax.broadcasted_iota(jnp.int32, roped.shape, 1)
    even = jnp.where(lane < QK_ROPE_DIM, roped, 0.0)
    kpe_ref[:, :LANES] = even.astype(BF16)
    kpe_ref[:, LANES:] = pltpu.roll(even, QK_ROPE_DIM, 1).astype(BF16)


def _inproj_a(hb, w_a, gq, gkv, cos4, sin4):
    T, D = hb.shape
    qr, kvr = gq.shape[0], gkv.shape[0]
    wa = w_a.shape[1]
    tm = _tile(T, 512)
    return pl.pallas_call(
        functools.partial(_inproj_a_kernel, qr=qr, kvr=kvr),
        grid=(T // tm,),
        in_specs=[pl.BlockSpec((tm, D), lambda i: (i, 0)),
                  pl.BlockSpec((D, wa), lambda i: (0, 0)),
                  pl.BlockSpec((1, qr), lambda i: (0, 0)),
                  pl.BlockSpec((1, kvr), lambda i: (0, 0)),
                  pl.BlockSpec((tm, LANES), lambda i: (i, 0)),
                  pl.BlockSpec((tm, LANES), lambda i: (i, 0))],
        out_specs=[pl.BlockSpec((tm, qr), lambda i: (i, 0)),
                   pl.BlockSpec((tm, kvr), lambda i: (i, 0)),
                   pl.BlockSpec((tm, 2 * LANES), lambda i: (i, 0))],
        out_shape=[jax.ShapeDtypeStruct((T, qr), BF16),
                   jax.ShapeDtypeStruct((T, kvr), BF16),
                   jax.ShapeDtypeStruct((T, 2 * LANES), BF16)],
        compiler_params=_cparams(("parallel",), 56),
        name="inproj_mla",
    )(hb, w_a, gq.reshape(1, qr), gkv.reshape(1, kvr), cos4, sin4)


def _q_up_kernel(c_ref, w_ref, cos_ref, sin_ref, q_ref, *, qscale):
    acc = jnp.dot(c_ref[...], w_ref[...], preferred_element_type=F32)
    nope_w = HEADS_PER_TILE * QK_NOPE_DIM
    lane = lax.broadcasted_iota(jnp.int32, (acc.shape[0], LANES), 1)
    for pair in range(HEADS_PER_TILE // 2):
        roped = _rope128(acc[:, nope_w + pair * LANES:nope_w + (pair + 1) * LANES],
                         cos_ref[...], sin_ref[...]) * qscale
        for par in range(2):
            j = 2 * pair + par
            keep = (lane < QK_ROPE_DIM) if par == 0 else (lane >= QK_ROPE_DIM)
            base = j * QK_PAD_DIM
            q_ref[:, base:base + QK_NOPE_DIM] = (
                acc[:, j * QK_NOPE_DIM:(j + 1) * QK_NOPE_DIM] * qscale).astype(BF16)
            q_ref[:, base + QK_NOPE_DIM:base + QK_PAD_DIM] = jnp.where(keep, roped, 0.0).astype(BF16)


def _q_up(cq, wq_perm, cos4, sin4, qscale):
    T, qr = cq.shape
    n_tiles = wq_perm.shape[1] // (HEADS_PER_TILE * (QK_NOPE_DIM + QK_ROPE_DIM))
    tw = HEADS_PER_TILE * (QK_NOPE_DIM + QK_ROPE_DIM)
    to = HEADS_PER_TILE * QK_PAD_DIM
    tm = _tile(T, 1024)
    return pl.pallas_call(
        functools.partial(_q_up_kernel, qscale=qscale),
        grid=(T // tm, n_tiles),
        in_specs=[pl.BlockSpec((tm, qr), lambda i, j: (i, 0)),
                  pl.BlockSpec((qr, tw), lambda i, j: (0, j)),
                  pl.BlockSpec((tm, LANES), lambda i, j: (i, 0)),
                  pl.BlockSpec((tm, LANES), lambda i, j: (i, 0))],
        out_specs=pl.BlockSpec((tm, to), lambda i, j: (i, j)),
        out_shape=jax.ShapeDtypeStruct((T, n_tiles * to), BF16),
        compiler_params=_cparams(("parallel", "arbitrary"), 40),
        name="q_up",
    )(cq, wq_perm, cos4, sin4)


def _kv_up_kernel(c_ref, w_ref, kpe_ref, k_ref, v_ref):
    acc = jnp.dot(c_ref[...], w_ref[...], preferred_element_type=F32)
    for j in range(HEADS_PER_TILE):
        src = j * (QK_NOPE_DIM + V_HEAD_DIM)
        k_ref[:, j * QK_PAD_DIM:j * QK_PAD_DIM + QK_NOPE_DIM] = acc[:, src:src + QK_NOPE_DIM].astype(BF16)
        par = j % 2
        k_ref[:, j * QK_PAD_DIM + QK_NOPE_DIM:(j + 1) * QK_PAD_DIM] = kpe_ref[:, par * LANES:(par + 1) * LANES]
        v_ref[:, j * V_PAD_DIM:j * V_PAD_DIM + V_HEAD_DIM] = (
            acc[:, src + QK_NOPE_DIM:src + QK_NOPE_DIM + V_HEAD_DIM].astype(BF16))
        v_ref[:, j * V_PAD_DIM + V_HEAD_DIM:(j + 1) * V_PAD_DIM] = jnp.ones(
            (acc.shape[0], V_PAD_DIM - V_HEAD_DIM), BF16)


def _kv_up(ckv, w_kv, kpe2):
    T, kvr = ckv.shape
    tw = HEADS_PER_TILE * (QK_NOPE_DIM + V_HEAD_DIM)
    n_tiles = w_kv.shape[1] // tw
    tm = _tile(T, 1024)
    return pl.pallas_call(
        _kv_up_kernel,
        grid=(T // tm, n_tiles),
        in_specs=[pl.BlockSpec((tm, kvr), lambda i, j: (i, 0)),
                  pl.BlockSpec((kvr, tw), lambda i, j: (0, j)),
                  pl.BlockSpec((tm, 2 * LANES), lambda i, j: (i, 0))],
        out_specs=[pl.BlockSpec((tm, HEADS_PER_TILE * QK_PAD_DIM), lambda i, j: (i, j)),
                   pl.BlockSpec((tm, HEADS_PER_TILE * V_PAD_DIM), lambda i, j: (i, j))],
        out_shape=[jax.ShapeDtypeStruct((T, n_tiles * HEADS_PER_TILE * QK_PAD_DIM), BF16),
                   jax.ShapeDtypeStruct((T, n_tiles * HEADS_PER_TILE * V_PAD_DIM), BF16)],
        compiler_params=_cparams(("parallel", "arbitrary"), 40),
        name="kv_up",
    )(ckv, w_kv, kpe2)


def _lane_tile(x, reps):
    return jnp.concatenate([x] * reps, axis=1)


def _attn_kernel(q_ref, k_ref, v_ref, o_ref, m_s, acc_s, s0_s, s1_s, x0_s, x1_s, p0_s, p1_s, a0_s, a1_s,
                 *, tkv):
    n_kv = k_ref.shape[0] // tkv
    s_buf, x_buf, p_buf, a_buf = (s0_s, s1_s), (x0_s, x1_s), (p0_s, p1_s), (a0_s, a1_s)
    m_s[...] = jnp.full(m_s.shape, -jnp.inf, F32)
    acc_s[...] = jnp.zeros(acc_s.shape, F32)

    def scores(i, slot):
        off = pl.multiple_of(i * tkv, tkv)
        s = lax.dot_general(q_ref[...], k_ref[pl.ds(off, tkv), :], (((1,), (1,)), ((), ())),
                            preferred_element_type=F32)
        s_buf[slot][...] = s
        x_buf[slot][...] = jnp.broadcast_to(jnp.max(s, axis=1, keepdims=True), x_buf[slot].shape)

    def probs(slot):
        m_prev = m_s[...]
        m_new = jnp.maximum(m_prev, x_buf[slot][...])
        m_s[...] = m_new
        a_buf[slot][...] = jnp.exp2(m_prev - m_new)
        p_buf[slot][...] = jnp.exp2(s_buf[slot][...] - _lane_tile(m_new, tkv // LANES)).astype(BF16)

    def values(i, slot):
        off = pl.multiple_of(i * tkv, tkv)
        acc_s[...] = (_lane_tile(a_buf[slot][...], V_PAD_DIM // LANES) * acc_s[...]
                      + jnp.dot(p_buf[slot][...], v_ref[pl.ds(off, tkv), :], preferred_element_type=F32))

    def trip(i, slot):
        scores(i + 1, 1 - slot)
        values(i - 1, 1 - slot)
        probs(slot)

    scores(0, 0)
    scores(1, 1)
    probs(0)
    for i in range(1, n_kv - 1):
        trip(i, i % 2)
    values(n_kv - 2, 0)
    probs(1)
    values(n_kv - 1, 1)
    acc = acc_s[...]
    o_ref[...] = (acc[:, :V_HEAD_DIM] / acc[:, V_HEAD_DIM:]).astype(o_ref.dtype)


def _attention(q, k, v, B, S, H):
    T = B * S
    tq = _tile(S, ATTN_TQ)
    tkv = _tile(S, min(ATTN_TKV, S // 2))
    assert (S // tkv) % 2 == 0, "the key-chunk pipeline is unrolled in pairs"
    nq = S // tq
    return pl.pallas_call(
        functools.partial(_attn_kernel, tkv=tkv),
        grid=(B, H, nq),
        in_specs=[pl.BlockSpec((tq, QK_PAD_DIM), lambda b, h, i: (b * nq + i, h)),
                  pl.BlockSpec((S, QK_PAD_DIM), lambda b, h, i: (b, h)),
                  pl.BlockSpec((S, V_PAD_DIM), lambda b, h, i: (b, h))],
        out_specs=pl.BlockSpec((tq, V_HEAD_DIM), lambda b, h, i: (b * nq + i, h)),
        out_shape=jax.ShapeDtypeStruct((T, H * V_HEAD_DIM), BF16),
        scratch_shapes=[pltpu.VMEM((tq, LANES), F32), pltpu.VMEM((tq, V_PAD_DIM), F32),
                        pltpu.VMEM((tq, tkv), F32), pltpu.VMEM((tq, tkv), F32),
                        pltpu.VMEM((tq, LANES), F32), pltpu.VMEM((tq, LANES), F32),
                        pltpu.VMEM((tq, tkv), BF16), pltpu.VMEM((tq, tkv), BF16),
                        pltpu.VMEM((tq, LANES), F32), pltpu.VMEM((tq, LANES), F32)],
        compiler_params=_cparams(("parallel", "parallel", "arbitrary"), 56),
        name="mla_attention",
    )(q, k, v)


def _inproj_f_kernel(h_ref, w_ref, cs_ref, a_ref, b_ref, *, groups):
    acc = jnp.dot(h_ref[...], w_ref[...], preferred_element_type=F32)
    C = FOURIER_GROUP_DIM
    for g in range(groups):
        ab = jnp.dot(acc[:, g * C:(g + 1) * C].astype(BF16), cs_ref[...], preferred_element_type=F32)
        a_ref[:, g * C:(g + 1) * C] = ab[:, :C].astype(BF16)
        b_ref[:, g * C:(g + 1) * C] = ab[:, C:].astype(BF16)


def _inproj_f(hb, w_f, cs_tab):
    T, D = hb.shape
    fw = w_f.shape[1]
    tm = _tile(T, 1024)
    tn = _tile(fw, 512)
    return pl.pallas_call(
        functools.partial(_inproj_f_kernel, groups=tn // FOURIER_GROUP_DIM),
        grid=(T // tm, fw // tn),
        in_specs=[pl.BlockSpec((tm, D), lambda i, j: (i, 0)),
                  pl.BlockSpec((D, tn), lambda i, j: (0, j)),
                  pl.BlockSpec((FOURIER_GROUP_DIM, 2 * FOURIER_GROUP_DIM), lambda i, j: (0, 0))],
        out_specs=[pl.BlockSpec((tm, tn), lambda i, j: (i, j)),
                   pl.BlockSpec((tm, tn), lambda i, j: (i, j))],
        out_shape=[jax.ShapeDtypeStruct((T, fw), BF16), jax.ShapeDtypeStruct((T, fw), BF16)],
        compiler_params=_cparams(("parallel", "arbitrary"), 48),
        name="inproj_fourier",
    )(hb, w_f, cs_tab)


def _dft_gen_kernel(tac_ref, tas_ref, tbc_ref, tbs_ref, cs_ref, sn_ref):
    tbc = tbc_ref[...]
    tbs = tbs_ref[...]
    for aa in range(tac_ref.shape[0]):
        ca = tac_ref[aa:aa + 1, :]
        sa = tas_ref[aa:aa + 1, :]
        rows = slice(aa * DFT_ROW_SPLIT, (aa + 1) * DFT_ROW_SPLIT)
        cs_ref[rows, :] = (ca * tbc - sa * tbs).astype(BF16)
        sn_ref[rows, :] = (-(sa * tbc + ca * tbs)).astype(BF16)


def _dft_matrices(S, n_chan):
    na = S // 2 // DFT_ROW_SPLIT
    scale = 1.0 / math.sqrt(S * n_chan)
    col = jnp.arange(S // 2, dtype=jnp.int32)[None, :]
    period = S // DFT_ROW_SPLIT
    ang_a = (2.0 * math.pi / period) * ((jnp.arange(na, dtype=jnp.int32)[:, None] * col) % period).astype(F32)
    ang_b = (2.0 * math.pi / S) * ((jnp.arange(DFT_ROW_SPLIT, dtype=jnp.int32)[:, None] * col) % S).astype(F32)
    tac, tas = jnp.cos(ang_a), jnp.sin(ang_a)
    tbc, tbs = scale * jnp.cos(ang_b), scale * jnp.sin(ang_b)
    ta = min(8, na)
    assert na % ta == 0
    tc = _tile(S // 2, 2048)
    return pl.pallas_call(
        _dft_gen_kernel,
        grid=(na // ta, S // 2 // tc),
        in_specs=[pl.BlockSpec((ta, tc), lambda i, j: (i, j)),
                  pl.BlockSpec((ta, tc), lambda i, j: (i, j)),
                  pl.BlockSpec((DFT_ROW_SPLIT, tc), lambda i, j: (0, j)),
                  pl.BlockSpec((DFT_ROW_SPLIT, tc), lambda i, j: (0, j))],
        out_specs=[pl.BlockSpec((ta * DFT_ROW_SPLIT, tc), lambda i, j: (i, j)),
                   pl.BlockSpec((ta * DFT_ROW_SPLIT, tc), lambda i, j: (i, j))],
        out_shape=[jax.ShapeDtypeStruct((S // 2, S // 2), BF16), jax.ShapeDtypeStruct((S // 2, S // 2), BF16)],
        compiler_params=_cparams(("parallel", "parallel"), 40),
        name="dft_matrices",
    )(tac, tas, tbc, tbs)


def _dft_fold_kernel(a_ref, am_ref, an_ref, b_ref, bm_ref, bn_ref, ae_ref, bo_ref, alt_ref):
    tm = a_ref.shape[0]
    r = lax.broadcasted_iota(jnp.int32, (tm, tm), 0)
    c = lax.broadcasted_iota(jnp.int32, (tm, tm), 1)
    rev = jnp.where(r + c == tm, 1.0, 0.0).astype(BF16)
    has_next = (pl.program_id(1) > 0).astype(F32)
    first = (jnp.where(r + c == 0, 1.0, 0.0) * has_next).astype(BF16)

    def mirrored(m_ref, n_ref):
        return (jnp.dot(rev, m_ref[...], preferred_element_type=F32)
                + jnp.dot(first, n_ref[...], preferred_element_type=F32))

    ae_ref[...] = (a_ref[...].astype(F32) + mirrored(am_ref, an_ref)).astype(BF16)
    bo_ref[...] = (b_ref[...].astype(F32) - mirrored(bm_ref, bn_ref)).astype(BF16)

    @pl.when(pl.program_id(1) == 0)
    def _():
        alt_ref[...] = jnp.zeros(alt_ref.shape, F32)

    row = lax.broadcasted_iota(jnp.int32, a_ref.shape, 0)
    both = a_ref[...].astype(F32) + am_ref[...].astype(F32)
    alt_ref[0:1, :] += jnp.sum(jnp.where(row % 2 == 0, both, -both), axis=0, keepdims=True)


def _dft_fold(a, b, B, S):
    fw = a.shape[1]
    tm = _tile(S // 2, 256)
    nb = S // tm
    nh = nb // 2

    def own(bb, i):
        return (bb * nb + i, 0)

    def mirror(bb, i):
        return (bb * nb + nb - 1 - i, 0)

    def mirror_next(bb, i):
        return (bb * nb + jnp.minimum(nb - i, nb - 1), 0)

    spec = lambda f: pl.BlockSpec((tm, fw), f)
    return pl.pallas_call(
        _dft_fold_kernel,
        grid=(B, nh),
        in_specs=[spec(own), spec(mirror), spec(mirror_next), spec(own), spec(mirror), spec(mirror_next)],
        out_specs=[pl.BlockSpec((tm, fw), lambda bb, i: (bb * nh + i, 0)),
                   pl.BlockSpec((tm, fw), lambda bb, i: (bb * nh + i, 0)),
                   pl.BlockSpec((8, fw), lambda bb, i: (bb, 0))],
        out_shape=[jax.ShapeDtypeStruct((B * S // 2, fw), BF16),
                   jax.ShapeDtypeStruct((B * S // 2, fw), BF16),
                   jax.ShapeDtypeStruct((B * 8, fw), F32)],
        compiler_params=_cparams(("parallel", "arbitrary"), 40),
        name="dft_fold",
    )(a, a, a, b, b, b)


def _seq_dft_kernel(cs_ref, sn_ref, ae_ref, bo_ref, mid_ref, y_ref, d_ref, p_s, q_s, *, scale):
    kk = pl.program_id(3)

    @pl.when(kk == 0)
    def _():
        p_s[...] = jnp.zeros(p_s.shape, F32)
        q_s[...] = jnp.zeros(q_s.shape, F32)

    p_s[...] += jnp.dot(cs_ref[...], ae_ref[...], preferred_element_type=F32)
    q_s[...] += jnp.dot(sn_ref[...], bo_ref[...], preferred_element_type=F32)

    @pl.when(kk == pl.num_programs(3) - 1)
    def _():
        row = lax.broadcasted_iota(jnp.int32, p_s.shape, 0)
        pm = p_s[...] + jnp.where(row % 2 == 0, scale, -scale) * mid_ref[0:1, :].astype(F32)
        y_ref[...] = (pm + q_s[...]).astype(y_ref.dtype)
        d_ref[...] = (pm - q_s[...]).astype(d_ref.dtype)


def _seq_dft(cs, sn, ae, bo, a, B, S, scale):
    fw = a.shape[1]
    half = S // 2
    tm = _tile(half, 1024)
    tn = _tile(fw, 1024)
    tk = _tile(half, 1024)
    nm, nk = half // tm, half // tk
    mid_rows = 16
    return pl.pallas_call(
        functools.partial(_seq_dft_kernel, scale=scale),
        grid=(B, nm, fw // tn, nk),
        in_specs=[pl.BlockSpec((tm, tk), lambda bb, i, j, k: (i, k)),
                  pl.BlockSpec((tm, tk), lambda bb, i, j, k: (i, k)),
                  pl.BlockSpec((tk, tn), lambda bb, i, j, k: (bb * nk + k, j)),
                  pl.BlockSpec((tk, tn), lambda bb, i, j, k: (bb * nk + k, j)),
                  pl.BlockSpec((mid_rows, tn),
                               lambda bb, i, j, k: ((bb * S + half) // mid_rows, j))],
        out_specs=[pl.BlockSpec((tm, tn), lambda bb, i, j, k: (bb * nm + i, j)),
                   pl.BlockSpec((tm, tn), lambda bb, i, j, k: (bb * nm + i, j))],
        out_shape=[jax.ShapeDtypeStruct((B * half, fw), BF16),
                   jax.ShapeDtypeStruct((B * half, fw), BF16)],
        scratch_shapes=[pltpu.VMEM((tm, tn), F32), pltpu.VMEM((tm, tn), F32)],
        compiler_params=_cparams(("parallel", "parallel", "parallel", "arbitrary"), 48),
        name="seq_dft",
    )(cs, sn, ae, bo, a)


def _dft_mirror_kernel(yh_ref, dm_ref, dn_ref, alt_ref, y_ref, *, scale, nh):
    j = pl.program_id(1)

    @pl.when(j < nh)
    def _():
        y_ref[...] = yh_ref[...]

    @pl.when(j >= nh)
    def _():
        tm = dm_ref.shape[0]
        r = lax.broadcasted_iota(jnp.int32, (tm, tm), 0)
        c = lax.broadcasted_iota(jnp.int32, (tm, tm), 1)
        rev = jnp.where(r + c == tm, 1.0, 0.0).astype(BF16)
        first = (jnp.where(r + c == 0, 1.0, 0.0) * (j > nh).astype(F32)).astype(BF16)
        out = (jnp.dot(rev, dm_ref[...], preferred_element_type=F32)
               + jnp.dot(first, dn_ref[...], preferred_element_type=F32))
        row = lax.broadcasted_iota(jnp.int32, out.shape, 0)
        nyquist = jnp.where(row == 0, scale * (j == nh).astype(F32), 0.0) * alt_ref[0:1, :]
        y_ref[...] = (out + nyquist).astype(y_ref.dtype)


def _dft_mirror(y_half, d, alt, B, S, scale):
    fw = d.shape[1]
    half = S // 2
    tm = _tile(half, 256)
    nh = half // tm

    def mirror(bb, j):
        return (bb * nh + nh - 1 - jnp.maximum(j - nh, 0), 0)

    def mirror_next(bb, j):
        return (bb * nh + jnp.minimum(nh - jnp.maximum(j - nh, 0), nh - 1), 0)

    return pl.pallas_call(
        functools.partial(_dft_mirror_kernel, scale=scale, nh=nh),
        grid=(B, 2 * nh),
        in_specs=[pl.BlockSpec((tm, fw), lambda bb, j: (bb * nh + jnp.minimum(j, nh - 1), 0)),
                  pl.BlockSpec((tm, fw), mirror),
                  pl.BlockSpec((tm, fw), mirror_next),
                  pl.BlockSpec((8, fw), lambda bb, j: (bb, 0))],
        out_specs=pl.BlockSpec((tm, fw), lambda bb, j: (bb * 2 * nh + j, 0)),
        out_shape=jax.ShapeDtypeStruct((B * S, fw), d.dtype),
        compiler_params=_cparams(("parallel", "arbitrary"), 40),
        name="dft_mirror",
    )(y_half, d, d, alt)


def _wo_kernel(ym_ref, yf_ref, gm_ref, gf_ref, w_ref, x_ref, mu_ref, rs_ref, lg_ref, lb_ref,
               pre_ref, mix_s, *, alpha, mw):
    @pl.when(pl.program_id(1) == 0)
    def _():
        ym = ym_ref[...].astype(F32)
        mix_s[:, :mw] = (ym * lax.rsqrt(jnp.mean(ym * ym, axis=-1, keepdims=True) + RMS_EPS)
                         * gm_ref[...]).astype(BF16)
        yf = yf_ref[...].astype(F32)
        mix_s[:, mw:] = (yf * lax.rsqrt(jnp.mean(yf * yf, axis=-1, keepdims=True) + RMS_EPS)
                         * gf_ref[...]).astype(BF16)

    h = (x_ref[...] - mu_ref[...]) * rs_ref[...] * lg_ref[...] + lb_ref[...]
    pre_ref[...] = alpha * h + jnp.dot(mix_s[...], w_ref[...], preferred_element_type=F32)


def _wo(ym, yf, gm, gf, w_o, x2, mu, rs, ln_g, ln_b, alpha):
    T, mw = ym.shape
    fw = yf.shape[1]
    D = w_o.shape[1]
    tm = _tile(T, 1024)
    tn = _tile(D, 512)
    return pl.pallas_call(
        functools.partial(_wo_kernel, alpha=alpha, mw=mw),
        grid=(T // tm, D // tn),
        in_specs=[pl.BlockSpec((tm, mw), lambda i, j: (i, 0)),
                  pl.BlockSpec((tm, fw), lambda i, j: (i, 0)),
                  pl.BlockSpec((1, mw), lambda i, j: (0, 0)),
                  pl.BlockSpec((1, fw), lambda i, j: (0, 0)),
                  pl.BlockSpec((mw + fw, tn), lambda i, j: (0, j)),
                  pl.BlockSpec((tm, tn), lambda i, j: (i, j)),
                  pl.BlockSpec((tm, 1), lambda i, j: (i, 0)),
                  pl.BlockSpec((tm, 1), lambda i, j: (i, 0)),
                  pl.BlockSpec((1, tn), lambda i, j: (0, j)),
                  pl.BlockSpec((1, tn), lambda i, j: (0, j))],
        out_specs=pl.BlockSpec((tm, tn), lambda i, j: (i, j)),
        out_shape=jax.ShapeDtypeStruct((T, D), F32),
        scratch_shapes=[pltpu.VMEM((tm, mw + fw), BF16)],
        compiler_params=_cparams(("parallel", "arbitrary"), 56),
        name="w_o_residual",
    )(ym, yf, gm.reshape(1, mw), gf.reshape(1, fw), w_o, x2, mu, rs,
      ln_g.reshape(1, D), ln_b.reshape(1, D))


def _pack_rounded_pair(lo, hi):
    return (pltpu.bitcast(lo, jnp.uint32) >> 16) | (pltpu.bitcast(hi, jnp.uint32) & jnp.uint32(0xFFFF0000))


def _pack_bf16_pair(lo, hi):
    return _pack_rounded_pair(lo.astype(BF16).astype(F32), hi.astype(BF16).astype(F32))


def _unpack_bf16_pair(w):
    lo = pltpu.bitcast(w << 16, F32).astype(BF16)
    hi = pltpu.bitcast(w & jnp.uint32(0xFFFF0000), F32).astype(BF16)
    return lo, hi


def _ln1_router_kernel(pre_ref, g_ref, b_ref, rwh_ref, rwl_ref, rb_ref, x1_ref, x1p_ref, idx_ref, gate_ref):
    x = pre_ref[...]
    mu = jnp.mean(x, axis=-1, keepdims=True)
    xc = x - mu
    var = jnp.mean(xc * xc, axis=-1, keepdims=True)
    x1 = xc * lax.rsqrt(var + LN_EPS) * g_ref[...] + b_ref[...]
    x1_ref[...] = x1
    half = x1.shape[1] // 2
    x_hi = x1.astype(BF16)
    x_hi_f = x_hi.astype(F32)
    x1p_ref[...] = _pack_rounded_pair(x_hi_f[:, :half], x_hi_f[:, half:])
    x_lo = (x1 - x_hi_f).astype(BF16)
    nt = (((1,), (1,)), ((), ()))
    logits = (lax.dot_general(rwh_ref[...], x_hi, nt, preferred_element_type=F32)
              + lax.dot_general(rwh_ref[...], x_lo, nt, preferred_element_type=F32)
              + lax.dot_general(rwl_ref[...], x_hi, nt, preferred_element_type=F32)) + rb_ref[...]
    n_e = logits.shape[0]
    eidx = lax.broadcasted_iota(jnp.int32, logits.shape, 0)
    vals, idxs = [], []
    for _ in range(TOP_K):
        m = jnp.max(logits, axis=0, keepdims=True)
        sel = jnp.min(jnp.where(logits == m, eidx, n_e), axis=0, keepdims=True)
        logits = jnp.where(eidx == sel, -jnp.inf, logits)
        vals.append(m)
        idxs.append(sel)
    exps = [jnp.exp(v - vals[0]) for v in vals]
    denom = exps[0] + exps[1] + exps[2] + exps[3]
    for kk in range(TOP_K):
        idx_ref[kk:kk + 1, :] = idxs[kk]
        gate_ref[kk:kk + 1, :] = exps[kk] / denom


def _ln1_router(pre, g, b, router_w, router_b):
    T, D = pre.shape
    E = router_w.shape[1]
    tm = _tile(T, 256)
    rw_t = router_w.T
    rw_hi = rw_t.astype(BF16)
    rw_lo = (rw_t - rw_hi.astype(F32)).astype(BF16)
    return pl.pallas_call(
        _ln1_router_kernel,
        grid=(T // tm,),
        in_specs=[pl.BlockSpec((tm, D), lambda i: (i, 0)),
                  pl.BlockSpec((1, D), lambda i: (0, 0)),
                  pl.BlockSpec((1, D), lambda i: (0, 0)),
                  pl.BlockSpec((E, D), lambda i: (0, 0)),
                  pl.BlockSpec((E, D), lambda i: (0, 0)),
                  pl.BlockSpec((E, 1), lambda i: (0, 0))],
        out_specs=[pl.BlockSpec((tm, D), lambda i: (i, 0)),
                   pl.BlockSpec((tm, D // 2), lambda i: (i, 0)),
                   pl.BlockSpec((TOP_K, tm), lambda i: (0, i)),
                   pl.BlockSpec((TOP_K, tm), lambda i: (0, i))],
        out_shape=[jax.ShapeDtypeStruct((T, D), F32),
                   jax.ShapeDtypeStruct((T, D // 2), jnp.uint32),
                   jax.ShapeDtypeStruct((TOP_K, T), jnp.int32),
                   jax.ShapeDtypeStruct((TOP_K, T), F32)],
        compiler_params=_cparams(("parallel",), 48),
        name="ln1_router",
    )(pre, g.reshape(1, D), b.reshape(1, D), rw_hi, rw_lo, router_b.reshape(E, 1))


def _row_copy(src_hbm, dst_ref, src_row, dst_row, sem):
    return pltpu.make_async_copy(src_hbm.at[pl.ds(src_row, 1)], dst_ref.at[pl.ds(dst_row, 1)], sem)


def _dispatch_kernel(dest_ref, pad_ref, npad_ref, nv_ref, x_ref, o_hbm, zero_s, sem, zsem, *, n_steps):
    b = pl.program_id(0)
    tt = x_ref.shape[0]

    def wait_rows(n):
        pltpu.make_async_copy(o_hbm.at[pl.ds(0, n)], o_hbm.at[pl.ds(0, n)], sem).wait()

    @pl.when(b < n_steps)
    def _():
        def issue(r, c):
            for kk in range(TOP_K):
                _row_copy(x_ref, o_hbm, r, dest_ref[(b * tt + r) * TOP_K + kk], sem).start(priority=kk % 2)
            return c
        lax.fori_loop(0, tt, issue, 0, unroll=2)
        wait_rows(tt * TOP_K)

    @pl.when(b == n_steps)
    def _():
        n_pad = npad_ref[0]

        def issue_pad(i, c):
            _row_copy(x_ref, o_hbm, 0, pad_ref[i], sem).start()
            return c
        lax.fori_loop(0, n_pad, issue_pad, 0)

        def wait_pad(i, c):
            wait_rows(1)
            return c
        lax.fori_loop(0, n_pad, wait_pad, 0)

        zero_s[...] = jnp.zeros(zero_s.shape, zero_s.dtype)
        n_sub = MOE_CHUNK // MOE_SUB

        def zero_copy(i):
            return pltpu.make_async_copy(zero_s, o_hbm.at[pl.ds(pl.multiple_of(i * MOE_SUB, MOE_SUB), MOE_SUB)], zsem)

        def empty(i):
            return (i % n_sub) * MOE_SUB >= nv_ref[i // n_sub]

        def issue_zero(i, c):
            @pl.when(empty(i))
            def _():
                zero_copy(i).start()
            return c
        lax.fori_loop(0, nv_ref.shape[0] * n_sub, issue_zero, 0)

        def wait_zero(i, c):
            @pl.when(empty(i))
            def _():
                zero_copy(i).wait()
            return c
        lax.fori_loop(0, nv_ref.shape[0] * n_sub, wait_zero, 0)


def _dispatch(dest, pad_slots, n_pad, chunk_nv, x1p):
    T, W = x1p.shape
    tt = _tile(T, DISPATCH_TOKENS)
    n_steps = T // tt
    grid_spec = pltpu.PrefetchScalarGridSpec(
        num_scalar_prefetch=4,
        grid=(n_steps + 1,),
        in_specs=[pl.BlockSpec((tt, W), lambda b, d, p, n, v: (jnp.minimum(b, n_steps - 1), 0))],
        out_specs=pl.BlockSpec(memory_space=pl.ANY),
        scratch_shapes=[pltpu.VMEM((MOE_SUB, W), x1p.dtype),
                        pltpu.SemaphoreType.DMA(()), pltpu.SemaphoreType.DMA(())],
    )
    return pl.pallas_call(
        functools.partial(_dispatch_kernel, n_steps=n_steps),
        grid_spec=grid_spec,
        out_shape=jax.ShapeDtypeStruct((chunk_nv.shape[0] * MOE_CHUNK, W), x1p.dtype),
        compiler_params=_cparams(("arbitrary",), 32),
        name="moe_dispatch",
    )(dest, pad_slots, n_pad, chunk_nv, x1p)


def _expert_up_kernel(ce_ref, nv_ref, nu_ref, x_ref, wg_ref, wu_ref, bg_ref, bu_ref, h_ref):
    c = pl.program_id(0)
    nv = nv_ref[c]
    brow = ce_ref[c] * pl.num_programs(1) + pl.program_id(1)

    n_sub = MOE_CHUNK // MOE_SUB
    n_act = (nv + MOE_SUB - 1) // MOE_SUB
    for k in range(n_sub + 1):
        @pl.when(n_act == k)
        def _(k=k):
            if k > 0:
                x_lo, x_hi = _unpack_bf16_pair(x_ref[:k * MOE_SUB, :])
                half = x_lo.shape[1]

                def proj(w_ref):
                    return (jnp.dot(x_lo, w_ref[:half, :].astype(BF16), preferred_element_type=F32)
                            + jnp.dot(x_hi, w_ref[half:, :].astype(BF16), preferred_element_type=F32))
                hg = jnp.minimum(proj(wg_ref) + bg_ref[pl.ds(brow, 1), :], SWIGLU_LIMIT)
                hu = jnp.clip(proj(wu_ref) + bu_ref[pl.ds(brow, 1), :], -SWIGLU_LIMIT, SWIGLU_LIMIT)
                act = (hu + 1.0) * (hg * jax.nn.sigmoid(SWIGLU_ALPHA * hg))
                h_ref[:k * MOE_SUB, :] = act.astype(h_ref.dtype)
            if k < n_sub:
                h_ref[k * MOE_SUB:, :] = jnp.zeros(((n_sub - k) * MOE_SUB, h_ref.shape[1]), h_ref.dtype)


def _expert_up(chunk_e, chunk_nv, n_used, xs, w_gate, w_up, b_gate, b_up, n_chunks):
    E, D, F = w_gate.shape
    tf = _tile(F, 256)
    nf = F // tf

    def used(c, nu):
        return jnp.minimum(c, nu[0] - 1)

    def jeff(c, j, nu):
        return jnp.where(c < nu[0], j, nf - 1)

    grid_spec = pltpu.PrefetchScalarGridSpec(
        num_scalar_prefetch=3,
        grid=(n_chunks, nf),
        in_specs=[pl.BlockSpec((MOE_CHUNK, D // 2), lambda c, j, ce, nv, nu: (used(c, nu), 0)),
                  pl.BlockSpec((None, D, tf), lambda c, j, ce, nv, nu: (ce[c], 0, jeff(c, j, nu))),
                  pl.BlockSpec((None, D, tf), lambda c, j, ce, nv, nu: (ce[c], 0, jeff(c, j, nu))),
                  pl.BlockSpec((E * nf, tf), lambda c, j, ce, nv, nu: (0, 0)),
                  pl.BlockSpec((E * nf, tf), lambda c, j, ce, nv, nu: (0, 0))],
        out_specs=pl.BlockSpec((MOE_CHUNK, tf), lambda c, j, ce, nv, nu: (c, j)),
    )
    return pl.pallas_call(
        _expert_up_kernel,
        grid_spec=grid_spec,
        out_shape=jax.ShapeDtypeStruct((n_chunks * MOE_CHUNK, F), BF16),
        compiler_params=_cparams(("arbitrary", "arbitrary"), 60),
        name="expert_gate_up",
    )(chunk_e, chunk_nv, n_used, xs, w_gate, w_up, b_gate.reshape(E * nf, tf), b_up.reshape(E * nf, tf))


def _expert_down_kernel(ce_ref, nv_ref, nu_ref, h_ref, wd_ref, bd_ref, y_ref):
    c = pl.program_id(0)
    nv = nv_ref[c]
    brow = ce_ref[c] * pl.num_programs(1) + pl.program_id(1)

    n_sub = MOE_CHUNK // MOE_SUB
    n_act = (nv + MOE_SUB - 1) // MOE_SUB
    for k in range(n_sub + 1):
        @pl.when(n_act == k)
        def _(k=k):
            if k > 0:
                out = jnp.dot(h_ref[:k * MOE_SUB, :], wd_ref[...].astype(BF16),
                              preferred_element_type=F32) + bd_ref[pl.ds(brow, 1), :]
                hw = out.shape[1] // 2
                y_ref[:k * MOE_SUB, :] = _pack_bf16_pair(out[:, :hw], out[:, hw:])
            if k < n_sub:
                y_ref[k * MOE_SUB:, :] = jnp.zeros(((n_sub - k) * MOE_SUB, y_ref.shape[1]), y_ref.dtype)


def _expert_down(chunk_e, chunk_nv, n_used, hmid, w_down, b_down, n_chunks):
    E, F, D = w_down.shape
    tn = _tile(D, MOE_DOWN_TN)
    nn = D // tn

    def used(c, nu):
        return jnp.minimum(c, nu[0] - 1)

    def jeff(c, j, nu):
        return jnp.where(c < nu[0], j, nn - 1)

    grid_spec = pltpu.PrefetchScalarGridSpec(
        num_scalar_prefetch=3,
        grid=(n_chunks, nn),
        in_specs=[pl.BlockSpec((MOE_CHUNK, F), lambda c, j, ce, nv, nu: (used(c, nu), 0)),
                  pl.BlockSpec((None, F, tn), lambda c, j, ce, nv, nu: (ce[c], 0, jeff(c, j, nu))),
                  pl.BlockSpec((E * nn, tn), lambda c, j, ce, nv, nu: (0, 0))],
        out_specs=pl.BlockSpec((MOE_CHUNK, tn // 2), lambda c, j, ce, nv, nu: (c, j)),
    )
    return pl.pallas_call(
        _expert_down_kernel,
        grid_spec=grid_spec,
        out_shape=jax.ShapeDtypeStruct((n_chunks * MOE_CHUNK, D // 2), jnp.uint32),
        compiler_params=_cparams(("arbitrary", "arbitrary"), 56),
        name="expert_down",
    )(chunk_e, chunk_nv, n_used, hmid, w_down, b_down.reshape(E * nn, tn))


def _combine_kernel(dest_ref, y_hbm, x1_ref, gate_ref, g_ref, b_ref, o_ref, *scratch, alpha, tn):
    i = pl.program_id(0)
    n = pl.num_programs(0)
    nb = COMBINE_BUFS
    tt = o_ref.shape[0] // nb
    bufs, sem = scratch[:nb], scratch[nb]

    def issue_tile(tile, slot):
        base = jnp.minimum(tile, nb * n - 1) * tt
        for r in range(tt):
            for kk in range(TOP_K):
                _row_copy(y_hbm, bufs[slot].at[kk], dest_ref[(base + r) * TOP_K + kk], r,
                          sem.at[slot]).start(priority=kk % 2)

    def wait_tile(slot):
        for kk in range(TOP_K):
            pltpu.make_async_copy(y_hbm.at[pl.ds(0, tt)], bufs[slot].at[kk], sem.at[slot]).wait()

    def finish_tile(slot):
        rows = slice(slot * tt, (slot + 1) * tt)
        gates = gate_ref[rows, :]
        y_lo = y_hi = None
        for kk in range(TOP_K):
            w = bufs[slot][kk]
            g_k = gates[:, kk:kk + 1]
            lo = g_k * pltpu.bitcast(w << 16, F32)
            hi = g_k * pltpu.bitcast(w & jnp.uint32(0xFFFF0000), F32)
            y_lo = lo if y_lo is None else y_lo + lo
            y_hi = hi if y_hi is None else y_hi + hi
        hw = tn // 2
        pieces = []
        for j in range(x1_ref.shape[1] // tn):
            pieces.append((slice(j * tn, j * tn + hw), y_lo[:, j * hw:(j + 1) * hw]))
            pieces.append((slice(j * tn + hw, (j + 1) * tn), y_hi[:, j * hw:(j + 1) * hw]))
        zs = [alpha * x1_ref[rows, cols] + y for cols, y in pieces]
        d = x1_ref.shape[1]
        mu = sum(jnp.sum(z, axis=-1, keepdims=True) for z in zs) / d
        zcs = [z - mu for z in zs]
        var = sum(jnp.sum(zc * zc, axis=-1, keepdims=True) for zc in zcs) / d
        rs = lax.rsqrt(var + LN_EPS)
        for (cols, _), zc in zip(pieces, zcs):
            o_ref[rows, cols] = zc * rs * g_ref[:, cols] + b_ref[:, cols]

    @pl.when(i == 0)
    def _():
        for t in range(COMBINE_AHEAD):
            issue_tile(t, t)

    for t in range(nb):
        wait_tile(t)
        issue_tile(nb * i + t + COMBINE_AHEAD, (t + COMBINE_AHEAD) % nb)
        finish_tile(t)

    @pl.when(i == n - 1)
    def _():
        for t in range(COMBINE_AHEAD):
            wait_tile(t)


def _combine(dest, yslots, x1, gates_tk, g, b, alpha):
    T, D = x1.shape
    nb = COMBINE_BUFS
    tt = _tile(T // nb, COMBINE_TOKENS)
    grid_spec = pltpu.PrefetchScalarGridSpec(
        num_scalar_prefetch=1,
        grid=(T // (nb * tt),),
        in_specs=[pl.BlockSpec(memory_space=pl.ANY),
                  pl.BlockSpec((nb * tt, D), lambda i, d: (i, 0)),
                  pl.BlockSpec((nb * tt, TOP_K), lambda i, d: (i, 0)),
                  pl.BlockSpec((1, D), lambda i, d: (0, 0)),
                  pl.BlockSpec((1, D), lambda i, d: (0, 0))],
        out_specs=pl.BlockSpec((nb * tt, D), lambda i, d: (i, 0)),
        scratch_shapes=[pltpu.VMEM((TOP_K, tt, D // 2), jnp.uint32) for _ in range(nb)]
        + [pltpu.SemaphoreType.DMA((nb,))],
    )
    return pl.pallas_call(
        functools.partial(_combine_kernel, alpha=alpha, tn=_tile(D, MOE_DOWN_TN)),
        grid_spec=grid_spec,
        out_shape=jax.ShapeDtypeStruct((T, D), F32),
        compiler_params=_cparams(("arbitrary",), 48),
        name="moe_combine_ln2",
    )(dest, yslots, x1, gates_tk, g.reshape(1, D), b.reshape(1, D))


def _routing_tables(top_idx, n_experts, n_chunks):
    T = top_idx.shape[1]
    M = T * TOP_K
    flat_e = top_idx.T.reshape(M)
    onehot = (flat_e[:, None] == jnp.arange(n_experts, dtype=jnp.int32)[None, :]).astype(jnp.int32)
    csum = jnp.cumsum(onehot, axis=0)
    rank = jnp.sum(csum * onehot, axis=1) - 1
    counts = csum[-1]
    chunks_e = (counts + MOE_CHUNK - 1) // MOE_CHUNK
    chunk_end = jnp.cumsum(chunks_e)
    chunk_start = chunk_end - chunks_e
    n_used = chunk_end[-1]
    dest = chunk_start[flat_e] * MOE_CHUNK + rank
    P = n_chunks * MOE_CHUNK
    pad_idx = counts[:, None] + jnp.arange(MOE_SUB, dtype=jnp.int32)[None, :]
    padded = (counts + MOE_SUB - 1) // MOE_SUB * MOE_SUB
    pad_slots = jnp.where(pad_idx < padded[:, None], chunk_start[:, None] * MOE_CHUNK + pad_idx, P)
    pad_slots = jnp.sort(pad_slots.reshape(-1)).astype(jnp.int32)
    n_pad = jnp.sum(padded - counts).astype(jnp.int32).reshape(1)
    cid = jnp.arange(n_chunks, dtype=jnp.int32)
    chunk_e = jnp.minimum(jnp.searchsorted(chunk_end, cid, side='right'), n_experts - 1).astype(jnp.int32)
    last_e = chunk_e[jnp.maximum(n_used - 1, 0)]
    chunk_e = jnp.where(cid < n_used, chunk_e, last_e)
    chunk_nv = jnp.where(cid < n_used,
                         jnp.clip(counts[chunk_e] - (cid - chunk_start[chunk_e]) * MOE_CHUNK, 0, MOE_CHUNK),
                         0).astype(jnp.int32)
    return (dest.astype(jnp.int32), pad_slots, n_pad, chunk_e, chunk_nv,
            n_used.astype(jnp.int32).reshape(1))


def kernel(x, positions, ln_in_g, ln_in_b, w_in, q_a_norm_g, w_q_b, kv_a_norm_g, w_kv_b, mla_out_norm_g, fourier_out_norm_g, w_o, ln1_g, ln1_b, router_w, router_b, w_gate, b_gate, w_up, b_up, w_down, b_down, ln2_g, ln2_b):
    B, S, D = x.shape
    T = B * S
    depth = w_in.shape[0]
    assert depth == 1, "single-layer trunk only"
    qr = q_a_norm_g.shape[1]
    kvr = kv_a_norm_g.shape[1]
    H = w_q_b.shape[2] // (QK_NOPE_DIM + QK_ROPE_DIM)
    fw = fourier_out_norm_g.shape[1]
    E = router_w.shape[2]
    assert H % HEADS_PER_TILE == 0 and (qr + kvr) % LANES == 0 and S % (2 * DFT_ROW_SPLIT) == 0
    alpha = (2.0 * depth) ** 0.25

    inv_freq = ROPE_THETA ** (-jnp.arange(0, QK_ROPE_DIM, 2, dtype=F32) / QK_ROPE_DIM)
    ang = positions.astype(F32)[..., None] * inv_freq
    cos4 = jnp.tile(jnp.cos(ang), (1, 1, 2 * LANES // QK_ROPE_DIM)).reshape(T, LANES)
    sin4 = jnp.tile(jnp.sin(ang), (1, 1, 2 * LANES // QK_ROPE_DIM)).reshape(T, LANES)

    rope_end = qr + kvr + QK_ROPE_DIM
    w_a = w_in[0, :, :qr + kvr + LANES].astype(BF16)
    w_f = w_in[0, :, rope_end:].astype(BF16)
    wq = w_q_b[0].reshape(qr, H // HEADS_PER_TILE, HEADS_PER_TILE, QK_NOPE_DIM + QK_ROPE_DIM)
    wq_perm = jnp.concatenate(
        [wq[..., :QK_NOPE_DIM].reshape(qr, H // HEADS_PER_TILE, HEADS_PER_TILE * QK_NOPE_DIM),
         wq[..., QK_NOPE_DIM:].reshape(qr, H // HEADS_PER_TILE, HEADS_PER_TILE * QK_ROPE_DIM)],
        axis=-1).reshape(qr, H * (QK_NOPE_DIM + QK_ROPE_DIM)).astype(BF16)
    w_kv = w_kv_b[0].astype(BF16)
    w_o_b = w_o[0].astype(BF16)
    ch = jnp.arange(FOURIER_GROUP_DIM, dtype=jnp.int32)
    ang_c = (2.0 * math.pi / FOURIER_GROUP_DIM) * ((ch[:, None] * ch[None, :]) % FOURIER_GROUP_DIM).astype(F32)
    cs_tab = jnp.concatenate([jnp.cos(ang_c), jnp.sin(ang_c)], axis=1).astype(BF16)

    x2 = x.reshape(T, D)
    hb, mu, rs = _ln_in(x2, ln_in_g, ln_in_b)

    cq, ckv, kpe2 = _inproj_a(hb, w_a, q_a_norm_g[0], kv_a_norm_g[0], cos4, sin4)
    qscale = (QK_NOPE_DIM + QK_ROPE_DIM) ** -0.5 * LOG2E
    q = _q_up(cq, wq_perm, cos4, sin4, qscale)
    k, v = _kv_up(ckv, w_kv, kpe2)
    y_mla = _attention(q, k, v, B, S, H)

    fa, fb = _inproj_f(hb, w_f, cs_tab)
    cs_mat, sn_mat = _dft_matrices(S, FOURIER_GROUP_DIM)
    fae, fbo, falt = _dft_fold(fa, fb, B, S)
    dft_scale = 1.0 / math.sqrt(S * FOURIER_GROUP_DIM)
    y_half, y_diff = _seq_dft(cs_mat, sn_mat, fae, fbo, fa, B, S, dft_scale)
    y_f = _dft_mirror(y_half, y_diff, falt, B, S, dft_scale)

    pre = _wo(y_mla, y_f, mla_out_norm_g[0], fourier_out_norm_g[0], w_o_b, x2, mu, rs,
              ln_in_g, ln_in_b, alpha)
    x1, x1p, top_idx, gates = _ln1_router(pre, ln1_g[0], ln1_b[0], router_w[0], router_b[0])

    n_chunks = -(-T * TOP_K // MOE_CHUNK) + E
    dest, pad_slots, n_pad, chunk_e, chunk_nv, n_used = _routing_tables(top_idx, E, n_chunks)
    xs = _dispatch(dest, pad_slots, n_pad, chunk_nv, x1p)
    hmid = _expert_up(chunk_e, chunk_nv, n_used, xs, w_gate[0], w_up[0], b_gate[0], b_up[0], n_chunks)
    yslots = _expert_down(chunk_e, chunk_nv, n_used, hmid, w_down[0], b_down[0], n_chunks)
    out = _combine(dest, yslots, x1, gates.T, ln2_g[0], ln2_b[0], alpha)
    return out.reshape(B, S, D)
```

```python
import functools
import math

import jax
import jax.numpy as jnp
from jax import lax
from jax.experimental import pallas as pl
from jax.experimental.pallas import tpu as pltpu

F32 = jnp.float32
BF16 = jnp.bfloat16

V_HEAD_DIM = 128
QK_NOPE_DIM = 128
QK_ROPE_DIM = 64
QK_PAD_DIM = 256
V_PAD_DIM = 256
ROPE_THETA = 10000.0
FOURIER_GROUP_DIM = 128
TOP_K = 4
SWIGLU_LIMIT = 7.0
SWIGLU_ALPHA = 1.702
LN_EPS = 1e-5
RMS_EPS = 1e-6
LOG2E = 1.4426950408889634

LANES = 128
V7X_VMEM_BYTES = 64 * 1024 * 1024
VMEM_COMPILER_RESERVE = 4 * 1024 * 1024
HEADS_PER_TILE = 4
DFT_ROW_SPLIT = 64
ATTN_TQ = 1024
ATTN_TKV = 2048

MOE_CHUNK = 1280
MOE_SUB = 128
DISPATCH_TOKENS = 512
COMBINE_TOKENS = 64
COMBINE_BUFS = 4
COMBINE_AHEAD = 2
MOE_DOWN_TN = 1024


def _cparams(semantics, vmem_mb):
    return pltpu.CompilerParams(dimension_semantics=semantics,
                                vmem_limit_bytes=min(vmem_mb * 1024 * 1024,
                                                     V7X_VMEM_BYTES - VMEM_COMPILER_RESERVE))


def _tile(dim, pref):
    t = min(dim, pref)
    while dim % t:
        t //= 2
    return t


def _ln_in_kernel(x_ref, g_ref, b_ref, hb_ref, mu_ref, rs_ref):
    x = x_ref[...]
    mu = jnp.mean(x, axis=-1, keepdims=True)
    xc = x - mu
    var = jnp.mean(xc * xc, axis=-1, keepdims=True)
    rs = lax.rsqrt(var + LN_EPS)
    hb_ref[...] = (xc * rs * g_ref[...] + b_ref[...]).astype(BF16)
    mu_ref[...] = mu
    rs_ref[...] = rs


def _ln_in(x2, g, b):
    T, D = x2.shape
    tm = _tile(T, 256)
    return pl.pallas_call(
        _ln_in_kernel,
        grid=(T // tm,),
        in_specs=[pl.BlockSpec((tm, D), lambda i: (i, 0)),
                  pl.BlockSpec((1, D), lambda i: (0, 0)),
                  pl.BlockSpec((1, D), lambda i: (0, 0))],
        out_specs=[pl.BlockSpec((tm, D), lambda i: (i, 0)),
                   pl.BlockSpec((tm, 1), lambda i: (i, 0)),
                   pl.BlockSpec((tm, 1), lambda i: (i, 0))],
        out_shape=[jax.ShapeDtypeStruct((T, D), BF16),
                   jax.ShapeDtypeStruct((T, 1), F32),
                   jax.ShapeDtypeStruct((T, 1), F32)],
        compiler_params=_cparams(("parallel",), 40),
        name="ln_in",
    )(x2, g.reshape(1, D), b.reshape(1, D))


def _rope128(p, cos4, sin4):
    lane = lax.broadcasted_iota(jnp.int32, p.shape, 1)
    first_half = (lane % QK_ROPE_DIM) < (QK_ROPE_DIM // 2)
    rot = jnp.where(first_half, -pltpu.roll(p, LANES - QK_ROPE_DIM // 2, 1),
                    pltpu.roll(p, QK_ROPE_DIM // 2, 1))
    return p * cos4 + rot * sin4


def _inproj_a_kernel(h_ref, w_ref, gq_ref, gkv_ref, cos_ref, sin_ref,
                     cq_ref, ckv_ref, kpe_ref, *, qr, kvr):
    acc = jnp.dot(h_ref[...], w_ref[...], preferred_element_type=F32)
    cq = acc[:, :qr]
    cq_ref[...] = (cq * lax.rsqrt(jnp.mean(cq * cq, axis=-1, keepdims=True) + RMS_EPS)
                   * gq_ref[...]).astype(BF16)
    ckv = acc[:, qr:qr + kvr]
    ckv_ref[...] = (ckv * lax.rsqrt(jnp.mean(ckv * ckv, axis=-1, keepdims=True) + RMS_EPS)
                    * gkv_ref[...]).astype(BF16)
    roped = _rope128(acc[:, qr + kvr:qr + kvr + LANES], cos_ref[...], sin_ref[...])
    lane = lax.broadcasted_iota(jnp.int32, roped.shape, 1)
    even = jnp.where(lane < QK_ROPE_DIM, roped, 0.0)
    kpe_ref[:, :LANES] = even.astype(BF16)
    kpe_ref[:, LANES:] = pltpu.roll(even, QK_ROPE_DIM, 1).astype(BF16)


def _inproj_a(hb, w_a, gq, gkv, cos4, sin4):
    T, D = hb.shape
    qr, kvr = gq.shape[0], gkv.shape[0]
    wa = w_a.shape[1]
    tm = _tile(T, 512)
    return pl.pallas_call(
        functools.partial(_inproj_a_kernel, qr=qr, kvr=kvr),
        grid=(T // tm,),
        in_specs=[pl.BlockSpec((tm, D), lambda i: (i, 0)),
                  pl.BlockSpec((D, wa), lambda i: (0, 0)),
                  pl.BlockSpec((1, qr), lambda i: (0, 0)),
                  pl.BlockSpec((1, kvr), lambda i: (0, 0)),
                  pl.BlockSpec((tm, LANES), lambda i: (i, 0)),
                  pl.BlockSpec((tm, LANES), lambda i: (i, 0))],
        out_specs=[pl.BlockSpec((tm, qr), lambda i: (i, 0)),
                   pl.BlockSpec((tm, kvr), lambda i: (i, 0)),
                   pl.BlockSpec((tm, 2 * LANES), lambda i: (i, 0))],
        out_shape=[jax.ShapeDtypeStruct((T, qr), BF16),
                   jax.ShapeDtypeStruct((T, kvr), BF16),
                   jax.ShapeDtypeStruct((T, 2 * LANES), BF16)],
        compiler_params=_cparams(("parallel",), 56),
        name="inproj_mla",
    )(hb, w_a, gq.reshape(1, qr), gkv.reshape(1, kvr), cos4, sin4)


def _q_up_kernel(c_ref, w_ref, cos_ref, sin_ref, q_ref, *, qscale):
    acc = jnp.dot(c_ref[...], w_ref[...], preferred_element_type=F32)
    nope_w = HEADS_PER_TILE * QK_NOPE_DIM
    lane = lax.broadcasted_iota(jnp.int32, (acc.shape[0], LANES), 1)
    for pair in range(HEADS_PER_TILE // 2):
        roped = _rope128(acc[:, nope_w + pair * LANES:nope_w + (pair + 1) * LANES],
                         cos_ref[...], sin_ref[...]) * qscale
        for par in range(2):
            j = 2 * pair + par
            keep = (lane < QK_ROPE_DIM) if par == 0 else (lane >= QK_ROPE_DIM)
            base = j * QK_PAD_DIM
            q_ref[:, base:base + QK_NOPE_DIM] = (
                acc[:, j * QK_NOPE_DIM:(j + 1) * QK_NOPE_DIM] * qscale).astype(BF16)
            q_ref[:, base + QK_NOPE_DIM:base + QK_PAD_DIM] = jnp.where(keep, roped, 0.0).astype(BF16)


def _q_up(cq, wq_perm, cos4, sin4, qscale):
    T, qr = cq.shape
    n_tiles = wq_perm.shape[1] // (HEADS_PER_TILE * (QK_NOPE_DIM + QK_ROPE_DIM))
    tw = HEADS_PER_TILE * (QK_NOPE_DIM + QK_ROPE_DIM)
    to = HEADS_PER_TILE * QK_PAD_DIM
    tm = _tile(T, 1024)
    return pl.pallas_call(
        functools.partial(_q_up_kernel, qscale=qscale),
        grid=(T // tm, n_tiles),
        in_specs=[pl.BlockSpec((tm, qr), lambda i, j: (i, 0)),
                  pl.BlockSpec((qr, tw), lambda i, j: (0, j)),
                  pl.BlockSpec((tm, LANES), lambda i, j: (i, 0)),
                  pl.BlockSpec((tm, LANES), lambda i, j: (i, 0))],
        out_specs=pl.BlockSpec((tm, to), lambda i, j: (i, j)),
        out_shape=jax.ShapeDtypeStruct((T, n_tiles * to), BF16),
        compiler_params=_cparams(("parallel", "arbitrary"), 40),
        name="q_up",
    )(cq, wq_perm, cos4, sin4)


def _kv_up_kernel(c_ref, w_ref, kpe_ref, k_ref, v_ref):
    acc = jnp.dot(c_ref[...], w_ref[...], preferred_element_type=F32)
    for j in range(HEADS_PER_TILE):
        src = j * (QK_NOPE_DIM + V_HEAD_DIM)
        k_ref[:, j * QK_PAD_DIM:j * QK_PAD_DIM + QK_NOPE_DIM] = acc[:, src:src + QK_NOPE_DIM].astype(BF16)
        par = j % 2
        k_ref[:, j * QK_PAD_DIM + QK_NOPE_DIM:(j + 1) * QK_PAD_DIM] = kpe_ref[:, par * LANES:(par + 1) * LANES]
        v_ref[:, j * V_PAD_DIM:j * V_PAD_DIM + V_HEAD_DIM] = (
            acc[:, src + QK_NOPE_DIM:src + QK_NOPE_DIM + V_HEAD_DIM].astype(BF16))
        v_ref[:, j * V_PAD_DIM + V_HEAD_DIM:(j + 1) * V_PAD_DIM] = jnp.ones(
            (acc.shape[0], V_PAD_DIM - V_HEAD_DIM), BF16)


def _kv_up(ckv, w_kv, kpe2):
    T, kvr = ckv.shape
    tw = HEADS_PER_TILE * (QK_NOPE_DIM + V_HEAD_DIM)
    n_tiles = w_kv.shape[1] // tw
    tm = _tile(T, 1024)
    return pl.pallas_call(
        _kv_up_kernel,
        grid=(T // tm, n_tiles),
        in_specs=[pl.BlockSpec((tm, kvr), lambda i, j: (i, 0)),
                  pl.BlockSpec((kvr, tw), lambda i, j: (0, j)),
                  pl.BlockSpec((tm, 2 * LANES), lambda i, j: (i, 0))],
        out_specs=[pl.BlockSpec((tm, HEADS_PER_TILE * QK_PAD_DIM), lambda i, j: (i, j)),
                   pl.BlockSpec((tm, HEADS_PER_TILE * V_PAD_DIM), lambda i, j: (i, j))],
        out_shape=[jax.ShapeDtypeStruct((T, n_tiles * HEADS_PER_TILE * QK_PAD_DIM), BF16),
                   jax.ShapeDtypeStruct((T, n_tiles * HEADS_PER_TILE * V_PAD_DIM), BF16)],
        compiler_params=_cparams(("parallel", "arbitrary"), 40),
        name="kv_up",
    )(ckv, w_kv, kpe2)


def _lane_tile(x, reps):
    return jnp.concatenate([x] * reps, axis=1)


def _attn_kernel(q_ref, k_ref, v_ref, o_ref, m_s, acc_s, s0_s, s1_s, x0_s, x1_s, p0_s, p1_s, a0_s, a1_s,
                 *, tkv):
    n_kv = k_ref.shape[0] // tkv
    s_buf, x_buf, p_buf, a_buf = (s0_s, s1_s), (x0_s, x1_s), (p0_s, p1_s), (a0_s, a1_s)
    m_s[...] = jnp.full(m_s.shape, -jnp.inf, F32)
    acc_s[...] = jnp.zeros(acc_s.shape, F32)

    def scores(i, slot):
        off = pl.multiple_of(i * tkv, tkv)
        s = lax.dot_general(q_ref[...], k_ref[pl.ds(off, tkv), :], (((1,), (1,)), ((), ())),
                            preferred_element_type=F32)
        s_buf[slot][...] = s
        x_buf[slot][...] = jnp.broadcast_to(jnp.max(s, axis=1, keepdims=True), x_buf[slot].shape)

    def probs(slot):
        m_prev = m_s[...]
        m_new = jnp.maximum(m_prev, x_buf[slot][...])
        m_s[...] = m_new
        a_buf[slot][...] = jnp.exp2(m_prev - m_new)
        p_buf[slot][...] = jnp.exp2(s_buf[slot][...] - _lane_tile(m_new, tkv // LANES)).astype(BF16)

    def values(i, slot):
        off = pl.multiple_of(i * tkv, tkv)
        acc_s[...] = (_lane_tile(a_buf[slot][...], V_PAD_DIM // LANES) * acc_s[...]
                      + jnp.dot(p_buf[slot][...], v_ref[pl.ds(off, tkv), :], preferred_element_type=F32))

    def trip(i, slot):
        scores(i + 1, 1 - slot)
        values(i - 1, 1 - slot)
        probs(slot)

    scores(0, 0)
    scores(1, 1)
    probs(0)
    for i in range(1, n_kv - 1):
        trip(i, i % 2)
    values(n_kv - 2, 0)
    probs(1)
    values(n_kv - 1, 1)
    acc = acc_s[...]
    o_ref[...] = (acc[:, :V_HEAD_DIM] / acc[:, V_HEAD_DIM:]).astype(o_ref.dtype)


def _attention(q, k, v, B, S, H):
    T = B * S
    tq = _tile(S, ATTN_TQ)
    tkv = _tile(S, min(ATTN_TKV, S // 2))
    assert (S // tkv) % 2 == 0, "the key-chunk pipeline is unrolled in pairs"
    nq = S // tq
    return pl.pallas_call(
        functools.partial(_attn_kernel, tkv=tkv),
        grid=(B, H, nq),
        in_specs=[pl.BlockSpec((tq, QK_PAD_DIM), lambda b, h, i: (b * nq + i, h)),
                  pl.BlockSpec((S, QK_PAD_DIM), lambda b, h, i: (b, h)),
                  pl.BlockSpec((S, V_PAD_DIM), lambda b, h, i: (b, h))],
        out_specs=pl.BlockSpec((tq, V_HEAD_DIM), lambda b, h, i: (b * nq + i, h)),
        out_shape=jax.ShapeDtypeStruct((T, H * V_HEAD_DIM), BF16),
        scratch_shapes=[pltpu.VMEM((tq, LANES), F32), pltpu.VMEM((tq, V_PAD_DIM), F32),
                        pltpu.VMEM((tq, tkv), F32), pltpu.VMEM((tq, tkv), F32),
                        pltpu.VMEM((tq, LANES), F32), pltpu.VMEM((tq, LANES), F32),
                        pltpu.VMEM((tq, tkv), BF16), pltpu.VMEM((tq, tkv), BF16),
                        pltpu.VMEM((tq, LANES), F32), pltpu.VMEM((tq, LANES), F32)],
        compiler_params=_cparams(("parallel", "parallel", "arbitrary"), 56),
        name="mla_attention",
    )(q, k, v)


def _inproj_f_kernel(h_ref, w_ref, cs_ref, a_ref, b_ref, *, groups):
    acc = jnp.dot(h_ref[...], w_ref[...], preferred_element_type=F32)
    C = FOURIER_GROUP_DIM
    for g in range(groups):
        ab = jnp.dot(acc[:, g * C:(g + 1) * C].astype(BF16), cs_ref[...], preferred_element_type=F32)
        a_ref[:, g * C:(g + 1) * C] = ab[:, :C].astype(BF16)
        b_ref[:, g * C:(g + 1) * C] = ab[:, C:].astype(BF16)


def _inproj_f(hb, w_f, cs_tab):
    T, D = hb.shape
    fw = w_f.shape[1]
    tm = _tile(T, 1024)
    tn = _tile(fw, 512)
    return pl.pallas_call(
        functools.partial(_inproj_f_kernel, groups=tn // FOURIER_GROUP_DIM),
        grid=(T // tm, fw // tn),
        in_specs=[pl.BlockSpec((tm, D), lambda i, j: (i, 0)),
                  pl.BlockSpec((D, tn), lambda i, j: (0, j)),
                  pl.BlockSpec((FOURIER_GROUP_DIM, 2 * FOURIER_GROUP_DIM), lambda i, j: (0, 0))],
        out_specs=[pl.BlockSpec((tm, tn), lambda i, j: (i, j)),
                   pl.BlockSpec((tm, tn), lambda i, j: (i, j))],
        out_shape=[jax.ShapeDtypeStruct((T, fw), BF16), jax.ShapeDtypeStruct((T, fw), BF16)],
        compiler_params=_cparams(("parallel", "arbitrary"), 48),
        name="inproj_fourier",
    )(hb, w_f, cs_tab)


def _dft_gen_kernel(tac_ref, tas_ref, tbc_ref, tbs_ref, cs_ref, sn_ref):
    tbc = tbc_ref[...]
    tbs = tbs_ref[...]
    for aa in range(tac_ref.shape[0]):
        ca = tac_ref[aa:aa + 1, :]
        sa = tas_ref[aa:aa + 1, :]
        rows = slice(aa * DFT_ROW_SPLIT, (aa + 1) * DFT_ROW_SPLIT)
        cs_ref[rows, :] = (ca * tbc - sa * tbs).astype(BF16)
        sn_ref[rows, :] = (-(sa * tbc + ca * tbs)).astype(BF16)


def _dft_matrices(S, n_chan):
    na = S // 2 // DFT_ROW_SPLIT
    scale = 1.0 / math.sqrt(S * n_chan)
    col = jnp.arange(S // 2, dtype=jnp.int32)[None, :]
    period = S // DFT_ROW_SPLIT
    ang_a = (2.0 * math.pi / period) * ((jnp.arange(na, dtype=jnp.int32)[:, None] * col) % period).astype(F32)
    ang_b = (2.0 * math.pi / S) * ((jnp.arange(DFT_ROW_SPLIT, dtype=jnp.int32)[:, None] * col) % S).astype(F32)
    tac, tas = jnp.cos(ang_a), jnp.sin(ang_a)
    tbc, tbs = scale * jnp.cos(ang_b), scale * jnp.sin(ang_b)
    ta = min(8, na)
    assert na % ta == 0
    tc = _tile(S // 2, 2048)
    return pl.pallas_call(
        _dft_gen_kernel,
        grid=(na // ta, S // 2 // tc),
        in_specs=[pl.BlockSpec((ta, tc), lambda i, j: (i, j)),
                  pl.BlockSpec((ta, tc), lambda i, j: (i, j)),
                  pl.BlockSpec((DFT_ROW_SPLIT, tc), lambda i, j: (0, j)),
                  pl.BlockSpec((DFT_ROW_SPLIT, tc), lambda i, j: (0, j))],
        out_specs=[pl.BlockSpec((ta * DFT_ROW_SPLIT, tc), lambda i, j: (i, j)),
                   pl.BlockSpec((ta * DFT_ROW_SPLIT, tc), lambda i, j: (i, j))],
        out_shape=[jax.ShapeDtypeStruct((S // 2, S // 2), BF16), jax.ShapeDtypeStruct((S // 2, S // 2), BF16)],
        compiler_params=_cparams(("parallel", "parallel"), 40),
        name="dft_matrices",
    )(tac, tas, tbc, tbs)


def _dft_fold_kernel(a_ref, am_ref, an_ref, b_ref, bm_ref, bn_ref, ae_ref, bo_ref, alt_ref):
    tm = a_ref.shape[0]
    r = lax.broadcasted_iota(jnp.int32, (tm, tm), 0)
    c = lax.broadcasted_iota(jnp.int32, (tm, tm), 1)
    rev = jnp.where(r + c == tm, 1.0, 0.0).astype(BF16)
    has_next = (pl.program_id(1) > 0).astype(F32)
    first = (jnp.where(r + c == 0, 1.0, 0.0) * has_next).astype(BF16)

    def mirrored(m_ref, n_ref):
        return (jnp.dot(rev, m_ref[...], preferred_element_type=F32)
                + jnp.dot(first, n_ref[...], preferred_element_type=F32))

    ae_ref[...] = (a_ref[...].astype(F32) + mirrored(am_ref, an_ref)).astype(BF16)
    bo_ref[...] = (b_ref[...].astype(F32) - mirrored(bm_ref, bn_ref)).astype(BF16)

    @pl.when(pl.program_id(1) == 0)
    def _():
        alt_ref[...] = jnp.zeros(alt_ref.shape, F32)

    row = lax.broadcasted_iota(jnp.int32, a_ref.shape, 0)
    both = a_ref[...].astype(F32) + am_ref[...].astype(F32)
    alt_ref[0:1, :] += jnp.sum(jnp.where(row % 2 == 0, both, -both), axis=0, keepdims=True)


def _dft_fold(a, b, B, S):
    fw = a.shape[1]
    tm = _tile(S // 2, 256)
    nb = S // tm
    nh = nb // 2

    def own(bb, i):
        return (bb * nb + i, 0)

    def mirror(bb, i):
        return (bb * nb + nb - 1 - i, 0)

    def mirror_next(bb, i):
        return (bb * nb + jnp.minimum(nb - i, nb - 1), 0)

    spec = lambda f: pl.BlockSpec((tm, fw), f)
    return pl.pallas_call(
        _dft_fold_kernel,
        grid=(B, nh),
        in_specs=[spec(own), spec(mirror), spec(mirror_next), spec(own), spec(mirror), spec(mirror_next)],
        out_specs=[pl.BlockSpec((tm, fw), lambda bb, i: (bb * nh + i, 0)),
                   pl.BlockSpec((tm, fw), lambda bb, i: (bb * nh + i, 0)),
                   pl.BlockSpec((8, fw), lambda bb, i: (bb, 0))],
        out_shape=[jax.ShapeDtypeStruct((B * S // 2, fw), BF16),
                   jax.ShapeDtypeStruct((B * S // 2, fw), BF16),
                   jax.ShapeDtypeStruct((B * 8, fw), F32)],
        compiler_params=_cparams(("parallel", "arbitrary"), 40),
        name="dft_fold",
    )(a, a, a, b, b, b)


def _seq_dft_kernel(cs_ref, sn_ref, ae_ref, bo_ref, mid_ref, y_ref, d_ref, p_s, q_s, *, scale):
    kk = pl.program_id(3)

    @pl.when(kk == 0)
    def _():
        p_s[...] = jnp.zeros(p_s.shape, F32)
        q_s[...] = jnp.zeros(q_s.shape, F32)

    p_s[...] += jnp.dot(cs_ref[...], ae_ref[...], preferred_element_type=F32)
    q_s[...] += jnp.dot(sn_ref[...], bo_ref[...], preferred_element_type=F32)

    @pl.when(kk == pl.num_programs(3) - 1)
    def _():
        row = lax.broadcasted_iota(jnp.int32, p_s.shape, 0)
        pm = p_s[...] + jnp.where(row % 2 == 0, scale, -scale) * mid_ref[0:1, :].astype(F32)
        y_ref[...] = (pm + q_s[...]).astype(y_ref.dtype)
        d_ref[...] = (pm - q_s[...]).astype(d_ref.dtype)


def _seq_dft(cs, sn, ae, bo, a, B, S, scale):
    fw = a.shape[1]
    half = S // 2
    tm = _tile(half, 1024)
    tn = _tile(fw, 1024)
    tk = _tile(half, 1024)
    nm, nk = half // tm, half // tk
    mid_rows = 16
    return pl.pallas_call(
        functools.partial(_seq_dft_kernel, scale=scale),
        grid=(B, nm, fw // tn, nk),
        in_specs=[pl.BlockSpec((tm, tk), lambda bb, i, j, k: (i, k)),
                  pl.BlockSpec((tm, tk), lambda bb, i, j, k: (i, k)),
                  pl.BlockSpec((tk, tn), lambda bb, i, j, k: (bb * nk + k, j)),
                  pl.BlockSpec((tk, tn), lambda bb, i, j, k: (bb * nk + k, j)),
                  pl.BlockSpec((mid_rows, tn),
                               lambda bb, i, j, k: ((bb * S + half) // mid_rows, j))],
        out_specs=[pl.BlockSpec((tm, tn), lambda bb, i, j, k: (bb * nm + i, j)),
                   pl.BlockSpec((tm, tn), lambda bb, i, j, k: (bb * nm + i, j))],
        out_shape=[jax.ShapeDtypeStruct((B * half, fw), BF16),
                   jax.ShapeDtypeStruct((B * half, fw), BF16)],
        scratch_shapes=[pltpu.VMEM((tm, tn), F32), pltpu.VMEM((tm, tn), F32)],
        compiler_params=_cparams(("parallel", "parallel", "parallel", "arbitrary"), 48),
        name="seq_dft",
    )(cs, sn, ae, bo, a)


def _dft_mirror_kernel(yh_ref, dm_ref, dn_ref, alt_ref, y_ref, *, scale, nh):
    j = pl.program_id(1)

    @pl.when(j < nh)
    def _():
        y_ref[...] = yh_ref[...]

    @pl.when(j >= nh)
    def _():
        tm = dm_ref.shape[0]
        r = lax.broadcasted_iota(jnp.int32, (tm, tm), 0)
        c = lax.broadcasted_iota(jnp.int32, (tm, tm), 1)
        rev = jnp.where(r + c == tm, 1.0, 0.0).astype(BF16)
        first = (jnp.where(r + c == 0, 1.0, 0.0) * (j > nh).astype(F32)).astype(BF16)
        out = (jnp.dot(rev, dm_ref[...], preferred_element_type=F32)
               + jnp.dot(first, dn_ref[...], preferred_element_type=F32))
        row = lax.broadcasted_iota(jnp.int32, out.shape, 0)
        nyquist = jnp.where(row == 0, scale * (j == nh).astype(F32), 0.0) * alt_ref[0:1, :]
        y_ref[...] = (out + nyquist).astype(y_ref.dtype)


def _dft_mirror(y_half, d, alt, B, S, scale):
    fw = d.shape[1]
    half = S // 2
    tm = _tile(half, 256)
    nh = half // tm

    def mirror(bb, j):
        return (bb * nh + nh - 1 - jnp.maximum(j - nh, 0), 0)

    def mirror_next(bb, j):
        return (bb * nh + jnp.minimum(nh - jnp.maximum(j - nh, 0), nh - 1), 0)

    return pl.pallas_call(
        functools.partial(_dft_mirror_kernel, scale=scale, nh=nh),
        grid=(B, 2 * nh),
        in_specs=[pl.BlockSpec((tm, fw), lambda bb, j: (bb * nh + jnp.minimum(j, nh - 1), 0)),
                  pl.BlockSpec((tm, fw), mirror),
                  pl.BlockSpec((tm, fw), mirror_next),
                  pl.BlockSpec((8, fw), lambda bb, j: (bb, 0))],
        out_specs=pl.BlockSpec((tm, fw), lambda bb, j: (bb * 2 * nh + j, 0)),
        out_shape=jax.ShapeDtypeStruct((B * S, fw), d.dtype),
        compiler_params=_cparams(("parallel", "arbitrary"), 40),
        name="dft_mirror",
    )(y_half, d, d, alt)


def _wo_kernel(ym_ref, yf_ref, gm_ref, gf_ref, w_ref, x_ref, mu_ref, rs_ref, lg_ref, lb_ref,
               pre_ref, mix_s, *, alpha, mw):
    @pl.when(pl.program_id(1) == 0)
    def _():
        ym = ym_ref[...].astype(F32)
        mix_s[:, :mw] = (ym * lax.rsqrt(jnp.mean(ym * ym, axis=-1, keepdims=True) + RMS_EPS)
                         * gm_ref[...]).astype(BF16)
        yf = yf_ref[...].astype(F32)
        mix_s[:, mw:] = (yf * lax.rsqrt(jnp.mean(yf * yf, axis=-1, keepdims=True) + RMS_EPS)
                         * gf_ref[...]).astype(BF16)

    h = (x_ref[...] - mu_ref[...]) * rs_ref[...] * lg_ref[...] + lb_ref[...]
    pre_ref[...] = alpha * h + jnp.dot(mix_s[...], w_ref[...], preferred_element_type=F32)


def _wo(ym, yf, gm, gf, w_o, x2, mu, rs, ln_g, ln_b, alpha):
    T, mw = ym.shape
    fw = yf.shape[1]
    D = w_o.shape[1]
    tm = _tile(T, 1024)
    tn = _tile(D, 512)
    return pl.pallas_call(
        functools.partial(_wo_kernel, alpha=alpha, mw=mw),
        grid=(T // tm, D // tn),
        in_specs=[pl.BlockSpec((tm, mw), lambda i, j: (i, 0)),
                  pl.BlockSpec((tm, fw), lambda i, j: (i, 0)),
                  pl.BlockSpec((1, mw), lambda i, j: (0, 0)),
                  pl.BlockSpec((1, fw), lambda i, j: (0, 0)),
                  pl.BlockSpec((mw + fw, tn), lambda i, j: (0, j)),
                  pl.BlockSpec((tm, tn), lambda i, j: (i, j)),
                  pl.BlockSpec((tm, 1), lambda i, j: (i, 0)),
                  pl.BlockSpec((tm, 1), lambda i, j: (i, 0)),
                  pl.BlockSpec((1, tn), lambda i, j: (0, j)),
                  pl.BlockSpec((1, tn), lambda i, j: (0, j))],
        out_specs=pl.BlockSpec((tm, tn), lambda i, j: (i, j)),
        out_shape=jax.ShapeDtypeStruct((T, D), F32),
        scratch_shapes=[pltpu.VMEM((tm, mw + fw), BF16)],
        compiler_params=_cparams(("parallel", "arbitrary"), 56),
        name="w_o_residual",
    )(ym, yf, gm.reshape(1, mw), gf.reshape(1, fw), w_o, x2, mu, rs,
      ln_g.reshape(1, D), ln_b.reshape(1, D))


def _pack_rounded_pair(lo, hi):
    return (pltpu.bitcast(lo, jnp.uint32) >> 16) | (pltpu.bitcast(hi, jnp.uint32) & jnp.uint32(0xFFFF0000))


def _pack_bf16_pair(lo, hi):
    return _pack_rounded_pair(lo.astype(BF16).astype(F32), hi.astype(BF16).astype(F32))


def _unpack_bf16_pair(w):
    lo = pltpu.bitcast(w << 16, F32).astype(BF16)
    hi = pltpu.bitcast(w & jnp.uint32(0xFFFF0000), F32).astype(BF16)
    return lo, hi


def _ln1_router_kernel(pre_ref, g_ref, b_ref, rwh_ref, rwl_ref, rb_ref, x1_ref, x1p_ref, idx_ref, gate_ref):
    x = pre_ref[...]
    mu = jnp.mean(x, axis=-1, keepdims=True)
    xc = x - mu
    var = jnp.mean(xc * xc, axis=-1, keepdims=True)
    x1 = xc * lax.rsqrt(var + LN_EPS) * g_ref[...] + b_ref[...]
    x1_ref[...] = x1
    half = x1.shape[1] // 2
    x_hi = x1.astype(BF16)
    x_hi_f = x_hi.astype(F32)
    x1p_ref[...] = _pack_rounded_pair(x_hi_f[:, :half], x_hi_f[:, half:])
    x_lo = (x1 - x_hi_f).astype(BF16)
    nt = (((1,), (1,)), ((), ()))
    logits = (lax.dot_general(rwh_ref[...], x_hi, nt, preferred_element_type=F32)
              + lax.dot_general(rwh_ref[...], x_lo, nt, preferred_element_type=F32)
              + lax.dot_general(rwl_ref[...], x_hi, nt, preferred_element_type=F32)) + rb_ref[...]
    n_e = logits.shape[0]
    eidx = lax.broadcasted_iota(jnp.int32, logits.shape, 0)
    vals, idxs = [], []
    for _ in range(TOP_K):
        m = jnp.max(logits, axis=0, keepdims=True)
        sel = jnp.min(jnp.where(logits == m, eidx, n_e), axis=0, keepdims=True)
        logits = jnp.where(eidx == sel, -jnp.inf, logits)
        vals.append(m)
        idxs.append(sel)
    exps = [jnp.exp(v - vals[0]) for v in vals]
    denom = exps[0] + exps[1] + exps[2] + exps[3]
    for kk in range(TOP_K):
        idx_ref[kk:kk + 1, :] = idxs[kk]
        gate_ref[kk:kk + 1, :] = exps[kk] / denom


def _ln1_router(pre, g, b, router_w, router_b):
    T, D = pre.shape
    E = router_w.shape[1]
    tm = _tile(T, 256)
    rw_t = router_w.T
    rw_hi = rw_t.astype(BF16)
    rw_lo = (rw_t - rw_hi.astype(F32)).astype(BF16)
    return pl.pallas_call(
        _ln1_router_kernel,
        grid=(T // tm,),
        in_specs=[pl.BlockSpec((tm, D), lambda i: (i, 0)),
                  pl.BlockSpec((1, D), lambda i: (0, 0)),
                  pl.BlockSpec((1, D), lambda i: (0, 0)),
                  pl.BlockSpec((E, D), lambda i: (0, 0)),
                  pl.BlockSpec((E, D), lambda i: (0, 0)),
                  pl.BlockSpec((E, 1), lambda i: (0, 0))],
        out_specs=[pl.BlockSpec((tm, D), lambda i: (i, 0)),
                   pl.BlockSpec((tm, D // 2), lambda i: (i, 0)),
                   pl.BlockSpec((TOP_K, tm), lambda i: (0, i)),
                   pl.BlockSpec((TOP_K, tm), lambda i: (0, i))],
        out_shape=[jax.ShapeDtypeStruct((T, D), F32),
                   jax.ShapeDtypeStruct((T, D // 2), jnp.uint32),
                   jax.ShapeDtypeStruct((TOP_K, T), jnp.int32),
                   jax.ShapeDtypeStruct((TOP_K, T), F32)],
        compiler_params=_cparams(("parallel",), 48),
        name="ln1_router",
    )(pre, g.reshape(1, D), b.reshape(1, D), rw_hi, rw_lo, router_b.reshape(E, 1))


def _row_copy(src_hbm, dst_ref, src_row, dst_row, sem):
    return pltpu.make_async_copy(src_hbm.at[pl.ds(src_row, 1)], dst_ref.at[pl.ds(dst_row, 1)], sem)


def _dispatch_kernel(dest_ref, pad_ref, npad_ref, nv_ref, x_ref, o_hbm, zero_s, sem, zsem, *, n_steps):
    b = pl.program_id(0)
    tt = x_ref.shape[0]

    def wait_rows(n):
        pltpu.make_async_copy(o_hbm.at[pl.ds(0, n)], o_hbm.at[pl.ds(0, n)], sem).wait()

    @pl.when(b < n_steps)
    def _():
        def issue(r, c):
            for kk in range(TOP_K):
                _row_copy(x_ref, o_hbm, r, dest_ref[(b * tt + r) * TOP_K + kk], sem).start()
            return c
        lax.fori_loop(0, tt, issue, 0, unroll=2)
        wait_rows(tt * TOP_K)

    @pl.when(b == n_steps)
    def _():
        n_pad = npad_ref[0]

        def issue_pad(i, c):
            _row_copy(x_ref, o_hbm, 0, pad_ref[i], sem).start()
            return c
        lax.fori_loop(0, n_pad, issue_pad, 0)

        def wait_pad(i, c):
            wait_rows(1)
            return c
        lax.fori_loop(0, n_pad, wait_pad, 0)

        zero_s[...] = jnp.zeros(zero_s.shape, zero_s.dtype)
        n_sub = MOE_CHUNK // MOE_SUB

        def zero_copy(i):
            return pltpu.make_async_copy(zero_s, o_hbm.at[pl.ds(pl.multiple_of(i * MOE_SUB, MOE_SUB), MOE_SUB)], zsem)

        def empty(i):
            return (i % n_sub) * MOE_SUB >= nv_ref[i // n_sub]

        def issue_zero(i, c):
            @pl.when(empty(i))
            def _():
                zero_copy(i).start()
            return c
        lax.fori_loop(0, nv_ref.shape[0] * n_sub, issue_zero, 0)

        def wait_zero(i, c):
            @pl.when(empty(i))
            def _():
                zero_copy(i).wait()
            return c
        lax.fori_loop(0, nv_ref.shape[0] * n_sub, wait_zero, 0)


def _dispatch(dest, pad_slots, n_pad, chunk_nv, x1p):
    T, W = x1p.shape
    tt = _tile(T, DISPATCH_TOKENS)
    n_steps = T // tt
    grid_spec = pltpu.PrefetchScalarGridSpec(
        num_scalar_prefetch=4,
        grid=(n_steps + 1,),
        in_specs=[pl.BlockSpec((tt, W), lambda b, d, p, n, v: (jnp.minimum(b, n_steps - 1), 0))],
        out_specs=pl.BlockSpec(memory_space=pl.ANY),
        scratch_shapes=[pltpu.VMEM((MOE_SUB, W), x1p.dtype),
                        pltpu.SemaphoreType.DMA(()), pltpu.SemaphoreType.DMA(())],
    )
    return pl.pallas_call(
        functools.partial(_dispatch_kernel, n_steps=n_steps),
        grid_spec=grid_spec,
        out_shape=jax.ShapeDtypeStruct((chunk_nv.shape[0] * MOE_CHUNK, W), x1p.dtype),
        compiler_params=_cparams(("arbitrary",), 32),
        name="moe_dispatch",
    )(dest, pad_slots, n_pad, chunk_nv, x1p)


def _expert_up_kernel(ce_ref, nv_ref, nu_ref, x_ref, wg_ref, wu_ref, bg_ref, bu_ref, h_ref):
    c = pl.program_id(0)
    nv = nv_ref[c]
    brow = ce_ref[c] * pl.num_programs(1) + pl.program_id(1)

    n_sub = MOE_CHUNK // MOE_SUB
    n_act = (nv + MOE_SUB - 1) // MOE_SUB
    for k in range(n_sub + 1):
        @pl.when(n_act == k)
        def _(k=k):
            if k > 0:
                x_lo, x_hi = _unpack_bf16_pair(x_ref[:k * MOE_SUB, :])
                half = x_lo.shape[1]

                def proj(w_ref):
                    return (jnp.dot(x_lo, w_ref[:half, :].astype(BF16), preferred_element_type=F32)
                            + jnp.dot(x_hi, w_ref[half:, :].astype(BF16), preferred_element_type=F32))
                hg = jnp.minimum(proj(wg_ref) + bg_ref[pl.ds(brow, 1), :], SWIGLU_LIMIT)
                hu = jnp.clip(proj(wu_ref) + bu_ref[pl.ds(brow, 1), :], -SWIGLU_LIMIT, SWIGLU_LIMIT)
                act = (hu + 1.0) * (hg * jax.nn.sigmoid(SWIGLU_ALPHA * hg))
                h_ref[:k * MOE_SUB, :] = act.astype(h_ref.dtype)
            if k < n_sub:
                h_ref[k * MOE_SUB:, :] = jnp.zeros(((n_sub - k) * MOE_SUB, h_ref.shape[1]), h_ref.dtype)


def _expert_up(chunk_e, chunk_nv, n_used, xs, w_gate, w_up, b_gate, b_up, n_chunks):
    E, D, F = w_gate.shape
    tf = _tile(F, 256)
    nf = F // tf

    def used(c, nu):
        return jnp.minimum(c, nu[0] - 1)

    def jeff(c, j, nu):
        return jnp.where(c < nu[0], j, nf - 1)

    grid_spec = pltpu.PrefetchScalarGridSpec(
        num_scalar_prefetch=3,
        grid=(n_chunks, nf),
        in_specs=[pl.BlockSpec((MOE_CHUNK, D // 2), lambda c, j, ce, nv, nu: (used(c, nu), 0)),
                  pl.BlockSpec((None, D, tf), lambda c, j, ce, nv, nu: (ce[c], 0, jeff(c, j, nu))),
                  pl.BlockSpec((None, D, tf), lambda c, j, ce, nv, nu: (ce[c], 0, jeff(c, j, nu))),
                  pl.BlockSpec((E * nf, tf), lambda c, j, ce, nv, nu: (0, 0)),
                  pl.BlockSpec((E * nf, tf), lambda c, j, ce, nv, nu: (0, 0))],
        out_specs=pl.BlockSpec((MOE_CHUNK, tf), lambda c, j, ce, nv, nu: (c, j)),
    )
    return pl.pallas_call(
        _expert_up_kernel,
        grid_spec=grid_spec,
        out_shape=jax.ShapeDtypeStruct((n_chunks * MOE_CHUNK, F), BF16),
        compiler_params=_cparams(("arbitrary", "arbitrary"), 60),
        name="expert_gate_up",
    )(chunk_e, chunk_nv, n_used, xs, w_gate, w_up, b_gate.reshape(E * nf, tf), b_up.reshape(E * nf, tf))


def _expert_down_kernel(ce_ref, nv_ref, nu_ref, h_ref, wd_ref, bd_ref, y_ref):
    c = pl.program_id(0)
    nv = nv_ref[c]
    brow = ce_ref[c] * pl.num_programs(1) + pl.program_id(1)

    n_sub = MOE_CHUNK // MOE_SUB
    n_act = (nv + MOE_SUB - 1) // MOE_SUB
    for k in range(n_sub + 1):
        @pl.when(n_act == k)
        def _(k=k):
            if k > 0:
                out = jnp.dot(h_ref[:k * MOE_SUB, :], wd_ref[...].astype(BF16),
                              preferred_element_type=F32) + bd_ref[pl.ds(brow, 1), :]
                hw = out.shape[1] // 2
                y_ref[:k * MOE_SUB, :] = _pack_bf16_pair(out[:, :hw], out[:, hw:])
            if k < n_sub:
                y_ref[k * MOE_SUB:, :] = jnp.zeros(((n_sub - k) * MOE_SUB, y_ref.shape[1]), y_ref.dtype)


def _expert_down(chunk_e, chunk_nv, n_used, hmid, w_down, b_down, n_chunks):
    E, F, D = w_down.shape
    tn = _tile(D, MOE_DOWN_TN)
    nn = D // tn

    def used(c, nu):
        return jnp.minimum(c, nu[0] - 1)

    def jeff(c, j, nu):
        return jnp.where(c < nu[0], j, nn - 1)

    grid_spec = pltpu.PrefetchScalarGridSpec(
        num_scalar_prefetch=3,
        grid=(n_chunks, nn),
        in_specs=[pl.BlockSpec((MOE_CHUNK, F), lambda c, j, ce, nv, nu: (used(c, nu), 0)),
                  pl.BlockSpec((None, F, tn), lambda c, j, ce, nv, nu: (ce[c], 0, jeff(c, j, nu))),
                  pl.BlockSpec((E * nn, tn), lambda c, j, ce, nv, nu: (0, 0))],
        out_specs=pl.BlockSpec((MOE_CHUNK, tn // 2), lambda c, j, ce, nv, nu: (c, j)),
    )
    return pl.pallas_call(
        _expert_down_kernel,
        grid_spec=grid_spec,
        out_shape=jax.ShapeDtypeStruct((n_chunks * MOE_CHUNK, D // 2), jnp.uint32),
        compiler_params=_cparams(("arbitrary", "arbitrary"), 56),
        name="expert_down",
    )(chunk_e, chunk_nv, n_used, hmid, w_down, b_down.reshape(E * nn, tn))


def _combine_kernel(dest_ref, y_hbm, x1_ref, gate_ref, g_ref, b_ref, o_ref, *scratch, alpha, tn):
    i = pl.program_id(0)
    n = pl.num_programs(0)
    nb = COMBINE_BUFS
    tt = o_ref.shape[0] // nb
    bufs, sem = scratch[:nb], scratch[nb]

    def issue_tile(tile, slot):
        base = jnp.minimum(tile, nb * n - 1) * tt
        for r in range(tt):
            for kk in range(TOP_K):
                _row_copy(y_hbm, bufs[slot].at[kk], dest_ref[(base + r) * TOP_K + kk], r, sem.at[slot]).start()

    def wait_tile(slot):
        for kk in range(TOP_K):
            pltpu.make_async_copy(y_hbm.at[pl.ds(0, tt)], bufs[slot].at[kk], sem.at[slot]).wait()

    def finish_tile(slot):
        rows = slice(slot * tt, (slot + 1) * tt)
        gates = gate_ref[rows, :]
        y_lo = y_hi = None
        for kk in range(TOP_K):
            w = bufs[slot][kk]
            g_k = gates[:, kk:kk + 1]
            lo = g_k * pltpu.bitcast(w << 16, F32)
            hi = g_k * pltpu.bitcast(w & jnp.uint32(0xFFFF0000), F32)
            y_lo = lo if y_lo is None else y_lo + lo
            y_hi = hi if y_hi is None else y_hi + hi
        hw = tn // 2
        pieces = []
        for j in range(x1_ref.shape[1] // tn):
            pieces.append((slice(j * tn, j * tn + hw), y_lo[:, j * hw:(j + 1) * hw]))
            pieces.append((slice(j * tn + hw, (j + 1) * tn), y_hi[:, j * hw:(j + 1) * hw]))
        zs = [alpha * x1_ref[rows, cols] + y for cols, y in pieces]
        d = x1_ref.shape[1]
        mu = sum(jnp.sum(z, axis=-1, keepdims=True) for z in zs) / d
        zcs = [z - mu for z in zs]
        var = sum(jnp.sum(zc * zc, axis=-1, keepdims=True) for zc in zcs) / d
        rs = lax.rsqrt(var + LN_EPS)
        for (cols, _), zc in zip(pieces, zcs):
            o_ref[rows, cols] = zc * rs * g_ref[:, cols] + b_ref[:, cols]

    @pl.when(i == 0)
    def _():
        for t in range(COMBINE_AHEAD):
            issue_tile(t, t)

    for t in range(nb):
        wait_tile(t)
        issue_tile(nb * i + t + COMBINE_AHEAD, (t + COMBINE_AHEAD) % nb)
        finish_tile(t)

    @pl.when(i == n - 1)
    def _():
        for t in range(COMBINE_AHEAD):
            wait_tile(t)


def _combine(dest, yslots, x1, gates_tk, g, b, alpha):
    T, D = x1.shape
    nb = COMBINE_BUFS
    tt = _tile(T // nb, COMBINE_TOKENS)
    grid_spec = pltpu.PrefetchScalarGridSpec(
        num_scalar_prefetch=1,
        grid=(T // (nb * tt),),
        in_specs=[pl.BlockSpec(memory_space=pl.ANY),
                  pl.BlockSpec((nb * tt, D), lambda i, d: (i, 0)),
                  pl.BlockSpec((nb * tt, TOP_K), lambda i, d: (i, 0)),
                  pl.BlockSpec((1, D), lambda i, d: (0, 0)),
                  pl.BlockSpec((1, D), lambda i, d: (0, 0))],
        out_specs=pl.BlockSpec((nb * tt, D), lambda i, d: (i, 0)),
        scratch_shapes=[pltpu.VMEM((TOP_K, tt, D // 2), jnp.uint32) for _ in range(nb)]
        + [pltpu.SemaphoreType.DMA((nb,))],
    )
    return pl.pallas_call(
        functools.partial(_combine_kernel, alpha=alpha, tn=_tile(D, MOE_DOWN_TN)),
        grid_spec=grid_spec,
        out_shape=jax.ShapeDtypeStruct((T, D), F32),
        compiler_params=_cparams(("arbitrary",), 48),
        name="moe_combine_ln2",
    )(dest, yslots, x1, gates_tk, g.reshape(1, D), b.reshape(1, D))


def _routing_tables(top_idx, n_experts, n_chunks):
    T = top_idx.shape[1]
    M = T * TOP_K
    flat_e = top_idx.T.reshape(M)
    onehot = (flat_e[:, None] == jnp.arange(n_experts, dtype=jnp.int32)[None, :]).astype(jnp.int32)
    csum = jnp.cumsum(onehot, axis=0)
    rank = jnp.sum(csum * onehot, axis=1) - 1
    counts = csum[-1]
    chunks_e = (counts + MOE_CHUNK - 1) // MOE_CHUNK
    chunk_end = jnp.cumsum(chunks_e)
    chunk_start = chunk_end - chunks_e
    n_used = chunk_end[-1]
    dest = chunk_start[flat_e] * MOE_CHUNK + rank
    P = n_chunks * MOE_CHUNK
    pad_idx = counts[:, None] + jnp.arange(MOE_SUB, dtype=jnp.int32)[None, :]
    padded = (counts + MOE_SUB - 1) // MOE_SUB * MOE_SUB
    pad_slots = jnp.where(pad_idx < padded[:, None], chunk_start[:, None] * MOE_CHUNK + pad_idx, P)
    pad_slots = jnp.sort(pad_slots.reshape(-1)).astype(jnp.int32)
    n_pad = jnp.sum(padded - counts).astype(jnp.int32).reshape(1)
    cid = jnp.arange(n_chunks, dtype=jnp.int32)
    chunk_e = jnp.minimum(jnp.searchsorted(chunk_end, cid, side='right'), n_experts - 1).astype(jnp.int32)
    last_e = chunk_e[jnp.maximum(n_used - 1, 0)]
    chunk_e = jnp.where(cid < n_used, chunk_e, last_e)
    chunk_nv = jnp.where(cid < n_used,
                         jnp.clip(counts[chunk_e] - (cid - chunk_start[chunk_e]) * MOE_CHUNK, 0, MOE_CHUNK),
                         0).astype(jnp.int32)
    return (dest.astype(jnp.int32), pad_slots, n_pad, chunk_e, chunk_nv,
            n_used.astype(jnp.int32).reshape(1))


def kernel(x, positions, ln_in_g, ln_in_b, w_in, q_a_norm_g, w_q_b, kv_a_norm_g, w_kv_b, mla_out_norm_g, fourier_out_norm_g, w_o, ln1_g, ln1_b, router_w, router_b, w_gate, b_gate, w_up, b_up, w_down, b_down, ln2_g, ln2_b):
    B, S, D = x.shape
    T = B * S
    depth = w_in.shape[0]
    assert depth == 1, "single-layer trunk only"
    qr = q_a_norm_g.shape[1]
    kvr = kv_a_norm_g.shape[1]
    H = w_q_b.shape[2] // (QK_NOPE_DIM + QK_ROPE_DIM)
    fw = fourier_out_norm_g.shape[1]
    E = router_w.shape[2]
    assert H % HEADS_PER_TILE == 0 and (qr + kvr) % LANES == 0 and S % (2 * DFT_ROW_SPLIT) == 0
    alpha = (2.0 * depth) ** 0.25

    inv_freq = ROPE_THETA ** (-jnp.arange(0, QK_ROPE_DIM, 2, dtype=F32) / QK_ROPE_DIM)
    ang = positions.astype(F32)[..., None] * inv_freq
    cos4 = jnp.tile(jnp.cos(ang), (1, 1, 2 * LANES // QK_ROPE_DIM)).reshape(T, LANES)
    sin4 = jnp.tile(jnp.sin(ang), (1, 1, 2 * LANES // QK_ROPE_DIM)).reshape(T, LANES)

    rope_end = qr + kvr + QK_ROPE_DIM
    w_a = w_in[0, :, :qr + kvr + LANES].astype(BF16)
    w_f = w_in[0, :, rope_end:].astype(BF16)
    wq = w_q_b[0].reshape(qr, H // HEADS_PER_TILE, HEADS_PER_TILE, QK_NOPE_DIM + QK_ROPE_DIM)
    wq_perm = jnp.concatenate(
        [wq[..., :QK_NOPE_DIM].reshape(qr, H // HEADS_PER_TILE, HEADS_PER_TILE * QK_NOPE_DIM),
         wq[..., QK_NOPE_DIM:].reshape(qr, H // HEADS_PER_TILE, HEADS_PER_TILE * QK_ROPE_DIM)],
        axis=-1).reshape(qr, H * (QK_NOPE_DIM + QK_ROPE_DIM)).astype(BF16)
    w_kv = w_kv_b[0].astype(BF16)
    w_o_b = w_o[0].astype(BF16)
    ch = jnp.arange(FOURIER_GROUP_DIM, dtype=jnp.int32)
    ang_c = (2.0 * math.pi / FOURIER_GROUP_DIM) * ((ch[:, None] * ch[None, :]) % FOURIER_GROUP_DIM).astype(F32)
    cs_tab = jnp.concatenate([jnp.cos(ang_c), jnp.sin(ang_c)], axis=1).astype(BF16)

    x2 = x.reshape(T, D)
    hb, mu, rs = _ln_in(x2, ln_in_g, ln_in_b)

    cq, ckv, kpe2 = _inproj_a(hb, w_a, q_a_norm_g[0], kv_a_norm_g[0], cos4, sin4)
    qscale = (QK_NOPE_DIM + QK_ROPE_DIM) ** -0.5 * LOG2E
    q = _q_up(cq, wq_perm, cos4, sin4, qscale)
    k, v = _kv_up(ckv, w_kv, kpe2)
    y_mla = _attention(q, k, v, B, S, H)

    fa, fb = _inproj_f(hb, w_f, cs_tab)
    cs_mat, sn_mat = _dft_matrices(S, FOURIER_GROUP_DIM)
    fae, fbo, falt = _dft_fold(fa, fb, B, S)
    dft_scale = 1.0 / math.sqrt(S * FOURIER_GROUP_DIM)
    y_half, y_diff = _seq_dft(cs_mat, sn_mat, fae, fbo, fa, B, S, dft_scale)
    y_f = _dft_mirror(y_half, y_diff, falt, B, S, dft_scale)

    pre = _wo(y_mla, y_f, mla_out_norm_g[0], fourier_out_norm_g[0], w_o_b, x2, mu, rs,
              ln_in_g, ln_in_b, alpha)
    x1, x1p, top_idx, gates = _ln1_router(pre, ln1_g[0], ln1_b[0], router_w[0], router_b[0])

    n_chunks = (T * TOP_K + E * (MOE_CHUNK - 1)) // MOE_CHUNK
    dest, pad_slots, n_pad, chunk_e, chunk_nv, n_used = _routing_tables(top_idx, E, n_chunks)
    xs = _dispatch(dest, pad_slots, n_pad, chunk_nv, x1p)
    hmid = _expert_up(chunk_e, chunk_nv, n_used, xs, w_gate[0], w_up[0], b_gate[0], b_up[0], n_chunks)
    yslots = _expert_down(chunk_e, chunk_nv, n_used, hmid, w_down[0], b_down[0], n_chunks)
    out = _combine(dest, yslots, x1, gates.T, ln2_g[0], ln2_b[0], alpha)
    return out.reshape(B, S, D)
```

```python
import functools
import math

import jax
import jax.numpy as jnp
from jax import lax
from jax.experimental import pallas as pl
from jax.experimental.pallas import tpu as pltpu

F32 = jnp.float32
BF16 = jnp.bfloat16

V_HEAD_DIM = 128
QK_NOPE_DIM = 128
QK_ROPE_DIM = 64
QK_PAD_DIM = 256
V_PAD_DIM = 256
ROPE_THETA = 10000.0
FOURIER_GROUP_DIM = 128
TOP_K = 4
SWIGLU_LIMIT = 7.0
SWIGLU_ALPHA = 1.702
LN_EPS = 1e-5
RMS_EPS = 1e-6
LOG2E = 1.4426950408889634

LANES = 128
V7X_VMEM_BYTES = 64 * 1024 * 1024
VMEM_COMPILER_RESERVE = 4 * 1024 * 1024
HEADS_PER_TILE = 4
DFT_ROW_SPLIT = 64
ATTN_TQ = 1024
ATTN_TKV = 2048
ATTN_SCORE_BUFS = 3

MOE_CHUNK = 1280
MOE_SUB = 128
DISPATCH_TOKENS = 512
COMBINE_TOKENS = 64
COMBINE_BUFS = 4
COMBINE_AHEAD = 2
MOE_DOWN_TN = 1024


def _cparams(semantics, vmem_mb):
    return pltpu.CompilerParams(dimension_semantics=semantics,
                                vmem_limit_bytes=min(vmem_mb * 1024 * 1024,
                                                     V7X_VMEM_BYTES - VMEM_COMPILER_RESERVE))


def _tile(dim, pref):
    t = min(dim, pref)
    while dim % t:
        t //= 2
    return t


def _ln_in_kernel(x_ref, g_ref, b_ref, hb_ref, mu_ref, rs_ref):
    x = x_ref[...]
    mu = jnp.mean(x, axis=-1, keepdims=True)
    xc = x - mu
    var = jnp.mean(xc * xc, axis=-1, keepdims=True)
    rs = lax.rsqrt(var + LN_EPS)
    hb_ref[...] = (xc * rs * g_ref[...] + b_ref[...]).astype(BF16)
    mu_ref[...] = mu
    rs_ref[...] = rs


def _ln_in(x2, g, b):
    T, D = x2.shape
    tm = _tile(T, 256)
    return pl.pallas_call(
        _ln_in_kernel,
        grid=(T // tm,),
        in_specs=[pl.BlockSpec((tm, D), lambda i: (i, 0)),
                  pl.BlockSpec((1, D), lambda i: (0, 0)),
                  pl.BlockSpec((1, D), lambda i: (0, 0))],
        out_specs=[pl.BlockSpec((tm, D), lambda i: (i, 0)),
                   pl.BlockSpec((tm, 1), lambda i: (i, 0)),
                   pl.BlockSpec((tm, 1), lambda i: (i, 0))],
        out_shape=[jax.ShapeDtypeStruct((T, D), BF16),
                   jax.ShapeDtypeStruct((T, 1), F32),
                   jax.ShapeDtypeStruct((T, 1), F32)],
        compiler_params=_cparams(("parallel",), 40),
        name="ln_in",
    )(x2, g.reshape(1, D), b.reshape(1, D))


def _rope128(p, cos4, sin4):
    lane = lax.broadcasted_iota(jnp.int32, p.shape, 1)
    first_half = (lane % QK_ROPE_DIM) < (QK_ROPE_DIM // 2)
    rot = jnp.where(first_half, -pltpu.roll(p, LANES - QK_ROPE_DIM // 2, 1),
                    pltpu.roll(p, QK_ROPE_DIM // 2, 1))
    return p * cos4 + rot * sin4


def _inproj_a_kernel(h_ref, w_ref, gq_ref, gkv_ref, cos_ref, sin_ref,
                     cq_ref, ckv_ref, kpe_ref, *, qr, kvr):
    acc = jnp.dot(h_ref[...], w_ref[...], preferred_element_type=F32)
    cq = acc[:, :qr]
    cq_ref[...] = (cq * lax.rsqrt(jnp.mean(cq * cq, axis=-1, keepdims=True) + RMS_EPS)
                   * gq_ref[...]).astype(BF16)
    ckv = acc[:, qr:qr + kvr]
    ckv_ref[...] = (ckv * lax.rsqrt(jnp.mean(ckv * ckv, axis=-1, keepdims=True) + RMS_EPS)
                    * gkv_ref[...]).astype(BF16)
    roped = _rope128(acc[:, qr + kvr:qr + kvr + LANES], cos_ref[...], sin_ref[...])
    lane = lax.broadcasted_iota(jnp.int32, roped.shape, 1)
    even = jnp.where(lane < QK_ROPE_DIM, roped, 0.0)
    kpe_ref[:, :LANES] = even.astype(BF16)
    kpe_ref[:, LANES:] = pltpu.roll(even, QK_ROPE_DIM, 1).astype(BF16)


def _inproj_a(hb, w_a, gq, gkv, cos4, sin4):
    T, D = hb.shape
    qr, kvr = gq.shape[0], gkv.shape[0]
    wa = w_a.shape[1]
    tm = _tile(T, 512)
    return pl.pallas_call(
        functools.partial(_inproj_a_kernel, qr=qr, kvr=kvr),
        grid=(T // tm,),
        in_specs=[pl.BlockSpec((tm, D), lambda i: (i, 0)),
                  pl.BlockSpec((D, wa), lambda i: (0, 0)),
                  pl.BlockSpec((1, qr), lambda i: (0, 0)),
                  pl.BlockSpec((1, kvr), lambda i: (0, 0)),
                  pl.BlockSpec((tm, LANES), lambda i: (i, 0)),
                  pl.BlockSpec((tm, LANES), lambda i: (i, 0))],
        out_specs=[pl.BlockSpec((tm, qr), lambda i: (i, 0)),
                   pl.BlockSpec((tm, kvr), lambda i: (i, 0)),
                   pl.BlockSpec((tm, 2 * LANES), lambda i: (i, 0))],
        out_shape=[jax.ShapeDtypeStruct((T, qr), BF16),
                   jax.ShapeDtypeStruct((T, kvr), BF16),
                   jax.ShapeDtypeStruct((T, 2 * LANES), BF16)],
        compiler_params=_cparams(("parallel",), 56),
        name="inproj_mla",
    )(hb, w_a, gq.reshape(1, qr), gkv.reshape(1, kvr), cos4, sin4)


def _q_up_kernel(c_ref, w_ref, cos_ref, sin_ref, q_ref, *, qscale):
    acc = jnp.dot(c_ref[...], w_ref[...], preferred_element_type=F32)
    nope_w = HEADS_PER_TILE * QK_NOPE_DIM
    lane = lax.broadcasted_iota(jnp.int32, (acc.shape[0], LANES), 1)
    for pair in range(HEADS_PER_TILE // 2):
        roped = _rope128(acc[:, nope_w + pair * LANES:nope_w + (pair + 1) * LANES],
                         cos_ref[...], sin_ref[...]) * qscale
        for par in range(2):
            j = 2 * pair + par
            keep = (lane < QK_ROPE_DIM) if par == 0 else (lane >= QK_ROPE_DIM)
            base = j * QK_PAD_DIM
            q_ref[:, base:base + QK_NOPE_DIM] = (
                acc[:, j * QK_NOPE_DIM:(j + 1) * QK_NOPE_DIM] * qscale).astype(BF16)
            q_ref[:, base + QK_NOPE_DIM:base + QK_PAD_DIM] = jnp.where(keep, roped, 0.0).astype(BF16)


def _q_up(cq, wq_perm, cos4, sin4, qscale):
    T, qr = cq.shape
    n_tiles = wq_perm.shape[1] // (HEADS_PER_TILE * (QK_NOPE_DIM + QK_ROPE_DIM))
    tw = HEADS_PER_TILE * (QK_NOPE_DIM + QK_ROPE_DIM)
    to = HEADS_PER_TILE * QK_PAD_DIM
    tm = _tile(T, 1024)
    return pl.pallas_call(
        functools.partial(_q_up_kernel, qscale=qscale),
        grid=(T // tm, n_tiles),
        in_specs=[pl.BlockSpec((tm, qr), lambda i, j: (i, 0)),
                  pl.BlockSpec((qr, tw), lambda i, j: (0, j)),
                  pl.BlockSpec((tm, LANES), lambda i, j: (i, 0)),
                  pl.BlockSpec((tm, LANES), lambda i, j: (i, 0))],
        out_specs=pl.BlockSpec((tm, to), lambda i, j: (i, j)),
        out_shape=jax.ShapeDtypeStruct((T, n_tiles * to), BF16),
        compiler_params=_cparams(("parallel", "arbitrary"), 40),
        name="q_up",
    )(cq, wq_perm, cos4, sin4)


def _kv_up_kernel(c_ref, w_ref, kpe_ref, k_ref, v_ref):
    acc = jnp.dot(c_ref[...], w_ref[...], preferred_element_type=F32)
    for j in range(HEADS_PER_TILE):
        src = j * (QK_NOPE_DIM + V_HEAD_DIM)
        k_ref[:, j * QK_PAD_DIM:j * QK_PAD_DIM + QK_NOPE_DIM] = acc[:, src:src + QK_NOPE_DIM].astype(BF16)
        par = j % 2
        k_ref[:, j * QK_PAD_DIM + QK_NOPE_DIM:(j + 1) * QK_PAD_DIM] = kpe_ref[:, par * LANES:(par + 1) * LANES]
        v_ref[:, j * V_PAD_DIM:j * V_PAD_DIM + V_HEAD_DIM] = (
            acc[:, src + QK_NOPE_DIM:src + QK_NOPE_DIM + V_HEAD_DIM].astype(BF16))
        v_ref[:, j * V_PAD_DIM + V_HEAD_DIM:(j + 1) * V_PAD_DIM] = jnp.ones(
            (acc.shape[0], V_PAD_DIM - V_HEAD_DIM), BF16)


def _kv_up(ckv, w_kv, kpe2):
    T, kvr = ckv.shape
    tw = HEADS_PER_TILE * (QK_NOPE_DIM + V_HEAD_DIM)
    n_tiles = w_kv.shape[1] // tw
    tm = _tile(T, 1024)
    return pl.pallas_call(
        _kv_up_kernel,
        grid=(T // tm, n_tiles),
        in_specs=[pl.BlockSpec((tm, kvr), lambda i, j: (i, 0)),
                  pl.BlockSpec((kvr, tw), lambda i, j: (0, j)),
                  pl.BlockSpec((tm, 2 * LANES), lambda i, j: (i, 0))],
        out_specs=[pl.BlockSpec((tm, HEADS_PER_TILE * QK_PAD_DIM), lambda i, j: (i, j)),
                   pl.BlockSpec((tm, HEADS_PER_TILE * V_PAD_DIM), lambda i, j: (i, j))],
        out_shape=[jax.ShapeDtypeStruct((T, n_tiles * HEADS_PER_TILE * QK_PAD_DIM), BF16),
                   jax.ShapeDtypeStruct((T, n_tiles * HEADS_PER_TILE * V_PAD_DIM), BF16)],
        compiler_params=_cparams(("parallel", "arbitrary"), 40),
        name="kv_up",
    )(ckv, w_kv, kpe2)


def _lane_tile(x, reps):
    return jnp.concatenate([x] * reps, axis=1)


def _attn_kernel(q_ref, k_ref, v_ref, o_ref, m_s, acc_s, *bufs, tkv):
    n_kv = k_ref.shape[0] // tkv
    ns = ATTN_SCORE_BUFS
    s_buf, x_buf, p_buf, a_buf = bufs[:ns], bufs[ns:2 * ns], bufs[2 * ns:2 * ns + 2], bufs[2 * ns + 2:]
    m_s[...] = jnp.full(m_s.shape, -jnp.inf, F32)
    acc_s[...] = jnp.zeros(acc_s.shape, F32)

    def scores(i):
        s = lax.dot_general(q_ref[...], k_ref[i * tkv:(i + 1) * tkv, :], (((1,), (1,)), ((), ())),
                            preferred_element_type=F32)
        s_buf[i % ns][...] = s
        x_buf[i % ns][...] = jnp.broadcast_to(jnp.max(s, axis=1, keepdims=True), x_buf[i % ns].shape)

    def probs(i):
        m_prev = m_s[...]
        m_new = jnp.maximum(m_prev, x_buf[i % ns][...])
        m_s[...] = m_new
        a_buf[i % 2][...] = jnp.exp2(m_prev - m_new)
        p_buf[i % 2][...] = jnp.exp2(s_buf[i % ns][...] - _lane_tile(m_new, tkv // LANES)).astype(BF16)

    def values(i):
        acc_s[...] = (_lane_tile(a_buf[i % 2][...], V_PAD_DIM // LANES) * acc_s[...]
                      + jnp.dot(p_buf[i % 2][...], v_ref[i * tkv:(i + 1) * tkv, :], preferred_element_type=F32))

    for i in range(min(ns - 1, n_kv)):
        scores(i)
    for i in range(n_kv):
        if i + ns - 1 < n_kv:
            scores(i + ns - 1)
        if i >= 1:
            values(i - 1)
        probs(i)
    values(n_kv - 1)
    acc = acc_s[...]
    o_ref[...] = (acc[:, :V_HEAD_DIM] / acc[:, V_HEAD_DIM:]).astype(o_ref.dtype)


def _attention(q, k, v, B, S, H):
    T = B * S
    tq = _tile(S, ATTN_TQ)
    tkv = _tile(S, ATTN_TKV)
    nq = S // tq
    return pl.pallas_call(
        functools.partial(_attn_kernel, tkv=tkv),
        grid=(B, H, nq),
        in_specs=[pl.BlockSpec((tq, QK_PAD_DIM), lambda b, h, i: (b * nq + i, h)),
                  pl.BlockSpec((S, QK_PAD_DIM), lambda b, h, i: (b, h)),
                  pl.BlockSpec((S, V_PAD_DIM), lambda b, h, i: (b, h))],
        out_specs=pl.BlockSpec((tq, V_HEAD_DIM), lambda b, h, i: (b * nq + i, h)),
        out_shape=jax.ShapeDtypeStruct((T, H * V_HEAD_DIM), BF16),
        scratch_shapes=([pltpu.VMEM((tq, LANES), F32), pltpu.VMEM((tq, V_PAD_DIM), F32)]
                        + [pltpu.VMEM((tq, tkv), F32)] * ATTN_SCORE_BUFS
                        + [pltpu.VMEM((tq, LANES), F32)] * ATTN_SCORE_BUFS
                        + [pltpu.VMEM((tq, tkv), BF16)] * 2
                        + [pltpu.VMEM((tq, LANES), F32)] * 2),
        compiler_params=_cparams(("parallel", "parallel", "arbitrary"), 56),
        name="mla_attention",
    )(q, k, v)


def _inproj_f_kernel(h_ref, w_ref, cs_ref, a_ref, b_ref, *, groups):
    acc = jnp.dot(h_ref[...], w_ref[...], preferred_element_type=F32)
    C = FOURIER_GROUP_DIM
    for g in range(groups):
        ab = jnp.dot(acc[:, g * C:(g + 1) * C].astype(BF16), cs_ref[...], preferred_element_type=F32)
        a_ref[:, g * C:(g + 1) * C] = ab[:, :C].astype(BF16)
        b_ref[:, g * C:(g + 1) * C] = ab[:, C:].astype(BF16)


def _inproj_f(hb, w_f, cs_tab):
    T, D = hb.shape
    fw = w_f.shape[1]
    tm = _tile(T, 1024)
    tn = _tile(fw, 512)
    return pl.pallas_call(
        functools.partial(_inproj_f_kernel, groups=tn // FOURIER_GROUP_DIM),
        grid=(T // tm, fw // tn),
        in_specs=[pl.BlockSpec((tm, D), lambda i, j: (i, 0)),
                  pl.BlockSpec((D, tn), lambda i, j: (0, j)),
                  pl.BlockSpec((FOURIER_GROUP_DIM, 2 * FOURIER_GROUP_DIM), lambda i, j: (0, 0))],
        out_specs=[pl.BlockSpec((tm, tn), lambda i, j: (i, j)),
                   pl.BlockSpec((tm, tn), lambda i, j: (i, j))],
        out_shape=[jax.ShapeDtypeStruct((T, fw), BF16), jax.ShapeDtypeStruct((T, fw), BF16)],
        compiler_params=_cparams(("parallel", "arbitrary"), 48),
        name="inproj_fourier",
    )(hb, w_f, cs_tab)


def _dft_gen_kernel(tac_ref, tas_ref, tbc_ref, tbs_ref, cs_ref, sn_ref):
    tbc = tbc_ref[...]
    tbs = tbs_ref[...]
    for aa in range(tac_ref.shape[0]):
        ca = tac_ref[aa:aa + 1, :]
        sa = tas_ref[aa:aa + 1, :]
        rows = slice(aa * DFT_ROW_SPLIT, (aa + 1) * DFT_ROW_SPLIT)
        cs_ref[rows, :] = (ca * tbc - sa * tbs).astype(BF16)
        sn_ref[rows, :] = (-(sa * tbc + ca * tbs)).astype(BF16)


def _dft_matrices(S, n_chan):
    na = S // 2 // DFT_ROW_SPLIT
    scale = 1.0 / math.sqrt(S * n_chan)
    col = jnp.arange(S // 2, dtype=jnp.int32)[None, :]
    period = S // DFT_ROW_SPLIT
    ang_a = (2.0 * math.pi / period) * ((jnp.arange(na, dtype=jnp.int32)[:, None] * col) % period).astype(F32)
    ang_b = (2.0 * math.pi / S) * ((jnp.arange(DFT_ROW_SPLIT, dtype=jnp.int32)[:, None] * col) % S).astype(F32)
    tac, tas = jnp.cos(ang_a), jnp.sin(ang_a)
    tbc, tbs = scale * jnp.cos(ang_b), scale * jnp.sin(ang_b)
    ta = min(8, na)
    assert na % ta == 0
    tc = _tile(S // 2, 2048)
    return pl.pallas_call(
        _dft_gen_kernel,
        grid=(na // ta, S // 2 // tc),
        in_specs=[pl.BlockSpec((ta, tc), lambda i, j: (i, j)),
                  pl.BlockSpec((ta, tc), lambda i, j: (i, j)),
                  pl.BlockSpec((DFT_ROW_SPLIT, tc), lambda i, j: (0, j)),
                  pl.BlockSpec((DFT_ROW_SPLIT, tc), lambda i, j: (0, j))],
        out_specs=[pl.BlockSpec((ta * DFT_ROW_SPLIT, tc), lambda i, j: (i, j)),
                   pl.BlockSpec((ta * DFT_ROW_SPLIT, tc), lambda i, j: (i, j))],
        out_shape=[jax.ShapeDtypeStruct((S // 2, S // 2), BF16), jax.ShapeDtypeStruct((S // 2, S // 2), BF16)],
        compiler_params=_cparams(("parallel", "parallel"), 40),
        name="dft_matrices",
    )(tac, tas, tbc, tbs)


def _dft_fold_kernel(a_ref, am_ref, an_ref, b_ref, bm_ref, bn_ref, ae_ref, bo_ref, alt_ref):
    tm = a_ref.shape[0]
    r = lax.broadcasted_iota(jnp.int32, (tm, tm), 0)
    c = lax.broadcasted_iota(jnp.int32, (tm, tm), 1)
    rev = jnp.where(r + c == tm, 1.0, 0.0).astype(BF16)
    has_next = (pl.program_id(1) > 0).astype(F32)
    first = (jnp.where(r + c == 0, 1.0, 0.0) * has_next).astype(BF16)

    def mirrored(m_ref, n_ref):
        return (jnp.dot(rev, m_ref[...], preferred_element_type=F32)
                + jnp.dot(first, n_ref[...], preferred_element_type=F32))

    ae_ref[...] = (a_ref[...].astype(F32) + mirrored(am_ref, an_ref)).astype(BF16)
    bo_ref[...] = (b_ref[...].astype(F32) - mirrored(bm_ref, bn_ref)).astype(BF16)

    @pl.when(pl.program_id(1) == 0)
    def _():
        alt_ref[...] = jnp.zeros(alt_ref.shape, F32)

    row = lax.broadcasted_iota(jnp.int32, a_ref.shape, 0)
    both = a_ref[...].astype(F32) + am_ref[...].astype(F32)
    alt_ref[0:1, :] += jnp.sum(jnp.where(row % 2 == 0, both, -both), axis=0, keepdims=True)


def _dft_fold(a, b, B, S):
    fw = a.shape[1]
    tm = _tile(S // 2, 256)
    nb = S // tm
    nh = nb // 2

    def own(bb, i):
        return (bb * nb + i, 0)

    def mirror(bb, i):
        return (bb * nb + nb - 1 - i, 0)

    def mirror_next(bb, i):
        return (bb * nb + jnp.minimum(nb - i, nb - 1), 0)

    spec = lambda f: pl.BlockSpec((tm, fw), f)
    return pl.pallas_call(
        _dft_fold_kernel,
        grid=(B, nh),
        in_specs=[spec(own), spec(mirror), spec(mirror_next), spec(own), spec(mirror), spec(mirror_next)],
        out_specs=[pl.BlockSpec((tm, fw), lambda bb, i: (bb * nh + i, 0)),
                   pl.BlockSpec((tm, fw), lambda bb, i: (bb * nh + i, 0)),
                   pl.BlockSpec((8, fw), lambda bb, i: (bb, 0))],
        out_shape=[jax.ShapeDtypeStruct((B * S // 2, fw), BF16),
                   jax.ShapeDtypeStruct((B * S // 2, fw), BF16),
                   jax.ShapeDtypeStruct((B * 8, fw), F32)],
        compiler_params=_cparams(("parallel", "arbitrary"), 40),
        name="dft_fold",
    )(a, a, a, b, b, b)


def _seq_dft_kernel(cs_ref, sn_ref, ae_ref, bo_ref, mid_ref, y_ref, d_ref, p_s, q_s, *, scale):
    kk = pl.program_id(3)

    @pl.when(kk == 0)
    def _():
        p_s[...] = jnp.zeros(p_s.shape, F32)
        q_s[...] = jnp.zeros(q_s.shape, F32)

    p_s[...] += jnp.dot(cs_ref[...], ae_ref[...], preferred_element_type=F32)
    q_s[...] += jnp.dot(sn_ref[...], bo_ref[...], preferred_element_type=F32)

    @pl.when(kk == pl.num_programs(3) - 1)
    def _():
        row = lax.broadcasted_iota(jnp.int32, p_s.shape, 0)
        pm = p_s[...] + jnp.where(row % 2 == 0, scale, -scale) * mid_ref[0:1, :].astype(F32)
        y_ref[...] = (pm + q_s[...]).astype(y_ref.dtype)
        d_ref[...] = (pm - q_s[...]).astype(d_ref.dtype)


def _seq_dft(cs, sn, ae, bo, a, B, S, scale):
    fw = a.shape[1]
    half = S // 2
    tm = _tile(half, 1024)
    tn = _tile(fw, 1024)
    tk = _tile(half, 1024)
    nm, nk = half // tm, half // tk
    mid_rows = 16
    return pl.pallas_call(
        functools.partial(_seq_dft_kernel, scale=scale),
        grid=(B, nm, fw // tn, nk),
        in_specs=[pl.BlockSpec((tm, tk), lambda bb, i, j, k: (i, k)),
                  pl.BlockSpec((tm, tk), lambda bb, i, j, k: (i, k)),
                  pl.BlockSpec((tk, tn), lambda bb, i, j, k: (bb * nk + k, j)),
                  pl.BlockSpec((tk, tn), lambda bb, i, j, k: (bb * nk + k, j)),
                  pl.BlockSpec((mid_rows, tn),
                               lambda bb, i, j, k: ((bb * S + half) // mid_rows, j))],
        out_specs=[pl.BlockSpec((tm, tn), lambda bb, i, j, k: (bb * nm + i, j)),
                   pl.BlockSpec((tm, tn), lambda bb, i, j, k: (bb * nm + i, j))],
        out_shape=[jax.ShapeDtypeStruct((B * half, fw), BF16),
                   jax.ShapeDtypeStruct((B * half, fw), BF16)],
        scratch_shapes=[pltpu.VMEM((tm, tn), F32), pltpu.VMEM((tm, tn), F32)],
        compiler_params=_cparams(("parallel", "parallel", "parallel", "arbitrary"), 48),
        name="seq_dft",
    )(cs, sn, ae, bo, a)


def _dft_mirror_kernel(yh_ref, dm_ref, dn_ref, alt_ref, y_ref, *, scale, nh):
    j = pl.program_id(1)

    @pl.when(j < nh)
    def _():
        y_ref[...] = yh_ref[...]

    @pl.when(j >= nh)
    def _():
        tm = dm_ref.shape[0]
        r = lax.broadcasted_iota(jnp.int32, (tm, tm), 0)
        c = lax.broadcasted_iota(jnp.int32, (tm, tm), 1)
        rev = jnp.where(r + c == tm, 1.0, 0.0).astype(BF16)
        first = (jnp.where(r + c == 0, 1.0, 0.0) * (j > nh).astype(F32)).astype(BF16)
        out = (jnp.dot(rev, dm_ref[...], preferred_element_type=F32)
               + jnp.dot(first, dn_ref[...], preferred_element_type=F32))
        row = lax.broadcasted_iota(jnp.int32, out.shape, 0)
        nyquist = jnp.where(row == 0, scale * (j == nh).astype(F32), 0.0) * alt_ref[0:1, :]
        y_ref[...] = (out + nyquist).astype(y_ref.dtype)


def _dft_mirror(y_half, d, alt, B, S, scale):
    fw = d.shape[1]
    half = S // 2
    tm = _tile(half, 256)
    nh = half // tm

    def mirror(bb, j):
        return (bb * nh + nh - 1 - jnp.maximum(j - nh, 0), 0)

    def mirror_next(bb, j):
        return (bb * nh + jnp.minimum(nh - jnp.maximum(j - nh, 0), nh - 1), 0)

    return pl.pallas_call(
        functools.partial(_dft_mirror_kernel, scale=scale, nh=nh),
        grid=(B, 2 * nh),
        in_specs=[pl.BlockSpec((tm, fw), lambda bb, j: (bb * nh + jnp.minimum(j, nh - 1), 0)),
                  pl.BlockSpec((tm, fw), mirror),
                  pl.BlockSpec((tm, fw), mirror_next),
                  pl.BlockSpec((8, fw), lambda bb, j: (bb, 0))],
        out_specs=pl.BlockSpec((tm, fw), lambda bb, j: (bb * 2 * nh + j, 0)),
        out_shape=jax.ShapeDtypeStruct((B * S, fw), d.dtype),
        compiler_params=_cparams(("parallel", "arbitrary"), 40),
        name="dft_mirror",
    )(y_half, d, d, alt)


def _wo_kernel(ym_ref, yf_ref, gm_ref, gf_ref, w_ref, x_ref, mu_ref, rs_ref, lg_ref, lb_ref,
               pre_ref, mix_s, *, alpha, mw):
    @pl.when(pl.program_id(1) == 0)
    def _():
        ym = ym_ref[...].astype(F32)
        mix_s[:, :mw] = (ym * lax.rsqrt(jnp.mean(ym * ym, axis=-1, keepdims=True) + RMS_EPS)
                         * gm_ref[...]).astype(BF16)
        yf = yf_ref[...].astype(F32)
        mix_s[:, mw:] = (yf * lax.rsqrt(jnp.mean(yf * yf, axis=-1, keepdims=True) + RMS_EPS)
                         * gf_ref[...]).astype(BF16)

    h = (x_ref[...] - mu_ref[...]) * rs_ref[...] * lg_ref[...] + lb_ref[...]
    pre_ref[...] = alpha * h + jnp.dot(mix_s[...], w_ref[...], preferred_element_type=F32)


def _wo(ym, yf, gm, gf, w_o, x2, mu, rs, ln_g, ln_b, alpha):
    T, mw = ym.shape
    fw = yf.shape[1]
    D = w_o.shape[1]
    tm = _tile(T, 1024)
    tn = _tile(D, 512)
    return pl.pallas_call(
        functools.partial(_wo_kernel, alpha=alpha, mw=mw),
        grid=(T // tm, D // tn),
        in_specs=[pl.BlockSpec((tm, mw), lambda i, j: (i, 0)),
                  pl.BlockSpec((tm, fw), lambda i, j: (i, 0)),
                  pl.BlockSpec((1, mw), lambda i, j: (0, 0)),
                  pl.BlockSpec((1, fw), lambda i, j: (0, 0)),
                  pl.BlockSpec((mw + fw, tn), lambda i, j: (0, j)),
                  pl.BlockSpec((tm, tn), lambda i, j: (i, j)),
                  pl.BlockSpec((tm, 1), lambda i, j: (i, 0)),
                  pl.BlockSpec((tm, 1), lambda i, j: (i, 0)),
                  pl.BlockSpec((1, tn), lambda i, j: (0, j)),
                  pl.BlockSpec((1, tn), lambda i, j: (0, j))],
        out_specs=pl.BlockSpec((tm, tn), lambda i, j: (i, j)),
        out_shape=jax.ShapeDtypeStruct((T, D), F32),
        scratch_shapes=[pltpu.VMEM((tm, mw + fw), BF16)],
        compiler_params=_cparams(("parallel", "arbitrary"), 56),
        name="w_o_residual",
    )(ym, yf, gm.reshape(1, mw), gf.reshape(1, fw), w_o, x2, mu, rs,
      ln_g.reshape(1, D), ln_b.reshape(1, D))


def _pack_rounded_pair(lo, hi):
    return (pltpu.bitcast(lo, jnp.uint32) >> 16) | (pltpu.bitcast(hi, jnp.uint32) & jnp.uint32(0xFFFF0000))


def _pack_bf16_pair(lo, hi):
    return _pack_rounded_pair(lo.astype(BF16).astype(F32), hi.astype(BF16).astype(F32))


def _unpack_bf16_pair(w):
    lo = pltpu.bitcast(w << 16, F32).astype(BF16)
    hi = pltpu.bitcast(w & jnp.uint32(0xFFFF0000), F32).astype(BF16)
    return lo, hi


def _ln1_router_kernel(pre_ref, g_ref, b_ref, rwh_ref, rwl_ref, rb_ref, x1_ref, x1p_ref, idx_ref, gate_ref):
    x = pre_ref[...]
    mu = jnp.mean(x, axis=-1, keepdims=True)
    xc = x - mu
    var = jnp.mean(xc * xc, axis=-1, keepdims=True)
    x1 = xc * lax.rsqrt(var + LN_EPS) * g_ref[...] + b_ref[...]
    x1_ref[...] = x1
    half = x1.shape[1] // 2
    x_hi = x1.astype(BF16)
    x_hi_f = x_hi.astype(F32)
    x1p_ref[...] = _pack_rounded_pair(x_hi_f[:, :half], x_hi_f[:, half:])
    x_lo = (x1 - x_hi_f).astype(BF16)
    nt = (((1,), (1,)), ((), ()))
    logits = (lax.dot_general(rwh_ref[...], x_hi, nt, preferred_element_type=F32)
              + lax.dot_general(rwh_ref[...], x_lo, nt, preferred_element_type=F32)
              + lax.dot_general(rwl_ref[...], x_hi, nt, preferred_element_type=F32)) + rb_ref[...]
    n_e = logits.shape[0]
    eidx = lax.broadcasted_iota(jnp.int32, logits.shape, 0)
    vals, idxs = [], []
    for _ in range(TOP_K):
        m = jnp.max(logits, axis=0, keepdims=True)
        sel = jnp.min(jnp.where(logits == m, eidx, n_e), axis=0, keepdims=True)
        logits = jnp.where(eidx == sel, -jnp.inf, logits)
        vals.append(m)
        idxs.append(sel)
    exps = [jnp.exp(v - vals[0]) for v in vals]
    denom = exps[0] + exps[1] + exps[2] + exps[3]
    for kk in range(TOP_K):
        idx_ref[kk:kk + 1, :] = idxs[kk]
        gate_ref[kk:kk + 1, :] = exps[kk] / denom


def _ln1_router(pre, g, b, router_w, router_b):
    T, D = pre.shape
    E = router_w.shape[1]
    tm = _tile(T, 256)
    rw_t = router_w.T
    rw_hi = rw_t.astype(BF16)
    rw_lo = (rw_t - rw_hi.astype(F32)).astype(BF16)
    return pl.pallas_call(
        _ln1_router_kernel,
        grid=(T // tm,),
        in_specs=[pl.BlockSpec((tm, D), lambda i: (i, 0)),
                  pl.BlockSpec((1, D), lambda i: (0, 0)),
                  pl.BlockSpec((1, D), lambda i: (0, 0)),
                  pl.BlockSpec((E, D), lambda i: (0, 0)),
                  pl.BlockSpec((E, D), lambda i: (0, 0)),
                  pl.BlockSpec((E, 1), lambda i: (0, 0))],
        out_specs=[pl.BlockSpec((tm, D), lambda i: (i, 0)),
                   pl.BlockSpec((tm, D // 2), lambda i: (i, 0)),
                   pl.BlockSpec((TOP_K, tm), lambda i: (0, i)),
                   pl.BlockSpec((TOP_K, tm), lambda i: (0, i))],
        out_shape=[jax.ShapeDtypeStruct((T, D), F32),
                   jax.ShapeDtypeStruct((T, D // 2), jnp.uint32),
                   jax.ShapeDtypeStruct((TOP_K, T), jnp.int32),
                   jax.ShapeDtypeStruct((TOP_K, T), F32)],
        compiler_params=_cparams(("parallel",), 48),
        name="ln1_router",
    )(pre, g.reshape(1, D), b.reshape(1, D), rw_hi, rw_lo, router_b.reshape(E, 1))


def _row_copy(src_hbm, dst_ref, src_row, dst_row, sem):
    return pltpu.make_async_copy(src_hbm.at[pl.ds(src_row, 1)], dst_ref.at[pl.ds(dst_row, 1)], sem)


def _dispatch_kernel(dest_ref, pad_ref, npad_ref, nv_ref, x_ref, o_hbm, zero_s, sem, zsem, *, n_steps):
    b = pl.program_id(0)
    tt = x_ref.shape[0]

    def wait_rows(n):
        pltpu.make_async_copy(o_hbm.at[pl.ds(0, n)], o_hbm.at[pl.ds(0, n)], sem).wait()

    @pl.when(b < n_steps)
    def _():
        def issue(r, c):
            for kk in range(TOP_K):
                _row_copy(x_ref, o_hbm, r, dest_ref[(b * tt + r) * TOP_K + kk], sem).start()
            return c
        lax.fori_loop(0, tt, issue, 0, unroll=2)
        wait_rows(tt * TOP_K)

    @pl.when(b == n_steps)
    def _():
        n_pad = npad_ref[0]

        def issue_pad(i, c):
            _row_copy(x_ref, o_hbm, 0, pad_ref[i], sem).start()
            return c
        lax.fori_loop(0, n_pad, issue_pad, 0)

        def wait_pad(i, c):
            wait_rows(1)
            return c
        lax.fori_loop(0, n_pad, wait_pad, 0)

        zero_s[...] = jnp.zeros(zero_s.shape, zero_s.dtype)
        n_sub = MOE_CHUNK // MOE_SUB

        def zero_copy(i):
            return pltpu.make_async_copy(zero_s, o_hbm.at[pl.ds(pl.multiple_of(i * MOE_SUB, MOE_SUB), MOE_SUB)], zsem)

        def empty(i):
            return (i % n_sub) * MOE_SUB >= nv_ref[i // n_sub]

        def issue_zero(i, c):
            @pl.when(empty(i))
            def _():
                zero_copy(i).start()
            return c
        lax.fori_loop(0, nv_ref.shape[0] * n_sub, issue_zero, 0)

        def wait_zero(i, c):
            @pl.when(empty(i))
            def _():
                zero_copy(i).wait()
            return c
        lax.fori_loop(0, nv_ref.shape[0] * n_sub, wait_zero, 0)


def _dispatch(dest, pad_slots, n_pad, chunk_nv, x1p):
    T, W = x1p.shape
    tt = _tile(T, DISPATCH_TOKENS)
    n_steps = T // tt
    grid_spec = pltpu.PrefetchScalarGridSpec(
        num_scalar_prefetch=4,
        grid=(n_steps + 1,),
        in_specs=[pl.BlockSpec((tt, W), lambda b, d, p, n, v: (jnp.minimum(b, n_steps - 1), 0))],
        out_specs=pl.BlockSpec(memory_space=pl.ANY),
        scratch_shapes=[pltpu.VMEM((MOE_SUB, W), x1p.dtype),
                        pltpu.SemaphoreType.DMA(()), pltpu.SemaphoreType.DMA(())],
    )
    return pl.pallas_call(
        functools.partial(_dispatch_kernel, n_steps=n_steps),
        grid_spec=grid_spec,
        out_shape=jax.ShapeDtypeStruct((chunk_nv.shape[0] * MOE_CHUNK, W), x1p.dtype),
        compiler_params=_cparams(("arbitrary",), 32),
        name="moe_dispatch",
    )(dest, pad_slots, n_pad, chunk_nv, x1p)


def _expert_up_kernel(ce_ref, nv_ref, nu_ref, x_ref, wg_ref, wu_ref, bg_ref, bu_ref, h_ref):
    c = pl.program_id(0)
    nv = nv_ref[c]
    brow = ce_ref[c] * pl.num_programs(1) + pl.program_id(1)

    n_sub = MOE_CHUNK // MOE_SUB
    n_act = (nv + MOE_SUB - 1) // MOE_SUB
    for k in range(n_sub + 1):
        @pl.when(n_act == k)
        def _(k=k):
            if k > 0:
                x_lo, x_hi = _unpack_bf16_pair(x_ref[:k * MOE_SUB, :])
                half = x_lo.shape[1]

                def proj(w_ref):
                    return (jnp.dot(x_lo, w_ref[:half, :].astype(BF16), preferred_element_type=F32)
                            + jnp.dot(x_hi, w_ref[half:, :].astype(BF16), preferred_element_type=F32))
                hg = jnp.minimum(proj(wg_ref) + bg_ref[pl.ds(brow, 1), :], SWIGLU_LIMIT)
                hu = jnp.clip(proj(wu_ref) + bu_ref[pl.ds(brow, 1), :], -SWIGLU_LIMIT, SWIGLU_LIMIT)
                act = (hu + 1.0) * (hg * jax.nn.sigmoid(SWIGLU_ALPHA * hg))
                h_ref[:k * MOE_SUB, :] = act.astype(h_ref.dtype)
            if k < n_sub:
                h_ref[k * MOE_SUB:, :] = jnp.zeros(((n_sub - k) * MOE_SUB, h_ref.shape[1]), h_ref.dtype)


def _expert_up(chunk_e, chunk_nv, n_used, xs, w_gate, w_up, b_gate, b_up, n_chunks):
    E, D, F = w_gate.shape
    tf = _tile(F, 256)
    nf = F // tf

    def used(c, nu):
        return jnp.minimum(c, nu[0] - 1)

    def jeff(c, j, nu):
        return jnp.where(c < nu[0], j, nf - 1)

    grid_spec = pltpu.PrefetchScalarGridSpec(
        num_scalar_prefetch=3,
        grid=(n_chunks, nf),
        in_specs=[pl.BlockSpec((MOE_CHUNK, D // 2), lambda c, j, ce, nv, nu: (used(c, nu), 0)),
                  pl.BlockSpec((None, D, tf), lambda c, j, ce, nv, nu: (ce[c], 0, jeff(c, j, nu))),
                  pl.BlockSpec((None, D, tf), lambda c, j, ce, nv, nu: (ce[c], 0, jeff(c, j, nu))),
                  pl.BlockSpec((E * nf, tf), lambda c, j, ce, nv, nu: (0, 0)),
                  pl.BlockSpec((E * nf, tf), lambda c, j, ce, nv, nu: (0, 0))],
        out_specs=pl.BlockSpec((MOE_CHUNK, tf), lambda c, j, ce, nv, nu: (c, j)),
    )
    return pl.pallas_call(
        _expert_up_kernel,
        grid_spec=grid_spec,
        out_shape=jax.ShapeDtypeStruct((n_chunks * MOE_CHUNK, F), BF16),
        compiler_params=_cparams(("arbitrary", "arbitrary"), 60),
        name="expert_gate_up",
    )(chunk_e, chunk_nv, n_used, xs, w_gate, w_up, b_gate.reshape(E * nf, tf), b_up.reshape(E * nf, tf))


def _expert_down_kernel(ce_ref, nv_ref, nu_ref, h_ref, wd_ref, bd_ref, y_ref):
    c = pl.program_id(0)
    nv = nv_ref[c]
    brow = ce_ref[c] * pl.num_programs(1) + pl.program_id(1)

    n_sub = MOE_CHUNK // MOE_SUB
    n_act = (nv + MOE_SUB - 1) // MOE_SUB
    for k in range(n_sub + 1):
        @pl.when(n_act == k)
        def _(k=k):
            if k > 0:
                out = jnp.dot(h_ref[:k * MOE_SUB, :], wd_ref[...].astype(BF16),
                              preferred_element_type=F32) + bd_ref[pl.ds(brow, 1), :]
                hw = out.shape[1] // 2
                y_ref[:k * MOE_SUB, :] = _pack_bf16_pair(out[:, :hw], out[:, hw:])
            if k < n_sub:
                y_ref[k * MOE_SUB:, :] = jnp.zeros(((n_sub - k) * MOE_SUB, y_ref.shape[1]), y_ref.dtype)


def _expert_down(chunk_e, chunk_nv, n_used, hmid, w_down, b_down, n_chunks):
    E, F, D = w_down.shape
    tn = _tile(D, MOE_DOWN_TN)
    nn = D // tn

    def used(c, nu):
        return jnp.minimum(c, nu[0] - 1)

    def jeff(c, j, nu):
        return jnp.where(c < nu[0], j, nn - 1)

    grid_spec = pltpu.PrefetchScalarGridSpec(
        num_scalar_prefetch=3,
        grid=(n_chunks, nn),
        in_specs=[pl.BlockSpec((MOE_CHUNK, F), lambda c, j, ce, nv, nu: (used(c, nu), 0)),
                  pl.BlockSpec((None, F, tn), lambda c, j, ce, nv, nu: (ce[c], 0, jeff(c, j, nu))),
                  pl.BlockSpec((E * nn, tn), lambda c, j, ce, nv, nu: (0, 0))],
        out_specs=pl.BlockSpec((MOE_CHUNK, tn // 2), lambda c, j, ce, nv, nu: (c, j)),
    )
    return pl.pallas_call(
        _expert_down_kernel,
        grid_spec=grid_spec,
        out_shape=jax.ShapeDtypeStruct((n_chunks * MOE_CHUNK, D // 2), jnp.uint32),
        compiler_params=_cparams(("arbitrary", "arbitrary"), 56),
        name="expert_down",
    )(chunk_e, chunk_nv, n_used, hmid, w_down, b_down.reshape(E * nn, tn))


def _combine_kernel(dest_ref, y_hbm, x1_ref, gate_ref, g_ref, b_ref, o_ref, *scratch, alpha, tn):
    i = pl.program_id(0)
    n = pl.num_programs(0)
    nb = COMBINE_BUFS
    tt = o_ref.shape[0] // nb
    bufs, sem = scratch[:nb], scratch[nb]

    def issue_tile(tile, slot):
        base = jnp.minimum(tile, nb * n - 1) * tt
        for r in range(tt):
            for kk in range(TOP_K):
                _row_copy(y_hbm, bufs[slot].at[kk], dest_ref[(base + r) * TOP_K + kk], r, sem.at[slot]).start()

    def wait_tile(slot):
        for kk in range(TOP_K):
            pltpu.make_async_copy(y_hbm.at[pl.ds(0, tt)], bufs[slot].at[kk], sem.at[slot]).wait()

    def finish_tile(slot):
        rows = slice(slot * tt, (slot + 1) * tt)
        gates = gate_ref[rows, :]
        y_lo = y_hi = None
        for kk in range(TOP_K):
            w = bufs[slot][kk]
            g_k = gates[:, kk:kk + 1]
            lo = g_k * pltpu.bitcast(w << 16, F32)
            hi = g_k * pltpu.bitcast(w & jnp.uint32(0xFFFF0000), F32)
            y_lo = lo if y_lo is None else y_lo + lo
            y_hi = hi if y_hi is None else y_hi + hi
        hw = tn // 2
        pieces = []
        for j in range(x1_ref.shape[1] // tn):
            pieces.append((slice(j * tn, j * tn + hw), y_lo[:, j * hw:(j + 1) * hw]))
            pieces.append((slice(j * tn + hw, (j + 1) * tn), y_hi[:, j * hw:(j + 1) * hw]))
        zs = [alpha * x1_ref[rows, cols] + y for cols, y in pieces]
        d = x1_ref.shape[1]
        mu = sum(jnp.sum(z, axis=-1, keepdims=True) for z in zs) / d
        zcs = [z - mu for z in zs]
        var = sum(jnp.sum(zc * zc, axis=-1, keepdims=True) for zc in zcs) / d
        rs = lax.rsqrt(var + LN_EPS)
        for (cols, _), zc in zip(pieces, zcs):
            o_ref[rows, cols] = zc * rs * g_ref[:, cols] + b_ref[:, cols]

    @pl.when(i == 0)
    def _():
        for t in range(COMBINE_AHEAD):
            issue_tile(t, t)

    for t in range(nb):
        wait_tile(t)
        issue_tile(nb * i + t + COMBINE_AHEAD, (t + COMBINE_AHEAD) % nb)
        finish_tile(t)

    @pl.when(i == n - 1)
    def _():
        for t in range(COMBINE_AHEAD):
            wait_tile(t)


def _combine(dest, yslots, x1, gates_tk, g, b, alpha):
    T, D = x1.shape
    nb = COMBINE_BUFS
    tt = _tile(T // nb, COMBINE_TOKENS)
    grid_spec = pltpu.PrefetchScalarGridSpec(
        num_scalar_prefetch=1,
        grid=(T // (nb * tt),),
        in_specs=[pl.BlockSpec(memory_space=pl.ANY),
                  pl.BlockSpec((nb * tt, D), lambda i, d: (i, 0)),
                  pl.BlockSpec((nb * tt, TOP_K), lambda i, d: (i, 0)),
                  pl.BlockSpec((1, D), lambda i, d: (0, 0)),
                  pl.BlockSpec((1, D), lambda i, d: (0, 0))],
        out_specs=pl.BlockSpec((nb * tt, D), lambda i, d: (i, 0)),
        scratch_shapes=[pltpu.VMEM((TOP_K, tt, D // 2), jnp.uint32) for _ in range(nb)]
        + [pltpu.SemaphoreType.DMA((nb,))],
    )
    return pl.pallas_call(
        functools.partial(_combine_kernel, alpha=alpha, tn=_tile(D, MOE_DOWN_TN)),
        grid_spec=grid_spec,
        out_shape=jax.ShapeDtypeStruct((T, D), F32),
        compiler_params=_cparams(("arbitrary",), 48),
        name="moe_combine_ln2",
    )(dest, yslots, x1, gates_tk, g.reshape(1, D), b.reshape(1, D))


def _routing_tables(top_idx, n_experts, n_chunks):
    T = top_idx.shape[1]
    M = T * TOP_K
    flat_e = top_idx.T.reshape(M)
    onehot = (flat_e[:, None] == jnp.arange(n_experts, dtype=jnp.int32)[None, :]).astype(jnp.int32)
    csum = jnp.cumsum(onehot, axis=0)
    rank = jnp.sum(csum * onehot, axis=1) - 1
    counts = csum[-1]
    chunks_e = (counts + MOE_CHUNK - 1) // MOE_CHUNK
    chunk_end = jnp.cumsum(chunks_e)
    chunk_start = chunk_end - chunks_e
    n_used = chunk_end[-1]
    dest = chunk_start[flat_e] * MOE_CHUNK + rank
    P = n_chunks * MOE_CHUNK
    pad_idx = counts[:, None] + jnp.arange(MOE_SUB, dtype=jnp.int32)[None, :]
    padded = (counts + MOE_SUB - 1) // MOE_SUB * MOE_SUB
    pad_slots = jnp.where(pad_idx < padded[:, None], chunk_start[:, None] * MOE_CHUNK + pad_idx, P)
    pad_slots = jnp.sort(pad_slots.reshape(-1)).astype(jnp.int32)
    n_pad = jnp.sum(padded - counts).astype(jnp.int32).reshape(1)
    cid = jnp.arange(n_chunks, dtype=jnp.int32)
    chunk_e = jnp.minimum(jnp.searchsorted(chunk_end, cid, side='right'), n_experts - 1).astype(jnp.int32)
    last_e = chunk_e[jnp.maximum(n_used - 1, 0)]
    chunk_e = jnp.where(cid < n_used, chunk_e, last_e)
    chunk_nv = jnp.where(cid < n_used,
                         jnp.clip(counts[chunk_e] - (cid - chunk_start[chunk_e]) * MOE_CHUNK, 0, MOE_CHUNK),
                         0).astype(jnp.int32)
    return (dest.astype(jnp.int32), pad_slots, n_pad, chunk_e, chunk_nv,
            n_used.astype(jnp.int32).reshape(1))


def kernel(x, positions, ln_in_g, ln_in_b, w_in, q_a_norm_g, w_q_b, kv_a_norm_g, w_kv_b, mla_out_norm_g, fourier_out_norm_g, w_o, ln1_g, ln1_b, router_w, router_b, w_gate, b_gate, w_up, b_up, w_down, b_down, ln2_g, ln2_b):
    B, S, D = x.shape
    T = B * S
    depth = w_in.shape[0]
    assert depth == 1, "single-layer trunk only"
    qr = q_a_norm_g.shape[1]
    kvr = kv_a_norm_g.shape[1]
    H = w_q_b.shape[2] // (QK_NOPE_DIM + QK_ROPE_DIM)
    fw = fourier_out_norm_g.shape[1]
    E = router_w.shape[2]
    assert H % HEADS_PER_TILE == 0 and (qr + kvr) % LANES == 0 and S % (2 * DFT_ROW_SPLIT) == 0
    alpha = (2.0 * depth) ** 0.25

    inv_freq = ROPE_THETA ** (-jnp.arange(0, QK_ROPE_DIM, 2, dtype=F32) / QK_ROPE_DIM)
    ang = positions.astype(F32)[..., None] * inv_freq
    cos4 = jnp.tile(jnp.cos(ang), (1, 1, 2 * LANES // QK_ROPE_DIM)).reshape(T, LANES)
    sin4 = jnp.tile(jnp.sin(ang), (1, 1, 2 * LANES // QK_ROPE_DIM)).reshape(T, LANES)

    rope_end = qr + kvr + QK_ROPE_DIM
    w_a = w_in[0, :, :qr + kvr + LANES].astype(BF16)
    w_f = w_in[0, :, rope_end:].astype(BF16)
    wq = w_q_b[0].reshape(qr, H // HEADS_PER_TILE, HEADS_PER_TILE, QK_NOPE_DIM + QK_ROPE_DIM)
    wq_perm = jnp.concatenate(
        [wq[..., :QK_NOPE_DIM].reshape(qr, H // HEADS_PER_TILE, HEADS_PER_TILE * QK_NOPE_DIM),
         wq[..., QK_NOPE_DIM:].reshape(qr, H // HEADS_PER_TILE, HEADS_PER_TILE * QK_ROPE_DIM)],
        axis=-1).reshape(qr, H * (QK_NOPE_DIM + QK_ROPE_DIM)).astype(BF16)
    w_kv = w_kv_b[0].astype(BF16)
    w_o_b = w_o[0].astype(BF16)
    ch = jnp.arange(FOURIER_GROUP_DIM, dtype=jnp.int32)
    ang_c = (2.0 * math.pi / FOURIER_GROUP_DIM) * ((ch[:, None] * ch[None, :]) % FOURIER_GROUP_DIM).astype(F32)
    cs_tab = jnp.concatenate([jnp.cos(ang_c), jnp.sin(ang_c)], axis=1).astype(BF16)

    x2 = x.reshape(T, D)
    hb, mu, rs = _ln_in(x2, ln_in_g, ln_in_b)

    cq, ckv, kpe2 = _inproj_a(hb, w_a, q_a_norm_g[0], kv_a_norm_g[0], cos4, sin4)
    qscale = (QK_NOPE_DIM + QK_ROPE_DIM) ** -0.5 * LOG2E
    q = _q_up(cq, wq_perm, cos4, sin4, qscale)
    k, v = _kv_up(ckv, w_kv, kpe2)
    y_mla = _attention(q, k, v, B, S, H)

    fa, fb = _inproj_f(hb, w_f, cs_tab)
    cs_mat, sn_mat = _dft_matrices(S, FOURIER_GROUP_DIM)
    fae, fbo, falt = _dft_fold(fa, fb, B, S)
    dft_scale = 1.0 / math.sqrt(S * FOURIER_GROUP_DIM)
    y_half, y_diff = _seq_dft(cs_mat, sn_mat, fae, fbo, fa, B, S, dft_scale)
    y_f = _dft_mirror(y_half, y_diff, falt, B, S, dft_scale)

    pre = _wo(y_mla, y_f, mla_out_norm_g[0], fourier_out_norm_g[0], w_o_b, x2, mu, rs,
              ln_in_g, ln_in_b, alpha)
    x1, x1p, top_idx, gates = _ln1_router(pre, ln1_g[0], ln1_b[0], router_w[0], router_b[0])

    n_chunks = (T * TOP_K + E * (MOE_CHUNK - 1)) // MOE_CHUNK
    dest, pad_slots, n_pad, chunk_e, chunk_nv, n_used = _routing_tables(top_idx, E, n_chunks)
    xs = _dispatch(dest, pad_slots, n_pad, chunk_nv, x1p)
    hmid = _expert_up(chunk_e, chunk_nv, n_used, xs, w_gate[0], w_up[0], b_gate[0], b_up[0], n_chunks)
    yslots = _expert_down(chunk_e, chunk_nv, n_used, hmid, w_down[0], b_down[0], n_chunks)
    out = _combine(dest, yslots, x1, gates.T, ln2_g[0], ln2_b[0], alpha)
    return out.reshape(B, S, D)
```

```python
import functools
import math

import jax
import jax.numpy as jnp
from jax import lax
from jax.experimental import pallas as pl
from jax.experimental.pallas import tpu as pltpu

F32 = jnp.float32
BF16 = jnp.bfloat16

V_HEAD_DIM = 128
QK_NOPE_DIM = 128
QK_ROPE_DIM = 64
QK_PAD_DIM = 256
V_PAD_DIM = 256
ROPE_THETA = 10000.0
FOURIER_GROUP_DIM = 128
TOP_K = 4
SWIGLU_LIMIT = 7.0
SWIGLU_ALPHA = 1.702
LN_EPS = 1e-5
RMS_EPS = 1e-6
LOG2E = 1.4426950408889634

LANES = 128
V7X_VMEM_BYTES = 64 * 1024 * 1024
VMEM_COMPILER_RESERVE = 4 * 1024 * 1024
HEADS_PER_TILE = 4
DFT_ROW_SPLIT = 64
ATTN_TQ = 1024
ATTN_TKV = 2048
ATTN_SCORE_BUFS = 3

MOE_CHUNK = 1280
MOE_SUB = 128
DISPATCH_TOKENS = 512
COMBINE_TOKENS = 64
COMBINE_BUFS = 4
COMBINE_AHEAD = 2
MOE_DOWN_TN = 1024


def _cparams(semantics, vmem_mb):
    return pltpu.CompilerParams(dimension_semantics=semantics,
                                vmem_limit_bytes=min(vmem_mb * 1024 * 1024,
                                                     V7X_VMEM_BYTES - VMEM_COMPILER_RESERVE))


def _tile(dim, pref):
    t = min(dim, pref)
    while dim % t:
        t //= 2
    return t


def _ln_in_kernel(x_ref, g_ref, b_ref, hb_ref, mu_ref, rs_ref):
    x = x_ref[...]
    mu = jnp.mean(x, axis=-1, keepdims=True)
    xc = x - mu
    var = jnp.mean(xc * xc, axis=-1, keepdims=True)
    rs = lax.rsqrt(var + LN_EPS)
    hb_ref[...] = (xc * rs * g_ref[...] + b_ref[...]).astype(BF16)
    mu_ref[...] = mu
    rs_ref[...] = rs


def _ln_in(x2, g, b):
    T, D = x2.shape
    tm = _tile(T, 256)
    return pl.pallas_call(
        _ln_in_kernel,
        grid=(T // tm,),
        in_specs=[pl.BlockSpec((tm, D), lambda i: (i, 0)),
                  pl.BlockSpec((1, D), lambda i: (0, 0)),
                  pl.BlockSpec((1, D), lambda i: (0, 0))],
        out_specs=[pl.BlockSpec((tm, D), lambda i: (i, 0)),
                   pl.BlockSpec((tm, 1), lambda i: (i, 0)),
                   pl.BlockSpec((tm, 1), lambda i: (i, 0))],
        out_shape=[jax.ShapeDtypeStruct((T, D), BF16),
                   jax.ShapeDtypeStruct((T, 1), F32),
                   jax.ShapeDtypeStruct((T, 1), F32)],
        compiler_params=_cparams(("parallel",), 40),
        name="ln_in",
    )(x2, g.reshape(1, D), b.reshape(1, D))


def _rope128(p, cos4, sin4):
    lane = lax.broadcasted_iota(jnp.int32, p.shape, 1)
    first_half = (lane % QK_ROPE_DIM) < (QK_ROPE_DIM // 2)
    rot = jnp.where(first_half, -pltpu.roll(p, LANES - QK_ROPE_DIM // 2, 1),
                    pltpu.roll(p, QK_ROPE_DIM // 2, 1))
    return p * cos4 + rot * sin4


def _inproj_a_kernel(h_ref, w_ref, gq_ref, gkv_ref, cos_ref, sin_ref,
                     cq_ref, ckv_ref, kpe_ref, *, qr, kvr):
    acc = jnp.dot(h_ref[...], w_ref[...], preferred_element_type=F32)
    cq = acc[:, :qr]
    cq_ref[...] = (cq * lax.rsqrt(jnp.mean(cq * cq, axis=-1, keepdims=True) + RMS_EPS)
                   * gq_ref[...]).astype(BF16)
    ckv = acc[:, qr:qr + kvr]
    ckv_ref[...] = (ckv * lax.rsqrt(jnp.mean(ckv * ckv, axis=-1, keepdims=True) + RMS_EPS)
                    * gkv_ref[...]).astype(BF16)
    roped = _rope128(acc[:, qr + kvr:qr + kvr + LANES], cos_ref[...], sin_ref[...])
    lane = lax.broadcasted_iota(jnp.int32, roped.shape, 1)
    even = jnp.where(lane < QK_ROPE_DIM, roped, 0.0)
    kpe_ref[:, :LANES] = even.astype(BF16)
    kpe_ref[:, LANES:] = pltpu.roll(even, QK_ROPE_DIM, 1).astype(BF16)


def _inproj_a(hb, w_a, gq, gkv, cos4, sin4):
    T, D = hb.shape
    qr, kvr = gq.shape[0], gkv.shape[0]
    wa = w_a.shape[1]
    tm = _tile(T, 512)
    return pl.pallas_call(
        functools.partial(_inproj_a_kernel, qr=qr, kvr=kvr),
        grid=(T // tm,),
        in_specs=[pl.BlockSpec((tm, D), lambda i: (i, 0)),
                  pl.BlockSpec((D, wa), lambda i: (0, 0)),
                  pl.BlockSpec((1, qr), lambda i: (0, 0)),
                  pl.BlockSpec((1, kvr), lambda i: (0, 0)),
                  pl.BlockSpec((tm, LANES), lambda i: (i, 0)),
                  pl.BlockSpec((tm, LANES), lambda i: (i, 0))],
        out_specs=[pl.BlockSpec((tm, qr), lambda i: (i, 0)),
                   pl.BlockSpec((tm, kvr), lambda i: (i, 0)),
                   pl.BlockSpec((tm, 2 * LANES), lambda i: (i, 0))],
        out_shape=[jax.ShapeDtypeStruct((T, qr), BF16),
                   jax.ShapeDtypeStruct((T, kvr), BF16),
                   jax.ShapeDtypeStruct((T, 2 * LANES), BF16)],
        compiler_params=_cparams(("parallel",), 56),
        name="inproj_mla",
    )(hb, w_a, gq.reshape(1, qr), gkv.reshape(1, kvr), cos4, sin4)


def _q_up_kernel(c_ref, w_ref, cos_ref, sin_ref, q_ref, *, qscale):
    acc = jnp.dot(c_ref[...], w_ref[...], preferred_element_type=F32)
    nope_w = HEADS_PER_TILE * QK_NOPE_DIM
    lane = lax.broadcasted_iota(jnp.int32, (acc.shape[0], LANES), 1)
    for pair in range(HEADS_PER_TILE // 2):
        roped = _rope128(acc[:, nope_w + pair * LANES:nope_w + (pair + 1) * LANES],
                         cos_ref[...], sin_ref[...]) * qscale
        for par in range(2):
            j = 2 * pair + par
            keep = (lane < QK_ROPE_DIM) if par == 0 else (lane >= QK_ROPE_DIM)
            base = j * QK_PAD_DIM
            q_ref[:, base:base + QK_NOPE_DIM] = (
                acc[:, j * QK_NOPE_DIM:(j + 1) * QK_NOPE_DIM] * qscale).astype(BF16)
            q_ref[:, base + QK_NOPE_DIM:base + QK_PAD_DIM] = jnp.where(keep, roped, 0.0).astype(BF16)


def _q_up(cq, wq_perm, cos4, sin4, qscale):
    T, qr = cq.shape
    n_tiles = wq_perm.shape[1] // (HEADS_PER_TILE * (QK_NOPE_DIM + QK_ROPE_DIM))
    tw = HEADS_PER_TILE * (QK_NOPE_DIM + QK_ROPE_DIM)
    to = HEADS_PER_TILE * QK_PAD_DIM
    tm = _tile(T, 1024)
    return pl.pallas_call(
        functools.partial(_q_up_kernel, qscale=qscale),
        grid=(T // tm, n_tiles),
        in_specs=[pl.BlockSpec((tm, qr), lambda i, j: (i, 0)),
                  pl.BlockSpec((qr, tw), lambda i, j: (0, j)),
                  pl.BlockSpec((tm, LANES), lambda i, j: (i, 0)),
                  pl.BlockSpec((tm, LANES), lambda i, j: (i, 0))],
        out_specs=pl.BlockSpec((tm, to), lambda i, j: (i, j)),
        out_shape=jax.ShapeDtypeStruct((T, n_tiles * to), BF16),
        compiler_params=_cparams(("parallel", "arbitrary"), 40),
        name="q_up",
    )(cq, wq_perm, cos4, sin4)


def _kv_up_kernel(c_ref, w_ref, kpe_ref, k_ref, v_ref):
    acc = jnp.dot(c_ref[...], w_ref[...], preferred_element_type=F32)
    for j in range(HEADS_PER_TILE):
        src = j * (QK_NOPE_DIM + V_HEAD_DIM)
        k_ref[:, j * QK_PAD_DIM:j * QK_PAD_DIM + QK_NOPE_DIM] = acc[:, src:src + QK_NOPE_DIM].astype(BF16)
        par = j % 2
        k_ref[:, j * QK_PAD_DIM + QK_NOPE_DIM:(j + 1) * QK_PAD_DIM] = kpe_ref[:, par * LANES:(par + 1) * LANES]
        v_ref[:, j * V_HEAD_DIM:(j + 1) * V_HEAD_DIM] = (
            acc[:, src + QK_NOPE_DIM:src + QK_NOPE_DIM + V_HEAD_DIM].astype(BF16))


def _kv_up(ckv, w_kv, kpe2):
    T, kvr = ckv.shape
    tw = HEADS_PER_TILE * (QK_NOPE_DIM + V_HEAD_DIM)
    n_tiles = w_kv.shape[1] // tw
    tm = _tile(T, 1024)
    return pl.pallas_call(
        _kv_up_kernel,
        grid=(T // tm, n_tiles),
        in_specs=[pl.BlockSpec((tm, kvr), lambda i, j: (i, 0)),
                  pl.BlockSpec((kvr, tw), lambda i, j: (0, j)),
                  pl.BlockSpec((tm, 2 * LANES), lambda i, j: (i, 0))],
        out_specs=[pl.BlockSpec((tm, HEADS_PER_TILE * QK_PAD_DIM), lambda i, j: (i, j)),
                   pl.BlockSpec((tm, HEADS_PER_TILE * V_HEAD_DIM), lambda i, j: (i, j))],
        out_shape=[jax.ShapeDtypeStruct((T, n_tiles * HEADS_PER_TILE * QK_PAD_DIM), BF16),
                   jax.ShapeDtypeStruct((T, n_tiles * HEADS_PER_TILE * V_HEAD_DIM), BF16)],
        compiler_params=_cparams(("parallel", "arbitrary"), 40),
        name="kv_up",
    )(ckv, w_kv, kpe2)


def _lane_tile(x, reps):
    return jnp.concatenate([x] * reps, axis=1)


def _attn_kernel(q_ref, k_ref, v_ref, o_ref, m_s, acc_s, *bufs, tkv):
    n_kv = k_ref.shape[0] // tkv
    ns = ATTN_SCORE_BUFS
    s_buf, x_buf, p_buf, a_buf = bufs[:ns], bufs[ns:2 * ns], bufs[2 * ns:2 * ns + 2], bufs[2 * ns + 2:2 * ns + 4]
    vp_s = bufs[2 * ns + 4]

    @pl.when(pl.program_id(2) == 0)
    def _():
        vp_s[:, :V_HEAD_DIM] = v_ref[...]
        vp_s[:, V_HEAD_DIM:] = jnp.ones((vp_s.shape[0], V_PAD_DIM - V_HEAD_DIM), BF16)

    m_s[...] = jnp.full(m_s.shape, -jnp.inf, F32)
    acc_s[...] = jnp.zeros(acc_s.shape, F32)

    def scores(i):
        s = lax.dot_general(q_ref[...], k_ref[i * tkv:(i + 1) * tkv, :], (((1,), (1,)), ((), ())),
                            preferred_element_type=F32)
        s_buf[i % ns][...] = s
        x_buf[i % ns][...] = jnp.broadcast_to(jnp.max(s, axis=1, keepdims=True), x_buf[i % ns].shape)

    def probs(i):
        m_prev = m_s[...]
        m_new = jnp.maximum(m_prev, x_buf[i % ns][...])
        m_s[...] = m_new
        a_buf[i % 2][...] = jnp.exp2(m_prev - m_new)
        p_buf[i % 2][...] = jnp.exp2(s_buf[i % ns][...] - _lane_tile(m_new, tkv // LANES)).astype(BF16)

    def values(i):
        acc_s[...] = (_lane_tile(a_buf[i % 2][...], V_PAD_DIM // LANES) * acc_s[...]
                      + jnp.dot(p_buf[i % 2][...], vp_s[i * tkv:(i + 1) * tkv, :], preferred_element_type=F32))

    for i in range(min(ns - 1, n_kv)):
        scores(i)
    for i in range(n_kv):
        if i + ns - 1 < n_kv:
            scores(i + ns - 1)
        if i >= 1:
            values(i - 1)
        probs(i)
    values(n_kv - 1)
    acc = acc_s[...]
    o_ref[...] = (acc[:, :V_HEAD_DIM] / acc[:, V_HEAD_DIM:]).astype(o_ref.dtype)


def _attention(q, k, v, B, S, H):
    T = B * S
    tq = _tile(S, ATTN_TQ)
    tkv = _tile(S, ATTN_TKV)
    nq = S // tq
    return pl.pallas_call(
        functools.partial(_attn_kernel, tkv=tkv),
        grid=(B, H, nq),
        in_specs=[pl.BlockSpec((tq, QK_PAD_DIM), lambda b, h, i: (b * nq + i, h)),
                  pl.BlockSpec((S, QK_PAD_DIM), lambda b, h, i: (b, h)),
                  pl.BlockSpec((S, V_HEAD_DIM), lambda b, h, i: (b, h))],
        out_specs=pl.BlockSpec((tq, V_HEAD_DIM), lambda b, h, i: (b * nq + i, h)),
        out_shape=jax.ShapeDtypeStruct((T, H * V_HEAD_DIM), BF16),
        scratch_shapes=([pltpu.VMEM((tq, LANES), F32), pltpu.VMEM((tq, V_PAD_DIM), F32)]
                        + [pltpu.VMEM((tq, tkv), F32)] * ATTN_SCORE_BUFS
                        + [pltpu.VMEM((tq, LANES), F32)] * ATTN_SCORE_BUFS
                        + [pltpu.VMEM((tq, tkv), BF16)] * 2
                        + [pltpu.VMEM((tq, LANES), F32)] * 2
                        + [pltpu.VMEM((S, V_PAD_DIM), BF16)]),
        compiler_params=_cparams(("parallel", "parallel", "arbitrary"), 56),
        name="mla_attention",
    )(q, k, v)


def _inproj_f_kernel(h_ref, w_ref, cs_ref, a_ref, b_ref, *, groups):
    acc = jnp.dot(h_ref[...], w_ref[...], preferred_element_type=F32)
    C = FOURIER_GROUP_DIM
    for g in range(groups):
        ab = jnp.dot(acc[:, g * C:(g + 1) * C].astype(BF16), cs_ref[...], preferred_element_type=F32)
        a_ref[:, g * C:(g + 1) * C] = ab[:, :C].astype(BF16)
        b_ref[:, g * C:(g + 1) * C] = ab[:, C:].astype(BF16)


def _inproj_f(hb, w_f, cs_tab):
    T, D = hb.shape
    fw = w_f.shape[1]
    tm = _tile(T, 1024)
    tn = _tile(fw, 512)
    return pl.pallas_call(
        functools.partial(_inproj_f_kernel, groups=tn // FOURIER_GROUP_DIM),
        grid=(T // tm, fw // tn),
        in_specs=[pl.BlockSpec((tm, D), lambda i, j: (i, 0)),
                  pl.BlockSpec((D, tn), lambda i, j: (0, j)),
                  pl.BlockSpec((FOURIER_GROUP_DIM, 2 * FOURIER_GROUP_DIM), lambda i, j: (0, 0))],
        out_specs=[pl.BlockSpec((tm, tn), lambda i, j: (i, j)),
                   pl.BlockSpec((tm, tn), lambda i, j: (i, j))],
        out_shape=[jax.ShapeDtypeStruct((T, fw), BF16), jax.ShapeDtypeStruct((T, fw), BF16)],
        compiler_params=_cparams(("parallel", "arbitrary"), 48),
        name="inproj_fourier",
    )(hb, w_f, cs_tab)


def _dft_gen_kernel(tac_ref, tas_ref, tbc_ref, tbs_ref, cs_ref, sn_ref):
    tbc = tbc_ref[...]
    tbs = tbs_ref[...]
    for aa in range(tac_ref.shape[0]):
        ca = tac_ref[aa:aa + 1, :]
        sa = tas_ref[aa:aa + 1, :]
        rows = slice(aa * DFT_ROW_SPLIT, (aa + 1) * DFT_ROW_SPLIT)
        cs_ref[rows, :] = (ca * tbc - sa * tbs).astype(BF16)
        sn_ref[rows, :] = (-(sa * tbc + ca * tbs)).astype(BF16)


def _dft_matrices(S, n_chan):
    na = S // 2 // DFT_ROW_SPLIT
    scale = 1.0 / math.sqrt(S * n_chan)
    col = jnp.arange(S // 2, dtype=jnp.int32)[None, :]
    period = S // DFT_ROW_SPLIT
    ang_a = (2.0 * math.pi / period) * ((jnp.arange(na, dtype=jnp.int32)[:, None] * col) % period).astype(F32)
    ang_b = (2.0 * math.pi / S) * ((jnp.arange(DFT_ROW_SPLIT, dtype=jnp.int32)[:, None] * col) % S).astype(F32)
    tac, tas = jnp.cos(ang_a), jnp.sin(ang_a)
    tbc, tbs = scale * jnp.cos(ang_b), scale * jnp.sin(ang_b)
    ta = min(8, na)
    assert na % ta == 0
    tc = _tile(S // 2, 2048)
    return pl.pallas_call(
        _dft_gen_kernel,
        grid=(na // ta, S // 2 // tc),
        in_specs=[pl.BlockSpec((ta, tc), lambda i, j: (i, j)),
                  pl.BlockSpec((ta, tc), lambda i, j: (i, j)),
                  pl.BlockSpec((DFT_ROW_SPLIT, tc), lambda i, j: (0, j)),
                  pl.BlockSpec((DFT_ROW_SPLIT, tc), lambda i, j: (0, j))],
        out_specs=[pl.BlockSpec((ta * DFT_ROW_SPLIT, tc), lambda i, j: (i, j)),
                   pl.BlockSpec((ta * DFT_ROW_SPLIT, tc), lambda i, j: (i, j))],
        out_shape=[jax.ShapeDtypeStruct((S // 2, S // 2), BF16), jax.ShapeDtypeStruct((S // 2, S // 2), BF16)],
        compiler_params=_cparams(("parallel", "parallel"), 40),
        name="dft_matrices",
    )(tac, tas, tbc, tbs)


def _dft_fold_kernel(a_ref, am_ref, an_ref, b_ref, bm_ref, bn_ref, ae_ref, bo_ref, alt_ref):
    tm = a_ref.shape[0]
    r = lax.broadcasted_iota(jnp.int32, (tm, tm), 0)
    c = lax.broadcasted_iota(jnp.int32, (tm, tm), 1)
    rev = jnp.where(r + c == tm, 1.0, 0.0).astype(BF16)
    has_next = (pl.program_id(1) > 0).astype(F32)
    first = (jnp.where(r + c == 0, 1.0, 0.0) * has_next).astype(BF16)

    def mirrored(m_ref, n_ref):
        return (jnp.dot(rev, m_ref[...], preferred_element_type=F32)
                + jnp.dot(first, n_ref[...], preferred_element_type=F32))

    ae_ref[...] = (a_ref[...].astype(F32) + mirrored(am_ref, an_ref)).astype(BF16)
    bo_ref[...] = (b_ref[...].astype(F32) - mirrored(bm_ref, bn_ref)).astype(BF16)

    @pl.when(pl.program_id(1) == 0)
    def _():
        alt_ref[...] = jnp.zeros(alt_ref.shape, F32)

    row = lax.broadcasted_iota(jnp.int32, a_ref.shape, 0)
    both = a_ref[...].astype(F32) + am_ref[...].astype(F32)
    alt_ref[0:1, :] += jnp.sum(jnp.where(row % 2 == 0, both, -both), axis=0, keepdims=True)


def _dft_fold(a, b, B, S):
    fw = a.shape[1]
    tm = _tile(S // 2, 256)
    nb = S // tm
    nh = nb // 2

    def own(bb, i):
        return (bb * nb + i, 0)

    def mirror(bb, i):
        return (bb * nb + nb - 1 - i, 0)

    def mirror_next(bb, i):
        return (bb * nb + jnp.minimum(nb - i, nb - 1), 0)

    spec = lambda f: pl.BlockSpec((tm, fw), f)
    return pl.pallas_call(
        _dft_fold_kernel,
        grid=(B, nh),
        in_specs=[spec(own), spec(mirror), spec(mirror_next), spec(own), spec(mirror), spec(mirror_next)],
        out_specs=[pl.BlockSpec((tm, fw), lambda bb, i: (bb * nh + i, 0)),
                   pl.BlockSpec((tm, fw), lambda bb, i: (bb * nh + i, 0)),
                   pl.BlockSpec((8, fw), lambda bb, i: (bb, 0))],
        out_shape=[jax.ShapeDtypeStruct((B * S // 2, fw), BF16),
                   jax.ShapeDtypeStruct((B * S // 2, fw), BF16),
                   jax.ShapeDtypeStruct((B * 8, fw), F32)],
        compiler_params=_cparams(("parallel", "arbitrary"), 40),
        name="dft_fold",
    )(a, a, a, b, b, b)


def _seq_dft_kernel(cs_ref, sn_ref, ae_ref, bo_ref, mid_ref, y_ref, d_ref, p_s, q_s, *, scale):
    kk = pl.program_id(3)

    @pl.when(kk == 0)
    def _():
        p_s[...] = jnp.zeros(p_s.shape, F32)
        q_s[...] = jnp.zeros(q_s.shape, F32)

    p_s[...] += jnp.dot(cs_ref[...], ae_ref[...], preferred_element_type=F32)
    q_s[...] += jnp.dot(sn_ref[...], bo_ref[...], preferred_element_type=F32)

    @pl.when(kk == pl.num_programs(3) - 1)
    def _():
        row = lax.broadcasted_iota(jnp.int32, p_s.shape, 0)
        pm = p_s[...] + jnp.where(row % 2 == 0, scale, -scale) * mid_ref[0:1, :].astype(F32)
        y_ref[...] = (pm + q_s[...]).astype(y_ref.dtype)
        d_ref[...] = (pm - q_s[...]).astype(d_ref.dtype)


def _seq_dft(cs, sn, ae, bo, a, B, S, scale):
    fw = a.shape[1]
    half = S // 2
    tm = _tile(half, 1024)
    tn = _tile(fw, 1024)
    tk = _tile(half, 1024)
    nm, nk = half // tm, half // tk
    mid_rows = 16
    return pl.pallas_call(
        functools.partial(_seq_dft_kernel, scale=scale),
        grid=(B, nm, fw // tn, nk),
        in_specs=[pl.BlockSpec((tm, tk), lambda bb, i, j, k: (i, k)),
                  pl.BlockSpec((tm, tk), lambda bb, i, j, k: (i, k)),
                  pl.BlockSpec((tk, tn), lambda bb, i, j, k: (bb * nk + k, j)),
                  pl.BlockSpec((tk, tn), lambda bb, i, j, k: (bb * nk + k, j)),
                  pl.BlockSpec((mid_rows, tn),
                               lambda bb, i, j, k: ((bb * S + half) // mid_rows, j))],
        out_specs=[pl.BlockSpec((tm, tn), lambda bb, i, j, k: (bb * nm + i, j)),
                   pl.BlockSpec((tm, tn), lambda bb, i, j, k: (bb * nm + i, j))],
        out_shape=[jax.ShapeDtypeStruct((B * half, fw), BF16),
                   jax.ShapeDtypeStruct((B * half, fw), BF16)],
        scratch_shapes=[pltpu.VMEM((tm, tn), F32), pltpu.VMEM((tm, tn), F32)],
        compiler_params=_cparams(("parallel", "parallel", "parallel", "arbitrary"), 48),
        name="seq_dft",
    )(cs, sn, ae, bo, a)


def _dft_mirror_kernel(yh_ref, dm_ref, dn_ref, alt_ref, y_ref, *, scale, nh):
    j = pl.program_id(1)

    @pl.when(j < nh)
    def _():
        y_ref[...] = yh_ref[...]

    @pl.when(j >= nh)
    def _():
        tm = dm_ref.shape[0]
        r = lax.broadcasted_iota(jnp.int32, (tm, tm), 0)
        c = lax.broadcasted_iota(jnp.int32, (tm, tm), 1)
        rev = jnp.where(r + c == tm, 1.0, 0.0).astype(BF16)
        first = (jnp.where(r + c == 0, 1.0, 0.0) * (j > nh).astype(F32)).astype(BF16)
        out = (jnp.dot(rev, dm_ref[...], preferred_element_type=F32)
               + jnp.dot(first, dn_ref[...], preferred_element_type=F32))
        row = lax.broadcasted_iota(jnp.int32, out.shape, 0)
        nyquist = jnp.where(row == 0, scale * (j == nh).astype(F32), 0.0) * alt_ref[0:1, :]
        y_ref[...] = (out + nyquist).astype(y_ref.dtype)


def _dft_mirror(y_half, d, alt, B, S, scale):
    fw = d.shape[1]
    half = S // 2
    tm = _tile(half, 256)
    nh = half // tm

    def mirror(bb, j):
        return (bb * nh + nh - 1 - jnp.maximum(j - nh, 0), 0)

    def mirror_next(bb, j):
        return (bb * nh + jnp.minimum(nh - jnp.maximum(j - nh, 0), nh - 1), 0)

    return pl.pallas_call(
        functools.partial(_dft_mirror_kernel, scale=scale, nh=nh),
        grid=(B, 2 * nh),
        in_specs=[pl.BlockSpec((tm, fw), lambda bb, j: (bb * nh + jnp.minimum(j, nh - 1), 0)),
                  pl.BlockSpec((tm, fw), mirror),
                  pl.BlockSpec((tm, fw), mirror_next),
                  pl.BlockSpec((8, fw), lambda bb, j: (bb, 0))],
        out_specs=pl.BlockSpec((tm, fw), lambda bb, j: (bb * 2 * nh + j, 0)),
        out_shape=jax.ShapeDtypeStruct((B * S, fw), d.dtype),
        compiler_params=_cparams(("parallel", "arbitrary"), 40),
        name="dft_mirror",
    )(y_half, d, d, alt)


def _wo_kernel(ym_ref, yf_ref, gm_ref, gf_ref, w_ref, x_ref, mu_ref, rs_ref, lg_ref, lb_ref,
               pre_ref, mix_s, *, alpha, mw):
    @pl.when(pl.program_id(1) == 0)
    def _():
        ym = ym_ref[...].astype(F32)
        mix_s[:, :mw] = (ym * lax.rsqrt(jnp.mean(ym * ym, axis=-1, keepdims=True) + RMS_EPS)
                         * gm_ref[...]).astype(BF16)
        yf = yf_ref[...].astype(F32)
        mix_s[:, mw:] = (yf * lax.rsqrt(jnp.mean(yf * yf, axis=-1, keepdims=True) + RMS_EPS)
                         * gf_ref[...]).astype(BF16)

    h = (x_ref[...] - mu_ref[...]) * rs_ref[...] * lg_ref[...] + lb_ref[...]
    pre_ref[...] = alpha * h + jnp.dot(mix_s[...], w_ref[...], preferred_element_type=F32)


def _wo(ym, yf, gm, gf, w_o, x2, mu, rs, ln_g, ln_b, alpha):
    T, mw = ym.shape
    fw = yf.shape[1]
    D = w_o.shape[1]
    tm = _tile(T, 1024)
    tn = _tile(D, 512)
    return pl.pallas_call(
        functools.partial(_wo_kernel, alpha=alpha, mw=mw),
        grid=(T // tm, D // tn),
        in_specs=[pl.BlockSpec((tm, mw), lambda i, j: (i, 0)),
                  pl.BlockSpec((tm, fw), lambda i, j: (i, 0)),
                  pl.BlockSpec((1, mw), lambda i, j: (0, 0)),
                  pl.BlockSpec((1, fw), lambda i, j: (0, 0)),
                  pl.BlockSpec((mw + fw, tn), lambda i, j: (0, j)),
                  pl.BlockSpec((tm, tn), lambda i, j: (i, j)),
                  pl.BlockSpec((tm, 1), lambda i, j: (i, 0)),
                  pl.BlockSpec((tm, 1), lambda i, j: (i, 0)),
                  pl.BlockSpec((1, tn), lambda i, j: (0, j)),
                  pl.BlockSpec((1, tn), lambda i, j: (0, j))],
        out_specs=pl.BlockSpec((tm, tn), lambda i, j: (i, j)),
        out_shape=jax.ShapeDtypeStruct((T, D), F32),
        scratch_shapes=[pltpu.VMEM((tm, mw + fw), BF16)],
        compiler_params=_cparams(("parallel", "arbitrary"), 56),
        name="w_o_residual",
    )(ym, yf, gm.reshape(1, mw), gf.reshape(1, fw), w_o, x2, mu, rs,
      ln_g.reshape(1, D), ln_b.reshape(1, D))


def _pack_rounded_pair(lo, hi):
    return (pltpu.bitcast(lo, jnp.uint32) >> 16) | (pltpu.bitcast(hi, jnp.uint32) & jnp.uint32(0xFFFF0000))


def _pack_bf16_pair(lo, hi):
    return _pack_rounded_pair(lo.astype(BF16).astype(F32), hi.astype(BF16).astype(F32))


def _unpack_bf16_pair(w):
    lo = pltpu.bitcast(w << 16, F32).astype(BF16)
    hi = pltpu.bitcast(w & jnp.uint32(0xFFFF0000), F32).astype(BF16)
    return lo, hi


def _ln1_router_kernel(pre_ref, g_ref, b_ref, rwh_ref, rwl_ref, rb_ref, x1_ref, x1p_ref, idx_ref, gate_ref):
    x = pre_ref[...]
    mu = jnp.mean(x, axis=-1, keepdims=True)
    xc = x - mu
    var = jnp.mean(xc * xc, axis=-1, keepdims=True)
    x1 = xc * lax.rsqrt(var + LN_EPS) * g_ref[...] + b_ref[...]
    x1_ref[...] = x1
    half = x1.shape[1] // 2
    x_hi = x1.astype(BF16)
    x_hi_f = x_hi.astype(F32)
    x1p_ref[...] = _pack_rounded_pair(x_hi_f[:, :half], x_hi_f[:, half:])
    x_lo = (x1 - x_hi_f).astype(BF16)
    nt = (((1,), (1,)), ((), ()))
    logits = (lax.dot_general(rwh_ref[...], x_hi, nt, preferred_element_type=F32)
              + lax.dot_general(rwh_ref[...], x_lo, nt, preferred_element_type=F32)
              + lax.dot_general(rwl_ref[...], x_hi, nt, preferred_element_type=F32)) + rb_ref[...]
    n_e = logits.shape[0]
    eidx = lax.broadcasted_iota(jnp.int32, logits.shape, 0)
    vals, idxs = [], []
    for _ in range(TOP_K):
        m = jnp.max(logits, axis=0, keepdims=True)
        sel = jnp.min(jnp.where(logits == m, eidx, n_e), axis=0, keepdims=True)
        logits = jnp.where(eidx == sel, -jnp.inf, logits)
        vals.append(m)
        idxs.append(sel)
    exps = [jnp.exp(v - vals[0]) for v in vals]
    denom = exps[0] + exps[1] + exps[2] + exps[3]
    for kk in range(TOP_K):
        idx_ref[kk:kk + 1, :] = idxs[kk]
        gate_ref[kk:kk + 1, :] = exps[kk] / denom


def _ln1_router(pre, g, b, router_w, router_b):
    T, D = pre.shape
    E = router_w.shape[1]
    tm = _tile(T, 256)
    rw_t = router_w.T
    rw_hi = rw_t.astype(BF16)
    rw_lo = (rw_t - rw_hi.astype(F32)).astype(BF16)
    return pl.pallas_call(
        _ln1_router_kernel,
        grid=(T // tm,),
        in_specs=[pl.BlockSpec((tm, D), lambda i: (i, 0)),
                  pl.BlockSpec((1, D), lambda i: (0, 0)),
                  pl.BlockSpec((1, D), lambda i: (0, 0)),
                  pl.BlockSpec((E, D), lambda i: (0, 0)),
                  pl.BlockSpec((E, D), lambda i: (0, 0)),
                  pl.BlockSpec((E, 1), lambda i: (0, 0))],
        out_specs=[pl.BlockSpec((tm, D), lambda i: (i, 0)),
                   pl.BlockSpec((tm, D // 2), lambda i: (i, 0)),
                   pl.BlockSpec((TOP_K, tm), lambda i: (0, i)),
                   pl.BlockSpec((TOP_K, tm), lambda i: (0, i))],
        out_shape=[jax.ShapeDtypeStruct((T, D), F32),
                   jax.ShapeDtypeStruct((T, D // 2), jnp.uint32),
                   jax.ShapeDtypeStruct((TOP_K, T), jnp.int32),
                   jax.ShapeDtypeStruct((TOP_K, T), F32)],
        compiler_params=_cparams(("parallel",), 48),
        name="ln1_router",
    )(pre, g.reshape(1, D), b.reshape(1, D), rw_hi, rw_lo, router_b.reshape(E, 1))


def _row_copy(src_hbm, dst_ref, src_row, dst_row, sem):
    return pltpu.make_async_copy(src_hbm.at[pl.ds(src_row, 1)], dst_ref.at[pl.ds(dst_row, 1)], sem)


def _dispatch_kernel(dest_ref, pad_ref, npad_ref, nv_ref, x_ref, o_hbm, zero_s, sem, zsem, *, n_steps):
    b = pl.program_id(0)
    tt = x_ref.shape[0]

    def wait_rows(n):
        pltpu.make_async_copy(o_hbm.at[pl.ds(0, n)], o_hbm.at[pl.ds(0, n)], sem).wait()

    @pl.when(b < n_steps)
    def _():
        def issue(r, c):
            for kk in range(TOP_K):
                _row_copy(x_ref, o_hbm, r, dest_ref[(b * tt + r) * TOP_K + kk], sem).start()
            return c
        lax.fori_loop(0, tt, issue, 0, unroll=2)
        wait_rows(tt * TOP_K)

    @pl.when(b == n_steps)
    def _():
        n_pad = npad_ref[0]

        def issue_pad(i, c):
            _row_copy(x_ref, o_hbm, 0, pad_ref[i], sem).start()
            return c
        lax.fori_loop(0, n_pad, issue_pad, 0)

        def wait_pad(i, c):
            wait_rows(1)
            return c
        lax.fori_loop(0, n_pad, wait_pad, 0)

        zero_s[...] = jnp.zeros(zero_s.shape, zero_s.dtype)
        n_sub = MOE_CHUNK // MOE_SUB

        def zero_copy(i):
            return pltpu.make_async_copy(zero_s, o_hbm.at[pl.ds(pl.multiple_of(i * MOE_SUB, MOE_SUB), MOE_SUB)], zsem)

        def empty(i):
            return (i % n_sub) * MOE_SUB >= nv_ref[i // n_sub]

        def issue_zero(i, c):
            @pl.when(empty(i))
            def _():
                zero_copy(i).start()
            return c
        lax.fori_loop(0, nv_ref.shape[0] * n_sub, issue_zero, 0)

        def wait_zero(i, c):
            @pl.when(empty(i))
            def _():
                zero_copy(i).wait()
            return c
        lax.fori_loop(0, nv_ref.shape[0] * n_sub, wait_zero, 0)


def _dispatch(dest, pad_slots, n_pad, chunk_nv, x1p):
    T, W = x1p.shape
    tt = _tile(T, DISPATCH_TOKENS)
    n_steps = T // tt
    grid_spec = pltpu.PrefetchScalarGridSpec(
        num_scalar_prefetch=4,
        grid=(n_steps + 1,),
        in_specs=[pl.BlockSpec((tt, W), lambda b, d, p, n, v: (jnp.minimum(b, n_steps - 1), 0))],
        out_specs=pl.BlockSpec(memory_space=pl.ANY),
        scratch_shapes=[pltpu.VMEM((MOE_SUB, W), x1p.dtype),
                        pltpu.SemaphoreType.DMA(()), pltpu.SemaphoreType.DMA(())],
    )
    return pl.pallas_call(
        functools.partial(_dispatch_kernel, n_steps=n_steps),
        grid_spec=grid_spec,
        out_shape=jax.ShapeDtypeStruct((chunk_nv.shape[0] * MOE_CHUNK, W), x1p.dtype),
        compiler_params=_cparams(("arbitrary",), 32),
        name="moe_dispatch",
    )(dest, pad_slots, n_pad, chunk_nv, x1p)


def _expert_up_kernel(ce_ref, nv_ref, nu_ref, x_ref, wg_ref, wu_ref, bg_ref, bu_ref, h_ref):
    c = pl.program_id(0)
    nv = nv_ref[c]
    brow = ce_ref[c] * pl.num_programs(1) + pl.program_id(1)

    n_sub = MOE_CHUNK // MOE_SUB
    n_act = (nv + MOE_SUB - 1) // MOE_SUB
    for k in range(n_sub + 1):
        @pl.when(n_act == k)
        def _(k=k):
            if k > 0:
                x_lo, x_hi = _unpack_bf16_pair(x_ref[:k * MOE_SUB, :])
                half = x_lo.shape[1]

                def proj(w_ref):
                    return (jnp.dot(x_lo, w_ref[:half, :].astype(BF16), preferred_element_type=F32)
                            + jnp.dot(x_hi, w_ref[half:, :].astype(BF16), preferred_element_type=F32))
                hg = jnp.minimum(proj(wg_ref) + bg_ref[pl.ds(brow, 1), :], SWIGLU_LIMIT)
                hu = jnp.clip(proj(wu_ref) + bu_ref[pl.ds(brow, 1), :], -SWIGLU_LIMIT, SWIGLU_LIMIT)
                act = (hu + 1.0) * (hg * jax.nn.sigmoid(SWIGLU_ALPHA * hg))
                h_ref[:k * MOE_SUB, :] = act.astype(h_ref.dtype)
            if k < n_sub:
                h_ref[k * MOE_SUB:, :] = jnp.zeros(((n_sub - k) * MOE_SUB, h_ref.shape[1]), h_ref.dtype)


def _expert_up(chunk_e, chunk_nv, n_used, xs, w_gate, w_up, b_gate, b_up, n_chunks):
    E, D, F = w_gate.shape
    tf = _tile(F, 256)
    nf = F // tf

    def used(c, nu):
        return jnp.minimum(c, nu[0] - 1)

    def jeff(c, j, nu):
        return jnp.where(c < nu[0], j, nf - 1)

    grid_spec = pltpu.PrefetchScalarGridSpec(
        num_scalar_prefetch=3,
        grid=(n_chunks, nf),
        in_specs=[pl.BlockSpec((MOE_CHUNK, D // 2), lambda c, j, ce, nv, nu: (used(c, nu), 0)),
                  pl.BlockSpec((None, D, tf), lambda c, j, ce, nv, nu: (ce[c], 0, jeff(c, j, nu))),
                  pl.BlockSpec((None, D, tf), lambda c, j, ce, nv, nu: (ce[c], 0, jeff(c, j, nu))),
                  pl.BlockSpec((E * nf, tf), lambda c, j, ce, nv, nu: (0, 0)),
                  pl.BlockSpec((E * nf, tf), lambda c, j, ce, nv, nu: (0, 0))],
        out_specs=pl.BlockSpec((MOE_CHUNK, tf), lambda c, j, ce, nv, nu: (c, j)),
    )
    return pl.pallas_call(
        _expert_up_kernel,
        grid_spec=grid_spec,
        out_shape=jax.ShapeDtypeStruct((n_chunks * MOE_CHUNK, F), BF16),
        compiler_params=_cparams(("arbitrary", "arbitrary"), 60),
        name="expert_gate_up",
    )(chunk_e, chunk_nv, n_used, xs, w_gate, w_up, b_gate.reshape(E * nf, tf), b_up.reshape(E * nf, tf))


def _expert_down_kernel(ce_ref, nv_ref, nu_ref, h_ref, wd_ref, bd_ref, y_ref):
    c = pl.program_id(0)
    nv = nv_ref[c]
    brow = ce_ref[c] * pl.num_programs(1) + pl.program_id(1)

    n_sub = MOE_CHUNK // MOE_SUB
    n_act = (nv + MOE_SUB - 1) // MOE_SUB
    for k in range(n_sub + 1):
        @pl.when(n_act == k)
        def _(k=k):
            if k > 0:
                out = jnp.dot(h_ref[:k * MOE_SUB, :], wd_ref[...].astype(BF16),
                              preferred_element_type=F32) + bd_ref[pl.ds(brow, 1), :]
                hw = out.shape[1] // 2
                y_ref[:k * MOE_SUB, :] = _pack_bf16_pair(out[:, :hw], out[:, hw:])
            if k < n_sub:
                y_ref[k * MOE_SUB:, :] = jnp.zeros(((n_sub - k) * MOE_SUB, y_ref.shape[1]), y_ref.dtype)


def _expert_down(chunk_e, chunk_nv, n_used, hmid, w_down, b_down, n_chunks):
    E, F, D = w_down.shape
    tn = _tile(D, MOE_DOWN_TN)
    nn = D // tn

    def used(c, nu):
        return jnp.minimum(c, nu[0] - 1)

    def jeff(c, j, nu):
        return jnp.where(c < nu[0], j, nn - 1)

    grid_spec = pltpu.PrefetchScalarGridSpec(
        num_scalar_prefetch=3,
        grid=(n_chunks, nn),
        in_specs=[pl.BlockSpec((MOE_CHUNK, F), lambda c, j, ce, nv, nu: (used(c, nu), 0)),
                  pl.BlockSpec((None, F, tn), lambda c, j, ce, nv, nu: (ce[c], 0, jeff(c, j, nu))),
                  pl.BlockSpec((E * nn, tn), lambda c, j, ce, nv, nu: (0, 0))],
        out_specs=pl.BlockSpec((MOE_CHUNK, tn // 2), lambda c, j, ce, nv, nu: (c, j)),
    )
    return pl.pallas_call(
        _expert_down_kernel,
        grid_spec=grid_spec,
        out_shape=jax.ShapeDtypeStruct((n_chunks * MOE_CHUNK, D // 2), jnp.uint32),
        compiler_params=_cparams(("arbitrary", "arbitrary"), 56),
        name="expert_down",
    )(chunk_e, chunk_nv, n_used, hmid, w_down, b_down.reshape(E * nn, tn))


def _combine_kernel(dest_ref, y_hbm, x1_ref, gate_ref, g_ref, b_ref, o_ref, *scratch, alpha, tn):
    i = pl.program_id(0)
    n = pl.num_programs(0)
    nb = COMBINE_BUFS
    tt = o_ref.shape[0] // nb
    bufs, sem = scratch[:nb], scratch[nb]

    def issue_tile(tile, slot):
        base = jnp.minimum(tile, nb * n - 1) * tt
        for r in range(tt):
            for kk in range(TOP_K):
                _row_copy(y_hbm, bufs[slot].at[kk], dest_ref[(base + r) * TOP_K + kk], r, sem.at[slot]).start()

    def wait_tile(slot):
        for kk in range(TOP_K):
            pltpu.make_async_copy(y_hbm.at[pl.ds(0, tt)], bufs[slot].at[kk], sem.at[slot]).wait()

    def finish_tile(slot):
        rows = slice(slot * tt, (slot + 1) * tt)
        gates = gate_ref[rows, :]
        y_lo = y_hi = None
        for kk in range(TOP_K):
            w = bufs[slot][kk]
            g_k = gates[:, kk:kk + 1]
            lo = g_k * pltpu.bitcast(w << 16, F32)
            hi = g_k * pltpu.bitcast(w & jnp.uint32(0xFFFF0000), F32)
            y_lo = lo if y_lo is None else y_lo + lo
            y_hi = hi if y_hi is None else y_hi + hi
        hw = tn // 2
        pieces = []
        for j in range(x1_ref.shape[1] // tn):
            pieces.append((slice(j * tn, j * tn + hw), y_lo[:, j * hw:(j + 1) * hw]))
            pieces.append((slice(j * tn + hw, (j + 1) * tn), y_hi[:, j * hw:(j + 1) * hw]))
        zs = [alpha * x1_ref[rows, cols] + y for cols, y in pieces]
        d = x1_ref.shape[1]
        mu = sum(jnp.sum(z, axis=-1, keepdims=True) for z in zs) / d
        zcs = [z - mu for z in zs]
        var = sum(jnp.sum(zc * zc, axis=-1, keepdims=True) for zc in zcs) / d
        rs = lax.rsqrt(var + LN_EPS)
        for (cols, _), zc in zip(pieces, zcs):
            o_ref[rows, cols] = zc * rs * g_ref[:, cols] + b_ref[:, cols]

    @pl.when(i == 0)
    def _():
        for t in range(COMBINE_AHEAD):
            issue_tile(t, t)

    for t in range(nb):
        wait_tile(t)
        issue_tile(nb * i + t + COMBINE_AHEAD, (t + COMBINE_AHEAD) % nb)
        finish_tile(t)

    @pl.when(i == n - 1)
    def _():
        for t in range(COMBINE_AHEAD):
            wait_tile(t)


def _combine(dest, yslots, x1, gates_tk, g, b, alpha):
    T, D = x1.shape
    nb = COMBINE_BUFS
    tt = _tile(T // nb, COMBINE_TOKENS)
    grid_spec = pltpu.PrefetchScalarGridSpec(
        num_scalar_prefetch=1,
        grid=(T // (nb * tt),),
        in_specs=[pl.BlockSpec(memory_space=pl.ANY),
                  pl.BlockSpec((nb * tt, D), lambda i, d: (i, 0)),
                  pl.BlockSpec((nb * tt, TOP_K), lambda i, d: (i, 0)),
                  pl.BlockSpec((1, D), lambda i, d: (0, 0)),
                  pl.BlockSpec((1, D), lambda i, d: (0, 0))],
        out_specs=pl.BlockSpec((nb * tt, D), lambda i, d: (i, 0)),
        scratch_shapes=[pltpu.VMEM((TOP_K, tt, D // 2), jnp.uint32) for _ in range(nb)]
        + [pltpu.SemaphoreType.DMA((nb,))],
    )
    return pl.pallas_call(
        functools.partial(_combine_kernel, alpha=alpha, tn=_tile(D, MOE_DOWN_TN)),
        grid_spec=grid_spec,
        out_shape=jax.ShapeDtypeStruct((T, D), F32),
        compiler_params=_cparams(("arbitrary",), 48),
        name="moe_combine_ln2",
    )(dest, yslots, x1, gates_tk, g.reshape(1, D), b.reshape(1, D))


def _routing_tables(top_idx, n_experts, n_chunks):
    T = top_idx.shape[1]
    M = T * TOP_K
    flat_e = top_idx.T.reshape(M)
    onehot = (flat_e[:, None] == jnp.arange(n_experts, dtype=jnp.int32)[None, :]).astype(jnp.int32)
    csum = jnp.cumsum(onehot, axis=0)
    rank = jnp.sum(csum * onehot, axis=1) - 1
    counts = csum[-1]
    chunks_e = (counts + MOE_CHUNK - 1) // MOE_CHUNK
    chunk_end = jnp.cumsum(chunks_e)
    chunk_start = chunk_end - chunks_e
    n_used = chunk_end[-1]
    dest = chunk_start[flat_e] * MOE_CHUNK + rank
    P = n_chunks * MOE_CHUNK
    pad_idx = counts[:, None] + jnp.arange(MOE_SUB, dtype=jnp.int32)[None, :]
    padded = (counts + MOE_SUB - 1) // MOE_SUB * MOE_SUB
    pad_slots = jnp.where(pad_idx < padded[:, None], chunk_start[:, None] * MOE_CHUNK + pad_idx, P)
    pad_slots = jnp.sort(pad_slots.reshape(-1)).astype(jnp.int32)
    n_pad = jnp.sum(padded - counts).astype(jnp.int32).reshape(1)
    cid = jnp.arange(n_chunks, dtype=jnp.int32)
    chunk_e = jnp.minimum(jnp.searchsorted(chunk_end, cid, side='right'), n_experts - 1).astype(jnp.int32)
    last_e = chunk_e[jnp.maximum(n_used - 1, 0)]
    chunk_e = jnp.where(cid < n_used, chunk_e, last_e)
    chunk_nv = jnp.where(cid < n_used,
                         jnp.clip(counts[chunk_e] - (cid - chunk_start[chunk_e]) * MOE_CHUNK, 0, MOE_CHUNK),
                         0).astype(jnp.int32)
    return (dest.astype(jnp.int32), pad_slots, n_pad, chunk_e, chunk_nv,
            n_used.astype(jnp.int32).reshape(1))


def kernel(x, positions, ln_in_g, ln_in_b, w_in, q_a_norm_g, w_q_b, kv_a_norm_g, w_kv_b, mla_out_norm_g, fourier_out_norm_g, w_o, ln1_g, ln1_b, router_w, router_b, w_gate, b_gate, w_up, b_up, w_down, b_down, ln2_g, ln2_b):
    B, S, D = x.shape
    T = B * S
    depth = w_in.shape[0]
    assert depth == 1, "single-layer trunk only"
    qr = q_a_norm_g.shape[1]
    kvr = kv_a_norm_g.shape[1]
    H = w_q_b.shape[2] // (QK_NOPE_DIM + QK_ROPE_DIM)
    fw = fourier_out_norm_g.shape[1]
    E = router_w.shape[2]
    assert H % HEADS_PER_TILE == 0 and (qr + kvr) % LANES == 0 and S % (2 * DFT_ROW_SPLIT) == 0
    alpha = (2.0 * depth) ** 0.25

    inv_freq = ROPE_THETA ** (-jnp.arange(0, QK_ROPE_DIM, 2, dtype=F32) / QK_ROPE_DIM)
    ang = positions.astype(F32)[..., None] * inv_freq
    cos4 = jnp.tile(jnp.cos(ang), (1, 1, 2 * LANES // QK_ROPE_DIM)).reshape(T, LANES)
    sin4 = jnp.tile(jnp.sin(ang), (1, 1, 2 * LANES // QK_ROPE_DIM)).reshape(T, LANES)

    rope_end = qr + kvr + QK_ROPE_DIM
    w_a = w_in[0, :, :qr + kvr + LANES].astype(BF16)
    w_f = w_in[0, :, rope_end:].astype(BF16)
    wq = w_q_b[0].reshape(qr, H // HEADS_PER_TILE, HEADS_PER_TILE, QK_NOPE_DIM + QK_ROPE_DIM)
    wq_perm = jnp.concatenate(
        [wq[..., :QK_NOPE_DIM].reshape(qr, H // HEADS_PER_TILE, HEADS_PER_TILE * QK_NOPE_DIM),
         wq[..., QK_NOPE_DIM:].reshape(qr, H // HEADS_PER_TILE, HEADS_PER_TILE * QK_ROPE_DIM)],
        axis=-1).reshape(qr, H * (QK_NOPE_DIM + QK_ROPE_DIM)).astype(BF16)
    w_kv = w_kv_b[0].astype(BF16)
    w_o_b = w_o[0].astype(BF16)
    ch = jnp.arange(FOURIER_GROUP_DIM, dtype=jnp.int32)
    ang_c = (2.0 * math.pi / FOURIER_GROUP_DIM) * ((ch[:, None] * ch[None, :]) % FOURIER_GROUP_DIM).astype(F32)
    cs_tab = jnp.concatenate([jnp.cos(ang_c), jnp.sin(ang_c)], axis=1).astype(BF16)

    x2 = x.reshape(T, D)
    hb, mu, rs = _ln_in(x2, ln_in_g, ln_in_b)

    cq, ckv, kpe2 = _inproj_a(hb, w_a, q_a_norm_g[0], kv_a_norm_g[0], cos4, sin4)
    qscale = (QK_NOPE_DIM + QK_ROPE_DIM) ** -0.5 * LOG2E
    q = _q_up(cq, wq_perm, cos4, sin4, qscale)
    k, v = _kv_up(ckv, w_kv, kpe2)
    y_mla = _attention(q, k, v, B, S, H)

    fa, fb = _inproj_f(hb, w_f, cs_tab)
    cs_mat, sn_mat = _dft_matrices(S, FOURIER_GROUP_DIM)
    fae, fbo, falt = _dft_fold(fa, fb, B, S)
    dft_scale = 1.0 / math.sqrt(S * FOURIER_GROUP_DIM)
    y_half, y_diff = _seq_dft(cs_mat, sn_mat, fae, fbo, fa, B, S, dft_scale)
    y_f = _dft_mirror(y_half, y_diff, falt, B, S, dft_scale)

    pre = _wo(y_mla, y_f, mla_out_norm_g[0], fourier_out_norm_g[0], w_o_b, x2, mu, rs,
              ln_in_g, ln_in_b, alpha)
    x1, x1p, top_idx, gates = _ln1_router(pre, ln1_g[0], ln1_b[0], router_w[0], router_b[0])

    n_chunks = (T * TOP_K + E * (MOE_CHUNK - 1)) // MOE_CHUNK
    dest, pad_slots, n_pad, chunk_e, chunk_nv, n_used = _routing_tables(top_idx, E, n_chunks)
    xs = _dispatch(dest, pad_slots, n_pad, chunk_nv, x1p)
    hmid = _expert_up(chunk_e, chunk_nv, n_used, xs, w_gate[0], w_up[0], b_gate[0], b_up[0], n_chunks)
    yslots = _expert_down(chunk_e, chunk_nv, n_used, hmid, w_down[0], b_down[0], n_chunks)
    out = _combine(dest, yslots, x1, gates.T, ln2_g[0], ln2_b[0], alpha)
    return out.reshape(B, S, D)
```
